```python
import jax, jax.numpy as jnp
from jax import lax
import numpy as np

D_MODEL = 1024
BATCH = 8
SEQ = 4096
DEPTH = 1

N_MEM = 256
RMS_EPS = 1e-6
LN_EPS = 1e-5

POOL_WINDOWS = (2, 4, 8, 16)
N_POOL_GROUPS = len(POOL_WINDOWS)
POOL_WIDTH = D_MODEL
POOL_GROUP = POOL_WIDTH // N_POOL_GROUPS

SGU_WIDTH = D_MODEL
SGU_CHUNK = 128
SGU_HEADS = 8
SGU_HEAD_DIM = SGU_WIDTH // SGU_HEADS

XA_HEADS = 4
XA_HEAD_DIM = D_MODEL // XA_HEADS
XA_WIDTH = XA_HEADS * XA_HEAD_DIM

N_BRANCHES = 3
GATE_WIDTH = N_BRANCHES * D_MODEL
IN_WIDTH = POOL_WIDTH + 2 * SGU_WIDTH + XA_WIDTH + GATE_WIDTH
SPLIT_POINTS = (POOL_WIDTH,
                POOL_WIDTH + SGU_WIDTH,
                POOL_WIDTH + 2 * SGU_WIDTH,
                POOL_WIDTH + 2 * SGU_WIDTH + XA_WIDTH)

PEER_HEADS = 8
PEER_N_KEYS = 128
PEER_N_EXPERTS = PEER_N_KEYS * PEER_N_KEYS
PEER_QUERY_DIM = 256
PEER_HALF = PEER_QUERY_DIM // 2
PEER_TOPK = 16
PEER_TOKEN_BLOCK = 128

kernel_name = "hybrid_pool_sgu_xattn_peer"


def rmsnorm(x, gain):
    xf = x.astype(jnp.float32)
    y = xf * lax.rsqrt(jnp.mean(xf * xf, axis=-1, keepdims=True) + RMS_EPS)
    return (y * gain.astype(jnp.float32)).astype(x.dtype)


def layernorm(x, gain, bias):
    xf = x.astype(jnp.float32)
    mu = jnp.mean(xf, axis=-1, keepdims=True)
    var = jnp.mean(jnp.square(xf - mu), axis=-1, keepdims=True)
    y = (xf - mu) * lax.rsqrt(var + LN_EPS)
    return (y * gain.astype(jnp.float32) + bias.astype(jnp.float32)).astype(x.dtype)


def causal_multiscale_pool(p, w_pool, pool_scale):
    B, S, _ = p.shape
    pf = p.astype(jnp.float32)
    cs = jnp.cumsum(pf, axis=1)
    t = jnp.arange(S)
    means = []
    for g, w in enumerate(POOL_WINDOWS):
        c = cs[..., g * POOL_GROUP:(g + 1) * POOL_GROUP]
        lagged = jnp.pad(c, ((0, 0), (w, 0), (0, 0)))[:, :S]
        count = jnp.minimum(t + 1, w).astype(jnp.float32)[None, :, None]
        means.append((c - lagged) / count)
    diff = (jnp.concatenate(means, axis=-1) - pf).astype(p.dtype)
    diff = diff.reshape(B, S, N_POOL_GROUPS, POOL_GROUP)
    y = jnp.einsum('bsgc,gcd->bsgd', diff, w_pool).reshape(B, S, POOL_WIDTH)
    return y * pool_scale


def chunked_spatial_gating(u, v, ln_gain, ln_bias, w_s, b_s, w_o):
    B, S, _ = u.shape
    u = jax.nn.gelu(u, approximate=False)
    v = layernorm(jax.nn.gelu(v, approximate=False), ln_gain, ln_bias)
    n_chunks = S // SGU_CHUNK
    vc = v.reshape(B, n_chunks, SGU_CHUNK, SGU_HEADS, SGU_HEAD_DIM)
    causal = jnp.tril(jnp.ones((SGU_CHUNK, SGU_CHUNK), dtype=bool))
    w_masked = jnp.where(causal[None], w_s, jnp.zeros((), w_s.dtype))
    mixed = jnp.einsum('hts,bcshd->bcthd', w_masked, vc) + b_s.T[None, None, :, :, None]
    gated = u * mixed.reshape(B, S, SGU_WIDTH)
    return gated @ w_o


def memory_cross_attention(q, mem_n, w_kv, w_o):
    B, S, _ = q.shape
    M = mem_n.shape[1]
    kv = mem_n @ w_kv
    k, v = jnp.split(kv, 2, axis=-1)
    qh = q.reshape(B, S, XA_HEADS, XA_HEAD_DIM)
    kh = k.reshape(B, M, XA_HEADS, XA_HEAD_DIM)
    vh = v.reshape(B, M, XA_HEADS, XA_HEAD_DIM)
    scores = jnp.einsum('bshd,bmhd->bhsm', qh, kh).astype(jnp.float32) * (XA_HEAD_DIM ** -0.5)
    probs = jax.nn.softmax(scores, axis=-1).astype(vh.dtype)
    o = jnp.einsum('bhsm,bmhd->bshd', probs, vh).reshape(B, S, XA_WIDTH)
    return o @ w_o


def peer_ffn(x, w_q, sub_keys1, sub_keys2, expert_u, expert_v):
    B, S, D = x.shape
    T = B * S
    xt = x.reshape(T, D)
    q = (xt @ w_q).reshape(T, PEER_HEADS, 2, PEER_HALF)
    s1 = jnp.einsum('thc,kc->thk', q[:, :, 0], sub_keys1).astype(jnp.float32)
    s2 = jnp.einsum('thc,kc->thk', q[:, :, 1], sub_keys2).astype(jnp.float32)
    v1, i1 = lax.top_k(s1, PEER_TOPK)
    v2, i2 = lax.top_k(s2, PEER_TOPK)
    cand = (v1[..., :, None] + v2[..., None, :]).reshape(T, PEER_HEADS, PEER_TOPK * PEER_TOPK)
    best, ci = lax.top_k(cand, PEER_TOPK)
    e1 = jnp.take_along_axis(i1, ci // PEER_TOPK, axis=-1)
    e2 = jnp.take_along_axis(i2, ci % PEER_TOPK, axis=-1)
    experts = e1 * PEER_N_KEYS + e2
    gates = jax.nn.softmax(best, axis=-1).astype(x.dtype)
    n_blocks = T // PEER_TOKEN_BLOCK

    def block(args):
        xb, eb, gb = args
        ub = jnp.take(expert_u, eb, axis=0)
        vb = jnp.take(expert_v, eb, axis=0)
        a = jax.nn.gelu(jnp.einsum('td,thkd->thk', xb, ub), approximate=False) * gb
        return jnp.einsum('thk,thkd->td', a, vb)

    y = lax.map(block, (xt.reshape(n_blocks, PEER_TOKEN_BLOCK, D),
                        experts.reshape(n_blocks, PEER_TOKEN_BLOCK, PEER_HEADS, PEER_TOPK),
                        gates.reshape(n_blocks, PEER_TOKEN_BLOCK, PEER_HEADS, PEER_TOPK)))
    return y.reshape(B, S, D)


def setup_inputs(seed: int = 0) -> dict:
    key = jax.random.key(seed)
    ks = jax.random.split(key, 24)
    f32 = jnp.float32
    nrm = lambda k, shape, scale: jax.random.normal(k, shape, f32) * scale
    L, D = DEPTH, D_MODEL
    return {
        "x": nrm(ks[0], (BATCH, SEQ, D), 1.0),
        "mem": nrm(ks[1], (BATCH, N_MEM, D), 1.0),
        "norm1_gain": 1.0 + nrm(ks[2], (L, D), 0.02),
        "w_in": nrm(ks[3], (L, D, IN_WIDTH), D ** -0.5),
        "pool_w": nrm(ks[4], (L, N_POOL_GROUPS, POOL_GROUP, POOL_GROUP), POOL_GROUP ** -0.5),
        "pool_scale": 1.0 + nrm(ks[5], (L, POOL_WIDTH), 0.02),
        "sgu_ln_gain": 1.0 + nrm(ks[6], (L, SGU_WIDTH), 0.02),
        "sgu_ln_bias": nrm(ks[7], (L, SGU_WIDTH), 0.02),
        "sgu_w_s": nrm(ks[8], (L, SGU_HEADS, SGU_CHUNK, SGU_CHUNK), SGU_CHUNK ** -0.5),
        "sgu_b_s": 1.0 + nrm(ks[9], (L, SGU_HEADS, SGU_CHUNK), 0.01),
        "sgu_w_out": nrm(ks[10], (L, SGU_WIDTH, D), SGU_WIDTH ** -0.5),
        "mem_norm_gain": 1.0 + nrm(ks[11], (L, D), 0.02),
        "xa_w_kv": nrm(ks[12], (L, D, 2 * XA_WIDTH), D ** -0.5),
        "xa_w_out": nrm(ks[13], (L, XA_WIDTH, D), XA_WIDTH ** -0.5),
        "w_out": nrm(ks[14], (L, D, D), D ** -0.5),
        "norm2_gain": 1.0 + nrm(ks[15], (L, D), 0.02),
        "peer_w_q": nrm(ks[16], (L, D, PEER_HEADS * PEER_QUERY_DIM), D ** -0.5),
        "peer_keys1": nrm(ks[17], (L, PEER_N_KEYS, PEER_HALF), PEER_HALF ** -0.5),
        "peer_keys2": nrm(ks[18], (L, PEER_N_KEYS, PEER_HALF), PEER_HALF ** -0.5),
        "peer_u": nrm(ks[19], (L, PEER_N_EXPERTS, D), D ** -0.5),
        "peer_v": nrm(ks[20], (L, PEER_N_EXPERTS, D), PEER_HEADS ** -0.5),
        "final_norm_gain": 1.0 + nrm(ks[21], (D,), 0.02),
    }


def reference(x, mem, norm1_gain, w_in, pool_w, pool_scale, sgu_ln_gain, sgu_ln_bias,
              sgu_w_s, sgu_b_s, sgu_w_out, mem_norm_gain, xa_w_kv, xa_w_out, w_out,
              norm2_gain, peer_w_q, peer_keys1, peer_keys2, peer_u, peer_v, final_norm_gain):
    B, S, D = x.shape
    h = x
    for l in range(DEPTH):
        n = rmsnorm(h, norm1_gain[l])
        proj = n @ w_in[l]
        p, u, v, q, g = jnp.split(proj, SPLIT_POINTS, axis=-1)
        gates = jax.nn.sigmoid(g.astype(jnp.float32)).astype(h.dtype).reshape(B, S, N_BRANCHES, D)
        y_pool = causal_multiscale_pool(p, pool_w[l], pool_scale[l])
        y_sgu = chunked_spatial_gating(u, v, sgu_ln_gain[l], sgu_ln_bias[l],
                                       sgu_w_s[l], sgu_b_s[l], sgu_w_out[l])
        y_xa = memory_cross_attention(q, rmsnorm(mem, mem_norm_gain[l]), xa_w_kv[l], xa_w_out[l])
        merged = gates[:, :, 0] * y_pool + gates[:, :, 1] * y_sgu + gates[:, :, 2] * y_xa
        h = h + merged @ w_out[l]
        h = h + peer_ffn(rmsnorm(h, norm2_gain[l]), peer_w_q[l], peer_keys1[l], peer_keys2[l],
                         peer_u[l], peer_v[l])
    return rmsnorm(h, final_norm_gain)
```

```python
import functools
import math

import jax
import jax.numpy as jnp
from jax import lax
from jax.experimental import pallas as pl
from jax.experimental.pallas import tpu as pltpu

F32 = jnp.float32
MXU_DTYPE = jnp.bfloat16

RMS_EPS = 1e-6
LN_EPS = 1e-5
POOL_WINDOWS = (2, 4, 8, 16)
POOL_HALO = 16
SGU_CHUNK = 128
SGU_HEADS = 8
XA_HEADS = 4
PEER_HEADS = 8
PEER_N_KEYS = 128
PEER_TOPK = 16

V7X_VMEM_BYTES = 64 * 1024 * 1024
VMEM_LIMIT_BYTES = 56 * 1024 * 1024

SEQ_TILE = 256
SELECT_TILE = 256
EXPERT_TILE = 32


def _rmsnorm(x, gain):
    return x * lax.rsqrt(jnp.mean(x * x, axis=-1, keepdims=True) + RMS_EPS) * gain


def _gelu(x):
    return 0.5 * x * (1.0 + lax.erf(x * (1.0 / math.sqrt(2.0))))


def _mm(a, b):
    return jnp.dot(a.astype(MXU_DTYPE), b.astype(MXU_DTYPE), preferred_element_type=F32)


def _mm_nt(a, b):
    return lax.dot_general(a.astype(MXU_DTYPE), b.astype(MXU_DTYPE),
                           (((1,), (1,)), ((), ())), preferred_element_type=F32)


def _memkv_kernel(mem_ref, gain_ref, wkv_ref, k_ref, v_ref):
    width = k_ref.shape[-1]
    kv = _mm(_rmsnorm(mem_ref[0], gain_ref[...]), wkv_ref[...])
    k_ref[0] = kv[:, :width].astype(k_ref.dtype)
    v_ref[0] = kv[:, width:].astype(v_ref.dtype)


def _memkv(mem, gain, w_kv):
    B, M, D = mem.shape
    width = w_kv.shape[1] // 2
    const = lambda b: (0, 0)
    return pl.pallas_call(
        _memkv_kernel,
        grid=(B,),
        in_specs=[pl.BlockSpec((1, M, D), lambda b: (b, 0, 0)),
                  pl.BlockSpec((1, D), const),
                  pl.BlockSpec(w_kv.shape, const)],
        out_specs=[pl.BlockSpec((1, M, width), lambda b: (b, 0, 0))] * 2,
        out_shape=[jax.ShapeDtypeStruct((B, M, width), MXU_DTYPE)] * 2,
        compiler_params=pltpu.CompilerParams(dimension_semantics=("arbitrary",),
                                             vmem_limit_bytes=VMEM_LIMIT_BYTES),
    )(mem, gain, w_kv)


def _mixer_kernel(x_ref, k_ref, v_ref, n1g_ref, win_ref, poolw_ref, pools_ref, lng_ref, lnb_ref,
                  ws_ref, bst_ref, sguwo_ref, xawo_ref, wout_ref, h_ref, tail_ref):
    ts, d = x_ref.shape[1], x_ref.shape[2]
    s_idx = pl.program_id(1)
    x = x_ref[0]
    nb = _rmsnorm(x, n1g_ref[...]).astype(MXU_DTYPE)

    def proj(col):
        return jnp.dot(nb, win_ref[:, col * d:(col + 1) * d], preferred_element_type=F32)

    @pl.when(s_idx == 0)
    def _():
        tail_ref[...] = jnp.zeros_like(tail_ref)

    p = proj(0)
    ext = jnp.concatenate([tail_ref[...], p], axis=0)
    tail_ref[...] = p[ts - POOL_HALO:, :]
    pos = s_idx * ts + lax.broadcasted_iota(jnp.int32, (ts, 1), 0)
    group = d // len(POOL_WINDOWS)
    y_pool = []
    for g, w in enumerate(POOL_WINDOWS):
        acc = ext[:, g * group:(g + 1) * group]
        shift = 1
        while shift < w:
            acc = acc + pltpu.roll(acc, shift, 0)
            shift *= 2
        count = jnp.minimum(pos + 1, w).astype(F32)
        diff = acc[POOL_HALO:, :] / count - p[:, g * group:(g + 1) * group]
        y_pool.append(_mm(diff, poolw_ref[g]))
    y_pool = jnp.concatenate(y_pool, axis=1) * pools_ref[...]

    u = _gelu(proj(1))
    v = _gelu(proj(2))
    mu = jnp.mean(v, axis=-1, keepdims=True)
    vc = v - mu
    var = jnp.mean(vc * vc, axis=-1, keepdims=True)
    v = (vc * lax.rsqrt(var + LN_EPS) * lng_ref[...] + lnb_ref[...]).astype(MXU_DTYPE)
    hd = d // SGU_HEADS
    causal = (lax.broadcasted_iota(jnp.int32, (SGU_CHUNK, SGU_CHUNK), 0)
              >= lax.broadcasted_iota(jnp.int32, (SGU_CHUNK, SGU_CHUNK), 1))
    mixed_rows = []
    w_masked = [jnp.where(causal, ws_ref[h], 0.0).astype(MXU_DTYPE) for h in range(SGU_HEADS)]
    for c in range(ts // SGU_CHUNK):
        rows = slice(c * SGU_CHUNK, (c + 1) * SGU_CHUNK)
        heads = []
        for h in range(SGU_HEADS):
            mixed = jnp.dot(w_masked[h], v[rows, h * hd:(h + 1) * hd], preferred_element_type=F32)
            heads.append(mixed + bst_ref[:, h:h + 1])
        mixed_rows.append(jnp.concatenate(heads, axis=1))
    mixed = jnp.concatenate(mixed_rows, axis=0) if len(mixed_rows) > 1 else mixed_rows[0]
    y_sgu = _mm(u * mixed, sguwo_ref[...])

    q = proj(3).astype(MXU_DTYPE)
    xd = d // XA_HEADS
    outs = []
    for h in range(XA_HEADS):
        cols = slice(h * xd, (h + 1) * xd)
        s = _mm_nt(q[:, cols], k_ref[0, :, cols]) * (xd ** -0.5)
        e = jnp.exp(s - jnp.max(s, axis=-1, keepdims=True))
        probs = e / jnp.sum(e, axis=-1, keepdims=True)
        outs.append(_mm(probs, v_ref[0, :, cols]))
    y_xa = _mm(jnp.concatenate(outs, axis=1), xawo_ref[...])

    merged = (jax.nn.sigmoid(proj(4)) * y_pool + jax.nn.sigmoid(proj(5)) * y_sgu
              + jax.nn.sigmoid(proj(6)) * y_xa)
    h_ref[0] = x + _mm(merged, wout_ref[...])


def _resident(shape):
    zeros = (0,) * len(shape)
    return pl.BlockSpec(shape, lambda *_: zeros, pipeline_mode=pl.Buffered(1))


def _mixer(x, k, v, n1g, w_in, pool_w, pool_scale, ln_g, ln_b, w_s, b_s_t, sgu_wo, xa_wo, w_out):
    B, S, D = x.shape
    ts = min(SEQ_TILE, S)
    M = k.shape[1]
    weights = (n1g, w_in, pool_w, pool_scale, ln_g, ln_b, w_s, b_s_t, sgu_wo, xa_wo, w_out)
    return pl.pallas_call(
        _mixer_kernel,
        grid=(B, S // ts),
        in_specs=[pl.BlockSpec((1, ts, D), lambda b, s: (b, s, 0)),
                  pl.BlockSpec((1, M, k.shape[2]), lambda b, s: (b, 0, 0)),
                  pl.BlockSpec((1, M, v.shape[2]), lambda b, s: (b, 0, 0))]
                 + [_resident(w.shape) for w in weights],
        out_specs=pl.BlockSpec((1, ts, D), lambda b, s: (b, s, 0)),
        out_shape=jax.ShapeDtypeStruct((B, S, D), F32),
        scratch_shapes=[pltpu.VMEM((POOL_HALO, D), F32)],
        compiler_params=pltpu.CompilerParams(dimension_semantics=("arbitrary", "arbitrary"),
                                             vmem_limit_bytes=VMEM_LIMIT_BYTES),
    )(x, k, v, *weights)


def _topk_rows(vals, payload=None):
    n_rows = vals.shape[0]
    row = lax.broadcasted_iota(jnp.int32, vals.shape, 0).astype(F32)
    out_v, out_i = [], []
    for _ in range(PEER_TOPK):
        m = jnp.max(vals, axis=0, keepdims=True)
        first = jnp.min(jnp.where(vals == m, row, float(n_rows)), axis=0, keepdims=True)
        sel = row == first
        out_v.append(m)
        if payload is None:
            out_i.append(first)
        else:
            out_i.append(jnp.max(jnp.where(sel, payload, -1.0), axis=0, keepdims=True))
        vals = jnp.where(sel, -jnp.inf, vals)
    return jnp.concatenate(out_v, axis=0), jnp.concatenate(out_i, axis=0)


def _candidates(v1, i1, v2, i2):
    sub = 8
    jrow = lax.broadcasted_iota(jnp.int32, (sub, 1), 0)
    vals = [v1[0:1] + v2]
    ids = [i1[0:1] * float(PEER_N_KEYS) + i2]
    for i in range(1, sub):
        keep = jrow < (PEER_TOPK // (i + 1))
        vals.append(jnp.where(keep, v1[i:i + 1] + v2[0:sub], -jnp.inf))
        ids.append(i1[i:i + 1] * float(PEER_N_KEYS) + i2[0:sub])
    vals.append(v1[sub:] + v2[0:1])
    ids.append(i1[sub:] * float(PEER_N_KEYS) + i2[0:1])
    return jnp.concatenate(vals, axis=0), jnp.concatenate(ids, axis=0)


def _select_kernel(h_ref, n2g_ref, wq_ref, k1_ref, k2_ref, xn_ref, exp_ref, gate_ref):
    xn = _rmsnorm(h_ref[...], n2g_ref[...])
    xn_ref[...] = xn
    q = _mm(xn, wq_ref[...])
    half = k1_ref.shape[1]
    experts, gates = [], []
    for h in range(PEER_HEADS):
        q1 = q[:, (2 * h) * half:(2 * h + 1) * half]
        q2 = q[:, (2 * h + 1) * half:(2 * h + 2) * half]
        v1, i1 = _topk_rows(_mm_nt(k1_ref[...], q1))
        v2, i2 = _topk_rows(_mm_nt(k2_ref[...], q2))
        best, eid = _topk_rows(*_candidates(v1, i1, v2, i2))
        e = jnp.exp(best - jnp.max(best, axis=0, keepdims=True))
        gates.append(e / jnp.sum(e, axis=0, keepdims=True))
        experts.append(eid)
    exp_ref[...] = jnp.concatenate(experts, axis=0).T.astype(jnp.int32)
    gate_ref[...] = jnp.concatenate(gates, axis=0).T


def _select(h, n2g, w_q, keys1, keys2):
    T, D = h.shape
    tt = min(SELECT_TILE, T)
    hk = PEER_HEADS * PEER_TOPK
    return pl.pallas_call(
        _select_kernel,
        grid=(T // tt,),
        in_specs=[pl.BlockSpec((tt, D), lambda i: (i, 0)),
                  _resident(n2g.shape), _resident(w_q.shape),
                  _resident(keys1.shape), _resident(keys2.shape)],
        out_specs=[pl.BlockSpec((tt, D), lambda i: (i, 0)),
                   pl.BlockSpec((tt, hk), lambda i: (i, 0)),
                   pl.BlockSpec((tt, hk), lambda i: (i, 0))],
        out_shape=[jax.ShapeDtypeStruct((T, D), F32),
                   jax.ShapeDtypeStruct((T, hk), jnp.int32),
                   jax.ShapeDtypeStruct((T, hk), F32)],
        compiler_params=pltpu.CompilerParams(dimension_semantics=("arbitrary",),
                                             vmem_limit_bytes=VMEM_LIMIT_BYTES),
    )(h, n2g, w_q, keys1, keys2)


def _expert_kernel(exp_ref, h_ref, xn_ref, gate_ref, fg_ref, u_hbm, v_hbm, out_ref, ubuf, vbuf, sem):
    tb = h_ref.shape[0]
    hk = gate_ref.shape[1]

    def row_copy(table, buf, which, slot, e, k):
        return pltpu.make_async_copy(table.at[pl.ds(e, 1)], buf.at[slot, pl.ds(k, 1)], sem.at[which, slot])

    def issue(t, slot):
        def body(k, carry):
            e = exp_ref[t, k]
            row_copy(u_hbm, ubuf, 0, slot, e, k).start()
            row_copy(v_hbm, vbuf, 1, slot, e, k).start()
            return carry
        lax.fori_loop(0, hk, body, 0)

    def wait(slot):
        pltpu.make_async_copy(u_hbm.at[pl.ds(0, hk)], ubuf.at[slot], sem.at[0, slot]).wait()
        pltpu.make_async_copy(v_hbm.at[pl.ds(0, hk)], vbuf.at[slot], sem.at[1, slot]).wait()

    eye = (lax.broadcasted_iota(jnp.int32, (hk, hk), 0) == lax.broadcasted_iota(jnp.int32, (hk, hk), 1))
    issue(0, 0)

    def token(t, carry):
        slot = t % 2

        @pl.when(t + 1 < tb)
        def _():
            issue(t + 1, 1 - slot)

        wait(slot)
        x = xn_ref[pl.ds(t, 1), :]
        dots = jnp.sum(ubuf[slot] * x, axis=1, keepdims=True)
        gate_col = jnp.sum(jnp.where(eye, gate_ref[pl.ds(t, 1), :], 0.0), axis=1, keepdims=True)
        y = jnp.sum(vbuf[slot] * (_gelu(dots) * gate_col), axis=0, keepdims=True)
        out_ref[pl.ds(t, 1), :] = _rmsnorm(h_ref[pl.ds(t, 1), :] + y, fg_ref[...])
        return carry

    lax.fori_loop(0, tb, token, 0)


def _experts(experts, h, xn, gates, final_gain, peer_u, peer_v):
    T, D = h.shape
    hk = experts.shape[1]
    tb = min(EXPERT_TILE, T)
    return pl.pallas_call(
        _expert_kernel,
        grid=(T // tb,),
        in_specs=[pl.BlockSpec((tb, hk), lambda i: (i, 0), memory_space=pltpu.SMEM),
                  pl.BlockSpec((tb, D), lambda i: (i, 0)),
                  pl.BlockSpec((tb, D), lambda i: (i, 0)),
                  pl.BlockSpec((tb, hk), lambda i: (i, 0)),
                  pl.BlockSpec((1, D), lambda i: (0, 0)),
                  pl.BlockSpec(memory_space=pl.ANY),
                  pl.BlockSpec(memory_space=pl.ANY)],
        out_specs=pl.BlockSpec((tb, D), lambda i: (i, 0)),
        out_shape=jax.ShapeDtypeStruct((T, D), F32),
        scratch_shapes=[pltpu.VMEM((2, hk, D), F32), pltpu.VMEM((2, hk, D), F32),
                        pltpu.SemaphoreType.DMA((2, 2))],
        compiler_params=pltpu.CompilerParams(dimension_semantics=("arbitrary",),
                                             vmem_limit_bytes=VMEM_LIMIT_BYTES),
    )(experts, h, xn, gates, final_gain, peer_u, peer_v)


def kernel(x, mem, norm1_gain, w_in, pool_w, pool_scale, sgu_ln_gain, sgu_ln_bias, sgu_w_s, sgu_b_s,
           sgu_w_out, mem_norm_gain, xa_w_kv, xa_w_out, w_out, norm2_gain, peer_w_q, peer_keys1,
           peer_keys2, peer_u, peer_v, final_norm_gain):
    B, S, D = x.shape
    depth = w_in.shape[0]
    lo = lambda w: w.astype(MXU_DTYPE)
    row = lambda w: w.reshape(1, -1)
    h = x
    for l in range(depth):
        k, v = _memkv(mem, row(mem_norm_gain[l]), lo(xa_w_kv[l]))
        h = _mixer(h, k, v, row(norm1_gain[l]), lo(w_in[l]), lo(pool_w[l]), row(pool_scale[l]),
                   row(sgu_ln_gain[l]), row(sgu_ln_bias[l]), sgu_w_s[l], sgu_b_s[l].T,
                   lo(sgu_w_out[l]), lo(xa_w_out[l]), lo(w_out[l]))
        ht = h.reshape(B * S, D)
        xn, experts, gates = _select(ht, row(norm2_gain[l]), lo(peer_w_q[l]), lo(peer_keys1[l]),
                                     lo(peer_keys2[l]))
        assert l == depth - 1, "fused final norm assumes the expert stage is the last layer op"
        out = _experts(experts, ht, xn, gates, row(final_norm_gain), peer_u[l], peer_v[l])
    return out.reshape(B, S, D)
```

```python
import functools
import math

import jax
import jax.numpy as jnp
from jax import lax
from jax.experimental import pallas as pl
from jax.experimental.pallas import tpu as pltpu
from jax.experimental.pallas import tpu_sc as plsc

F32 = jnp.float32
I32 = jnp.int32
MXU_DTYPE = jnp.bfloat16

RMS_EPS = 1e-6
LN_EPS = 1e-5
POOL_WINDOWS = (2, 4, 8, 16)
POOL_HALO = 16
SGU_CHUNK = 128
SGU_HEADS = 8
XA_HEADS = 4
PEER_HEADS = 8
PEER_N_KEYS = 128
PEER_TOPK = 16

V7X_VMEM_BYTES = 64 * 1024 * 1024
VMEM_LIMIT_BYTES = 56 * 1024 * 1024

SEQ_TILE = 256
SELECT_TILE = 256
ROW_TILE = 512

SC_CORES = 2
SC_SUBCORES = 16
SC_LANES = 16
SC_WORKERS = SC_CORES * SC_SUBCORES
SC_TOKENS = 8
SC_CHUNK = 32


def _rmsnorm(x, gain):
    return x * lax.rsqrt(jnp.mean(x * x, axis=-1, keepdims=True) + RMS_EPS) * gain


def _gelu(x):
    return 0.5 * x * (1.0 + lax.erf(x * (1.0 / math.sqrt(2.0))))


def _mm(a, b):
    return jnp.dot(a.astype(MXU_DTYPE), b.astype(MXU_DTYPE), preferred_element_type=F32)


def _mm_nt(a, b):
    return lax.dot_general(a.astype(MXU_DTYPE), b.astype(MXU_DTYPE),
                           (((1,), (1,)), ((), ())), preferred_element_type=F32)


def _memkv_kernel(mem_ref, gain_ref, wkv_ref, k_ref, v_ref):
    width = k_ref.shape[-1]
    kv = _mm(_rmsnorm(mem_ref[0], gain_ref[...]), wkv_ref[...])
    k_ref[0] = kv[:, :width].astype(k_ref.dtype)
    v_ref[0] = kv[:, width:].astype(v_ref.dtype)


def _memkv(mem, gain, w_kv):
    B, M, D = mem.shape
    width = w_kv.shape[1] // 2
    const = lambda b: (0, 0)
    return pl.pallas_call(
        _memkv_kernel,
        grid=(B,),
        in_specs=[pl.BlockSpec((1, M, D), lambda b: (b, 0, 0)),
                  pl.BlockSpec((1, D), const),
                  pl.BlockSpec(w_kv.shape, const)],
        out_specs=[pl.BlockSpec((1, M, width), lambda b: (b, 0, 0))] * 2,
        out_shape=[jax.ShapeDtypeStruct((B, M, width), MXU_DTYPE)] * 2,
        compiler_params=pltpu.CompilerParams(dimension_semantics=("arbitrary",),
                                             vmem_limit_bytes=VMEM_LIMIT_BYTES),
    )(mem, gain, w_kv)


def _mixer_kernel(x_ref, k_ref, v_ref, n1g_ref, win_ref, poolw_ref, pools_ref, lng_ref, lnb_ref,
                  ws_ref, bst_ref, sguwo_ref, xawo_ref, wout_ref, h_ref, tail_ref):
    ts, d = x_ref.shape[1], x_ref.shape[2]
    s_idx = pl.program_id(1)
    x = x_ref[0]
    nb = _rmsnorm(x, n1g_ref[...]).astype(MXU_DTYPE)

    def proj(col):
        return jnp.dot(nb, win_ref[:, col * d:(col + 1) * d], preferred_element_type=F32)

    @pl.when(s_idx == 0)
    def _():
        tail_ref[...] = jnp.zeros_like(tail_ref)

    p = proj(0)
    ext = jnp.concatenate([tail_ref[...], p], axis=0)
    tail_ref[...] = p[ts - POOL_HALO:, :]
    pos = s_idx * ts + lax.broadcasted_iota(jnp.int32, (ts, 1), 0)
    group = d // len(POOL_WINDOWS)
    y_pool = []
    for g, w in enumerate(POOL_WINDOWS):
        acc = ext[:, g * group:(g + 1) * group]
        shift = 1
        while shift < w:
            acc = acc + pltpu.roll(acc, shift, 0)
            shift *= 2
        count = jnp.minimum(pos + 1, w).astype(F32)
        diff = acc[POOL_HALO:, :] / count - p[:, g * group:(g + 1) * group]
        y_pool.append(_mm(diff, poolw_ref[g]))
    y_pool = jnp.concatenate(y_pool, axis=1) * pools_ref[...]

    u = _gelu(proj(1))
    v = _gelu(proj(2))
    mu = jnp.mean(v, axis=-1, keepdims=True)
    vc = v - mu
    var = jnp.mean(vc * vc, axis=-1, keepdims=True)
    v = (vc * lax.rsqrt(var + LN_EPS) * lng_ref[...] + lnb_ref[...]).astype(MXU_DTYPE)
    hd = d // SGU_HEADS
    causal = (lax.broadcasted_iota(jnp.int32, (SGU_CHUNK, SGU_CHUNK), 0)
              >= lax.broadcasted_iota(jnp.int32, (SGU_CHUNK, SGU_CHUNK), 1))
    mixed_rows = []
    w_masked = [jnp.where(causal, ws_ref[h], 0.0).astype(MXU_DTYPE) for h in range(SGU_HEADS)]
    for c in range(ts // SGU_CHUNK):
        rows = slice(c * SGU_CHUNK, (c + 1) * SGU_CHUNK)
        heads = []
        for h in range(SGU_HEADS):
            mixed = jnp.dot(w_masked[h], v[rows, h * hd:(h + 1) * hd], preferred_element_type=F32)
            heads.append(mixed + bst_ref[:, h:h + 1])
        mixed_rows.append(jnp.concatenate(heads, axis=1))
    mixed = jnp.concatenate(mixed_rows, axis=0) if len(mixed_rows) > 1 else mixed_rows[0]
    y_sgu = _mm(u * mixed, sguwo_ref[...])

    q = proj(3).astype(MXU_DTYPE)
    xd = d // XA_HEADS
    outs = []
    for h in range(XA_HEADS):
        cols = slice(h * xd, (h + 1) * xd)
        s = _mm_nt(q[:, cols], k_ref[0, :, cols]) * (xd ** -0.5)
        e = jnp.exp(s - jnp.max(s, axis=-1, keepdims=True))
        probs = e / jnp.sum(e, axis=-1, keepdims=True)
        outs.append(_mm(probs, v_ref[0, :, cols]))
    y_xa = _mm(jnp.concatenate(outs, axis=1), xawo_ref[...])

    merged = (jax.nn.sigmoid(proj(4)) * y_pool + jax.nn.sigmoid(proj(5)) * y_sgu
              + jax.nn.sigmoid(proj(6)) * y_xa)
    h_ref[0] = x + _mm(merged, wout_ref[...])


def _resident(shape):
    zeros = (0,) * len(shape)
    return pl.BlockSpec(shape, lambda *_: zeros, pipeline_mode=pl.Buffered(1))


def _mixer(x, k, v, n1g, w_in, pool_w, pool_scale, ln_g, ln_b, w_s, b_s_t, sgu_wo, xa_wo, w_out):
    B, S, D = x.shape
    ts = min(SEQ_TILE, S)
    M = k.shape[1]
    weights = (n1g, w_in, pool_w, pool_scale, ln_g, ln_b, w_s, b_s_t, sgu_wo, xa_wo, w_out)
    return pl.pallas_call(
        _mixer_kernel,
        grid=(B, S // ts),
        in_specs=[pl.BlockSpec((1, ts, D), lambda b, s: (b, s, 0)),
                  pl.BlockSpec((1, M, k.shape[2]), lambda b, s: (b, 0, 0)),
                  pl.BlockSpec((1, M, v.shape[2]), lambda b, s: (b, 0, 0))]
                 + [_resident(w.shape) for w in weights],
        out_specs=pl.BlockSpec((1, ts, D), lambda b, s: (b, s, 0)),
        out_shape=jax.ShapeDtypeStruct((B, S, D), F32),
        scratch_shapes=[pltpu.VMEM((POOL_HALO, D), F32)],
        compiler_params=pltpu.CompilerParams(dimension_semantics=("arbitrary", "arbitrary"),
                                             vmem_limit_bytes=VMEM_LIMIT_BYTES),
    )(x, k, v, *weights)


def _topk_rows(vals, payload=None):
    n_rows = vals.shape[0]
    row = lax.broadcasted_iota(jnp.int32, vals.shape, 0).astype(F32)
    out_v, out_i = [], []
    for _ in range(PEER_TOPK):
        m = jnp.max(vals, axis=0, keepdims=True)
        first = jnp.min(jnp.where(vals == m, row, float(n_rows)), axis=0, keepdims=True)
        sel = row == first
        out_v.append(m)
        if payload is None:
            out_i.append(first)
        else:
            out_i.append(jnp.max(jnp.where(sel, payload, -1.0), axis=0, keepdims=True))
        vals = jnp.where(sel, -jnp.inf, vals)
    return jnp.concatenate(out_v, axis=0), jnp.concatenate(out_i, axis=0)


def _candidates(v1, i1, v2, i2):
    sub = 8
    jrow = lax.broadcasted_iota(jnp.int32, (sub, 1), 0)
    vals = [v1[0:1] + v2]
    ids = [i1[0:1] * float(PEER_N_KEYS) + i2]
    for i in range(1, sub):
        keep = jrow < (PEER_TOPK // (i + 1))
        vals.append(jnp.where(keep, v1[i:i + 1] + v2[0:sub], -jnp.inf))
        ids.append(i1[i:i + 1] * float(PEER_N_KEYS) + i2[0:sub])
    vals.append(v1[sub:] + v2[0:1])
    ids.append(i1[sub:] * float(PEER_N_KEYS) + i2[0:1])
    return jnp.concatenate(vals, axis=0), jnp.concatenate(ids, axis=0)


def _select_kernel(h_ref, n2g_ref, wq_ref, k1_ref, k2_ref, xn_ref, exp_ref, gate_ref):
    xn = _rmsnorm(h_ref[...], n2g_ref[...])
    xn_ref[...] = xn
    q = _mm(xn, wq_ref[...])
    half = k1_ref.shape[1]
    experts, gates = [], []
    for h in range(PEER_HEADS):
        q1 = q[:, (2 * h) * half:(2 * h + 1) * half]
        q2 = q[:, (2 * h + 1) * half:(2 * h + 2) * half]
        v1, i1 = _topk_rows(_mm_nt(k1_ref[...], q1))
        v2, i2 = _topk_rows(_mm_nt(k2_ref[...], q2))
        best, eid = _topk_rows(*_candidates(v1, i1, v2, i2))
        e = jnp.exp(best - jnp.max(best, axis=0, keepdims=True))
        gates.append(e / jnp.sum(e, axis=0, keepdims=True))
        experts.append(eid)
    exp_ref[...] = jnp.concatenate(experts, axis=0).T.astype(jnp.int32)
    gate_ref[...] = jnp.concatenate(gates, axis=0).T


def _select(h, n2g, w_q, keys1, keys2):
    T, D = h.shape
    tt = min(SELECT_TILE, T)
    hk = PEER_HEADS * PEER_TOPK
    return pl.pallas_call(
        _select_kernel,
        grid=(T // tt,),
        in_specs=[pl.BlockSpec((tt, D), lambda i: (i, 0)),
                  _resident(n2g.shape), _resident(w_q.shape),
                  _resident(keys1.shape), _resident(keys2.shape)],
        out_specs=[pl.BlockSpec((tt, D), lambda i: (i, 0)),
                   pl.BlockSpec((tt, hk), lambda i: (i, 0)),
                   pl.BlockSpec((tt, hk), lambda i: (i, 0))],
        out_shape=[jax.ShapeDtypeStruct((T, D), F32),
                   jax.ShapeDtypeStruct((T, hk), jnp.int32),
                   jax.ShapeDtypeStruct((T, hk), F32)],
        compiler_params=pltpu.CompilerParams(dimension_semantics=("arbitrary",),
                                             vmem_limit_bytes=VMEM_LIMIT_BYTES),
    )(h, n2g, w_q, keys1, keys2)


def _pack_table(table):
    half = table.shape[1] // 2
    bits = lax.bitcast_convert_type(table.astype(jnp.bfloat16), jnp.uint16).astype(jnp.uint32)
    return lax.bitcast_convert_type((bits[:, half:] << 16) | bits[:, :half], I32)


def _unpack_words(w):
    lo = lax.bitcast_convert_type(lax.shift_left(w, jnp.full(w.shape, 16, I32)), F32)
    hi = lax.bitcast_convert_type(lax.bitwise_and(w, jnp.full(w.shape, -65536, I32)), F32)
    return lo, hi


def _sc_kernel(body, out_type, scratch_types):
    mesh = plsc.VectorSubcoreMesh(core_axis_name="c", subcore_axis_name="s")
    return pl.kernel(body, out_type=out_type, mesh=mesh, scratch_types=scratch_types,
                     compiler_params=pltpu.CompilerParams(needs_layout_passes=False))


def _sc_worker_base(tokens_per_worker):
    return (lax.axis_index("s") * SC_CORES + lax.axis_index("c")) * tokens_per_worker


def _sc_pipeline(nchunk, gather, compute):
    gather(0, 0).start()

    @pl.loop(0, nchunk, step=2)
    def _(ci):
        gather(ci + 1, 1).start()
        gather(ci, 0).wait()
        compute(ci, 0)

        @pl.when(ci + 2 < nchunk)
        def _():
            gather(ci + 2, 0).start()

        gather(ci + 1, 1).wait()
        compute(ci + 1, 1)


def _sc_dots(xn, experts, ptab):
    T, D = xn.shape
    HK = experts.shape[1]
    W = ptab.shape[1]
    L, TG, CH = SC_LANES, SC_TOKENS, SC_CHUNK
    tpw = T // SC_WORKERS
    cpt = HK // CH
    nchunk = TG * cpt
    DG = 8
    assert W * 2 == D and T % (SC_WORKERS * TG) == 0 and HK % CH == 0 and CH % L == 0
    assert W % (DG * L) == 0 and nchunk % 2 == 0

    def body(x_hbm, idx_hbm, tab_hbm, out_hbm, x_v, idx_v, out_v, rows_v, acc_v, sem):
        base = _sc_worker_base(tpw)
        lane = lax.iota(I32, L)

        def gather(ci, slot):
            tok, c = ci // cpt, ci % cpt
            return pltpu.make_async_copy(tab_hbm.at[idx_v.at[tok, pl.ds(c * CH, CH)]],
                                         rows_v.at[slot], sem.at[slot])

        def compute(ci, slot):
            tok, c = ci // cpt, ci % cpt
            for eg in range(CH // L):
                @pl.loop(0, W // (DG * L))
                def _(dg):
                    off = dg * (DG * L)
                    cont = jnp.full((L,), dg, I32) != 0
                    zero = jnp.zeros((L,), F32)
                    accs = [jnp.where(cont, acc_v[kk, :], zero) for kk in range(L)]
                    for j in range(DG):
                        xlo = x_v[tok, pl.ds(off + j * L, L)]
                        xhi = x_v[tok, pl.ds(W + off + j * L, L)]
                        for kk in range(L):
                            lo, hi = _unpack_words(rows_v[slot, eg * L + kk, pl.ds(off + j * L, L)])
                            accs[kk] = accs[kk] + lo * xlo + hi * xhi
                    for kk in range(L):
                        acc_v[kk, :] = accs[kk]

                s = jnp.zeros((L,), F32)
                for j in range(L):
                    s = s + plsc.load_gather(acc_v, [lane, jnp.full((L,), j, I32)])
                out_v[tok, pl.ds(c * CH + eg * L, L)] = s

        @pl.loop(0, tpw // TG)
        def _(g):
            t0 = base + g * TG
            pltpu.sync_copy(x_hbm.at[pl.ds(t0, TG)], x_v)
            pltpu.sync_copy(idx_hbm.at[pl.ds(t0, TG)], idx_v)
            _sc_pipeline(nchunk, gather, compute)
            pltpu.sync_copy(out_v, out_hbm.at[pl.ds(t0, TG)])

    return _sc_kernel(
        body, jax.ShapeDtypeStruct((T, HK), F32),
        [pltpu.VMEM((TG, D), F32), pltpu.VMEM((TG, HK), I32), pltpu.VMEM((TG, HK), F32),
         pltpu.VMEM((2, CH, W), I32), pltpu.VMEM((L, L), F32), pltpu.SemaphoreType.DMA((2,))],
    )(xn, experts, ptab)


def _sc_wsum(w, experts, ptab):
    T, HK = w.shape
    W = ptab.shape[1]
    D = 2 * W
    L, TG, CH = SC_LANES, SC_TOKENS, SC_CHUNK
    tpw = T // SC_WORKERS
    cpt = HK // CH
    nchunk = TG * cpt
    DG = 8
    assert T % (SC_WORKERS * TG) == 0 and HK % CH == 0 and W % (DG * L) == 0 and nchunk % 2 == 0

    def body(w_hbm, idx_hbm, tab_hbm, out_hbm, w_v, idx_v, y_v, rows_v, sem):
        base = _sc_worker_base(tpw)

        def gather(ci, slot):
            tok, c = ci // cpt, ci % cpt
            return pltpu.make_async_copy(tab_hbm.at[idx_v.at[tok, pl.ds(c * CH, CH)]],
                                         rows_v.at[slot], sem.at[slot])

        def compute(ci, slot):
            tok, c = ci // cpt, ci % cpt
            tokv = jnp.full((L,), tok, I32)
            cont = jnp.full((L,), c, I32) != 0

            @pl.loop(0, W // (DG * L))
            def _(dg):
                off = dg * (DG * L)
                zero = jnp.zeros((L,), F32)
                alo = [jnp.where(cont, y_v[tok, pl.ds(off + j * L, L)], zero) for j in range(DG)]
                ahi = [jnp.where(cont, y_v[tok, pl.ds(W + off + j * L, L)], zero) for j in range(DG)]
                for kk in range(CH):
                    wk = plsc.load_gather(w_v, [tokv, jnp.full((L,), c * CH + kk, I32)])
                    for j in range(DG):
                        lo, hi = _unpack_words(rows_v[slot, kk, pl.ds(off + j * L, L)])
                        alo[j] = alo[j] + wk * lo
                        ahi[j] = ahi[j] + wk * hi
                for j in range(DG):
                    y_v[tok, pl.ds(off + j * L, L)] = alo[j]
                    y_v[tok, pl.ds(W + off + j * L, L)] = ahi[j]

        @pl.loop(0, tpw // TG)
        def _(g):
            t0 = base + g * TG
            pltpu.sync_copy(w_hbm.at[pl.ds(t0, TG)], w_v)
            pltpu.sync_copy(idx_hbm.at[pl.ds(t0, TG)], idx_v)
            _sc_pipeline(nchunk, gather, compute)
            pltpu.sync_copy(y_v, out_hbm.at[pl.ds(t0, TG)])

    return _sc_kernel(
        body, jax.ShapeDtypeStruct((T, D), F32),
        [pltpu.VMEM((TG, HK), F32), pltpu.VMEM((TG, HK), I32), pltpu.VMEM((TG, D), F32),
         pltpu.VMEM((2, CH, W), I32), pltpu.SemaphoreType.DMA((2,))],
    )(w, experts, ptab)


def _act_kernel(dots_ref, gate_ref, w_ref):
    w_ref[...] = _gelu(dots_ref[...]) * gate_ref[...]


def _act(dots, gates):
    T, hk = dots.shape
    tt = min(ROW_TILE, T)
    spec = pl.BlockSpec((tt, hk), lambda i: (i, 0))
    return pl.pallas_call(
        _act_kernel, grid=(T // tt,), in_specs=[spec, spec], out_specs=spec,
        out_shape=jax.ShapeDtypeStruct((T, hk), F32),
        compiler_params=pltpu.CompilerParams(dimension_semantics=("arbitrary",)),
    )(dots, gates)


def _final_kernel(h_ref, y_ref, g_ref, o_ref):
    o_ref[...] = _rmsnorm(h_ref[...] + y_ref[...], g_ref[...])


def _final(h, y, gain):
    T, D = h.shape
    tt = min(ROW_TILE, T)
    spec = pl.BlockSpec((tt, D), lambda i: (i, 0))
    return pl.pallas_call(
        _final_kernel, grid=(T // tt,),
        in_specs=[spec, spec, pl.BlockSpec((1, D), lambda i: (0, 0))], out_specs=spec,
        out_shape=jax.ShapeDtypeStruct((T, D), F32),
        compiler_params=pltpu.CompilerParams(dimension_semantics=("arbitrary",)),
    )(h, y, gain)


def kernel(x, mem, norm1_gain, w_in, pool_w, pool_scale, sgu_ln_gain, sgu_ln_bias, sgu_w_s, sgu_b_s,
           sgu_w_out, mem_norm_gain, xa_w_kv, xa_w_out, w_out, norm2_gain, peer_w_q, peer_keys1,
           peer_keys2, peer_u, peer_v, final_norm_gain):
    B, S, D = x.shape
    depth = w_in.shape[0]
    lo = lambda w: w.astype(MXU_DTYPE)
    row = lambda w: w.reshape(1, -1)
    h = x
    for l in range(depth):
        k, v = _memkv(mem, row(mem_norm_gain[l]), lo(xa_w_kv[l]))
        h = _mixer(h, k, v, row(norm1_gain[l]), lo(w_in[l]), lo(pool_w[l]), row(pool_scale[l]),
                   row(sgu_ln_gain[l]), row(sgu_ln_bias[l]), sgu_w_s[l], sgu_b_s[l].T,
                   lo(sgu_w_out[l]), lo(xa_w_out[l]), lo(w_out[l]))
        ht = h.reshape(B * S, D)
        xn, experts, gates = _select(ht, row(norm2_gain[l]), lo(peer_w_q[l]), lo(peer_keys1[l]),
                                     lo(peer_keys2[l]))
        assert l == depth - 1, "the final RMSNorm is fused with the last layer's residual add"
        dots = _sc_dots(xn, experts, _pack_table(peer_u[l]))
        y = _sc_wsum(_act(dots, gates), experts, _pack_table(peer_v[l]))
        out = _final(ht, y, row(final_norm_gain))
    return out.reshape(B, S, D)
```

```python
import functools
import math

import jax
import jax.numpy as jnp
from jax import lax
from jax.experimental import pallas as pl
from jax.experimental.pallas import tpu as pltpu
from jax.experimental.pallas import tpu_sc as plsc

F32 = jnp.float32
I32 = jnp.int32
MXU_DTYPE = jnp.bfloat16

RMS_EPS = 1e-6
LN_EPS = 1e-5
POOL_WINDOWS = (2, 4, 8, 16)
POOL_HALO = 16
SGU_CHUNK = 128
SGU_HEADS = 8
XA_HEADS = 4
PEER_HEADS = 8
PEER_N_KEYS = 128
PEER_TOPK = 16

V7X_VMEM_BYTES = 64 * 1024 * 1024
VMEM_LIMIT_BYTES = 56 * 1024 * 1024

SEQ_TILE = 256
SELECT_TILE = 256
ROW_TILE = 512
BATCH_CHUNKS = 4

SC_CORES = 2
SC_SUBCORES = 16
SC_LANES = 16
SC_WORKERS = SC_CORES * SC_SUBCORES
SC_TOKENS = 8
SC_CHUNK = 32


def _rmsnorm(x, gain):
    return x * lax.rsqrt(jnp.mean(x * x, axis=-1, keepdims=True) + RMS_EPS) * gain


def _gelu(x):
    return 0.5 * x * (1.0 + lax.erf(x * (1.0 / math.sqrt(2.0))))


def _mm(a, b):
    return jnp.dot(a.astype(MXU_DTYPE), b.astype(MXU_DTYPE), preferred_element_type=F32)


def _mm_nt(a, b):
    return lax.dot_general(a.astype(MXU_DTYPE), b.astype(MXU_DTYPE),
                           (((1,), (1,)), ((), ())), preferred_element_type=F32)


def _memkv_kernel(mem_ref, gain_ref, wkv_ref, k_ref, v_ref):
    width = k_ref.shape[-1]
    kv = _mm(_rmsnorm(mem_ref[0], gain_ref[...]), wkv_ref[...])
    k_ref[0] = kv[:, :width].astype(k_ref.dtype)
    v_ref[0] = kv[:, width:].astype(v_ref.dtype)


def _memkv(mem, gain, w_kv):
    B, M, D = mem.shape
    width = w_kv.shape[1] // 2
    const = lambda b: (0, 0)
    return pl.pallas_call(
        _memkv_kernel,
        grid=(B,),
        in_specs=[pl.BlockSpec((1, M, D), lambda b: (b, 0, 0)),
                  pl.BlockSpec((1, D), const),
                  pl.BlockSpec(w_kv.shape, const)],
        out_specs=[pl.BlockSpec((1, M, width), lambda b: (b, 0, 0))] * 2,
        out_shape=[jax.ShapeDtypeStruct((B, M, width), MXU_DTYPE)] * 2,
        compiler_params=pltpu.CompilerParams(dimension_semantics=("arbitrary",),
                                             vmem_limit_bytes=VMEM_LIMIT_BYTES),
    )(mem, gain, w_kv)


def _mixer_kernel(x_ref, k_ref, v_ref, n1g_ref, win_ref, poolw_ref, pools_ref, lng_ref, lnb_ref,
                  ws_ref, bst_ref, sguwo_ref, xawo_ref, wout_ref, h_ref, tail_ref):
    ts, d = x_ref.shape[1], x_ref.shape[2]
    s_idx = pl.program_id(1)
    x = x_ref[0]
    nb = _rmsnorm(x, n1g_ref[...]).astype(MXU_DTYPE)

    def proj(col):
        return jnp.dot(nb, win_ref[:, col * d:(col + 1) * d], preferred_element_type=F32)

    @pl.when(s_idx == 0)
    def _():
        tail_ref[...] = jnp.zeros_like(tail_ref)

    p = proj(0)
    ext = jnp.concatenate([tail_ref[...], p], axis=0)
    tail_ref[...] = p[ts - POOL_HALO:, :]
    pos = s_idx * ts + lax.broadcasted_iota(jnp.int32, (ts, 1), 0)
    group = d // len(POOL_WINDOWS)
    y_pool = []
    for g, w in enumerate(POOL_WINDOWS):
        acc = ext[:, g * group:(g + 1) * group]
        shift = 1
        while shift < w:
            acc = acc + pltpu.roll(acc, shift, 0)
            shift *= 2
        count = jnp.minimum(pos + 1, w).astype(F32)
        diff = acc[POOL_HALO:, :] / count - p[:, g * group:(g + 1) * group]
        y_pool.append(_mm(diff, poolw_ref[g]))
    y_pool = jnp.concatenate(y_pool, axis=1) * pools_ref[...]

    u = _gelu(proj(1))
    v = _gelu(proj(2))
    mu = jnp.mean(v, axis=-1, keepdims=True)
    vc = v - mu
    var = jnp.mean(vc * vc, axis=-1, keepdims=True)
    v = (vc * lax.rsqrt(var + LN_EPS) * lng_ref[...] + lnb_ref[...]).astype(MXU_DTYPE)
    hd = d // SGU_HEADS
    causal = (lax.broadcasted_iota(jnp.int32, (SGU_CHUNK, SGU_CHUNK), 0)
              >= lax.broadcasted_iota(jnp.int32, (SGU_CHUNK, SGU_CHUNK), 1))
    mixed_rows = []
    w_masked = [jnp.where(causal, ws_ref[h], 0.0).astype(MXU_DTYPE) for h in range(SGU_HEADS)]
    for c in range(ts // SGU_CHUNK):
        rows = slice(c * SGU_CHUNK, (c + 1) * SGU_CHUNK)
        heads = []
        for h in range(SGU_HEADS):
            mixed = jnp.dot(w_masked[h], v[rows, h * hd:(h + 1) * hd], preferred_element_type=F32)
            heads.append(mixed + bst_ref[:, h:h + 1])
        mixed_rows.append(jnp.concatenate(heads, axis=1))
    mixed = jnp.concatenate(mixed_rows, axis=0) if len(mixed_rows) > 1 else mixed_rows[0]
    y_sgu = _mm(u * mixed, sguwo_ref[...])

    q = proj(3).astype(MXU_DTYPE)
    xd = d // XA_HEADS
    outs = []
    for h in range(XA_HEADS):
        cols = slice(h * xd, (h + 1) * xd)
        s = _mm_nt(q[:, cols], k_ref[0, :, cols]) * (xd ** -0.5)
        e = jnp.exp(s - jnp.max(s, axis=-1, keepdims=True))
        probs = e / jnp.sum(e, axis=-1, keepdims=True)
        outs.append(_mm(probs, v_ref[0, :, cols]))
    y_xa = _mm(jnp.concatenate(outs, axis=1), xawo_ref[...])

    merged = (jax.nn.sigmoid(proj(4)) * y_pool + jax.nn.sigmoid(proj(5)) * y_sgu
              + jax.nn.sigmoid(proj(6)) * y_xa)
    h_ref[0] = x + _mm(merged, wout_ref[...])


def _resident(shape):
    zeros = (0,) * len(shape)
    return pl.BlockSpec(shape, lambda *_: zeros, pipeline_mode=pl.Buffered(1))


def _mixer(x, k, v, n1g, w_in, pool_w, pool_scale, ln_g, ln_b, w_s, b_s_t, sgu_wo, xa_wo, w_out):
    B, S, D = x.shape
    ts = min(SEQ_TILE, S)
    M = k.shape[1]
    weights = (n1g, w_in, pool_w, pool_scale, ln_g, ln_b, w_s, b_s_t, sgu_wo, xa_wo, w_out)
    return pl.pallas_call(
        _mixer_kernel,
        grid=(B, S // ts),
        in_specs=[pl.BlockSpec((1, ts, D), lambda b, s: (b, s, 0)),
                  pl.BlockSpec((1, M, k.shape[2]), lambda b, s: (b, 0, 0)),
                  pl.BlockSpec((1, M, v.shape[2]), lambda b, s: (b, 0, 0))]
                 + [_resident(w.shape) for w in weights],
        out_specs=pl.BlockSpec((1, ts, D), lambda b, s: (b, s, 0)),
        out_shape=jax.ShapeDtypeStruct((B, S, D), F32),
        scratch_shapes=[pltpu.VMEM((POOL_HALO, D), F32)],
        compiler_params=pltpu.CompilerParams(dimension_semantics=("arbitrary", "arbitrary"),
                                             vmem_limit_bytes=VMEM_LIMIT_BYTES),
    )(x, k, v, *weights)


def _topk_rows(vals, payload=None):
    n_rows = vals.shape[0]
    row = lax.broadcasted_iota(jnp.int32, vals.shape, 0).astype(F32)
    out_v, out_i = [], []
    for _ in range(PEER_TOPK):
        m = jnp.max(vals, axis=0, keepdims=True)
        first = jnp.min(jnp.where(vals == m, row, float(n_rows)), axis=0, keepdims=True)
        sel = row == first
        out_v.append(m)
        if payload is None:
            out_i.append(first)
        else:
            out_i.append(jnp.max(jnp.where(sel, payload, -1.0), axis=0, keepdims=True))
        vals = jnp.where(sel, -jnp.inf, vals)
    return jnp.concatenate(out_v, axis=0), jnp.concatenate(out_i, axis=0)


def _candidates(v1, i1, v2, i2):
    sub = 8
    jrow = lax.broadcasted_iota(jnp.int32, (sub, 1), 0)
    vals = [v1[0:1] + v2]
    ids = [i1[0:1] * float(PEER_N_KEYS) + i2]
    for i in range(1, sub):
        keep = jrow < (PEER_TOPK // (i + 1))
        vals.append(jnp.where(keep, v1[i:i + 1] + v2[0:sub], -jnp.inf))
        ids.append(i1[i:i + 1] * float(PEER_N_KEYS) + i2[0:sub])
    vals.append(v1[sub:] + v2[0:1])
    ids.append(i1[sub:] * float(PEER_N_KEYS) + i2[0:1])
    return jnp.concatenate(vals, axis=0), jnp.concatenate(ids, axis=0)


def _select_kernel(h_ref, n2g_ref, wq_ref, k1_ref, k2_ref, xn_ref, exp_ref, gate_ref):
    xn = _rmsnorm(h_ref[...], n2g_ref[...])
    xn_ref[...] = xn
    q = _mm(xn, wq_ref[...])
    half = k1_ref.shape[1]
    experts, gates = [], []
    for h in range(PEER_HEADS):
        q1 = q[:, (2 * h) * half:(2 * h + 1) * half]
        q2 = q[:, (2 * h + 1) * half:(2 * h + 2) * half]
        v1, i1 = _topk_rows(_mm_nt(k1_ref[...], q1))
        v2, i2 = _topk_rows(_mm_nt(k2_ref[...], q2))
        best, eid = _topk_rows(*_candidates(v1, i1, v2, i2))
        e = jnp.exp(best - jnp.max(best, axis=0, keepdims=True))
        gates.append(e / jnp.sum(e, axis=0, keepdims=True))
        experts.append(eid)
    exp_ref[...] = jnp.concatenate(experts, axis=0).T.astype(jnp.int32)
    gate_ref[...] = jnp.concatenate(gates, axis=0).T


def _select(h, n2g, w_q, keys1, keys2):
    T, D = h.shape
    tt = min(SELECT_TILE, T)
    hk = PEER_HEADS * PEER_TOPK
    return pl.pallas_call(
        _select_kernel,
        grid=(T // tt,),
        in_specs=[pl.BlockSpec((tt, D), lambda i: (i, 0)),
                  _resident(n2g.shape), _resident(w_q.shape),
                  _resident(keys1.shape), _resident(keys2.shape)],
        out_specs=[pl.BlockSpec((tt, D), lambda i: (i, 0)),
                   pl.BlockSpec((tt, hk), lambda i: (i, 0)),
                   pl.BlockSpec((tt, hk), lambda i: (i, 0))],
        out_shape=[jax.ShapeDtypeStruct((T, D), F32),
                   jax.ShapeDtypeStruct((T, hk), jnp.int32),
                   jax.ShapeDtypeStruct((T, hk), F32)],
        compiler_params=pltpu.CompilerParams(dimension_semantics=("arbitrary",),
                                             vmem_limit_bytes=VMEM_LIMIT_BYTES),
    )(h, n2g, w_q, keys1, keys2)


def _pack_table(table):
    half = table.shape[1] // 2
    bits = lax.bitcast_convert_type(table.astype(jnp.bfloat16), jnp.uint16).astype(jnp.uint32)
    return lax.bitcast_convert_type((bits[:, half:] << 16) | bits[:, :half], I32)


def _unpack_words(w):
    lo = lax.bitcast_convert_type(lax.shift_left(w, jnp.full(w.shape, 16, I32)), F32)
    hi = lax.bitcast_convert_type(lax.bitwise_and(w, jnp.full(w.shape, -65536, I32)), F32)
    return lo, hi


def _sc_kernel(body, out_type, scratch_types):
    mesh = plsc.VectorSubcoreMesh(core_axis_name="c", subcore_axis_name="s")
    return pl.kernel(body, out_type=out_type, mesh=mesh, scratch_types=scratch_types,
                     compiler_params=pltpu.CompilerParams(needs_layout_passes=False))


def _sc_worker_base(tokens_per_worker):
    return (lax.axis_index("s") * SC_CORES + lax.axis_index("c")) * tokens_per_worker


def _sc_pipeline(nchunk, gather, compute):
    gather(0, 0).start()

    @pl.loop(0, nchunk, step=2)
    def _(ci):
        gather(ci + 1, 1).start()
        gather(ci, 0).wait()
        compute(ci, 0)

        @pl.when(ci + 2 < nchunk)
        def _():
            gather(ci + 2, 0).start()

        gather(ci + 1, 1).wait()
        compute(ci + 1, 1)


def _sc_dots(xn, experts, ptab):
    T, D = xn.shape
    HK = experts.shape[1]
    W = ptab.shape[1]
    L, TG, CH = SC_LANES, SC_TOKENS, SC_CHUNK
    tpw = T // SC_WORKERS
    cpt = HK // CH
    nchunk = TG * cpt
    DG = 8
    assert W * 2 == D and T % (SC_WORKERS * TG) == 0 and HK % CH == 0 and CH % L == 0
    assert W % (DG * L) == 0 and nchunk % 2 == 0

    def body(x_hbm, idx_hbm, tab_hbm, out_hbm, x_v, idx_v, out_v, rows_v, acc_v, sem):
        base = _sc_worker_base(tpw)
        lane = lax.iota(I32, L)

        def gather(ci, slot):
            tok, c = ci // cpt, ci % cpt
            return pltpu.make_async_copy(tab_hbm.at[idx_v.at[tok, pl.ds(c * CH, CH)]],
                                         rows_v.at[slot], sem.at[slot])

        def compute(ci, slot):
            tok, c = ci // cpt, ci % cpt
            for eg in range(CH // L):
                @pl.loop(0, W // (DG * L))
                def _(dg):
                    off = dg * (DG * L)
                    cont = jnp.full((L,), dg, I32) != 0
                    zero = jnp.zeros((L,), F32)
                    accs = [jnp.where(cont, acc_v[kk, :], zero) for kk in range(L)]
                    for j in range(DG):
                        xlo = x_v[tok, pl.ds(off + j * L, L)]
                        xhi = x_v[tok, pl.ds(W + off + j * L, L)]
                        for kk in range(L):
                            lo, hi = _unpack_words(rows_v[slot, eg * L + kk, pl.ds(off + j * L, L)])
                            accs[kk] = accs[kk] + lo * xlo + hi * xhi
                    for kk in range(L):
                        acc_v[kk, :] = accs[kk]

                s = jnp.zeros((L,), F32)
                for j in range(L):
                    s = s + plsc.load_gather(acc_v, [lane, jnp.full((L,), j, I32)])
                out_v[tok, pl.ds(c * CH + eg * L, L)] = s

        @pl.loop(0, tpw // TG)
        def _(g):
            t0 = base + g * TG
            pltpu.sync_copy(x_hbm.at[pl.ds(t0, TG)], x_v)
            pltpu.sync_copy(idx_hbm.at[pl.ds(t0, TG)], idx_v)
            _sc_pipeline(nchunk, gather, compute)
            pltpu.sync_copy(out_v, out_hbm.at[pl.ds(t0, TG)])

    return _sc_kernel(
        body, jax.ShapeDtypeStruct((T, HK), F32),
        [pltpu.VMEM((TG, D), F32), pltpu.VMEM((TG, HK), I32), pltpu.VMEM((TG, HK), F32),
         pltpu.VMEM((2, CH, W), I32), pltpu.VMEM((L, L), F32), pltpu.SemaphoreType.DMA((2,))],
    )(xn, experts, ptab)


def _sc_wsum(w, experts, ptab):
    T, HK = w.shape
    W = ptab.shape[1]
    D = 2 * W
    L, TG, CH = SC_LANES, SC_TOKENS, SC_CHUNK
    tpw = T // SC_WORKERS
    cpt = HK // CH
    nchunk = TG * cpt
    DG = 8
    assert T % (SC_WORKERS * TG) == 0 and HK % CH == 0 and W % (DG * L) == 0 and nchunk % 2 == 0

    def body(w_hbm, idx_hbm, tab_hbm, out_hbm, w_v, idx_v, y_v, rows_v, sem):
        base = _sc_worker_base(tpw)

        def gather(ci, slot):
            tok, c = ci // cpt, ci % cpt
            return pltpu.make_async_copy(tab_hbm.at[idx_v.at[tok, pl.ds(c * CH, CH)]],
                                         rows_v.at[slot], sem.at[slot])

        def compute(ci, slot):
            tok, c = ci // cpt, ci % cpt
            tokv = jnp.full((L,), tok, I32)
            cont = jnp.full((L,), c, I32) != 0

            @pl.loop(0, W // (DG * L))
            def _(dg):
                off = dg * (DG * L)
                zero = jnp.zeros((L,), F32)
                alo = [jnp.where(cont, y_v[tok, pl.ds(off + j * L, L)], zero) for j in range(DG)]
                ahi = [jnp.where(cont, y_v[tok, pl.ds(W + off + j * L, L)], zero) for j in range(DG)]
                for kk in range(CH):
                    wk = plsc.load_gather(w_v, [tokv, jnp.full((L,), c * CH + kk, I32)])
                    for j in range(DG):
                        lo, hi = _unpack_words(rows_v[slot, kk, pl.ds(off + j * L, L)])
                        alo[j] = alo[j] + wk * lo
                        ahi[j] = ahi[j] + wk * hi
                for j in range(DG):
                    y_v[tok, pl.ds(off + j * L, L)] = alo[j]
                    y_v[tok, pl.ds(W + off + j * L, L)] = ahi[j]

        @pl.loop(0, tpw // TG)
        def _(g):
            t0 = base + g * TG
            pltpu.sync_copy(w_hbm.at[pl.ds(t0, TG)], w_v)
            pltpu.sync_copy(idx_hbm.at[pl.ds(t0, TG)], idx_v)
            _sc_pipeline(nchunk, gather, compute)
            pltpu.sync_copy(y_v, out_hbm.at[pl.ds(t0, TG)])

    return _sc_kernel(
        body, jax.ShapeDtypeStruct((T, D), F32),
        [pltpu.VMEM((TG, HK), F32), pltpu.VMEM((TG, HK), I32), pltpu.VMEM((TG, D), F32),
         pltpu.VMEM((2, CH, W), I32), pltpu.SemaphoreType.DMA((2,))],
    )(w, experts, ptab)


def _act_kernel(dots_ref, gate_ref, w_ref):
    w_ref[...] = _gelu(dots_ref[...]) * gate_ref[...]


def _act(dots, gates):
    T, hk = dots.shape
    tt = min(ROW_TILE, T)
    spec = pl.BlockSpec((tt, hk), lambda i: (i, 0))
    return pl.pallas_call(
        _act_kernel, grid=(T // tt,), in_specs=[spec, spec], out_specs=spec,
        out_shape=jax.ShapeDtypeStruct((T, hk), F32),
        compiler_params=pltpu.CompilerParams(dimension_semantics=("arbitrary",)),
    )(dots, gates)


def _final_kernel(h_ref, y_ref, g_ref, o_ref):
    o_ref[...] = _rmsnorm(h_ref[...] + y_ref[...], g_ref[...])


def _final(h, y, gain):
    T, D = h.shape
    tt = min(ROW_TILE, T)
    spec = pl.BlockSpec((tt, D), lambda i: (i, 0))
    return pl.pallas_call(
        _final_kernel, grid=(T // tt,),
        in_specs=[spec, spec, pl.BlockSpec((1, D), lambda i: (0, 0))], out_specs=spec,
        out_shape=jax.ShapeDtypeStruct((T, D), F32),
        compiler_params=pltpu.CompilerParams(dimension_semantics=("arbitrary",)),
    )(h, y, gain)


def kernel(x, mem, norm1_gain, w_in, pool_w, pool_scale, sgu_ln_gain, sgu_ln_bias, sgu_w_s, sgu_b_s,
           sgu_w_out, mem_norm_gain, xa_w_kv, xa_w_out, w_out, norm2_gain, peer_w_q, peer_keys1,
           peer_keys2, peer_u, peer_v, final_norm_gain):
    B, S, D = x.shape
    depth = w_in.shape[0]
    lo = lambda w: w.astype(MXU_DTYPE)
    row = lambda w: w.reshape(1, -1)
    assert depth == 1, "the final RMSNorm is fused with the last layer's residual add"
    l = 0
    weights = (row(norm1_gain[l]), lo(w_in[l]), lo(pool_w[l]), row(pool_scale[l]),
               row(sgu_ln_gain[l]), row(sgu_ln_bias[l]), sgu_w_s[l], sgu_b_s[l].T,
               lo(sgu_w_out[l]), lo(xa_w_out[l]), lo(w_out[l]))
    select_w = (row(norm2_gain[l]), lo(peer_w_q[l]), lo(peer_keys1[l]), lo(peer_keys2[l]))
    u_packed, v_packed = _pack_table(peer_u[l]), _pack_table(peer_v[l])
    k, v = _memkv(mem, row(mem_norm_gain[l]), lo(xa_w_kv[l]))
    bc = B // BATCH_CHUNKS if B % BATCH_CHUNKS == 0 else B
    outs = []
    for b0 in range(0, B, bc):
        h = _mixer(x[b0:b0 + bc], k[b0:b0 + bc], v[b0:b0 + bc], *weights).reshape(bc * S, D)
        xn, experts, gates = _select(h, *select_w)
        dots = _sc_dots(xn, experts, u_packed)
        y = _sc_wsum(_act(dots, gates), experts, v_packed)
        outs.append(_final(h, y, row(final_norm_gain)))
    return jnp.concatenate(outs, axis=0).reshape(B, S, D)
```

```python
import functools
import math

import jax
import jax.numpy as jnp
from jax import lax
from jax.experimental import pallas as pl
from jax.experimental.pallas import tpu as pltpu
from jax.experimental.pallas import tpu_sc as plsc

F32 = jnp.float32
I32 = jnp.int32
MXU_DTYPE = jnp.bfloat16

RMS_EPS = 1e-6
LN_EPS = 1e-5
POOL_WINDOWS = (2, 4, 8, 16)
POOL_HALO = 16
SGU_CHUNK = 128
SGU_HEADS = 8
XA_HEADS = 4
PEER_HEADS = 8
PEER_N_KEYS = 128
PEER_TOPK = 16

V7X_VMEM_BYTES = 64 * 1024 * 1024
VMEM_LIMIT_BYTES = 56 * 1024 * 1024

SEQ_TILE = 256
SELECT_TILE = 256
ROW_TILE = 512
BATCH_CHUNKS = 4

SC_CORES = 2
SC_SUBCORES = 16
SC_LANES = 16
SC_WORKERS = SC_CORES * SC_SUBCORES
SC_TOKENS = 16
SC_CHUNK = 32
SC_NBUF = 4
HIGH_HALF = -65536


def _rmsnorm(x, gain):
    return x * lax.rsqrt(jnp.mean(x * x, axis=-1, keepdims=True) + RMS_EPS) * gain


def _gelu(x):
    return 0.5 * x * (1.0 + lax.erf(x * (1.0 / math.sqrt(2.0))))


def _mm(a, b):
    return jnp.dot(a.astype(MXU_DTYPE), b.astype(MXU_DTYPE), preferred_element_type=F32)


def _mm_nt(a, b):
    return lax.dot_general(a.astype(MXU_DTYPE), b.astype(MXU_DTYPE),
                           (((1,), (1,)), ((), ())), preferred_element_type=F32)


def _memkv_kernel(mem_ref, gain_ref, wkv_ref, k_ref, v_ref):
    width = k_ref.shape[-1]
    kv = _mm(_rmsnorm(mem_ref[0], gain_ref[...]), wkv_ref[...])
    k_ref[0] = kv[:, :width].astype(k_ref.dtype)
    v_ref[0] = kv[:, width:].astype(v_ref.dtype)


def _memkv(mem, gain, w_kv):
    B, M, D = mem.shape
    width = w_kv.shape[1] // 2
    const = lambda b: (0, 0)
    return pl.pallas_call(
        _memkv_kernel,
        grid=(B,),
        in_specs=[pl.BlockSpec((1, M, D), lambda b: (b, 0, 0)),
                  pl.BlockSpec((1, D), const),
                  pl.BlockSpec(w_kv.shape, const)],
        out_specs=[pl.BlockSpec((1, M, width), lambda b: (b, 0, 0))] * 2,
        out_shape=[jax.ShapeDtypeStruct((B, M, width), MXU_DTYPE)] * 2,
        compiler_params=pltpu.CompilerParams(dimension_semantics=("arbitrary",),
                                             vmem_limit_bytes=VMEM_LIMIT_BYTES),
    )(mem, gain, w_kv)


def _mixer_kernel(x_ref, k_ref, v_ref, n1g_ref, win_ref, poolw_ref, pools_ref, lng_ref, lnb_ref,
                  ws_ref, bst_ref, sguwo_ref, xawo_ref, wout_ref, h_ref, tail_ref):
    ts, d = x_ref.shape[1], x_ref.shape[2]
    s_idx = pl.program_id(1)
    x = x_ref[0]
    nb = _rmsnorm(x, n1g_ref[...]).astype(MXU_DTYPE)

    def proj(col):
        return jnp.dot(nb, win_ref[:, col * d:(col + 1) * d], preferred_element_type=F32)

    @pl.when(s_idx == 0)
    def _():
        tail_ref[...] = jnp.zeros_like(tail_ref)

    p = proj(0)
    ext = jnp.concatenate([tail_ref[...], p], axis=0)
    tail_ref[...] = p[ts - POOL_HALO:, :]
    pos = s_idx * ts + lax.broadcasted_iota(jnp.int32, (ts, 1), 0)
    group = d // len(POOL_WINDOWS)
    y_pool = []
    for g, w in enumerate(POOL_WINDOWS):
        acc = ext[:, g * group:(g + 1) * group]
        shift = 1
        while shift < w:
            acc = acc + pltpu.roll(acc, shift, 0)
            shift *= 2
        count = jnp.minimum(pos + 1, w).astype(F32)
        diff = acc[POOL_HALO:, :] / count - p[:, g * group:(g + 1) * group]
        y_pool.append(_mm(diff, poolw_ref[g]))
    y_pool = jnp.concatenate(y_pool, axis=1) * pools_ref[...]

    u = _gelu(proj(1))
    v = _gelu(proj(2))
    mu = jnp.mean(v, axis=-1, keepdims=True)
    vc = v - mu
    var = jnp.mean(vc * vc, axis=-1, keepdims=True)
    v = (vc * lax.rsqrt(var + LN_EPS) * lng_ref[...] + lnb_ref[...]).astype(MXU_DTYPE)
    hd = d // SGU_HEADS
    causal = (lax.broadcasted_iota(jnp.int32, (SGU_CHUNK, SGU_CHUNK), 0)
              >= lax.broadcasted_iota(jnp.int32, (SGU_CHUNK, SGU_CHUNK), 1))
    mixed_rows = []
    w_masked = [jnp.where(causal, ws_ref[h], 0.0).astype(MXU_DTYPE) for h in range(SGU_HEADS)]
    for c in range(ts // SGU_CHUNK):
        rows = slice(c * SGU_CHUNK, (c + 1) * SGU_CHUNK)
        heads = []
        for h in range(SGU_HEADS):
            mixed = jnp.dot(w_masked[h], v[rows, h * hd:(h + 1) * hd], preferred_element_type=F32)
            heads.append(mixed + bst_ref[:, h:h + 1])
        mixed_rows.append(jnp.concatenate(heads, axis=1))
    mixed = jnp.concatenate(mixed_rows, axis=0) if len(mixed_rows) > 1 else mixed_rows[0]
    y_sgu = _mm(u * mixed, sguwo_ref[...])

    q = proj(3).astype(MXU_DTYPE)
    xd = d // XA_HEADS
    outs = []
    for h in range(XA_HEADS):
        cols = slice(h * xd, (h + 1) * xd)
        s = _mm_nt(q[:, cols], k_ref[0, :, cols]) * (xd ** -0.5)
        e = jnp.exp(s - jnp.max(s, axis=-1, keepdims=True))
        probs = e / jnp.sum(e, axis=-1, keepdims=True)
        outs.append(_mm(probs, v_ref[0, :, cols]))
    y_xa = _mm(jnp.concatenate(outs, axis=1), xawo_ref[...])

    merged = (jax.nn.sigmoid(proj(4)) * y_pool + jax.nn.sigmoid(proj(5)) * y_sgu
              + jax.nn.sigmoid(proj(6)) * y_xa)
    h_ref[0] = x + _mm(merged, wout_ref[...])


def _resident(shape):
    zeros = (0,) * len(shape)
    return pl.BlockSpec(shape, lambda *_: zeros, pipeline_mode=pl.Buffered(1))


def _mixer(x, k, v, n1g, w_in, pool_w, pool_scale, ln_g, ln_b, w_s, b_s_t, sgu_wo, xa_wo, w_out):
    B, S, D = x.shape
    ts = min(SEQ_TILE, S)
    M = k.shape[1]
    weights = (n1g, w_in, pool_w, pool_scale, ln_g, ln_b, w_s, b_s_t, sgu_wo, xa_wo, w_out)
    return pl.pallas_call(
        _mixer_kernel,
        grid=(B, S // ts),
        in_specs=[pl.BlockSpec((1, ts, D), lambda b, s: (b, s, 0)),
                  pl.BlockSpec((1, M, k.shape[2]), lambda b, s: (b, 0, 0)),
                  pl.BlockSpec((1, M, v.shape[2]), lambda b, s: (b, 0, 0))]
                 + [_resident(w.shape) for w in weights],
        out_specs=pl.BlockSpec((1, ts, D), lambda b, s: (b, s, 0)),
        out_shape=jax.ShapeDtypeStruct((B, S, D), F32),
        scratch_shapes=[pltpu.VMEM((POOL_HALO, D), F32)],
        compiler_params=pltpu.CompilerParams(dimension_semantics=("arbitrary", "arbitrary"),
                                             vmem_limit_bytes=VMEM_LIMIT_BYTES),
    )(x, k, v, *weights)


def _topk_rows(vals, payload=None):
    n_rows = vals.shape[0]
    row = lax.broadcasted_iota(jnp.int32, vals.shape, 0).astype(F32)
    out_v, out_i = [], []
    for _ in range(PEER_TOPK):
        m = jnp.max(vals, axis=0, keepdims=True)
        first = jnp.min(jnp.where(vals == m, row, float(n_rows)), axis=0, keepdims=True)
        sel = row == first
        out_v.append(m)
        if payload is None:
            out_i.append(first)
        else:
            out_i.append(jnp.max(jnp.where(sel, payload, -1.0), axis=0, keepdims=True))
        vals = jnp.where(sel, -jnp.inf, vals)
    return jnp.concatenate(out_v, axis=0), jnp.concatenate(out_i, axis=0)


def _candidates(v1, i1, v2, i2):
    sub = 8
    jrow = lax.broadcasted_iota(jnp.int32, (sub, 1), 0)
    vals = [v1[0:1] + v2]
    ids = [i1[0:1] * float(PEER_N_KEYS) + i2]
    for i in range(1, sub):
        keep = jrow < (PEER_TOPK // (i + 1))
        vals.append(jnp.where(keep, v1[i:i + 1] + v2[0:sub], -jnp.inf))
        ids.append(i1[i:i + 1] * float(PEER_N_KEYS) + i2[0:sub])
    vals.append(v1[sub:] + v2[0:1])
    ids.append(i1[sub:] * float(PEER_N_KEYS) + i2[0:1])
    return jnp.concatenate(vals, axis=0), jnp.concatenate(ids, axis=0)


def _select_kernel(h_ref, n2g_ref, wq_ref, k1_ref, k2_ref, xn_ref, exp_ref, gate_ref):
    xn = _rmsnorm(h_ref[...], n2g_ref[...])
    xn_ref[...] = _pack_halves(xn)
    q = _mm(xn, wq_ref[...])
    half = k1_ref.shape[1]
    experts, gates = [], []
    for h in range(PEER_HEADS):
        q1 = q[:, (2 * h) * half:(2 * h + 1) * half]
        q2 = q[:, (2 * h + 1) * half:(2 * h + 2) * half]
        v1, i1 = _topk_rows(_mm_nt(k1_ref[...], q1))
        v2, i2 = _topk_rows(_mm_nt(k2_ref[...], q2))
        best, eid = _topk_rows(*_candidates(v1, i1, v2, i2))
        e = jnp.exp(best - jnp.max(best, axis=0, keepdims=True))
        gates.append(e / jnp.sum(e, axis=0, keepdims=True))
        experts.append(eid)
    exp_ref[...] = jnp.concatenate(experts, axis=0).T.astype(jnp.int32)
    gate_ref[...] = jnp.concatenate(gates, axis=0).T


def _select(h, n2g, w_q, keys1, keys2):
    T, D = h.shape
    tt = min(SELECT_TILE, T)
    hk = PEER_HEADS * PEER_TOPK
    return pl.pallas_call(
        _select_kernel,
        grid=(T // tt,),
        in_specs=[pl.BlockSpec((tt, D), lambda i: (i, 0)),
                  _resident(n2g.shape), _resident(w_q.shape),
                  _resident(keys1.shape), _resident(keys2.shape)],
        out_specs=[pl.BlockSpec((tt, D // 2), lambda i: (i, 0)),
                   pl.BlockSpec((tt, hk), lambda i: (i, 0)),
                   pl.BlockSpec((tt, hk), lambda i: (i, 0))],
        out_shape=[jax.ShapeDtypeStruct((T, D // 2), I32),
                   jax.ShapeDtypeStruct((T, hk), jnp.int32),
                   jax.ShapeDtypeStruct((T, hk), F32)],
        compiler_params=pltpu.CompilerParams(dimension_semantics=("arbitrary",),
                                             vmem_limit_bytes=VMEM_LIMIT_BYTES),
    )(h, n2g, w_q, keys1, keys2)


def _pack_table(table):
    half = table.shape[1] // 2
    bits = lax.bitcast_convert_type(table.astype(jnp.bfloat16), jnp.uint16).astype(jnp.uint32)
    return lax.bitcast_convert_type((bits[:, half:] << 16) | bits[:, :half], I32)


def _bf16_bits(x):
    return lax.bitcast_convert_type(x.astype(jnp.bfloat16).astype(F32), I32)


def _pack_halves(x):
    half = x.shape[1] // 2
    return (_bf16_bits(x[:, half:]) & HIGH_HALF) | lax.shift_right_logical(_bf16_bits(x[:, :half]), 16)


def _pack_splat(x):
    bits = _bf16_bits(x)
    return (bits & HIGH_HALF) | lax.shift_right_logical(bits, 16)


def _unpack_words(w):
    lo = lax.bitcast_convert_type(lax.shift_left(w, jnp.full(w.shape, 16, I32)), F32)
    hi = lax.bitcast_convert_type(lax.bitwise_and(w, jnp.full(w.shape, HIGH_HALF, I32)), F32)
    return lo, hi


def _as_bf16(w):
    return plsc.bitcast(w, jnp.bfloat16)


def _halves_f32(p):
    return _unpack_words(plsc.bitcast(p, I32))


def _sc_kernel(body, out_type, scratch_types):
    mesh = plsc.VectorSubcoreMesh(core_axis_name="c", subcore_axis_name="s")
    return pl.kernel(body, out_type=out_type, mesh=mesh, scratch_types=scratch_types,
                     compiler_params=pltpu.CompilerParams(needs_layout_passes=False))


def _sc_worker_base(tokens_per_worker):
    return (lax.axis_index("s") * SC_CORES + lax.axis_index("c")) * tokens_per_worker


def _sc_pipeline(nchunk, gather, compute):
    ahead = SC_NBUF - 1
    for i in range(ahead):
        gather(i, i).start()

    @pl.loop(0, nchunk)
    def _(ci):
        @pl.when(ci + ahead < nchunk)
        def _():
            gather(ci + ahead, (ci + ahead) % SC_NBUF).start()

        slot = ci % SC_NBUF
        gather(ci, slot).wait()
        compute(ci, slot)


def _sc_dots(xp, experts, ptab):
    T, W = xp.shape
    HK = experts.shape[1]
    L, TG, CH = SC_LANES, SC_TOKENS, SC_CHUNK
    tpw = T // SC_WORKERS
    cpt = HK // CH
    nchunk = TG * cpt
    DG = 8
    assert W == ptab.shape[1] and T % (SC_WORKERS * TG) == 0 and HK % CH == 0 and CH % L == 0
    assert W % (DG * L) == 0 and DG % 2 == 0 and nchunk >= SC_NBUF

    def body(x_hbm, idx_hbm, tab_hbm, out_hbm, x_v, idx_v, out_v, rows_v, acc_v, sem):
        base = _sc_worker_base(tpw)
        lane = lax.iota(I32, L)

        def gather(ci, slot):
            tok, c = ci // cpt, ci % cpt
            return pltpu.make_async_copy(tab_hbm.at[idx_v.at[tok, pl.ds(c * CH, CH)]],
                                         rows_v.at[slot], sem.at[slot])

        def compute(ci, slot):
            tok, c = ci // cpt, ci % cpt
            for eg in range(CH // L):
                @pl.loop(0, W // (DG * L))
                def _(dg):
                    off = dg * (DG * L)
                    cont = jnp.full((L,), dg, I32) != 0
                    zero = jnp.zeros((L,), F32)
                    accs = [jnp.where(cont, acc_v[kk, :], zero) for kk in range(L)]
                    for j in range(0, DG, 2):
                        xa = _as_bf16(x_v[tok, pl.ds(off + j * L, L)])
                        xb = _as_bf16(x_v[tok, pl.ds(off + (j + 1) * L, L)])
                        for kk in range(L):
                            ra = _as_bf16(rows_v[slot, eg * L + kk, pl.ds(off + j * L, L)])
                            rb = _as_bf16(rows_v[slot, eg * L + kk, pl.ds(off + (j + 1) * L, L)])
                            lo, hi = _halves_f32(ra * xa + rb * xb)
                            accs[kk] = accs[kk] + lo + hi
                    for kk in range(L):
                        acc_v[kk, :] = accs[kk]

                s = jnp.zeros((L,), F32)
                for j in range(L):
                    s = s + plsc.load_gather(acc_v, [lane, jnp.full((L,), j, I32)])
                out_v[tok, pl.ds(c * CH + eg * L, L)] = s

        @pl.loop(0, tpw // TG)
        def _(g):
            t0 = base + g * TG
            pltpu.sync_copy(x_hbm.at[pl.ds(t0, TG)], x_v)
            pltpu.sync_copy(idx_hbm.at[pl.ds(t0, TG)], idx_v)
            _sc_pipeline(nchunk, gather, compute)
            pltpu.sync_copy(out_v, out_hbm.at[pl.ds(t0, TG)])

    return _sc_kernel(
        body, jax.ShapeDtypeStruct((T, HK), F32),
        [pltpu.VMEM((TG, W), I32), pltpu.VMEM((TG, HK), I32), pltpu.VMEM((TG, HK), F32),
         pltpu.VMEM((SC_NBUF, CH, W), I32), pltpu.VMEM((L, L), F32),
         pltpu.SemaphoreType.DMA((SC_NBUF,))],
    )(xp, experts, ptab)


def _sc_wsum(wp, experts, ptab):
    T, HK = wp.shape
    W = ptab.shape[1]
    D = 2 * W
    L, TG, CH = SC_LANES, SC_TOKENS, SC_CHUNK
    tpw = T // SC_WORKERS
    cpt = HK // CH
    nchunk = TG * cpt
    DG = 8
    assert T % (SC_WORKERS * TG) == 0 and HK % CH == 0 and CH % 2 == 0 and W % (DG * L) == 0
    assert nchunk >= SC_NBUF

    def body(w_hbm, idx_hbm, tab_hbm, out_hbm, w_v, idx_v, y_v, rows_v, sem):
        base = _sc_worker_base(tpw)

        def gather(ci, slot):
            tok, c = ci // cpt, ci % cpt
            return pltpu.make_async_copy(tab_hbm.at[idx_v.at[tok, pl.ds(c * CH, CH)]],
                                         rows_v.at[slot], sem.at[slot])

        def compute(ci, slot):
            tok, c = ci // cpt, ci % cpt
            tokv = jnp.full((L,), tok, I32)
            cont = jnp.full((L,), c, I32) != 0

            @pl.loop(0, W // (DG * L))
            def _(dg):
                off = dg * (DG * L)
                zero = jnp.zeros((L,), F32)
                alo = [jnp.where(cont, y_v[tok, pl.ds(off + j * L, L)], zero) for j in range(DG)]
                ahi = [jnp.where(cont, y_v[tok, pl.ds(W + off + j * L, L)], zero) for j in range(DG)]
                for kk in range(0, CH, 2):
                    wa = _as_bf16(plsc.load_gather(w_v, [tokv, jnp.full((L,), c * CH + kk, I32)]))
                    wb = _as_bf16(plsc.load_gather(w_v, [tokv, jnp.full((L,), c * CH + kk + 1, I32)]))
                    for j in range(DG):
                        ra = _as_bf16(rows_v[slot, kk, pl.ds(off + j * L, L)])
                        rb = _as_bf16(rows_v[slot, kk + 1, pl.ds(off + j * L, L)])
                        lo, hi = _halves_f32(wa * ra + wb * rb)
                        alo[j] = alo[j] + lo
                        ahi[j] = ahi[j] + hi
                for j in range(DG):
                    y_v[tok, pl.ds(off + j * L, L)] = alo[j]
                    y_v[tok, pl.ds(W + off + j * L, L)] = ahi[j]

        @pl.loop(0, tpw // TG)
        def _(g):
            t0 = base + g * TG
            pltpu.sync_copy(w_hbm.at[pl.ds(t0, TG)], w_v)
            pltpu.sync_copy(idx_hbm.at[pl.ds(t0, TG)], idx_v)
            _sc_pipeline(nchunk, gather, compute)
            pltpu.sync_copy(y_v, out_hbm.at[pl.ds(t0, TG)])

    return _sc_kernel(
        body, jax.ShapeDtypeStruct((T, D), F32),
        [pltpu.VMEM((TG, HK), I32), pltpu.VMEM((TG, HK), I32), pltpu.VMEM((TG, D), F32),
         pltpu.VMEM((SC_NBUF, CH, W), I32), pltpu.SemaphoreType.DMA((SC_NBUF,))],
    )(wp, experts, ptab)


def _act_kernel(dots_ref, gate_ref, w_ref):
    w_ref[...] = _pack_splat(_gelu(dots_ref[...]) * gate_ref[...])


def _act(dots, gates):
    T, hk = dots.shape
    tt = min(ROW_TILE, T)
    spec = pl.BlockSpec((tt, hk), lambda i: (i, 0))
    return pl.pallas_call(
        _act_kernel, grid=(T // tt,), in_specs=[spec, spec], out_specs=spec,
        out_shape=jax.ShapeDtypeStruct((T, hk), I32),
        compiler_params=pltpu.CompilerParams(dimension_semantics=("arbitrary",)),
    )(dots, gates)


def _final_kernel(h_ref, y_ref, g_ref, o_ref):
    o_ref[...] = _rmsnorm(h_ref[...] + y_ref[...], g_ref[...])


def _final(h, y, gain):
    T, D = h.shape
    tt = min(ROW_TILE, T)
    spec = pl.BlockSpec((tt, D), lambda i: (i, 0))
    return pl.pallas_call(
        _final_kernel, grid=(T // tt,),
        in_specs=[spec, spec, pl.BlockSpec((1, D), lambda i: (0, 0))], out_specs=spec,
        out_shape=jax.ShapeDtypeStruct((T, D), F32),
        compiler_params=pltpu.CompilerParams(dimension_semantics=("arbitrary",)),
    )(h, y, gain)


def kernel(x, mem, norm1_gain, w_in, pool_w, pool_scale, sgu_ln_gain, sgu_ln_bias, sgu_w_s, sgu_b_s,
           sgu_w_out, mem_norm_gain, xa_w_kv, xa_w_out, w_out, norm2_gain, peer_w_q, peer_keys1,
           peer_keys2, peer_u, peer_v, final_norm_gain):
    B, S, D = x.shape
    depth = w_in.shape[0]
    lo = lambda w: w.astype(MXU_DTYPE)
    row = lambda w: w.reshape(1, -1)
    assert depth == 1, "the final RMSNorm is fused with the last layer's residual add"
    l = 0
    weights = (row(norm1_gain[l]), lo(w_in[l]), lo(pool_w[l]), row(pool_scale[l]),
               row(sgu_ln_gain[l]), row(sgu_ln_bias[l]), sgu_w_s[l], sgu_b_s[l].T,
               lo(sgu_w_out[l]), lo(xa_w_out[l]), lo(w_out[l]))
    select_w = (row(norm2_gain[l]), lo(peer_w_q[l]), lo(peer_keys1[l]), lo(peer_keys2[l]))
    u_packed, v_packed = _pack_table(peer_u[l]), _pack_table(peer_v[l])
    k, v = _memkv(mem, row(mem_norm_gain[l]), lo(xa_w_kv[l]))
    bc = B // BATCH_CHUNKS if B % BATCH_CHUNKS == 0 else B
    outs = []
    for b0 in range(0, B, bc):
        h = _mixer(x[b0:b0 + bc], k[b0:b0 + bc], v[b0:b0 + bc], *weights).reshape(bc * S, D)
        xn, experts, gates = _select(h, *select_w)
        dots = _sc_dots(xn, experts, u_packed)
        y = _sc_wsum(_act(dots, gates), experts, v_packed)
        outs.append(_final(h, y, row(final_norm_gain)))
    return jnp.concatenate(outs, axis=0).reshape(B, S, D)
```

```python
import functools
import math

import jax
import jax.numpy as jnp
from jax import lax
from jax.experimental import pallas as pl
from jax.experimental.pallas import tpu as pltpu
from jax.experimental.pallas import tpu_sc as plsc

F32 = jnp.float32
I32 = jnp.int32
MXU_DTYPE = jnp.bfloat16

RMS_EPS = 1e-6
LN_EPS = 1e-5
POOL_WINDOWS = (2, 4, 8, 16)
POOL_HALO = 16
SGU_CHUNK = 128
SGU_HEADS = 8
XA_HEADS = 4
PEER_HEADS = 8
PEER_N_KEYS = 128
PEER_TOPK = 16

V7X_VMEM_BYTES = 64 * 1024 * 1024
VMEM_LIMIT_BYTES = 56 * 1024 * 1024

SEQ_TILE = 256
SELECT_TILE = 256
ROW_TILE = 512
BATCH_CHUNKS = (1, 1, 2, 2, 2)

SC_CORES = 2
SC_SUBCORES = 16
SC_LANES = 16
SC_WORKERS = SC_CORES * SC_SUBCORES
SC_TOKENS = 16
SC_CHUNK = 32
SC_NBUF = 4
HIGH_HALF = -65536


def _rmsnorm(x, gain):
    return x * lax.rsqrt(jnp.mean(x * x, axis=-1, keepdims=True) + RMS_EPS) * gain


def _gelu(x):
    return 0.5 * x * (1.0 + lax.erf(x * (1.0 / math.sqrt(2.0))))


def _mm(a, b):
    return jnp.dot(a.astype(MXU_DTYPE), b.astype(MXU_DTYPE), preferred_element_type=F32)


def _mm_nt(a, b):
    return lax.dot_general(a.astype(MXU_DTYPE), b.astype(MXU_DTYPE),
                           (((1,), (1,)), ((), ())), preferred_element_type=F32)


def _memkv_kernel(mem_ref, gain_ref, wkv_ref, k_ref, v_ref):
    width = k_ref.shape[-1]
    kv = _mm(_rmsnorm(mem_ref[0], gain_ref[...]), wkv_ref[...])
    k_ref[0] = kv[:, :width].astype(k_ref.dtype)
    v_ref[0] = kv[:, width:].astype(v_ref.dtype)


def _memkv(mem, gain, w_kv):
    B, M, D = mem.shape
    width = w_kv.shape[1] // 2
    const = lambda b: (0, 0)
    return pl.pallas_call(
        _memkv_kernel,
        grid=(B,),
        in_specs=[pl.BlockSpec((1, M, D), lambda b: (b, 0, 0)),
                  pl.BlockSpec((1, D), const),
                  pl.BlockSpec(w_kv.shape, const)],
        out_specs=[pl.BlockSpec((1, M, width), lambda b: (b, 0, 0))] * 2,
        out_shape=[jax.ShapeDtypeStruct((B, M, width), MXU_DTYPE)] * 2,
        compiler_params=pltpu.CompilerParams(dimension_semantics=("arbitrary",),
                                             vmem_limit_bytes=VMEM_LIMIT_BYTES),
    )(mem, gain, w_kv)


def _mixer_kernel(x_ref, k_ref, v_ref, n1g_ref, win_ref, poolw_ref, pools_ref, lng_ref, lnb_ref,
                  ws_ref, bst_ref, sguwo_ref, xawo_ref, wout_ref, h_ref, tail_ref):
    ts, d = x_ref.shape[1], x_ref.shape[2]
    s_idx = pl.program_id(1)
    x = x_ref[0]
    nb = _rmsnorm(x, n1g_ref[...]).astype(MXU_DTYPE)

    def proj(col):
        return jnp.dot(nb, win_ref[:, col * d:(col + 1) * d], preferred_element_type=F32)

    @pl.when(s_idx == 0)
    def _():
        tail_ref[...] = jnp.zeros_like(tail_ref)

    p = proj(0)
    ext = jnp.concatenate([tail_ref[...], p], axis=0)
    tail_ref[...] = p[ts - POOL_HALO:, :]
    pos = s_idx * ts + lax.broadcasted_iota(jnp.int32, (ts, 1), 0)
    group = d // len(POOL_WINDOWS)
    y_pool = []
    for g, w in enumerate(POOL_WINDOWS):
        acc = ext[:, g * group:(g + 1) * group]
        shift = 1
        while shift < w:
            acc = acc + pltpu.roll(acc, shift, 0)
            shift *= 2
        count = jnp.minimum(pos + 1, w).astype(F32)
        diff = acc[POOL_HALO:, :] / count - p[:, g * group:(g + 1) * group]
        y_pool.append(_mm(diff, poolw_ref[g]))
    y_pool = jnp.concatenate(y_pool, axis=1) * pools_ref[...]

    u = _gelu(proj(1))
    v = _gelu(proj(2))
    mu = jnp.mean(v, axis=-1, keepdims=True)
    vc = v - mu
    var = jnp.mean(vc * vc, axis=-1, keepdims=True)
    v = (vc * lax.rsqrt(var + LN_EPS) * lng_ref[...] + lnb_ref[...]).astype(MXU_DTYPE)
    hd = d // SGU_HEADS
    causal = (lax.broadcasted_iota(jnp.int32, (SGU_CHUNK, SGU_CHUNK), 0)
              >= lax.broadcasted_iota(jnp.int32, (SGU_CHUNK, SGU_CHUNK), 1))
    mixed_rows = []
    w_masked = [jnp.where(causal, ws_ref[h], 0.0).astype(MXU_DTYPE) for h in range(SGU_HEADS)]
    for c in range(ts // SGU_CHUNK):
        rows = slice(c * SGU_CHUNK, (c + 1) * SGU_CHUNK)
        heads = []
        for h in range(SGU_HEADS):
            mixed = jnp.dot(w_masked[h], v[rows, h * hd:(h + 1) * hd], preferred_element_type=F32)
            heads.append(mixed + bst_ref[:, h:h + 1])
        mixed_rows.append(jnp.concatenate(heads, axis=1))
    mixed = jnp.concatenate(mixed_rows, axis=0) if len(mixed_rows) > 1 else mixed_rows[0]
    y_sgu = _mm(u * mixed, sguwo_ref[...])

    q = proj(3).astype(MXU_DTYPE)
    xd = d // XA_HEADS
    outs = []
    for h in range(XA_HEADS):
        cols = slice(h * xd, (h + 1) * xd)
        s = _mm_nt(q[:, cols], k_ref[0, :, cols]) * (xd ** -0.5)
        e = jnp.exp(s - jnp.max(s, axis=-1, keepdims=True))
        probs = e / jnp.sum(e, axis=-1, keepdims=True)
        outs.append(_mm(probs, v_ref[0, :, cols]))
    y_xa = _mm(jnp.concatenate(outs, axis=1), xawo_ref[...])

    merged = (jax.nn.sigmoid(proj(4)) * y_pool + jax.nn.sigmoid(proj(5)) * y_sgu
              + jax.nn.sigmoid(proj(6)) * y_xa)
    h_ref[0] = x + _mm(merged, wout_ref[...])


def _resident(shape):
    zeros = (0,) * len(shape)
    return pl.BlockSpec(shape, lambda *_: zeros, pipeline_mode=pl.Buffered(1))


def _mixer(x, k, v, weights, b0, nb):
    _, S, D = x.shape
    ts = min(SEQ_TILE, S)
    M = k.shape[1]
    return pl.pallas_call(
        _mixer_kernel,
        grid=(nb, S // ts),
        in_specs=[pl.BlockSpec((1, ts, D), lambda b, s: (b + b0, s, 0)),
                  pl.BlockSpec((1, M, k.shape[2]), lambda b, s: (b + b0, 0, 0)),
                  pl.BlockSpec((1, M, v.shape[2]), lambda b, s: (b + b0, 0, 0))]
                 + [_resident(w.shape) for w in weights],
        out_specs=pl.BlockSpec((1, ts, D), lambda b, s: (b, s, 0)),
        out_shape=jax.ShapeDtypeStruct((nb, S, D), F32),
        scratch_shapes=[pltpu.VMEM((POOL_HALO, D), F32)],
        compiler_params=pltpu.CompilerParams(dimension_semantics=("arbitrary", "arbitrary"),
                                             vmem_limit_bytes=VMEM_LIMIT_BYTES),
    )(x, k, v, *weights)


def _topk_rows(vals, payload=None):
    n_rows = vals.shape[0]
    row = lax.broadcasted_iota(jnp.int32, vals.shape, 0).astype(F32)
    out_v, out_i = [], []
    for _ in range(PEER_TOPK):
        m = jnp.max(vals, axis=0, keepdims=True)
        first = jnp.min(jnp.where(vals == m, row, float(n_rows)), axis=0, keepdims=True)
        sel = row == first
        out_v.append(m)
        if payload is None:
            out_i.append(first)
        else:
            out_i.append(jnp.max(jnp.where(sel, payload, -1.0), axis=0, keepdims=True))
        vals = jnp.where(sel, -jnp.inf, vals)
    return jnp.concatenate(out_v, axis=0), jnp.concatenate(out_i, axis=0)


def _candidates(v1, i1, v2, i2):
    sub = 8
    jrow = lax.broadcasted_iota(jnp.int32, (sub, 1), 0)
    vals = [v1[0:1] + v2]
    ids = [i1[0:1] * float(PEER_N_KEYS) + i2]
    for i in range(1, sub):
        keep = jrow < (PEER_TOPK // (i + 1))
        vals.append(jnp.where(keep, v1[i:i + 1] + v2[0:sub], -jnp.inf))
        ids.append(i1[i:i + 1] * float(PEER_N_KEYS) + i2[0:sub])
    vals.append(v1[sub:] + v2[0:1])
    ids.append(i1[sub:] * float(PEER_N_KEYS) + i2[0:1])
    return jnp.concatenate(vals, axis=0), jnp.concatenate(ids, axis=0)


def _select_kernel(h_ref, n2g_ref, wq_ref, k1_ref, k2_ref, xn_ref, exp_ref, gate_ref):
    xn = _rmsnorm(h_ref[...], n2g_ref[...])
    xn_ref[...] = _pack_halves(xn)
    q = _mm(xn, wq_ref[...])
    half = k1_ref.shape[1]
    experts, gates = [], []
    for h in range(PEER_HEADS):
        q1 = q[:, (2 * h) * half:(2 * h + 1) * half]
        q2 = q[:, (2 * h + 1) * half:(2 * h + 2) * half]
        v1, i1 = _topk_rows(_mm_nt(k1_ref[...], q1))
        v2, i2 = _topk_rows(_mm_nt(k2_ref[...], q2))
        best, eid = _topk_rows(*_candidates(v1, i1, v2, i2))
        e = jnp.exp(best - jnp.max(best, axis=0, keepdims=True))
        gates.append(e / jnp.sum(e, axis=0, keepdims=True))
        experts.append(eid)
    exp_ref[...] = jnp.concatenate(experts, axis=0).T.astype(jnp.int32)
    gate_ref[...] = jnp.concatenate(gates, axis=0).T


def _select(h, n2g, w_q, keys1, keys2):
    T, D = h.shape
    tt = min(SELECT_TILE, T)
    hk = PEER_HEADS * PEER_TOPK
    return pl.pallas_call(
        _select_kernel,
        grid=(T // tt,),
        in_specs=[pl.BlockSpec((tt, D), lambda i: (i, 0)),
                  _resident(n2g.shape), _resident(w_q.shape),
                  _resident(keys1.shape), _resident(keys2.shape)],
        out_specs=[pl.BlockSpec((tt, D // 2), lambda i: (i, 0)),
                   pl.BlockSpec((tt, hk), lambda i: (i, 0)),
                   pl.BlockSpec((tt, hk), lambda i: (i, 0))],
        out_shape=[jax.ShapeDtypeStruct((T, D // 2), I32),
                   jax.ShapeDtypeStruct((T, hk), jnp.int32),
                   jax.ShapeDtypeStruct((T, hk), F32)],
        compiler_params=pltpu.CompilerParams(dimension_semantics=("arbitrary",),
                                             vmem_limit_bytes=VMEM_LIMIT_BYTES),
    )(h, n2g, w_q, keys1, keys2)


def _pack_table(table):
    half = table.shape[1] // 2
    bits = lax.bitcast_convert_type(table.astype(jnp.bfloat16), jnp.uint16).astype(jnp.uint32)
    return lax.bitcast_convert_type((bits[:, half:] << 16) | bits[:, :half], I32)


def _bf16_bits(x):
    return lax.bitcast_convert_type(x.astype(jnp.bfloat16).astype(F32), I32)


def _pack_halves(x):
    half = x.shape[1] // 2
    return (_bf16_bits(x[:, half:]) & HIGH_HALF) | lax.shift_right_logical(_bf16_bits(x[:, :half]), 16)


def _pack_splat(x):
    bits = _bf16_bits(x)
    return (bits & HIGH_HALF) | lax.shift_right_logical(bits, 16)


def _unpack_words(w):
    lo = lax.bitcast_convert_type(lax.shift_left(w, jnp.full(w.shape, 16, I32)), F32)
    hi = lax.bitcast_convert_type(lax.bitwise_and(w, jnp.full(w.shape, HIGH_HALF, I32)), F32)
    return lo, hi


def _as_bf16(w):
    return plsc.bitcast(w, jnp.bfloat16)


def _halves_f32(p):
    return _unpack_words(plsc.bitcast(p, I32))


def _sc_kernel(body, out_type, scratch_types):
    mesh = plsc.VectorSubcoreMesh(core_axis_name="c", subcore_axis_name="s")
    return pl.kernel(body, out_type=out_type, mesh=mesh, scratch_types=scratch_types,
                     compiler_params=pltpu.CompilerParams(needs_layout_passes=False))


def _sc_worker_base(tokens_per_worker):
    return (lax.axis_index("s") * SC_CORES + lax.axis_index("c")) * tokens_per_worker


def _sc_pipeline(nchunk, gather, compute):
    ahead = SC_NBUF - 1
    for i in range(ahead):
        gather(i, i).start()

    @pl.loop(0, nchunk)
    def _(ci):
        @pl.when(ci + ahead < nchunk)
        def _():
            gather(ci + ahead, (ci + ahead) % SC_NBUF).start()

        slot = ci % SC_NBUF
        gather(ci, slot).wait()
        compute(ci, slot)


def _sc_dots(xp, experts, ptab):
    T, W = xp.shape
    HK = experts.shape[1]
    L, TG, CH = SC_LANES, SC_TOKENS, SC_CHUNK
    tpw = T // SC_WORKERS
    cpt = HK // CH
    nchunk = TG * cpt
    DG = 8
    assert W == ptab.shape[1] and T % (SC_WORKERS * TG) == 0 and HK % CH == 0 and CH % L == 0
    assert W % (DG * L) == 0 and DG % 2 == 0 and nchunk >= SC_NBUF

    def body(x_hbm, idx_hbm, tab_hbm, out_hbm, x_v, idx_v, out_v, rows_v, acc_v, sem):
        base = _sc_worker_base(tpw)
        lane = lax.iota(I32, L)

        def gather(ci, slot):
            tok, c = ci // cpt, ci % cpt
            return pltpu.make_async_copy(tab_hbm.at[idx_v.at[tok, pl.ds(c * CH, CH)]],
                                         rows_v.at[slot], sem.at[slot])

        def compute(ci, slot):
            tok, c = ci // cpt, ci % cpt
            for eg in range(CH // L):
                @pl.loop(0, W // (DG * L))
                def _(dg):
                    off = dg * (DG * L)
                    cont = jnp.full((L,), dg, I32) != 0
                    zero = jnp.zeros((L,), F32)
                    accs = [jnp.where(cont, acc_v[kk, :], zero) for kk in range(L)]
                    for j in range(0, DG, 2):
                        xa = _as_bf16(x_v[tok, pl.ds(off + j * L, L)])
                        xb = _as_bf16(x_v[tok, pl.ds(off + (j + 1) * L, L)])
                        for kk in range(L):
                            ra = _as_bf16(rows_v[slot, eg * L + kk, pl.ds(off + j * L, L)])
                            rb = _as_bf16(rows_v[slot, eg * L + kk, pl.ds(off + (j + 1) * L, L)])
                            lo, hi = _halves_f32(ra * xa + rb * xb)
                            accs[kk] = accs[kk] + lo + hi
                    for kk in range(L):
                        acc_v[kk, :] = accs[kk]

                s = jnp.zeros((L,), F32)
                for j in range(L):
                    s = s + plsc.load_gather(acc_v, [lane, jnp.full((L,), j, I32)])
                out_v[tok, pl.ds(c * CH + eg * L, L)] = s

        @pl.loop(0, tpw // TG)
        def _(g):
            t0 = base + g * TG
            pltpu.sync_copy(x_hbm.at[pl.ds(t0, TG)], x_v)
            pltpu.sync_copy(idx_hbm.at[pl.ds(t0, TG)], idx_v)
            _sc_pipeline(nchunk, gather, compute)
            pltpu.sync_copy(out_v, out_hbm.at[pl.ds(t0, TG)])

    return _sc_kernel(
        body, jax.ShapeDtypeStruct((T, HK), F32),
        [pltpu.VMEM((TG, W), I32), pltpu.VMEM((TG, HK), I32), pltpu.VMEM((TG, HK), F32),
         pltpu.VMEM((SC_NBUF, CH, W), I32), pltpu.VMEM((L, L), F32),
         pltpu.SemaphoreType.DMA((SC_NBUF,))],
    )(xp, experts, ptab)


def _sc_wsum(wp, experts, ptab):
    T, HK = wp.shape
    W = ptab.shape[1]
    D = 2 * W
    L, TG, CH = SC_LANES, SC_TOKENS, SC_CHUNK
    tpw = T // SC_WORKERS
    cpt = HK // CH
    nchunk = TG * cpt
    DG = 8
    assert T % (SC_WORKERS * TG) == 0 and HK % CH == 0 and CH % 2 == 0 and W % (DG * L) == 0
    assert nchunk >= SC_NBUF

    def body(w_hbm, idx_hbm, tab_hbm, out_hbm, w_v, idx_v, y_v, rows_v, sem):
        base = _sc_worker_base(tpw)

        def gather(ci, slot):
            tok, c = ci // cpt, ci % cpt
            return pltpu.make_async_copy(tab_hbm.at[idx_v.at[tok, pl.ds(c * CH, CH)]],
                                         rows_v.at[slot], sem.at[slot])

        def compute(ci, slot):
            tok, c = ci // cpt, ci % cpt
            tokv = jnp.full((L,), tok, I32)
            cont = jnp.full((L,), c, I32) != 0

            @pl.loop(0, W // (DG * L))
            def _(dg):
                off = dg * (DG * L)
                zero = jnp.zeros((L,), F32)
                alo = [jnp.where(cont, y_v[tok, pl.ds(off + j * L, L)], zero) for j in range(DG)]
                ahi = [jnp.where(cont, y_v[tok, pl.ds(W + off + j * L, L)], zero) for j in range(DG)]
                for kk in range(0, CH, 2):
                    wa = _as_bf16(plsc.load_gather(w_v, [tokv, jnp.full((L,), c * CH + kk, I32)]))
                    wb = _as_bf16(plsc.load_gather(w_v, [tokv, jnp.full((L,), c * CH + kk + 1, I32)]))
                    for j in range(DG):
                        ra = _as_bf16(rows_v[slot, kk, pl.ds(off + j * L, L)])
                        rb = _as_bf16(rows_v[slot, kk + 1, pl.ds(off + j * L, L)])
                        lo, hi = _halves_f32(wa * ra + wb * rb)
                        alo[j] = alo[j] + lo
                        ahi[j] = ahi[j] + hi
                for j in range(DG):
                    y_v[tok, pl.ds(off + j * L, L)] = alo[j]
                    y_v[tok, pl.ds(W + off + j * L, L)] = ahi[j]

        @pl.loop(0, tpw // TG)
        def _(g):
            t0 = base + g * TG
            pltpu.sync_copy(w_hbm.at[pl.ds(t0, TG)], w_v)
            pltpu.sync_copy(idx_hbm.at[pl.ds(t0, TG)], idx_v)
            _sc_pipeline(nchunk, gather, compute)
            pltpu.sync_copy(y_v, out_hbm.at[pl.ds(t0, TG)])

    return _sc_kernel(
        body, jax.ShapeDtypeStruct((T, D), F32),
        [pltpu.VMEM((TG, HK), I32), pltpu.VMEM((TG, HK), I32), pltpu.VMEM((TG, D), F32),
         pltpu.VMEM((SC_NBUF, CH, W), I32), pltpu.SemaphoreType.DMA((SC_NBUF,))],
    )(wp, experts, ptab)


def _act_kernel(dots_ref, gate_ref, w_ref):
    w_ref[...] = _pack_splat(_gelu(dots_ref[...]) * gate_ref[...])


def _act(dots, gates):
    T, hk = dots.shape
    tt = min(ROW_TILE, T)
    spec = pl.BlockSpec((tt, hk), lambda i: (i, 0))
    return pl.pallas_call(
        _act_kernel, grid=(T // tt,), in_specs=[spec, spec], out_specs=spec,
        out_shape=jax.ShapeDtypeStruct((T, hk), I32),
        compiler_params=pltpu.CompilerParams(dimension_semantics=("arbitrary",)),
    )(dots, gates)


def _final_kernel(h_ref, y_ref, g_ref, *rest):
    o_ref = rest[-1]
    o_ref[...] = _rmsnorm(h_ref[...] + y_ref[...], g_ref[...])


def _final(h, y, gain, out, row0, total_rows):
    T, D = h.shape
    tt = min(ROW_TILE, T)
    assert row0 % tt == 0
    spec = pl.BlockSpec((tt, D), lambda i: (i, 0))
    out_spec = pl.BlockSpec((tt, D), lambda i: (i + row0 // tt, 0))
    in_specs = [spec, spec, pl.BlockSpec((1, D), lambda i: (0, 0))]
    args = (h, y, gain)
    aliases = {}
    if out is not None:
        in_specs.append(pl.BlockSpec(memory_space=pl.ANY))
        args += (out,)
        aliases = {3: 0}
    return pl.pallas_call(
        _final_kernel, grid=(T // tt,), in_specs=in_specs, out_specs=out_spec,
        out_shape=jax.ShapeDtypeStruct((total_rows, D), F32),
        input_output_aliases=aliases,
        compiler_params=pltpu.CompilerParams(dimension_semantics=("arbitrary",)),
    )(*args)


def kernel(x, mem, norm1_gain, w_in, pool_w, pool_scale, sgu_ln_gain, sgu_ln_bias, sgu_w_s, sgu_b_s,
           sgu_w_out, mem_norm_gain, xa_w_kv, xa_w_out, w_out, norm2_gain, peer_w_q, peer_keys1,
           peer_keys2, peer_u, peer_v, final_norm_gain):
    B, S, D = x.shape
    depth = w_in.shape[0]
    lo = lambda w: w.astype(MXU_DTYPE)
    row = lambda w: w.reshape(1, -1)
    assert depth == 1, "the final RMSNorm is fused with the last layer's residual add"
    l = 0
    weights = (row(norm1_gain[l]), lo(w_in[l]), lo(pool_w[l]), row(pool_scale[l]),
               row(sgu_ln_gain[l]), row(sgu_ln_bias[l]), sgu_w_s[l], sgu_b_s[l].T,
               lo(sgu_w_out[l]), lo(xa_w_out[l]), lo(w_out[l]))
    select_w = (row(norm2_gain[l]), lo(peer_w_q[l]), lo(peer_keys1[l]), lo(peer_keys2[l]))
    u_packed, v_packed = _pack_table(peer_u[l]), _pack_table(peer_v[l])
    k, v = _memkv(mem, row(mem_norm_gain[l]), lo(xa_w_kv[l]))
    chunks = BATCH_CHUNKS if sum(BATCH_CHUNKS) == B else (B,)
    out, b0 = None, 0
    for nb in chunks:
        h = _mixer(x, k, v, weights, b0, nb).reshape(nb * S, D)
        xn, experts, gates = _select(h, *select_w)
        dots = _sc_dots(xn, experts, u_packed)
        y = _sc_wsum(_act(dots, gates), experts, v_packed)
        out = _final(h, y, row(final_norm_gain), out, b0 * S, B * S)
        b0 += nb
    return out.reshape(B, S, D)
```

```python
import functools
import math

import jax
import jax.numpy as jnp
from jax import lax
from jax.experimental import pallas as pl
from jax.experimental.pallas import tpu as pltpu
from jax.experimental.pallas import tpu_sc as plsc

F32 = jnp.float32
I32 = jnp.int32
MXU_DTYPE = jnp.bfloat16

RMS_EPS = 1e-6
LN_EPS = 1e-5
POOL_WINDOWS = (2, 4, 8, 16)
POOL_HALO = 16
SGU_CHUNK = 128
SGU_HEADS = 8
XA_HEADS = 4
PEER_HEADS = 8
PEER_N_KEYS = 128
PEER_TOPK = 16

V7X_VMEM_BYTES = 64 * 1024 * 1024
VMEM_LIMIT_BYTES = 56 * 1024 * 1024

SEQ_TILE = 256
SELECT_TILE = 256
ROW_TILE = 512
TC_WSUM_TILE = 32
TC_LANES = 128
BATCH_CHUNKS = (1, 1, 2, 2, 2)

SC_CORES = 2
SC_SUBCORES = 16
SC_LANES = 16
SC_WORKERS = SC_CORES * SC_SUBCORES
SC_TOKENS = 16
SC_CHUNK = 32
SC_NBUF = 4
HIGH_HALF = -65536
SC_ROW_QUANTUM = SC_WORKERS * SC_TOKENS
TC_WSUM_SHARE = (3, 8)


def _rmsnorm(x, gain):
    return x * lax.rsqrt(jnp.mean(x * x, axis=-1, keepdims=True) + RMS_EPS) * gain


def _gelu(x):
    return 0.5 * x * (1.0 + lax.erf(x * (1.0 / math.sqrt(2.0))))


def _mm(a, b):
    return jnp.dot(a.astype(MXU_DTYPE), b.astype(MXU_DTYPE), preferred_element_type=F32)


def _mm_nt(a, b):
    return lax.dot_general(a.astype(MXU_DTYPE), b.astype(MXU_DTYPE),
                           (((1,), (1,)), ((), ())), preferred_element_type=F32)


def _memkv_kernel(mem_ref, gain_ref, wkv_ref, k_ref, v_ref):
    width = k_ref.shape[-1]
    kv = _mm(_rmsnorm(mem_ref[0], gain_ref[...]), wkv_ref[...])
    k_ref[0] = kv[:, :width].astype(k_ref.dtype)
    v_ref[0] = kv[:, width:].astype(v_ref.dtype)


def _memkv(mem, gain, w_kv):
    B, M, D = mem.shape
    width = w_kv.shape[1] // 2
    const = lambda b: (0, 0)
    return pl.pallas_call(
        _memkv_kernel,
        grid=(B,),
        in_specs=[pl.BlockSpec((1, M, D), lambda b: (b, 0, 0)),
                  pl.BlockSpec((1, D), const),
                  pl.BlockSpec(w_kv.shape, const)],
        out_specs=[pl.BlockSpec((1, M, width), lambda b: (b, 0, 0))] * 2,
        out_shape=[jax.ShapeDtypeStruct((B, M, width), MXU_DTYPE)] * 2,
        compiler_params=pltpu.CompilerParams(dimension_semantics=("arbitrary",),
                                             vmem_limit_bytes=VMEM_LIMIT_BYTES),
    )(mem, gain, w_kv)


def _mixer_kernel(x_ref, k_ref, v_ref, n1g_ref, win_ref, poolw_ref, pools_ref, lng_ref, lnb_ref,
                  ws_ref, bst_ref, sguwo_ref, xawo_ref, wout_ref, h_ref, tail_ref):
    ts, d = x_ref.shape[1], x_ref.shape[2]
    s_idx = pl.program_id(1)
    x = x_ref[0]
    nb = _rmsnorm(x, n1g_ref[...]).astype(MXU_DTYPE)

    def proj(col):
        return jnp.dot(nb, win_ref[:, col * d:(col + 1) * d], preferred_element_type=F32)

    @pl.when(s_idx == 0)
    def _():
        tail_ref[...] = jnp.zeros_like(tail_ref)

    p = proj(0)
    ext = jnp.concatenate([tail_ref[...], p], axis=0)
    tail_ref[...] = p[ts - POOL_HALO:, :]
    pos = s_idx * ts + lax.broadcasted_iota(jnp.int32, (ts, 1), 0)
    group = d // len(POOL_WINDOWS)
    y_pool = []
    for g, w in enumerate(POOL_WINDOWS):
        acc = ext[:, g * group:(g + 1) * group]
        shift = 1
        while shift < w:
            acc = acc + pltpu.roll(acc, shift, 0)
            shift *= 2
        count = jnp.minimum(pos + 1, w).astype(F32)
        diff = acc[POOL_HALO:, :] / count - p[:, g * group:(g + 1) * group]
        y_pool.append(_mm(diff, poolw_ref[g]))
    y_pool = jnp.concatenate(y_pool, axis=1) * pools_ref[...]

    u = _gelu(proj(1))
    v = _gelu(proj(2))
    mu = jnp.mean(v, axis=-1, keepdims=True)
    vc = v - mu
    var = jnp.mean(vc * vc, axis=-1, keepdims=True)
    v = (vc * lax.rsqrt(var + LN_EPS) * lng_ref[...] + lnb_ref[...]).astype(MXU_DTYPE)
    hd = d // SGU_HEADS
    causal = (lax.broadcasted_iota(jnp.int32, (SGU_CHUNK, SGU_CHUNK), 0)
              >= lax.broadcasted_iota(jnp.int32, (SGU_CHUNK, SGU_CHUNK), 1))
    mixed_rows = []
    w_masked = [jnp.where(causal, ws_ref[h], 0.0).astype(MXU_DTYPE) for h in range(SGU_HEADS)]
    for c in range(ts // SGU_CHUNK):
        rows = slice(c * SGU_CHUNK, (c + 1) * SGU_CHUNK)
        heads = []
        for h in range(SGU_HEADS):
            mixed = jnp.dot(w_masked[h], v[rows, h * hd:(h + 1) * hd], preferred_element_type=F32)
            heads.append(mixed + bst_ref[:, h:h + 1])
        mixed_rows.append(jnp.concatenate(heads, axis=1))
    mixed = jnp.concatenate(mixed_rows, axis=0) if len(mixed_rows) > 1 else mixed_rows[0]
    y_sgu = _mm(u * mixed, sguwo_ref[...])

    q = proj(3).astype(MXU_DTYPE)
    xd = d // XA_HEADS
    outs = []
    for h in range(XA_HEADS):
        cols = slice(h * xd, (h + 1) * xd)
        s = _mm_nt(q[:, cols], k_ref[0, :, cols]) * (xd ** -0.5)
        e = jnp.exp(s - jnp.max(s, axis=-1, keepdims=True))
        probs = e / jnp.sum(e, axis=-1, keepdims=True)
        outs.append(_mm(probs, v_ref[0, :, cols]))
    y_xa = _mm(jnp.concatenate(outs, axis=1), xawo_ref[...])

    merged = (jax.nn.sigmoid(proj(4)) * y_pool + jax.nn.sigmoid(proj(5)) * y_sgu
              + jax.nn.sigmoid(proj(6)) * y_xa)
    h_ref[0] = x + _mm(merged, wout_ref[...])


def _resident(shape):
    zeros = (0,) * len(shape)
    return pl.BlockSpec(shape, lambda *_: zeros, pipeline_mode=pl.Buffered(1))


def _mixer(x, k, v, weights, b0, nb):
    _, S, D = x.shape
    ts = min(SEQ_TILE, S)
    M = k.shape[1]
    return pl.pallas_call(
        _mixer_kernel,
        grid=(nb, S // ts),
        in_specs=[pl.BlockSpec((1, ts, D), lambda b, s: (b + b0, s, 0)),
                  pl.BlockSpec((1, M, k.shape[2]), lambda b, s: (b + b0, 0, 0)),
                  pl.BlockSpec((1, M, v.shape[2]), lambda b, s: (b + b0, 0, 0))]
                 + [_resident(w.shape) for w in weights],
        out_specs=pl.BlockSpec((1, ts, D), lambda b, s: (b, s, 0)),
        out_shape=jax.ShapeDtypeStruct((nb, S, D), F32),
        scratch_shapes=[pltpu.VMEM((POOL_HALO, D), F32)],
        compiler_params=pltpu.CompilerParams(dimension_semantics=("arbitrary", "arbitrary"),
                                             vmem_limit_bytes=VMEM_LIMIT_BYTES),
    )(x, k, v, *weights)


def _topk_rows(vals, payload=None):
    n_rows = vals.shape[0]
    row = lax.broadcasted_iota(jnp.int32, vals.shape, 0).astype(F32)
    out_v, out_i = [], []
    for _ in range(PEER_TOPK):
        m = jnp.max(vals, axis=0, keepdims=True)
        first = jnp.min(jnp.where(vals == m, row, float(n_rows)), axis=0, keepdims=True)
        sel = row == first
        out_v.append(m)
        if payload is None:
            out_i.append(first)
        else:
            out_i.append(jnp.max(jnp.where(sel, payload, -1.0), axis=0, keepdims=True))
        vals = jnp.where(sel, -jnp.inf, vals)
    return jnp.concatenate(out_v, axis=0), jnp.concatenate(out_i, axis=0)


def _candidates(v1, i1, v2, i2):
    sub = 8
    jrow = lax.broadcasted_iota(jnp.int32, (sub, 1), 0)
    vals = [v1[0:1] + v2]
    ids = [i1[0:1] * float(PEER_N_KEYS) + i2]
    for i in range(1, sub):
        keep = jrow < (PEER_TOPK // (i + 1))
        vals.append(jnp.where(keep, v1[i:i + 1] + v2[0:sub], -jnp.inf))
        ids.append(i1[i:i + 1] * float(PEER_N_KEYS) + i2[0:sub])
    vals.append(v1[sub:] + v2[0:1])
    ids.append(i1[sub:] * float(PEER_N_KEYS) + i2[0:1])
    return jnp.concatenate(vals, axis=0), jnp.concatenate(ids, axis=0)


def _select_kernel(h_ref, n2g_ref, wq_ref, k1_ref, k2_ref, xn_ref, exp_ref, gate_ref):
    xn = _rmsnorm(h_ref[...], n2g_ref[...])
    xn_ref[...] = _pack_halves(xn)
    q = _mm(xn, wq_ref[...])
    half = k1_ref.shape[1]
    experts, gates = [], []
    for h in range(PEER_HEADS):
        q1 = q[:, (2 * h) * half:(2 * h + 1) * half]
        q2 = q[:, (2 * h + 1) * half:(2 * h + 2) * half]
        v1, i1 = _topk_rows(_mm_nt(k1_ref[...], q1))
        v2, i2 = _topk_rows(_mm_nt(k2_ref[...], q2))
        best, eid = _topk_rows(*_candidates(v1, i1, v2, i2))
        e = jnp.exp(best - jnp.max(best, axis=0, keepdims=True))
        gates.append(e / jnp.sum(e, axis=0, keepdims=True))
        experts.append(eid)
    exp_ref[...] = jnp.concatenate(experts, axis=0).T.astype(jnp.int32)
    gate_ref[...] = jnp.concatenate(gates, axis=0).T


def _select(h, n2g, w_q, keys1, keys2):
    T, D = h.shape
    tt = min(SELECT_TILE, T)
    hk = PEER_HEADS * PEER_TOPK
    return pl.pallas_call(
        _select_kernel,
        grid=(T // tt,),
        in_specs=[pl.BlockSpec((tt, D), lambda i: (i, 0)),
                  _resident(n2g.shape), _resident(w_q.shape),
                  _resident(keys1.shape), _resident(keys2.shape)],
        out_specs=[pl.BlockSpec((tt, D // 2), lambda i: (i, 0)),
                   pl.BlockSpec((tt, hk), lambda i: (i, 0)),
                   pl.BlockSpec((tt, hk), lambda i: (i, 0))],
        out_shape=[jax.ShapeDtypeStruct((T, D // 2), I32),
                   jax.ShapeDtypeStruct((T, hk), jnp.int32),
                   jax.ShapeDtypeStruct((T, hk), F32)],
        compiler_params=pltpu.CompilerParams(dimension_semantics=("arbitrary",),
                                             vmem_limit_bytes=VMEM_LIMIT_BYTES),
    )(h, n2g, w_q, keys1, keys2)


def _pack_table(table):
    half = table.shape[1] // 2
    bits = lax.bitcast_convert_type(table.astype(jnp.bfloat16), jnp.uint16).astype(jnp.uint32)
    return lax.bitcast_convert_type((bits[:, half:] << 16) | bits[:, :half], I32)


def _bf16_bits(x):
    return lax.bitcast_convert_type(x.astype(jnp.bfloat16).astype(F32), I32)


def _pack_halves(x):
    half = x.shape[1] // 2
    return (_bf16_bits(x[:, half:]) & HIGH_HALF) | lax.shift_right_logical(_bf16_bits(x[:, :half]), 16)


def _pack_splat(x):
    bits = _bf16_bits(x)
    return (bits & HIGH_HALF) | lax.shift_right_logical(bits, 16)


def _unpack_words(w):
    lo = lax.bitcast_convert_type(lax.shift_left(w, jnp.full(w.shape, 16, I32)), F32)
    hi = lax.bitcast_convert_type(lax.bitwise_and(w, jnp.full(w.shape, HIGH_HALF, I32)), F32)
    return lo, hi


def _as_bf16(w):
    return plsc.bitcast(w, jnp.bfloat16)


def _halves_f32(p):
    return _unpack_words(plsc.bitcast(p, I32))


def _sc_kernel(body, out_type, scratch_types):
    mesh = plsc.VectorSubcoreMesh(core_axis_name="c", subcore_axis_name="s")
    return pl.kernel(body, out_type=out_type, mesh=mesh, scratch_types=scratch_types,
                     compiler_params=pltpu.CompilerParams(needs_layout_passes=False))


def _sc_worker_base(tokens_per_worker):
    return (lax.axis_index("s") * SC_CORES + lax.axis_index("c")) * tokens_per_worker


def _sc_pipeline(nchunk, gather, compute):
    ahead = SC_NBUF - 1
    for i in range(ahead):
        gather(i, i).start()

    @pl.loop(0, nchunk)
    def _(ci):
        @pl.when(ci + ahead < nchunk)
        def _():
            gather(ci + ahead, (ci + ahead) % SC_NBUF).start()

        slot = ci % SC_NBUF
        gather(ci, slot).wait()
        compute(ci, slot)


def _sc_dots(xp, experts, ptab):
    T, W = xp.shape
    HK = experts.shape[1]
    L, TG, CH = SC_LANES, SC_TOKENS, SC_CHUNK
    tpw = T // SC_WORKERS
    cpt = HK // CH
    nchunk = TG * cpt
    DG = 8
    assert W == ptab.shape[1] and T % (SC_WORKERS * TG) == 0 and HK % CH == 0 and CH % L == 0
    assert W % (DG * L) == 0 and DG % 2 == 0 and nchunk >= SC_NBUF

    def body(x_hbm, idx_hbm, tab_hbm, out_hbm, x_v, idx_v, out_v, rows_v, acc_v, sem):
        base = _sc_worker_base(tpw)
        lane = lax.iota(I32, L)

        def gather(ci, slot):
            tok, c = ci // cpt, ci % cpt
            return pltpu.make_async_copy(tab_hbm.at[idx_v.at[tok, pl.ds(c * CH, CH)]],
                                         rows_v.at[slot], sem.at[slot])

        def compute(ci, slot):
            tok, c = ci // cpt, ci % cpt
            for eg in range(CH // L):
                @pl.loop(0, W // (DG * L))
                def _(dg):
                    off = dg * (DG * L)
                    cont = jnp.full((L,), dg, I32) != 0
                    zero = jnp.zeros((L,), F32)
                    accs = [jnp.where(cont, acc_v[kk, :], zero) for kk in range(L)]
                    for j in range(0, DG, 2):
                        xa = _as_bf16(x_v[tok, pl.ds(off + j * L, L)])
                        xb = _as_bf16(x_v[tok, pl.ds(off + (j + 1) * L, L)])
                        for kk in range(L):
                            ra = _as_bf16(rows_v[slot, eg * L + kk, pl.ds(off + j * L, L)])
                            rb = _as_bf16(rows_v[slot, eg * L + kk, pl.ds(off + (j + 1) * L, L)])
                            lo, hi = _halves_f32(ra * xa + rb * xb)
                            accs[kk] = accs[kk] + lo + hi
                    for kk in range(L):
                        acc_v[kk, :] = accs[kk]

                s = jnp.zeros((L,), F32)
                for j in range(L):
                    s = s + plsc.load_gather(acc_v, [lane, jnp.full((L,), j, I32)])
                out_v[tok, pl.ds(c * CH + eg * L, L)] = s

        @pl.loop(0, tpw // TG)
        def _(g):
            t0 = base + g * TG
            pltpu.sync_copy(x_hbm.at[pl.ds(t0, TG)], x_v)
            pltpu.sync_copy(idx_hbm.at[pl.ds(t0, TG)], idx_v)
            _sc_pipeline(nchunk, gather, compute)
            pltpu.sync_copy(out_v, out_hbm.at[pl.ds(t0, TG)])

    return _sc_kernel(
        body, jax.ShapeDtypeStruct((T, HK), F32),
        [pltpu.VMEM((TG, W), I32), pltpu.VMEM((TG, HK), I32), pltpu.VMEM((TG, HK), F32),
         pltpu.VMEM((SC_NBUF, CH, W), I32), pltpu.VMEM((L, L), F32),
         pltpu.SemaphoreType.DMA((SC_NBUF,))],
    )(xp, experts, ptab)


def _sc_wsum(wp, experts, ptab):
    T, HK = wp.shape
    W = ptab.shape[1]
    D = 2 * W
    L, TG, CH = SC_LANES, SC_TOKENS, SC_CHUNK
    tpw = T // SC_WORKERS
    cpt = HK // CH
    nchunk = TG * cpt
    DG = 8
    assert T % (SC_WORKERS * TG) == 0 and HK % CH == 0 and CH % 2 == 0 and W % (DG * L) == 0
    assert nchunk >= SC_NBUF

    def body(w_hbm, idx_hbm, tab_hbm, out_hbm, w_v, idx_v, y_v, rows_v, sem):
        base = _sc_worker_base(tpw)

        def gather(ci, slot):
            tok, c = ci // cpt, ci % cpt
            return pltpu.make_async_copy(tab_hbm.at[idx_v.at[tok, pl.ds(c * CH, CH)]],
                                         rows_v.at[slot], sem.at[slot])

        def compute(ci, slot):
            tok, c = ci // cpt, ci % cpt
            tokv = jnp.full((L,), tok, I32)
            cont = jnp.full((L,), c, I32) != 0

            @pl.loop(0, W // (DG * L))
            def _(dg):
                off = dg * (DG * L)
                zero = jnp.zeros((L,), F32)
                alo = [jnp.where(cont, y_v[tok, pl.ds(off + j * L, L)], zero) for j in range(DG)]
                ahi = [jnp.where(cont, y_v[tok, pl.ds(W + off + j * L, L)], zero) for j in range(DG)]
                for kk in range(0, CH, 2):
                    wa = _as_bf16(plsc.load_gather(w_v, [tokv, jnp.full((L,), c * CH + kk, I32)]))
                    wb = _as_bf16(plsc.load_gather(w_v, [tokv, jnp.full((L,), c * CH + kk + 1, I32)]))
                    for j in range(DG):
                        ra = _as_bf16(rows_v[slot, kk, pl.ds(off + j * L, L)])
                        rb = _as_bf16(rows_v[slot, kk + 1, pl.ds(off + j * L, L)])
                        lo, hi = _halves_f32(wa * ra + wb * rb)
                        alo[j] = alo[j] + lo
                        ahi[j] = ahi[j] + hi
                for j in range(DG):
                    y_v[tok, pl.ds(off + j * L, L)] = alo[j]
                    y_v[tok, pl.ds(W + off + j * L, L)] = ahi[j]

        @pl.loop(0, tpw // TG)
        def _(g):
            t0 = base + g * TG
            pltpu.sync_copy(w_hbm.at[pl.ds(t0, TG)], w_v)
            pltpu.sync_copy(idx_hbm.at[pl.ds(t0, TG)], idx_v)
            _sc_pipeline(nchunk, gather, compute)
            pltpu.sync_copy(y_v, out_hbm.at[pl.ds(t0, TG)])

    return _sc_kernel(
        body, jax.ShapeDtypeStruct((T, D), F32),
        [pltpu.VMEM((TG, HK), I32), pltpu.VMEM((TG, HK), I32), pltpu.VMEM((TG, D), F32),
         pltpu.VMEM((SC_NBUF, CH, W), I32), pltpu.SemaphoreType.DMA((SC_NBUF,))],
    )(wp, experts, ptab)


def _tc_wsum_kernel(exp_ref, w_ref, tab_ref, y_ref):
    tb, hk = exp_ref.shape
    sub = tab_ref.shape[1]

    def token(t, carry):
        lo_acc = jnp.zeros(tab_ref.shape[1:], F32)
        hi_acc = jnp.zeros(tab_ref.shape[1:], F32)
        for k in range(hk):
            row = tab_ref[exp_ref[t, k]]
            w = w_ref[t, k]
            lo_acc = lo_acc + w * lax.bitcast_convert_type(lax.shift_left(row, 16), F32)
            hi_acc = hi_acc + w * lax.bitcast_convert_type(row & HIGH_HALF, F32)
        y_ref[t, pl.ds(0, sub), :] = lo_acc
        y_ref[t, pl.ds(sub, sub), :] = hi_acc
        return carry

    lax.fori_loop(0, tb, token, 0)


def _tc_wsum(w, experts, ptab):
    T, hk = w.shape
    E, W = ptab.shape
    sub = W // TC_LANES
    tb = min(TC_WSUM_TILE, T)
    smem = lambda: pl.BlockSpec((tb, hk), lambda i: (i, 0), memory_space=pltpu.SMEM)
    y = pl.pallas_call(
        _tc_wsum_kernel, grid=(T // tb,),
        in_specs=[smem(), smem(), _resident((E, sub, TC_LANES))],
        out_specs=pl.BlockSpec((tb, 2 * sub, TC_LANES), lambda i: (i, 0, 0)),
        out_shape=jax.ShapeDtypeStruct((T, 2 * sub, TC_LANES), F32),
        compiler_params=pltpu.CompilerParams(dimension_semantics=("arbitrary",),
                                             vmem_limit_bytes=VMEM_LIMIT_BYTES),
    )(experts, w, ptab.reshape(E, sub, TC_LANES))
    return y.reshape(T, 2 * W)


def _act_kernel(dots_ref, gate_ref, w_ref, wp_ref):
    w = _gelu(dots_ref[...]) * gate_ref[...]
    w_ref[...] = w
    wp_ref[...] = _pack_splat(w)


def _act(dots, gates):
    T, hk = dots.shape
    tt = min(ROW_TILE, T)
    spec = pl.BlockSpec((tt, hk), lambda i: (i, 0))
    return pl.pallas_call(
        _act_kernel, grid=(T // tt,), in_specs=[spec, spec], out_specs=[spec, spec],
        out_shape=[jax.ShapeDtypeStruct((T, hk), F32), jax.ShapeDtypeStruct((T, hk), I32)],
        compiler_params=pltpu.CompilerParams(dimension_semantics=("arbitrary",)),
    )(dots, gates)


def _final_kernel(h_ref, y_ref, g_ref, *rest):
    o_ref = rest[-1]
    o_ref[...] = _rmsnorm(h_ref[...] + y_ref[...], g_ref[...])


def _final(h, y, gain, out, row0, total_rows):
    T, D = h.shape
    tt = min(ROW_TILE, T)
    assert row0 % tt == 0
    spec = pl.BlockSpec((tt, D), lambda i: (i, 0))
    out_spec = pl.BlockSpec((tt, D), lambda i: (i + row0 // tt, 0))
    in_specs = [spec, spec, pl.BlockSpec((1, D), lambda i: (0, 0))]
    args = (h, y, gain)
    aliases = {}
    if out is not None:
        in_specs.append(pl.BlockSpec(memory_space=pl.ANY))
        args += (out,)
        aliases = {3: 0}
    return pl.pallas_call(
        _final_kernel, grid=(T // tt,), in_specs=in_specs, out_specs=out_spec,
        out_shape=jax.ShapeDtypeStruct((total_rows, D), F32),
        input_output_aliases=aliases,
        compiler_params=pltpu.CompilerParams(dimension_semantics=("arbitrary",)),
    )(*args)


def kernel(x, mem, norm1_gain, w_in, pool_w, pool_scale, sgu_ln_gain, sgu_ln_bias, sgu_w_s, sgu_b_s,
           sgu_w_out, mem_norm_gain, xa_w_kv, xa_w_out, w_out, norm2_gain, peer_w_q, peer_keys1,
           peer_keys2, peer_u, peer_v, final_norm_gain):
    B, S, D = x.shape
    depth = w_in.shape[0]
    lo = lambda w: w.astype(MXU_DTYPE)
    row = lambda w: w.reshape(1, -1)
    assert depth == 1, "the final RMSNorm is fused with the last layer's residual add"
    l = 0
    weights = (row(norm1_gain[l]), lo(w_in[l]), lo(pool_w[l]), row(pool_scale[l]),
               row(sgu_ln_gain[l]), row(sgu_ln_bias[l]), sgu_w_s[l], sgu_b_s[l].T,
               lo(sgu_w_out[l]), lo(xa_w_out[l]), lo(w_out[l]))
    select_w = (row(norm2_gain[l]), lo(peer_w_q[l]), lo(peer_keys1[l]), lo(peer_keys2[l]))
    u_packed, v_packed = _pack_table(peer_u[l]), _pack_table(peer_v[l])
    k, v = _memkv(mem, row(mem_norm_gain[l]), lo(xa_w_kv[l]))
    fgain = row(final_norm_gain)
    chunks = BATCH_CHUNKS if sum(BATCH_CHUNKS) == B else (B,)
    out, b0 = None, 0
    for nb in chunks:
        rows = nb * S
        h = _mixer(x, k, v, weights, b0, nb).reshape(rows, D)
        xn, experts, gates = _select(h, *select_w)
        w, wp = _act(_sc_dots(xn, experts, u_packed), gates)
        n_tc = (rows * TC_WSUM_SHARE[0] // TC_WSUM_SHARE[1]) // SC_ROW_QUANTUM * SC_ROW_QUANTUM
        y_tc = _tc_wsum(w[:n_tc], experts[:n_tc], v_packed)
        y_sc = _sc_wsum(wp[n_tc:], experts[n_tc:], v_packed)
        out = _final(h[:n_tc], y_tc, fgain, out, b0 * S, B * S)
        out = _final(h[n_tc:], y_sc, fgain, out, b0 * S + n_tc, B * S)
        b0 += nb
    return out.reshape(B, S, D)
```

```python
import functools
import math

import jax
import jax.numpy as jnp
from jax import lax
from jax.experimental import pallas as pl
from jax.experimental.pallas import tpu as pltpu
from jax.experimental.pallas import tpu_sc as plsc

F32 = jnp.float32
I32 = jnp.int32
MXU_DTYPE = jnp.bfloat16

RMS_EPS = 1e-6
LN_EPS = 1e-5
POOL_WINDOWS = (2, 4, 8, 16)
POOL_HALO = 16
SGU_CHUNK = 128
SGU_HEADS = 8
XA_HEADS = 4
PEER_HEADS = 8
PEER_N_KEYS = 128
PEER_TOPK = 16

V7X_VMEM_BYTES = 64 * 1024 * 1024
VMEM_LIMIT_BYTES = 56 * 1024 * 1024

SEQ_TILE = 256
SELECT_TILE = 256
ROW_TILE = 512
TC_WSUM_TILE = 32
TC_LANES = 128
BATCH_CHUNKS = (1, 1, 2, 2, 2)

SC_CORES = 2
SC_SUBCORES = 16
SC_LANES = 16
SC_WORKERS = SC_CORES * SC_SUBCORES
SC_TOKENS = 16
SC_CHUNK = 32
SC_NBUF = 4
HIGH_HALF = -65536
SC_ROW_QUANTUM = SC_WORKERS * SC_TOKENS
TC_WSUM_SHARE = (3, 8)


def _rmsnorm(x, gain):
    return x * lax.rsqrt(jnp.mean(x * x, axis=-1, keepdims=True) + RMS_EPS) * gain


def _gelu(x):
    return 0.5 * x * (1.0 + lax.erf(x * (1.0 / math.sqrt(2.0))))


def _mm(a, b):
    return jnp.dot(a.astype(MXU_DTYPE), b.astype(MXU_DTYPE), preferred_element_type=F32)


def _mm_nt(a, b):
    return lax.dot_general(a.astype(MXU_DTYPE), b.astype(MXU_DTYPE),
                           (((1,), (1,)), ((), ())), preferred_element_type=F32)


def _memkv_kernel(mem_ref, gain_ref, wkv_ref, k_ref, v_ref):
    width = k_ref.shape[-1]
    kv = _mm(_rmsnorm(mem_ref[0], gain_ref[...]), wkv_ref[...])
    k_ref[0] = kv[:, :width].astype(k_ref.dtype)
    v_ref[0] = kv[:, width:].astype(v_ref.dtype)


def _memkv(mem, gain, w_kv):
    B, M, D = mem.shape
    width = w_kv.shape[1] // 2
    const = lambda b: (0, 0)
    return pl.pallas_call(
        _memkv_kernel,
        grid=(B,),
        in_specs=[pl.BlockSpec((1, M, D), lambda b: (b, 0, 0)),
                  pl.BlockSpec((1, D), const),
                  pl.BlockSpec(w_kv.shape, const)],
        out_specs=[pl.BlockSpec((1, M, width), lambda b: (b, 0, 0))] * 2,
        out_shape=[jax.ShapeDtypeStruct((B, M, width), MXU_DTYPE)] * 2,
        compiler_params=pltpu.CompilerParams(dimension_semantics=("arbitrary",),
                                             vmem_limit_bytes=VMEM_LIMIT_BYTES),
    )(mem, gain, w_kv)


def _mixer_kernel(x_ref, k_ref, v_ref, n1g_ref, win_ref, poolw_ref, pools_ref, lng_ref, lnb_ref,
                  ws_ref, bst_ref, sguwo_ref, xawo_ref, wout_ref, h_ref, tail_ref):
    ts, d = x_ref.shape[1], x_ref.shape[2]
    s_idx = pl.program_id(1)
    x = x_ref[0]
    nb = _rmsnorm(x, n1g_ref[...]).astype(MXU_DTYPE)

    def proj(col):
        return jnp.dot(nb, win_ref[:, col * d:(col + 1) * d], preferred_element_type=F32)

    @pl.when(s_idx == 0)
    def _():
        tail_ref[...] = jnp.zeros_like(tail_ref)

    p = proj(0)
    ext = jnp.concatenate([tail_ref[...], p], axis=0)
    tail_ref[...] = p[ts - POOL_HALO:, :]
    pos = s_idx * ts + lax.broadcasted_iota(jnp.int32, (ts, 1), 0)
    group = d // len(POOL_WINDOWS)
    y_pool = []
    for g, w in enumerate(POOL_WINDOWS):
        acc = ext[:, g * group:(g + 1) * group]
        shift = 1
        while shift < w:
            acc = acc + pltpu.roll(acc, shift, 0)
            shift *= 2
        count = jnp.minimum(pos + 1, w).astype(F32)
        diff = acc[POOL_HALO:, :] / count - p[:, g * group:(g + 1) * group]
        y_pool.append(_mm(diff, poolw_ref[g]))
    y_pool = jnp.concatenate(y_pool, axis=1) * pools_ref[...]

    u = _gelu(proj(1))
    v = _gelu(proj(2))
    mu = jnp.mean(v, axis=-1, keepdims=True)
    vc = v - mu
    var = jnp.mean(vc * vc, axis=-1, keepdims=True)
    v = (vc * lax.rsqrt(var + LN_EPS) * lng_ref[...] + lnb_ref[...]).astype(MXU_DTYPE)
    hd = d // SGU_HEADS
    causal = (lax.broadcasted_iota(jnp.int32, (SGU_CHUNK, SGU_CHUNK), 0)
              >= lax.broadcasted_iota(jnp.int32, (SGU_CHUNK, SGU_CHUNK), 1))
    mixed_rows = []
    w_masked = [jnp.where(causal, ws_ref[h], 0.0).astype(MXU_DTYPE) for h in range(SGU_HEADS)]
    for c in range(ts // SGU_CHUNK):
        rows = slice(c * SGU_CHUNK, (c + 1) * SGU_CHUNK)
        heads = []
        for h in range(SGU_HEADS):
            mixed = jnp.dot(w_masked[h], v[rows, h * hd:(h + 1) * hd], preferred_element_type=F32)
            heads.append(mixed + bst_ref[:, h:h + 1])
        mixed_rows.append(jnp.concatenate(heads, axis=1))
    mixed = jnp.concatenate(mixed_rows, axis=0) if len(mixed_rows) > 1 else mixed_rows[0]
    y_sgu = _mm(u * mixed, sguwo_ref[...])

    q = proj(3).astype(MXU_DTYPE)
    xd = d // XA_HEADS
    outs = []
    for h in range(XA_HEADS):
        cols = slice(h * xd, (h + 1) * xd)
        s = _mm_nt(q[:, cols], k_ref[0, :, cols]) * (xd ** -0.5)
        e = jnp.exp(s - jnp.max(s, axis=-1, keepdims=True))
        probs = e / jnp.sum(e, axis=-1, keepdims=True)
        outs.append(_mm(probs, v_ref[0, :, cols]))
    y_xa = _mm(jnp.concatenate(outs, axis=1), xawo_ref[...])

    merged = (jax.nn.sigmoid(proj(4)) * y_pool + jax.nn.sigmoid(proj(5)) * y_sgu
              + jax.nn.sigmoid(proj(6)) * y_xa)
    h_ref[0] = x + _mm(merged, wout_ref[...])


def _resident(shape):
    zeros = (0,) * len(shape)
    return pl.BlockSpec(shape, lambda *_: zeros, pipeline_mode=pl.Buffered(1))


def _mixer(x, k, v, weights, b0, nb):
    _, S, D = x.shape
    ts = min(SEQ_TILE, S)
    M = k.shape[1]
    return pl.pallas_call(
        _mixer_kernel,
        grid=(nb, S // ts),
        in_specs=[pl.BlockSpec((1, ts, D), lambda b, s: (b + b0, s, 0)),
                  pl.BlockSpec((1, M, k.shape[2]), lambda b, s: (b + b0, 0, 0)),
                  pl.BlockSpec((1, M, v.shape[2]), lambda b, s: (b + b0, 0, 0))]
                 + [_resident(w.shape) for w in weights],
        out_specs=pl.BlockSpec((1, ts, D), lambda b, s: (b, s, 0)),
        out_shape=jax.ShapeDtypeStruct((nb, S, D), F32),
        scratch_shapes=[pltpu.VMEM((POOL_HALO, D), F32)],
        compiler_params=pltpu.CompilerParams(dimension_semantics=("arbitrary", "arbitrary"),
                                             vmem_limit_bytes=VMEM_LIMIT_BYTES),
    )(x, k, v, *weights)


def _topk_rows(vals, payload=None):
    n_rows = vals.shape[0]
    row = lax.broadcasted_iota(jnp.int32, vals.shape, 0).astype(F32)
    out_v, out_i = [], []
    for _ in range(PEER_TOPK):
        m = jnp.max(vals, axis=0, keepdims=True)
        first = jnp.min(jnp.where(vals == m, row, float(n_rows)), axis=0, keepdims=True)
        sel = row == first
        out_v.append(m)
        if payload is None:
            out_i.append(first)
        else:
            out_i.append(jnp.max(jnp.where(sel, payload, -1.0), axis=0, keepdims=True))
        vals = jnp.where(sel, -jnp.inf, vals)
    return jnp.concatenate(out_v, axis=0), jnp.concatenate(out_i, axis=0)


def _candidates(v1, i1, v2, i2):
    sub = 8
    jrow = lax.broadcasted_iota(jnp.int32, (sub, 1), 0)
    vals = [v1[0:1] + v2]
    ids = [i1[0:1] * float(PEER_N_KEYS) + i2]
    for i in range(1, sub):
        keep = jrow < (PEER_TOPK // (i + 1))
        vals.append(jnp.where(keep, v1[i:i + 1] + v2[0:sub], -jnp.inf))
        ids.append(i1[i:i + 1] * float(PEER_N_KEYS) + i2[0:sub])
    vals.append(v1[sub:] + v2[0:1])
    ids.append(i1[sub:] * float(PEER_N_KEYS) + i2[0:1])
    return jnp.concatenate(vals, axis=0), jnp.concatenate(ids, axis=0)


def _select_kernel(h_ref, n2g_ref, wq_ref, k1_ref, k2_ref, xn_ref, exp_ref, gate_ref):
    xn = _rmsnorm(h_ref[...], n2g_ref[...])
    xn_ref[...] = _pack_halves(xn)
    q = _mm(xn, wq_ref[...])
    half = k1_ref.shape[1]
    experts, gates = [], []
    for h in range(PEER_HEADS):
        q1 = q[:, (2 * h) * half:(2 * h + 1) * half]
        q2 = q[:, (2 * h + 1) * half:(2 * h + 2) * half]
        v1, i1 = _topk_rows(_mm_nt(k1_ref[...], q1))
        v2, i2 = _topk_rows(_mm_nt(k2_ref[...], q2))
        best, eid = _topk_rows(*_candidates(v1, i1, v2, i2))
        e = jnp.exp(best - jnp.max(best, axis=0, keepdims=True))
        gates.append(e / jnp.sum(e, axis=0, keepdims=True))
        experts.append(eid)
    exp_ref[...] = jnp.concatenate(experts, axis=0).T.astype(jnp.int32)
    gate_ref[...] = jnp.concatenate(gates, axis=0).T


def _select(h, n2g, w_q, keys1, keys2):
    T, D = h.shape
    tt = min(SELECT_TILE, T)
    hk = PEER_HEADS * PEER_TOPK
    return pl.pallas_call(
        _select_kernel,
        grid=(T // tt,),
        in_specs=[pl.BlockSpec((tt, D), lambda i: (i, 0)),
                  _resident(n2g.shape), _resident(w_q.shape),
                  _resident(keys1.shape), _resident(keys2.shape)],
        out_specs=[pl.BlockSpec((tt, D // 2), lambda i: (i, 0)),
                   pl.BlockSpec((tt, hk), lambda i: (i, 0)),
                   pl.BlockSpec((tt, hk), lambda i: (i, 0))],
        out_shape=[jax.ShapeDtypeStruct((T, D // 2), I32),
                   jax.ShapeDtypeStruct((T, hk), jnp.int32),
                   jax.ShapeDtypeStruct((T, hk), F32)],
        compiler_params=pltpu.CompilerParams(dimension_semantics=("arbitrary",),
                                             vmem_limit_bytes=VMEM_LIMIT_BYTES),
    )(h, n2g, w_q, keys1, keys2)


def _pack_table(table):
    half = table.shape[1] // 2
    bits = lax.bitcast_convert_type(table.astype(jnp.bfloat16), jnp.uint16).astype(jnp.uint32)
    return lax.bitcast_convert_type((bits[:, half:] << 16) | bits[:, :half], I32)


def _bf16_bits(x):
    return lax.bitcast_convert_type(x.astype(jnp.bfloat16).astype(F32), I32)


def _pack_halves(x):
    half = x.shape[1] // 2
    return (_bf16_bits(x[:, half:]) & HIGH_HALF) | lax.shift_right_logical(_bf16_bits(x[:, :half]), 16)


def _pack_splat(x):
    bits = _bf16_bits(x)
    return (bits & HIGH_HALF) | lax.shift_right_logical(bits, 16)


def _unpack_words(w):
    lo = lax.bitcast_convert_type(lax.shift_left(w, jnp.full(w.shape, 16, I32)), F32)
    hi = lax.bitcast_convert_type(lax.bitwise_and(w, jnp.full(w.shape, HIGH_HALF, I32)), F32)
    return lo, hi


def _as_bf16(w):
    return plsc.bitcast(w, jnp.bfloat16)


def _halves_f32(p):
    return _unpack_words(plsc.bitcast(p, I32))


def _sc_kernel(body, out_type, scratch_types):
    mesh = plsc.VectorSubcoreMesh(core_axis_name="c", subcore_axis_name="s")
    return pl.kernel(body, out_type=out_type, mesh=mesh, scratch_types=scratch_types,
                     compiler_params=pltpu.CompilerParams(needs_layout_passes=False))


def _sc_worker_base(tokens_per_worker):
    return (lax.axis_index("s") * SC_CORES + lax.axis_index("c")) * tokens_per_worker


def _sc_pipeline(nchunk, gather, compute):
    ahead = SC_NBUF - 1
    for i in range(ahead):
        gather(i, i).start()

    @pl.loop(0, nchunk)
    def _(ci):
        @pl.when(ci + ahead < nchunk)
        def _():
            gather(ci + ahead, (ci + ahead) % SC_NBUF).start()

        slot = ci % SC_NBUF
        gather(ci, slot).wait()
        compute(ci, slot)


def _sc_dots(xp, experts, ptab):
    T, W = xp.shape
    HK = experts.shape[1]
    L, TG, CH = SC_LANES, SC_TOKENS, SC_CHUNK
    tpw = T // SC_WORKERS
    cpt = HK // CH
    nchunk = TG * cpt
    DG = 8
    assert W == ptab.shape[1] and T % (SC_WORKERS * TG) == 0 and HK % CH == 0 and CH % L == 0
    assert W % (DG * L) == 0 and DG % 2 == 0 and nchunk >= SC_NBUF

    def body(x_hbm, idx_hbm, tab_hbm, out_hbm, x_v, idx_v, out_v, rows_v, acc_v, sem):
        base = _sc_worker_base(tpw)
        lane = lax.iota(I32, L)

        def gather(ci, slot):
            tok, c = ci // cpt, ci % cpt
            return pltpu.make_async_copy(tab_hbm.at[idx_v.at[tok, pl.ds(c * CH, CH)]],
                                         rows_v.at[slot], sem.at[slot])

        def compute(ci, slot):
            tok, c = ci // cpt, ci % cpt
            for eg in range(CH // L):
                @pl.loop(0, W // (DG * L))
                def _(dg):
                    off = dg * (DG * L)
                    cont = jnp.full((L,), dg, I32) != 0
                    zero = jnp.zeros((L,), F32)
                    accs = [jnp.where(cont, acc_v[kk, :], zero) for kk in range(L)]
                    for j in range(0, DG, 2):
                        xa = _as_bf16(x_v[tok, pl.ds(off + j * L, L)])
                        xb = _as_bf16(x_v[tok, pl.ds(off + (j + 1) * L, L)])
                        for kk in range(L):
                            ra = _as_bf16(rows_v[slot, eg * L + kk, pl.ds(off + j * L, L)])
                            rb = _as_bf16(rows_v[slot, eg * L + kk, pl.ds(off + (j + 1) * L, L)])
                            lo, hi = _halves_f32(ra * xa + rb * xb)
                            accs[kk] = accs[kk] + lo + hi
                    for kk in range(L):
                        acc_v[kk, :] = accs[kk]

                s = jnp.zeros((L,), F32)
                for j in range(L):
                    s = s + plsc.load_gather(acc_v, [lane, jnp.full((L,), j, I32)])
                out_v[tok, pl.ds(c * CH + eg * L, L)] = s

        @pl.loop(0, tpw // TG)
        def _(g):
            t0 = base + g * TG
            pltpu.sync_copy(x_hbm.at[pl.ds(t0, TG)], x_v)
            pltpu.sync_copy(idx_hbm.at[pl.ds(t0, TG)], idx_v)
            _sc_pipeline(nchunk, gather, compute)
            pltpu.sync_copy(out_v, out_hbm.at[pl.ds(t0, TG)])

    return _sc_kernel(
        body, jax.ShapeDtypeStruct((T, HK), F32),
        [pltpu.VMEM((TG, W), I32), pltpu.VMEM((TG, HK), I32), pltpu.VMEM((TG, HK), F32),
         pltpu.VMEM((SC_NBUF, CH, W), I32), pltpu.VMEM((L, L), F32),
         pltpu.SemaphoreType.DMA((SC_NBUF,))],
    )(xp, experts, ptab)


def _sc_wsum(wp, experts, ptab):
    T, HK = wp.shape
    W = ptab.shape[1]
    D = 2 * W
    L, TG, CH = SC_LANES, SC_TOKENS, SC_CHUNK
    tpw = T // SC_WORKERS
    cpt = HK // CH
    nchunk = TG * cpt
    DG = 8
    assert T % (SC_WORKERS * TG) == 0 and HK % CH == 0 and CH % 2 == 0 and W % (DG * L) == 0
    assert nchunk >= SC_NBUF

    def body(w_hbm, idx_hbm, tab_hbm, out_hbm, w_v, idx_v, y_v, rows_v, sem):
        base = _sc_worker_base(tpw)

        def gather(ci, slot):
            tok, c = ci // cpt, ci % cpt
            return pltpu.make_async_copy(tab_hbm.at[idx_v.at[tok, pl.ds(c * CH, CH)]],
                                         rows_v.at[slot], sem.at[slot])

        def compute(ci, slot):
            tok, c = ci // cpt, ci % cpt
            tokv = jnp.full((L,), tok, I32)
            cont = jnp.full((L,), c, I32) != 0

            @pl.loop(0, W // (DG * L))
            def _(dg):
                off = dg * (DG * L)
                zero = jnp.zeros((L,), F32)
                alo = [jnp.where(cont, y_v[tok, pl.ds(off + j * L, L)], zero) for j in range(DG)]
                ahi = [jnp.where(cont, y_v[tok, pl.ds(W + off + j * L, L)], zero) for j in range(DG)]
                for kk in range(0, CH, 2):
                    wa = _as_bf16(plsc.load_gather(w_v, [tokv, jnp.full((L,), c * CH + kk, I32)]))
                    wb = _as_bf16(plsc.load_gather(w_v, [tokv, jnp.full((L,), c * CH + kk + 1, I32)]))
                    for j in range(DG):
                        ra = _as_bf16(rows_v[slot, kk, pl.ds(off + j * L, L)])
                        rb = _as_bf16(rows_v[slot, kk + 1, pl.ds(off + j * L, L)])
                        lo, hi = _halves_f32(wa * ra + wb * rb)
                        alo[j] = alo[j] + lo
                        ahi[j] = ahi[j] + hi
                for j in range(DG):
                    y_v[tok, pl.ds(off + j * L, L)] = alo[j]
                    y_v[tok, pl.ds(W + off + j * L, L)] = ahi[j]

        @pl.loop(0, tpw // TG)
        def _(g):
            t0 = base + g * TG
            pltpu.sync_copy(w_hbm.at[pl.ds(t0, TG)], w_v)
            pltpu.sync_copy(idx_hbm.at[pl.ds(t0, TG)], idx_v)
            _sc_pipeline(nchunk, gather, compute)
            pltpu.sync_copy(y_v, out_hbm.at[pl.ds(t0, TG)])

    return _sc_kernel(
        body, jax.ShapeDtypeStruct((T, D), F32),
        [pltpu.VMEM((TG, HK), I32), pltpu.VMEM((TG, HK), I32), pltpu.VMEM((TG, D), F32),
         pltpu.VMEM((SC_NBUF, CH, W), I32), pltpu.SemaphoreType.DMA((SC_NBUF,))],
    )(wp, experts, ptab)


def _tc_wsum_kernel(exp_ref, w_ref, tab_ref, y_ref):
    tb, hk = exp_ref.shape
    sub = tab_ref.shape[1]

    def token(t, carry):
        lo_acc = jnp.zeros(tab_ref.shape[1:], F32)
        hi_acc = jnp.zeros(tab_ref.shape[1:], F32)
        for k in range(hk):
            row = tab_ref[exp_ref[t, k]]
            w = w_ref[t, k]
            lo_acc = lo_acc + w * lax.bitcast_convert_type(lax.shift_left(row, 16), F32)
            hi_acc = hi_acc + w * lax.bitcast_convert_type(row & HIGH_HALF, F32)
        y_ref[t, pl.ds(0, sub), :] = lo_acc
        y_ref[t, pl.ds(sub, sub), :] = hi_acc
        return carry

    lax.fori_loop(0, tb, token, 0)


def _tc_wsum(w, experts, ptab):
    T, hk = w.shape
    E, W = ptab.shape
    sub = W // TC_LANES
    tb = min(TC_WSUM_TILE, T)
    smem = lambda: pl.BlockSpec((tb, hk), lambda i: (i, 0), memory_space=pltpu.SMEM)
    y = pl.pallas_call(
        _tc_wsum_kernel, grid=(T // tb,),
        in_specs=[smem(), smem(), _resident((E, sub, TC_LANES))],
        out_specs=pl.BlockSpec((tb, 2 * sub, TC_LANES), lambda i: (i, 0, 0)),
        out_shape=jax.ShapeDtypeStruct((T, 2 * sub, TC_LANES), F32),
        cost_estimate=pl.CostEstimate(flops=4 * T * hk * W, transcendentals=0,
                                      bytes_accessed=4 * (T * hk * W + E * W + 2 * T * W + 2 * T * hk)),
        compiler_params=pltpu.CompilerParams(dimension_semantics=("arbitrary",),
                                             vmem_limit_bytes=VMEM_LIMIT_BYTES),
    )(experts, w, ptab.reshape(E, sub, TC_LANES))
    return y.reshape(T, 2 * W)


def _act_kernel(dots_ref, gate_ref, w_ref, wp_ref):
    w = _gelu(dots_ref[...]) * gate_ref[...]
    w_ref[...] = w
    wp_ref[...] = _pack_splat(w)


def _act(dots, gates):
    T, hk = dots.shape
    tt = min(ROW_TILE, T)
    spec = pl.BlockSpec((tt, hk), lambda i: (i, 0))
    return pl.pallas_call(
        _act_kernel, grid=(T // tt,), in_specs=[spec, spec], out_specs=[spec, spec],
        out_shape=[jax.ShapeDtypeStruct((T, hk), F32), jax.ShapeDtypeStruct((T, hk), I32)],
        compiler_params=pltpu.CompilerParams(dimension_semantics=("arbitrary",)),
    )(dots, gates)


def _final_kernel(h_ref, y_ref, g_ref, *rest):
    o_ref = rest[-1]
    o_ref[...] = _rmsnorm(h_ref[...] + y_ref[...], g_ref[...])


def _final(h, y, gain, out, row0, total_rows):
    T, D = h.shape
    tt = min(ROW_TILE, T)
    assert row0 % tt == 0
    spec = pl.BlockSpec((tt, D), lambda i: (i, 0))
    out_spec = pl.BlockSpec((tt, D), lambda i: (i + row0 // tt, 0))
    in_specs = [spec, spec, pl.BlockSpec((1, D), lambda i: (0, 0))]
    args = (h, y, gain)
    aliases = {}
    if out is not None:
        in_specs.append(pl.BlockSpec(memory_space=pl.ANY))
        args += (out,)
        aliases = {3: 0}
    return pl.pallas_call(
        _final_kernel, grid=(T // tt,), in_specs=in_specs, out_specs=out_spec,
        out_shape=jax.ShapeDtypeStruct((total_rows, D), F32),
        input_output_aliases=aliases,
        compiler_params=pltpu.CompilerParams(dimension_semantics=("arbitrary",)),
    )(*args)


def kernel(x, mem, norm1_gain, w_in, pool_w, pool_scale, sgu_ln_gain, sgu_ln_bias, sgu_w_s, sgu_b_s,
           sgu_w_out, mem_norm_gain, xa_w_kv, xa_w_out, w_out, norm2_gain, peer_w_q, peer_keys1,
           peer_keys2, peer_u, peer_v, final_norm_gain):
    B, S, D = x.shape
    depth = w_in.shape[0]
    lo = lambda w: w.astype(MXU_DTYPE)
    row = lambda w: w.reshape(1, -1)
    assert depth == 1, "the final RMSNorm is fused with the last layer's residual add"
    l = 0
    weights = (row(norm1_gain[l]), lo(w_in[l]), lo(pool_w[l]), row(pool_scale[l]),
               row(sgu_ln_gain[l]), row(sgu_ln_bias[l]), sgu_w_s[l], sgu_b_s[l].T,
               lo(sgu_w_out[l]), lo(xa_w_out[l]), lo(w_out[l]))
    select_w = (row(norm2_gain[l]), lo(peer_w_q[l]), lo(peer_keys1[l]), lo(peer_keys2[l]))
    u_packed, v_packed = _pack_table(peer_u[l]), _pack_table(peer_v[l])
    k, v = _memkv(mem, row(mem_norm_gain[l]), lo(xa_w_kv[l]))
    fgain = row(final_norm_gain)
    chunks = BATCH_CHUNKS if sum(BATCH_CHUNKS) == B else (B,)
    out, b0 = None, 0
    for nb in chunks:
        rows = nb * S
        h = _mixer(x, k, v, weights, b0, nb).reshape(rows, D)
        xn, experts, gates = _select(h, *select_w)
        w, wp = _act(_sc_dots(xn, experts, u_packed), gates)
        n_tc = (rows * TC_WSUM_SHARE[0] // TC_WSUM_SHARE[1]) // SC_ROW_QUANTUM * SC_ROW_QUANTUM
        y_tc = _tc_wsum(w[:n_tc], experts[:n_tc], v_packed)
        y_sc = _sc_wsum(wp[n_tc:], experts[n_tc:], v_packed)
        out = _final(h[:n_tc], y_tc, fgain, out, b0 * S, B * S)
        out = _final(h[n_tc:], y_sc, fgain, out, b0 * S + n_tc, B * S)
        b0 += nb
    return out.reshape(B, S, D)
```

```python
import functools
import math

import jax
import jax.numpy as jnp
from jax import lax
from jax.experimental import pallas as pl
from jax.experimental.pallas import tpu as pltpu
from jax.experimental.pallas import tpu_sc as plsc

F32 = jnp.float32
I32 = jnp.int32
MXU_DTYPE = jnp.bfloat16

RMS_EPS = 1e-6
LN_EPS = 1e-5
POOL_WINDOWS = (2, 4, 8, 16)
POOL_HALO = 16
SGU_CHUNK = 128
SGU_HEADS = 8
XA_HEADS = 4
PEER_HEADS = 8
PEER_N_KEYS = 128
PEER_TOPK = 16

V7X_VMEM_BYTES = 64 * 1024 * 1024
VMEM_LIMIT_BYTES = 56 * 1024 * 1024

SEQ_TILE = 256
SELECT_TILE = 256
ROW_TILE = 512
TC_WSUM_TILE = 32
TC_LANES = 128
BATCH_CHUNKS = (1, 1, 2, 2, 2)

SC_CORES = 2
SC_SUBCORES = 16
SC_LANES = 16
SC_WORKERS = SC_CORES * SC_SUBCORES
SC_TOKENS = 16
SC_CHUNK = 32
SC_NBUF = 4
HIGH_HALF = -65536
SC_ROW_QUANTUM = SC_WORKERS * SC_TOKENS
TC_WSUM_SHARE = (1, 2)
TC_WSUM_LAG = 2


def _rmsnorm(x, gain):
    return x * lax.rsqrt(jnp.mean(x * x, axis=-1, keepdims=True) + RMS_EPS) * gain


def _gelu(x):
    return 0.5 * x * (1.0 + lax.erf(x * (1.0 / math.sqrt(2.0))))


def _mm(a, b):
    return jnp.dot(a.astype(MXU_DTYPE), b.astype(MXU_DTYPE), preferred_element_type=F32)


def _mm_nt(a, b):
    return lax.dot_general(a.astype(MXU_DTYPE), b.astype(MXU_DTYPE),
                           (((1,), (1,)), ((), ())), preferred_element_type=F32)


def _memkv_kernel(mem_ref, gain_ref, wkv_ref, k_ref, v_ref):
    width = k_ref.shape[-1]
    kv = _mm(_rmsnorm(mem_ref[0], gain_ref[...]), wkv_ref[...])
    k_ref[0] = kv[:, :width].astype(k_ref.dtype)
    v_ref[0] = kv[:, width:].astype(v_ref.dtype)


def _memkv(mem, gain, w_kv):
    B, M, D = mem.shape
    width = w_kv.shape[1] // 2
    const = lambda b: (0, 0)
    return pl.pallas_call(
        _memkv_kernel,
        grid=(B,),
        in_specs=[pl.BlockSpec((1, M, D), lambda b: (b, 0, 0)),
                  pl.BlockSpec((1, D), const),
                  pl.BlockSpec(w_kv.shape, const)],
        out_specs=[pl.BlockSpec((1, M, width), lambda b: (b, 0, 0))] * 2,
        out_shape=[jax.ShapeDtypeStruct((B, M, width), MXU_DTYPE)] * 2,
        compiler_params=pltpu.CompilerParams(dimension_semantics=("arbitrary",),
                                             vmem_limit_bytes=VMEM_LIMIT_BYTES),
    )(mem, gain, w_kv)


def _mixer_kernel(x_ref, k_ref, v_ref, n1g_ref, win_ref, poolw_ref, pools_ref, lng_ref, lnb_ref,
                  ws_ref, bst_ref, sguwo_ref, xawo_ref, wout_ref, h_ref, tail_ref):
    ts, d = x_ref.shape[1], x_ref.shape[2]
    s_idx = pl.program_id(1)
    x = x_ref[0]
    nb = _rmsnorm(x, n1g_ref[...]).astype(MXU_DTYPE)

    def proj(col):
        return jnp.dot(nb, win_ref[:, col * d:(col + 1) * d], preferred_element_type=F32)

    @pl.when(s_idx == 0)
    def _():
        tail_ref[...] = jnp.zeros_like(tail_ref)

    p = proj(0)
    ext = jnp.concatenate([tail_ref[...], p], axis=0)
    tail_ref[...] = p[ts - POOL_HALO:, :]
    pos = s_idx * ts + lax.broadcasted_iota(jnp.int32, (ts, 1), 0)
    group = d // len(POOL_WINDOWS)
    y_pool = []
    for g, w in enumerate(POOL_WINDOWS):
        acc = ext[:, g * group:(g + 1) * group]
        shift = 1
        while shift < w:
            acc = acc + pltpu.roll(acc, shift, 0)
            shift *= 2
        count = jnp.minimum(pos + 1, w).astype(F32)
        diff = acc[POOL_HALO:, :] / count - p[:, g * group:(g + 1) * group]
        y_pool.append(_mm(diff, poolw_ref[g]))
    y_pool = jnp.concatenate(y_pool, axis=1) * pools_ref[...]

    u = _gelu(proj(1))
    v = _gelu(proj(2))
    mu = jnp.mean(v, axis=-1, keepdims=True)
    vc = v - mu
    var = jnp.mean(vc * vc, axis=-1, keepdims=True)
    v = (vc * lax.rsqrt(var + LN_EPS) * lng_ref[...] + lnb_ref[...]).astype(MXU_DTYPE)
    hd = d // SGU_HEADS
    causal = (lax.broadcasted_iota(jnp.int32, (SGU_CHUNK, SGU_CHUNK), 0)
              >= lax.broadcasted_iota(jnp.int32, (SGU_CHUNK, SGU_CHUNK), 1))
    mixed_rows = []
    w_masked = [jnp.where(causal, ws_ref[h], 0.0).astype(MXU_DTYPE) for h in range(SGU_HEADS)]
    for c in range(ts // SGU_CHUNK):
        rows = slice(c * SGU_CHUNK, (c + 1) * SGU_CHUNK)
        heads = []
        for h in range(SGU_HEADS):
            mixed = jnp.dot(w_masked[h], v[rows, h * hd:(h + 1) * hd], preferred_element_type=F32)
            heads.append(mixed + bst_ref[:, h:h + 1])
        mixed_rows.append(jnp.concatenate(heads, axis=1))
    mixed = jnp.concatenate(mixed_rows, axis=0) if len(mixed_rows) > 1 else mixed_rows[0]
    y_sgu = _mm(u * mixed, sguwo_ref[...])

    q = proj(3).astype(MXU_DTYPE)
    xd = d // XA_HEADS
    outs = []
    for h in range(XA_HEADS):
        cols = slice(h * xd, (h + 1) * xd)
        s = _mm_nt(q[:, cols], k_ref[0, :, cols]) * (xd ** -0.5)
        e = jnp.exp(s - jnp.max(s, axis=-1, keepdims=True))
        probs = e / jnp.sum(e, axis=-1, keepdims=True)
        outs.append(_mm(probs, v_ref[0, :, cols]))
    y_xa = _mm(jnp.concatenate(outs, axis=1), xawo_ref[...])

    merged = (jax.nn.sigmoid(proj(4)) * y_pool + jax.nn.sigmoid(proj(5)) * y_sgu
              + jax.nn.sigmoid(proj(6)) * y_xa)
    h_ref[0] = x + _mm(merged, wout_ref[...])


def _resident(shape):
    zeros = (0,) * len(shape)
    return pl.BlockSpec(shape, lambda *_: zeros, pipeline_mode=pl.Buffered(1))


def _mixer(x, k, v, weights, b0, nb):
    _, S, D = x.shape
    ts = min(SEQ_TILE, S)
    M = k.shape[1]
    return pl.pallas_call(
        _mixer_kernel,
        grid=(nb, S // ts),
        in_specs=[pl.BlockSpec((1, ts, D), lambda b, s: (b + b0, s, 0)),
                  pl.BlockSpec((1, M, k.shape[2]), lambda b, s: (b + b0, 0, 0)),
                  pl.BlockSpec((1, M, v.shape[2]), lambda b, s: (b + b0, 0, 0))]
                 + [_resident(w.shape) for w in weights],
        out_specs=pl.BlockSpec((1, ts, D), lambda b, s: (b, s, 0)),
        out_shape=jax.ShapeDtypeStruct((nb, S, D), F32),
        scratch_shapes=[pltpu.VMEM((POOL_HALO, D), F32)],
        compiler_params=pltpu.CompilerParams(dimension_semantics=("arbitrary", "arbitrary"),
                                             vmem_limit_bytes=VMEM_LIMIT_BYTES),
    )(x, k, v, *weights)


def _topk_rows(vals, payload=None):
    n_rows = vals.shape[0]
    row = lax.broadcasted_iota(jnp.int32, vals.shape, 0).astype(F32)
    out_v, out_i = [], []
    for _ in range(PEER_TOPK):
        m = jnp.max(vals, axis=0, keepdims=True)
        first = jnp.min(jnp.where(vals == m, row, float(n_rows)), axis=0, keepdims=True)
        sel = row == first
        out_v.append(m)
        if payload is None:
            out_i.append(first)
        else:
            out_i.append(jnp.max(jnp.where(sel, payload, -1.0), axis=0, keepdims=True))
        vals = jnp.where(sel, -jnp.inf, vals)
    return jnp.concatenate(out_v, axis=0), jnp.concatenate(out_i, axis=0)


def _candidates(v1, i1, v2, i2):
    sub = 8
    jrow = lax.broadcasted_iota(jnp.int32, (sub, 1), 0)
    vals = [v1[0:1] + v2]
    ids = [i1[0:1] * float(PEER_N_KEYS) + i2]
    for i in range(1, sub):
        keep = jrow < (PEER_TOPK // (i + 1))
        vals.append(jnp.where(keep, v1[i:i + 1] + v2[0:sub], -jnp.inf))
        ids.append(i1[i:i + 1] * float(PEER_N_KEYS) + i2[0:sub])
    vals.append(v1[sub:] + v2[0:1])
    ids.append(i1[sub:] * float(PEER_N_KEYS) + i2[0:1])
    return jnp.concatenate(vals, axis=0), jnp.concatenate(ids, axis=0)


def _select_kernel(h_ref, n2g_ref, wq_ref, k1_ref, k2_ref, xn_ref, exp_ref, gate_ref):
    xn = _rmsnorm(h_ref[...], n2g_ref[...])
    xn_ref[...] = _pack_halves(xn)
    q = _mm(xn, wq_ref[...])
    half = k1_ref.shape[1]
    experts, gates = [], []
    for h in range(PEER_HEADS):
        q1 = q[:, (2 * h) * half:(2 * h + 1) * half]
        q2 = q[:, (2 * h + 1) * half:(2 * h + 2) * half]
        v1, i1 = _topk_rows(_mm_nt(k1_ref[...], q1))
        v2, i2 = _topk_rows(_mm_nt(k2_ref[...], q2))
        best, eid = _topk_rows(*_candidates(v1, i1, v2, i2))
        e = jnp.exp(best - jnp.max(best, axis=0, keepdims=True))
        gates.append(e / jnp.sum(e, axis=0, keepdims=True))
        experts.append(eid)
    exp_ref[...] = jnp.concatenate(experts, axis=0).T.astype(jnp.int32)
    gate_ref[...] = jnp.concatenate(gates, axis=0).T


def _select(h, n2g, w_q, keys1, keys2):
    T, D = h.shape
    tt = min(SELECT_TILE, T)
    hk = PEER_HEADS * PEER_TOPK
    return pl.pallas_call(
        _select_kernel,
        grid=(T // tt,),
        in_specs=[pl.BlockSpec((tt, D), lambda i: (i, 0)),
                  _resident(n2g.shape), _resident(w_q.shape),
                  _resident(keys1.shape), _resident(keys2.shape)],
        out_specs=[pl.BlockSpec((tt, D // 2), lambda i: (i, 0)),
                   pl.BlockSpec((tt, hk), lambda i: (i, 0)),
                   pl.BlockSpec((tt, hk), lambda i: (i, 0))],
        out_shape=[jax.ShapeDtypeStruct((T, D // 2), I32),
                   jax.ShapeDtypeStruct((T, hk), jnp.int32),
                   jax.ShapeDtypeStruct((T, hk), F32)],
        compiler_params=pltpu.CompilerParams(dimension_semantics=("arbitrary",),
                                             vmem_limit_bytes=VMEM_LIMIT_BYTES),
    )(h, n2g, w_q, keys1, keys2)


def _pack_table(table):
    half = table.shape[1] // 2
    bits = lax.bitcast_convert_type(table.astype(jnp.bfloat16), jnp.uint16).astype(jnp.uint32)
    return lax.bitcast_convert_type((bits[:, half:] << 16) | bits[:, :half], I32)


def _bf16_bits(x):
    return lax.bitcast_convert_type(x.astype(jnp.bfloat16).astype(F32), I32)


def _pack_halves(x):
    half = x.shape[1] // 2
    return (_bf16_bits(x[:, half:]) & HIGH_HALF) | lax.shift_right_logical(_bf16_bits(x[:, :half]), 16)


def _pack_splat(x):
    bits = _bf16_bits(x)
    return (bits & HIGH_HALF) | lax.shift_right_logical(bits, 16)


def _unpack_words(w):
    lo = lax.bitcast_convert_type(lax.shift_left(w, jnp.full(w.shape, 16, I32)), F32)
    hi = lax.bitcast_convert_type(lax.bitwise_and(w, jnp.full(w.shape, HIGH_HALF, I32)), F32)
    return lo, hi


def _as_bf16(w):
    return plsc.bitcast(w, jnp.bfloat16)


def _halves_f32(p):
    return _unpack_words(plsc.bitcast(p, I32))


def _sc_kernel(body, out_type, scratch_types):
    mesh = plsc.VectorSubcoreMesh(core_axis_name="c", subcore_axis_name="s")
    return pl.kernel(body, out_type=out_type, mesh=mesh, scratch_types=scratch_types,
                     compiler_params=pltpu.CompilerParams(needs_layout_passes=False))


def _sc_worker_base(tokens_per_worker):
    return (lax.axis_index("s") * SC_CORES + lax.axis_index("c")) * tokens_per_worker


def _sc_pipeline(nchunk, gather, compute):
    ahead = SC_NBUF - 1
    for i in range(ahead):
        gather(i, i).start()

    @pl.loop(0, nchunk)
    def _(ci):
        @pl.when(ci + ahead < nchunk)
        def _():
            gather(ci + ahead, (ci + ahead) % SC_NBUF).start()

        slot = ci % SC_NBUF
        gather(ci, slot).wait()
        compute(ci, slot)


def _sc_dots(xp, experts, ptab):
    T, W = xp.shape
    HK = experts.shape[1]
    L, TG, CH = SC_LANES, SC_TOKENS, SC_CHUNK
    tpw = T // SC_WORKERS
    cpt = HK // CH
    nchunk = TG * cpt
    DG = 8
    assert W == ptab.shape[1] and T % (SC_WORKERS * TG) == 0 and HK % CH == 0 and CH % L == 0
    assert W % (DG * L) == 0 and DG % 2 == 0 and nchunk >= SC_NBUF

    def body(x_hbm, idx_hbm, tab_hbm, out_hbm, x_v, idx_v, out_v, rows_v, acc_v, sem):
        base = _sc_worker_base(tpw)
        lane = lax.iota(I32, L)

        def gather(ci, slot):
            tok, c = ci // cpt, ci % cpt
            return pltpu.make_async_copy(tab_hbm.at[idx_v.at[tok, pl.ds(c * CH, CH)]],
                                         rows_v.at[slot], sem.at[slot])

        def compute(ci, slot):
            tok, c = ci // cpt, ci % cpt
            for eg in range(CH // L):
                @pl.loop(0, W // (DG * L))
                def _(dg):
                    off = dg * (DG * L)
                    cont = jnp.full((L,), dg, I32) != 0
                    zero = jnp.zeros((L,), F32)
                    accs = [jnp.where(cont, acc_v[kk, :], zero) for kk in range(L)]
                    for j in range(0, DG, 2):
                        xa = _as_bf16(x_v[tok, pl.ds(off + j * L, L)])
                        xb = _as_bf16(x_v[tok, pl.ds(off + (j + 1) * L, L)])
                        for kk in range(L):
                            ra = _as_bf16(rows_v[slot, eg * L + kk, pl.ds(off + j * L, L)])
                            rb = _as_bf16(rows_v[slot, eg * L + kk, pl.ds(off + (j + 1) * L, L)])
                            lo, hi = _halves_f32(ra * xa + rb * xb)
                            accs[kk] = accs[kk] + lo + hi
                    for kk in range(L):
                        acc_v[kk, :] = accs[kk]

                s = jnp.zeros((L,), F32)
                for j in range(L):
                    s = s + plsc.load_gather(acc_v, [lane, jnp.full((L,), j, I32)])
                out_v[tok, pl.ds(c * CH + eg * L, L)] = s

        @pl.loop(0, tpw // TG)
        def _(g):
            t0 = base + g * TG
            pltpu.sync_copy(x_hbm.at[pl.ds(t0, TG)], x_v)
            pltpu.sync_copy(idx_hbm.at[pl.ds(t0, TG)], idx_v)
            _sc_pipeline(nchunk, gather, compute)
            pltpu.sync_copy(out_v, out_hbm.at[pl.ds(t0, TG)])

    return _sc_kernel(
        body, jax.ShapeDtypeStruct((T, HK), F32),
        [pltpu.VMEM((TG, W), I32), pltpu.VMEM((TG, HK), I32), pltpu.VMEM((TG, HK), F32),
         pltpu.VMEM((SC_NBUF, CH, W), I32), pltpu.VMEM((L, L), F32),
         pltpu.SemaphoreType.DMA((SC_NBUF,))],
    )(xp, experts, ptab)


def _sc_wsum(wp, experts, ptab):
    T, HK = wp.shape
    W = ptab.shape[1]
    D = 2 * W
    L, TG, CH = SC_LANES, SC_TOKENS, SC_CHUNK
    tpw = T // SC_WORKERS
    cpt = HK // CH
    nchunk = TG * cpt
    DG = 8
    assert T % (SC_WORKERS * TG) == 0 and HK % CH == 0 and CH % 2 == 0 and W % (DG * L) == 0
    assert nchunk >= SC_NBUF

    def body(w_hbm, idx_hbm, tab_hbm, out_hbm, w_v, idx_v, y_v, rows_v, sem):
        base = _sc_worker_base(tpw)

        def gather(ci, slot):
            tok, c = ci // cpt, ci % cpt
            return pltpu.make_async_copy(tab_hbm.at[idx_v.at[tok, pl.ds(c * CH, CH)]],
                                         rows_v.at[slot], sem.at[slot])

        def compute(ci, slot):
            tok, c = ci // cpt, ci % cpt
            tokv = jnp.full((L,), tok, I32)
            cont = jnp.full((L,), c, I32) != 0

            @pl.loop(0, W // (DG * L))
            def _(dg):
                off = dg * (DG * L)
                zero = jnp.zeros((L,), F32)
                alo = [jnp.where(cont, y_v[tok, pl.ds(off + j * L, L)], zero) for j in range(DG)]
                ahi = [jnp.where(cont, y_v[tok, pl.ds(W + off + j * L, L)], zero) for j in range(DG)]
                for kk in range(0, CH, 2):
                    wa = _as_bf16(plsc.load_gather(w_v, [tokv, jnp.full((L,), c * CH + kk, I32)]))
                    wb = _as_bf16(plsc.load_gather(w_v, [tokv, jnp.full((L,), c * CH + kk + 1, I32)]))
                    for j in range(DG):
                        ra = _as_bf16(rows_v[slot, kk, pl.ds(off + j * L, L)])
                        rb = _as_bf16(rows_v[slot, kk + 1, pl.ds(off + j * L, L)])
                        lo, hi = _halves_f32(wa * ra + wb * rb)
                        alo[j] = alo[j] + lo
                        ahi[j] = ahi[j] + hi
                for j in range(DG):
                    y_v[tok, pl.ds(off + j * L, L)] = alo[j]
                    y_v[tok, pl.ds(W + off + j * L, L)] = ahi[j]

        @pl.loop(0, tpw // TG)
        def _(g):
            t0 = base + g * TG
            pltpu.sync_copy(w_hbm.at[pl.ds(t0, TG)], w_v)
            pltpu.sync_copy(idx_hbm.at[pl.ds(t0, TG)], idx_v)
            _sc_pipeline(nchunk, gather, compute)
            pltpu.sync_copy(y_v, out_hbm.at[pl.ds(t0, TG)])

    return _sc_kernel(
        body, jax.ShapeDtypeStruct((T, D), F32),
        [pltpu.VMEM((TG, HK), I32), pltpu.VMEM((TG, HK), I32), pltpu.VMEM((TG, D), F32),
         pltpu.VMEM((SC_NBUF, CH, W), I32), pltpu.SemaphoreType.DMA((SC_NBUF,))],
    )(wp, experts, ptab)


def _tc_wsum_kernel(exp_ref, w_ref, tab_ref, y_ref):
    tb, hk = exp_ref.shape
    sub = tab_ref.shape[1]

    def token(t, carry):
        lo_acc = jnp.zeros(tab_ref.shape[1:], F32)
        hi_acc = jnp.zeros(tab_ref.shape[1:], F32)
        for k in range(hk):
            row = tab_ref[exp_ref[t, k]]
            w = w_ref[t, k]
            lo_acc = lo_acc + w * lax.bitcast_convert_type(lax.shift_left(row, 16), F32)
            hi_acc = hi_acc + w * lax.bitcast_convert_type(row & HIGH_HALF, F32)
        y_ref[t, pl.ds(0, sub), :] = lo_acc
        y_ref[t, pl.ds(sub, sub), :] = hi_acc
        return carry

    lax.fori_loop(0, tb, token, 0)


def _tc_wsum(w, experts, ptab):
    T, hk = w.shape
    E, W = ptab.shape
    sub = W // TC_LANES
    tb = min(TC_WSUM_TILE, T)
    smem = lambda: pl.BlockSpec((tb, hk), lambda i: (i, 0), memory_space=pltpu.SMEM)
    y = pl.pallas_call(
        _tc_wsum_kernel, grid=(T // tb,),
        in_specs=[smem(), smem(), _resident((E, sub, TC_LANES))],
        out_specs=pl.BlockSpec((tb, 2 * sub, TC_LANES), lambda i: (i, 0, 0)),
        out_shape=jax.ShapeDtypeStruct((T, 2 * sub, TC_LANES), F32),
        cost_estimate=pl.CostEstimate(flops=4 * T * hk * W, transcendentals=0,
                                      bytes_accessed=4 * (T * hk * W + E * W + 2 * T * W + 2 * T * hk)),
        compiler_params=pltpu.CompilerParams(dimension_semantics=("arbitrary",),
                                             vmem_limit_bytes=VMEM_LIMIT_BYTES),
    )(experts, w, ptab.reshape(E, sub, TC_LANES))
    return y.reshape(T, 2 * W)


def _act_kernel(dots_ref, gate_ref, w_ref, wp_ref):
    w = _gelu(dots_ref[...]) * gate_ref[...]
    w_ref[...] = w
    wp_ref[...] = _pack_splat(w)


def _act(dots, gates):
    T, hk = dots.shape
    tt = min(ROW_TILE, T)
    spec = pl.BlockSpec((tt, hk), lambda i: (i, 0))
    return pl.pallas_call(
        _act_kernel, grid=(T // tt,), in_specs=[spec, spec], out_specs=[spec, spec],
        out_shape=[jax.ShapeDtypeStruct((T, hk), F32), jax.ShapeDtypeStruct((T, hk), I32)],
        compiler_params=pltpu.CompilerParams(dimension_semantics=("arbitrary",)),
    )(dots, gates)


def _final_kernel(h_ref, y_ref, g_ref, *rest):
    o_ref = rest[-1]
    o_ref[...] = _rmsnorm(h_ref[...] + y_ref[...], g_ref[...])


def _final(h, y, gain, out, row0, total_rows):
    T, D = h.shape
    tt = min(ROW_TILE, T)
    assert row0 % tt == 0
    spec = pl.BlockSpec((tt, D), lambda i: (i, 0))
    out_spec = pl.BlockSpec((tt, D), lambda i: (i + row0 // tt, 0))
    in_specs = [spec, spec, pl.BlockSpec((1, D), lambda i: (0, 0))]
    args = (h, y, gain)
    aliases = {}
    if out is not None:
        in_specs.append(pl.BlockSpec(memory_space=pl.ANY))
        args += (out,)
        aliases = {3: 0}
    return pl.pallas_call(
        _final_kernel, grid=(T // tt,), in_specs=in_specs, out_specs=out_spec,
        out_shape=jax.ShapeDtypeStruct((total_rows, D), F32),
        input_output_aliases=aliases,
        compiler_params=pltpu.CompilerParams(dimension_semantics=("arbitrary",)),
    )(*args)


def kernel(x, mem, norm1_gain, w_in, pool_w, pool_scale, sgu_ln_gain, sgu_ln_bias, sgu_w_s, sgu_b_s,
           sgu_w_out, mem_norm_gain, xa_w_kv, xa_w_out, w_out, norm2_gain, peer_w_q, peer_keys1,
           peer_keys2, peer_u, peer_v, final_norm_gain):
    B, S, D = x.shape
    depth = w_in.shape[0]
    lo = lambda w: w.astype(MXU_DTYPE)
    row = lambda w: w.reshape(1, -1)
    assert depth == 1, "the final RMSNorm is fused with the last layer's residual add"
    l = 0
    weights = (row(norm1_gain[l]), lo(w_in[l]), lo(pool_w[l]), row(pool_scale[l]),
               row(sgu_ln_gain[l]), row(sgu_ln_bias[l]), sgu_w_s[l], sgu_b_s[l].T,
               lo(sgu_w_out[l]), lo(xa_w_out[l]), lo(w_out[l]))
    select_w = (row(norm2_gain[l]), lo(peer_w_q[l]), lo(peer_keys1[l]), lo(peer_keys2[l]))
    u_packed, v_packed = _pack_table(peer_u[l]), _pack_table(peer_v[l])
    k, v = _memkv(mem, row(mem_norm_gain[l]), lo(xa_w_kv[l]))
    fgain = row(final_norm_gain)
    chunks = BATCH_CHUNKS if sum(BATCH_CHUNKS) == B else (B,)
    out, b0, tc_sums, tc_h = None, 0, [], []
    for c, nb in enumerate(chunks):
        rows = nb * S
        h = _mixer(x, k, v, weights, b0, nb).reshape(rows, D)
        xn, experts, gates = _select(h, *select_w)
        if c >= TC_WSUM_LAG:
            xn, tc_sums[c - TC_WSUM_LAG] = lax.optimization_barrier((xn, tc_sums[c - TC_WSUM_LAG]))
        w, wp = _act(_sc_dots(xn, experts, u_packed), gates)
        n_tc = (rows * TC_WSUM_SHARE[0] // TC_WSUM_SHARE[1]) // SC_ROW_QUANTUM * SC_ROW_QUANTUM
        tc_sums.append(_tc_wsum(w[:n_tc], experts[:n_tc], v_packed))
        tc_h.append(h[:n_tc])
        y_sc = _sc_wsum(wp[n_tc:], experts[n_tc:], v_packed)
        out = _final(h[n_tc:], y_sc, fgain, out, b0 * S + n_tc, B * S)
        b0 += nb
    b0 = 0
    for nb, h_tc, y_tc in zip(chunks, tc_h, tc_sums):
        out = _final(h_tc, y_tc, fgain, out, b0 * S, B * S)
        b0 += nb
    return out.reshape(B, S, D)
```

```python
import functools
import math

import jax
import jax.numpy as jnp
from jax import lax
from jax.experimental import pallas as pl
from jax.experimental.pallas import tpu as pltpu
from jax.experimental.pallas import tpu_sc as plsc

F32 = jnp.float32
I32 = jnp.int32
MXU_DTYPE = jnp.bfloat16

RMS_EPS = 1e-6
LN_EPS = 1e-5
POOL_WINDOWS = (2, 4, 8, 16)
POOL_HALO = 16
SGU_CHUNK = 128
SGU_HEADS = 8
XA_HEADS = 4
PEER_HEADS = 8
PEER_N_KEYS = 128
PEER_TOPK = 16

V7X_VMEM_BYTES = 64 * 1024 * 1024
VMEM_LIMIT_BYTES = 56 * 1024 * 1024

SEQ_TILE = 256
SELECT_TILE = 256
ROW_TILE = 512
TC_WSUM_TILE = 32
TC_LANES = 128
BATCH_CHUNKS = (1, 1, 2, 2, 2)

SC_CORES = 2
SC_SUBCORES = 16
SC_LANES = 16
SC_WORKERS = SC_CORES * SC_SUBCORES
SC_TOKENS = 16
SC_CHUNK = 32
SC_NBUF = 4
HIGH_HALF = -65536
SC_ROW_QUANTUM = SC_WORKERS * SC_TOKENS
TC_WSUM_SHARE = (1, 2)


def _rmsnorm(x, gain):
    return x * lax.rsqrt(jnp.mean(x * x, axis=-1, keepdims=True) + RMS_EPS) * gain


def _gelu(x):
    return 0.5 * x * (1.0 + lax.erf(x * (1.0 / math.sqrt(2.0))))


def _mm(a, b):
    return jnp.dot(a.astype(MXU_DTYPE), b.astype(MXU_DTYPE), preferred_element_type=F32)


def _mm_nt(a, b):
    return lax.dot_general(a.astype(MXU_DTYPE), b.astype(MXU_DTYPE),
                           (((1,), (1,)), ((), ())), preferred_element_type=F32)


def _memkv_kernel(mem_ref, gain_ref, wkv_ref, k_ref, v_ref):
    width = k_ref.shape[-1]
    kv = _mm(_rmsnorm(mem_ref[0], gain_ref[...]), wkv_ref[...])
    k_ref[0] = kv[:, :width].astype(k_ref.dtype)
    v_ref[0] = kv[:, width:].astype(v_ref.dtype)


def _memkv(mem, gain, w_kv):
    B, M, D = mem.shape
    width = w_kv.shape[1] // 2
    const = lambda b: (0, 0)
    return pl.pallas_call(
        _memkv_kernel,
        grid=(B,),
        in_specs=[pl.BlockSpec((1, M, D), lambda b: (b, 0, 0)),
                  pl.BlockSpec((1, D), const),
                  pl.BlockSpec(w_kv.shape, const)],
        out_specs=[pl.BlockSpec((1, M, width), lambda b: (b, 0, 0))] * 2,
        out_shape=[jax.ShapeDtypeStruct((B, M, width), MXU_DTYPE)] * 2,
        compiler_params=pltpu.CompilerParams(dimension_semantics=("arbitrary",),
                                             vmem_limit_bytes=VMEM_LIMIT_BYTES),
    )(mem, gain, w_kv)


def _mixer_kernel(x_ref, k_ref, v_ref, n1g_ref, win_ref, poolw_ref, pools_ref, lng_ref, lnb_ref,
                  ws_ref, bst_ref, sguwo_ref, xawo_ref, wout_ref, h_ref, tail_ref):
    ts, d = x_ref.shape[1], x_ref.shape[2]
    s_idx = pl.program_id(1)
    x = x_ref[0]
    nb = _rmsnorm(x, n1g_ref[...]).astype(MXU_DTYPE)

    def proj(col):
        return jnp.dot(nb, win_ref[:, col * d:(col + 1) * d], preferred_element_type=F32)

    @pl.when(s_idx == 0)
    def _():
        tail_ref[...] = jnp.zeros_like(tail_ref)

    p = proj(0)
    ext = jnp.concatenate([tail_ref[...], p], axis=0)
    tail_ref[...] = p[ts - POOL_HALO:, :]
    pos = s_idx * ts + lax.broadcasted_iota(jnp.int32, (ts, 1), 0)
    group = d // len(POOL_WINDOWS)
    y_pool = []
    for g, w in enumerate(POOL_WINDOWS):
        acc = ext[:, g * group:(g + 1) * group]
        shift = 1
        while shift < w:
            acc = acc + pltpu.roll(acc, shift, 0)
            shift *= 2
        count = jnp.minimum(pos + 1, w).astype(F32)
        diff = acc[POOL_HALO:, :] / count - p[:, g * group:(g + 1) * group]
        y_pool.append(_mm(diff, poolw_ref[g]))
    y_pool = jnp.concatenate(y_pool, axis=1) * pools_ref[...]

    u = _gelu(proj(1))
    v = _gelu(proj(2))
    mu = jnp.mean(v, axis=-1, keepdims=True)
    vc = v - mu
    var = jnp.mean(vc * vc, axis=-1, keepdims=True)
    v = (vc * lax.rsqrt(var + LN_EPS) * lng_ref[...] + lnb_ref[...]).astype(MXU_DTYPE)
    hd = d // SGU_HEADS
    causal = (lax.broadcasted_iota(jnp.int32, (SGU_CHUNK, SGU_CHUNK), 0)
              >= lax.broadcasted_iota(jnp.int32, (SGU_CHUNK, SGU_CHUNK), 1))
    mixed_rows = []
    w_masked = [jnp.where(causal, ws_ref[h], 0.0).astype(MXU_DTYPE) for h in range(SGU_HEADS)]
    for c in range(ts // SGU_CHUNK):
        rows = slice(c * SGU_CHUNK, (c + 1) * SGU_CHUNK)
        heads = []
        for h in range(SGU_HEADS):
            mixed = jnp.dot(w_masked[h], v[rows, h * hd:(h + 1) * hd], preferred_element_type=F32)
            heads.append(mixed + bst_ref[:, h:h + 1])
        mixed_rows.append(jnp.concatenate(heads, axis=1))
    mixed = jnp.concatenate(mixed_rows, axis=0) if len(mixed_rows) > 1 else mixed_rows[0]
    y_sgu = _mm(u * mixed, sguwo_ref[...])

    q = proj(3).astype(MXU_DTYPE)
    xd = d // XA_HEADS
    outs = []
    for h in range(XA_HEADS):
        cols = slice(h * xd, (h + 1) * xd)
        s = _mm_nt(q[:, cols], k_ref[0, :, cols]) * (xd ** -0.5)
        e = jnp.exp(s - jnp.max(s, axis=-1, keepdims=True))
        probs = e / jnp.sum(e, axis=-1, keepdims=True)
        outs.append(_mm(probs, v_ref[0, :, cols]))
    y_xa = _mm(jnp.concatenate(outs, axis=1), xawo_ref[...])

    merged = (jax.nn.sigmoid(proj(4)) * y_pool + jax.nn.sigmoid(proj(5)) * y_sgu
              + jax.nn.sigmoid(proj(6)) * y_xa)
    h_ref[0] = x + _mm(merged, wout_ref[...])


def _resident(shape):
    zeros = (0,) * len(shape)
    return pl.BlockSpec(shape, lambda *_: zeros, pipeline_mode=pl.Buffered(1))


def _mixer(x, k, v, weights, b0, nb):
    _, S, D = x.shape
    ts = min(SEQ_TILE, S)
    M = k.shape[1]
    return pl.pallas_call(
        _mixer_kernel,
        grid=(nb, S // ts),
        in_specs=[pl.BlockSpec((1, ts, D), lambda b, s: (b + b0, s, 0)),
                  pl.BlockSpec((1, M, k.shape[2]), lambda b, s: (b + b0, 0, 0)),
                  pl.BlockSpec((1, M, v.shape[2]), lambda b, s: (b + b0, 0, 0))]
                 + [_resident(w.shape) for w in weights],
        out_specs=pl.BlockSpec((1, ts, D), lambda b, s: (b, s, 0)),
        out_shape=jax.ShapeDtypeStruct((nb, S, D), F32),
        scratch_shapes=[pltpu.VMEM((POOL_HALO, D), F32)],
        compiler_params=pltpu.CompilerParams(dimension_semantics=("arbitrary", "arbitrary"),
                                             vmem_limit_bytes=VMEM_LIMIT_BYTES),
    )(x, k, v, *weights)


def _topk_rows(vals, payload=None):
    n_rows = vals.shape[0]
    row = lax.broadcasted_iota(jnp.int32, vals.shape, 0).astype(F32)
    out_v, out_i = [], []
    for _ in range(PEER_TOPK):
        m = jnp.max(vals, axis=0, keepdims=True)
        first = jnp.min(jnp.where(vals == m, row, float(n_rows)), axis=0, keepdims=True)
        sel = row == first
        out_v.append(m)
        if payload is None:
            out_i.append(first)
        else:
            out_i.append(jnp.max(jnp.where(sel, payload, -1.0), axis=0, keepdims=True))
        vals = jnp.where(sel, -jnp.inf, vals)
    return jnp.concatenate(out_v, axis=0), jnp.concatenate(out_i, axis=0)


def _candidates(v1, i1, v2, i2):
    sub = 8
    jrow = lax.broadcasted_iota(jnp.int32, (sub, 1), 0)
    vals = [v1[0:1] + v2]
    ids = [i1[0:1] * float(PEER_N_KEYS) + i2]
    for i in range(1, sub):
        keep = jrow < (PEER_TOPK // (i + 1))
        vals.append(jnp.where(keep, v1[i:i + 1] + v2[0:sub], -jnp.inf))
        ids.append(i1[i:i + 1] * float(PEER_N_KEYS) + i2[0:sub])
    vals.append(v1[sub:] + v2[0:1])
    ids.append(i1[sub:] * float(PEER_N_KEYS) + i2[0:1])
    return jnp.concatenate(vals, axis=0), jnp.concatenate(ids, axis=0)


def _select_kernel(h_ref, n2g_ref, wq_ref, k1_ref, k2_ref, xn_ref, exp_ref, gate_ref):
    xn = _rmsnorm(h_ref[...], n2g_ref[...])
    xn_ref[...] = _pack_halves(xn)
    q = _mm(xn, wq_ref[...])
    half = k1_ref.shape[1]
    experts, gates = [], []
    for h in range(PEER_HEADS):
        q1 = q[:, (2 * h) * half:(2 * h + 1) * half]
        q2 = q[:, (2 * h + 1) * half:(2 * h + 2) * half]
        v1, i1 = _topk_rows(_mm_nt(k1_ref[...], q1))
        v2, i2 = _topk_rows(_mm_nt(k2_ref[...], q2))
        best, eid = _topk_rows(*_candidates(v1, i1, v2, i2))
        e = jnp.exp(best - jnp.max(best, axis=0, keepdims=True))
        gates.append(e / jnp.sum(e, axis=0, keepdims=True))
        experts.append(eid)
    exp_ref[...] = jnp.concatenate(experts, axis=0).T.astype(jnp.int32)
    gate_ref[...] = jnp.concatenate(gates, axis=0).T


def _select(h, n2g, w_q, keys1, keys2):
    T, D = h.shape
    tt = min(SELECT_TILE, T)
    hk = PEER_HEADS * PEER_TOPK
    return pl.pallas_call(
        _select_kernel,
        grid=(T // tt,),
        in_specs=[pl.BlockSpec((tt, D), lambda i: (i, 0)),
                  _resident(n2g.shape), _resident(w_q.shape),
                  _resident(keys1.shape), _resident(keys2.shape)],
        out_specs=[pl.BlockSpec((tt, D // 2), lambda i: (i, 0)),
                   pl.BlockSpec((tt, hk), lambda i: (i, 0)),
                   pl.BlockSpec((tt, hk), lambda i: (i, 0))],
        out_shape=[jax.ShapeDtypeStruct((T, D // 2), I32),
                   jax.ShapeDtypeStruct((T, hk), jnp.int32),
                   jax.ShapeDtypeStruct((T, hk), F32)],
        compiler_params=pltpu.CompilerParams(dimension_semantics=("arbitrary",),
                                             vmem_limit_bytes=VMEM_LIMIT_BYTES),
    )(h, n2g, w_q, keys1, keys2)


def _pack_table(table):
    half = table.shape[1] // 2
    bits = lax.bitcast_convert_type(table.astype(jnp.bfloat16), jnp.uint16).astype(jnp.uint32)
    return lax.bitcast_convert_type((bits[:, half:] << 16) | bits[:, :half], I32)


def _bf16_bits(x):
    return lax.bitcast_convert_type(x.astype(jnp.bfloat16).astype(F32), I32)


def _pack_halves(x):
    half = x.shape[1] // 2
    return (_bf16_bits(x[:, half:]) & HIGH_HALF) | lax.shift_right_logical(_bf16_bits(x[:, :half]), 16)


def _pack_splat(x):
    bits = _bf16_bits(x)
    return (bits & HIGH_HALF) | lax.shift_right_logical(bits, 16)


def _unpack_words(w):
    lo = lax.bitcast_convert_type(lax.shift_left(w, jnp.full(w.shape, 16, I32)), F32)
    hi = lax.bitcast_convert_type(lax.bitwise_and(w, jnp.full(w.shape, HIGH_HALF, I32)), F32)
    return lo, hi


def _as_bf16(w):
    return plsc.bitcast(w, jnp.bfloat16)


def _halves_f32(p):
    return _unpack_words(plsc.bitcast(p, I32))


def _sc_kernel(body, out_type, scratch_types):
    mesh = plsc.VectorSubcoreMesh(core_axis_name="c", subcore_axis_name="s")
    return pl.kernel(body, out_type=out_type, mesh=mesh, scratch_types=scratch_types,
                     compiler_params=pltpu.CompilerParams(needs_layout_passes=False))


def _sc_worker_base(tokens_per_worker):
    return (lax.axis_index("s") * SC_CORES + lax.axis_index("c")) * tokens_per_worker


def _sc_pipeline(nchunk, gather, compute):
    ahead = SC_NBUF - 1
    for i in range(ahead):
        gather(i, i).start()

    @pl.loop(0, nchunk)
    def _(ci):
        @pl.when(ci + ahead < nchunk)
        def _():
            gather(ci + ahead, (ci + ahead) % SC_NBUF).start()

        slot = ci % SC_NBUF
        gather(ci, slot).wait()
        compute(ci, slot)


def _sc_dots(xp, experts, ptab):
    T, W = xp.shape
    HK = experts.shape[1]
    L, TG, CH = SC_LANES, SC_TOKENS, SC_CHUNK
    tpw = T // SC_WORKERS
    cpt = HK // CH
    nchunk = TG * cpt
    DG = 8
    assert W == ptab.shape[1] and T % (SC_WORKERS * TG) == 0 and HK % CH == 0 and CH % L == 0
    assert W % (DG * L) == 0 and DG % 2 == 0 and nchunk >= SC_NBUF

    def body(x_hbm, idx_hbm, tab_hbm, out_hbm, x_v, idx_v, out_v, rows_v, acc_v, sem):
        base = _sc_worker_base(tpw)
        lane = lax.iota(I32, L)

        def gather(ci, slot):
            tok, c = ci // cpt, ci % cpt
            return pltpu.make_async_copy(tab_hbm.at[idx_v.at[tok, pl.ds(c * CH, CH)]],
                                         rows_v.at[slot], sem.at[slot])

        def compute(ci, slot):
            tok, c = ci // cpt, ci % cpt
            for eg in range(CH // L):
                @pl.loop(0, W // (DG * L))
                def _(dg):
                    off = dg * (DG * L)
                    cont = jnp.full((L,), dg, I32) != 0
                    zero = jnp.zeros((L,), F32)
                    accs = [jnp.where(cont, acc_v[kk, :], zero) for kk in range(L)]
                    for j in range(0, DG, 2):
                        xa = _as_bf16(x_v[tok, pl.ds(off + j * L, L)])
                        xb = _as_bf16(x_v[tok, pl.ds(off + (j + 1) * L, L)])
                        for kk in range(L):
                            ra = _as_bf16(rows_v[slot, eg * L + kk, pl.ds(off + j * L, L)])
                            rb = _as_bf16(rows_v[slot, eg * L + kk, pl.ds(off + (j + 1) * L, L)])
                            lo, hi = _halves_f32(ra * xa + rb * xb)
                            accs[kk] = accs[kk] + lo + hi
                    for kk in range(L):
                        acc_v[kk, :] = accs[kk]

                s = jnp.zeros((L,), F32)
                for j in range(L):
                    s = s + plsc.load_gather(acc_v, [lane, jnp.full((L,), j, I32)])
                out_v[tok, pl.ds(c * CH + eg * L, L)] = s

        @pl.loop(0, tpw // TG)
        def _(g):
            t0 = base + g * TG
            pltpu.sync_copy(x_hbm.at[pl.ds(t0, TG)], x_v)
            pltpu.sync_copy(idx_hbm.at[pl.ds(t0, TG)], idx_v)
            _sc_pipeline(nchunk, gather, compute)
            pltpu.sync_copy(out_v, out_hbm.at[pl.ds(t0, TG)])

    return _sc_kernel(
        body, jax.ShapeDtypeStruct((T, HK), F32),
        [pltpu.VMEM((TG, W), I32), pltpu.VMEM((TG, HK), I32), pltpu.VMEM((TG, HK), F32),
         pltpu.VMEM((SC_NBUF, CH, W), I32), pltpu.VMEM((L, L), F32),
         pltpu.SemaphoreType.DMA((SC_NBUF,))],
    )(xp, experts, ptab)


def _sc_wsum(wp, experts, ptab):
    T, HK = wp.shape
    W = ptab.shape[1]
    D = 2 * W
    L, TG, CH = SC_LANES, SC_TOKENS, SC_CHUNK
    tpw = T // SC_WORKERS
    cpt = HK // CH
    nchunk = TG * cpt
    DG = 8
    assert T % (SC_WORKERS * TG) == 0 and HK % CH == 0 and CH % 2 == 0 and W % (DG * L) == 0
    assert nchunk >= SC_NBUF

    def body(w_hbm, idx_hbm, tab_hbm, out_hbm, w_v, idx_v, y_v, rows_v, sem):
        base = _sc_worker_base(tpw)

        def gather(ci, slot):
            tok, c = ci // cpt, ci % cpt
            return pltpu.make_async_copy(tab_hbm.at[idx_v.at[tok, pl.ds(c * CH, CH)]],
                                         rows_v.at[slot], sem.at[slot])

        def compute(ci, slot):
            tok, c = ci // cpt, ci % cpt
            tokv = jnp.full((L,), tok, I32)
            cont = jnp.full((L,), c, I32) != 0

            @pl.loop(0, W // (DG * L))
            def _(dg):
                off = dg * (DG * L)
                zero = jnp.zeros((L,), F32)
                alo = [jnp.where(cont, y_v[tok, pl.ds(off + j * L, L)], zero) for j in range(DG)]
                ahi = [jnp.where(cont, y_v[tok, pl.ds(W + off + j * L, L)], zero) for j in range(DG)]
                for kk in range(0, CH, 2):
                    wa = _as_bf16(plsc.load_gather(w_v, [tokv, jnp.full((L,), c * CH + kk, I32)]))
                    wb = _as_bf16(plsc.load_gather(w_v, [tokv, jnp.full((L,), c * CH + kk + 1, I32)]))
                    for j in range(DG):
                        ra = _as_bf16(rows_v[slot, kk, pl.ds(off + j * L, L)])
                        rb = _as_bf16(rows_v[slot, kk + 1, pl.ds(off + j * L, L)])
                        lo, hi = _halves_f32(wa * ra + wb * rb)
                        alo[j] = alo[j] + lo
                        ahi[j] = ahi[j] + hi
                for j in range(DG):
                    y_v[tok, pl.ds(off + j * L, L)] = alo[j]
                    y_v[tok, pl.ds(W + off + j * L, L)] = ahi[j]

        @pl.loop(0, tpw // TG)
        def _(g):
            t0 = base + g * TG
            pltpu.sync_copy(w_hbm.at[pl.ds(t0, TG)], w_v)
            pltpu.sync_copy(idx_hbm.at[pl.ds(t0, TG)], idx_v)
            _sc_pipeline(nchunk, gather, compute)
            pltpu.sync_copy(y_v, out_hbm.at[pl.ds(t0, TG)])

    return _sc_kernel(
        body, jax.ShapeDtypeStruct((T, D), F32),
        [pltpu.VMEM((TG, HK), I32), pltpu.VMEM((TG, HK), I32), pltpu.VMEM((TG, D), F32),
         pltpu.VMEM((SC_NBUF, CH, W), I32), pltpu.SemaphoreType.DMA((SC_NBUF,))],
    )(wp, experts, ptab)


def _tc_wsum_kernel(exp_ref, w_ref, tab_ref, y_ref):
    tb, hk = exp_ref.shape
    sub = tab_ref.shape[1]

    def token(t, carry):
        lo_acc = jnp.zeros(tab_ref.shape[1:], F32)
        hi_acc = jnp.zeros(tab_ref.shape[1:], F32)
        for k in range(hk):
            row = tab_ref[exp_ref[t, k]]
            w = w_ref[t, k]
            lo_acc = lo_acc + w * lax.bitcast_convert_type(lax.shift_left(row, 16), F32)
            hi_acc = hi_acc + w * lax.bitcast_convert_type(row & HIGH_HALF, F32)
        y_ref[t, pl.ds(0, sub), :] = lo_acc
        y_ref[t, pl.ds(sub, sub), :] = hi_acc
        return carry

    lax.fori_loop(0, tb, token, 0)


def _tc_wsum(w, experts, ptab):
    T, hk = w.shape
    E, W = ptab.shape
    sub = W // TC_LANES
    tb = min(TC_WSUM_TILE, T)
    smem = lambda: pl.BlockSpec((tb, hk), lambda i: (i, 0), memory_space=pltpu.SMEM)
    y = pl.pallas_call(
        _tc_wsum_kernel, grid=(T // tb,),
        in_specs=[smem(), smem(), _resident((E, sub, TC_LANES))],
        out_specs=pl.BlockSpec((tb, 2 * sub, TC_LANES), lambda i: (i, 0, 0)),
        out_shape=jax.ShapeDtypeStruct((T, 2 * sub, TC_LANES), F32),
        cost_estimate=pl.CostEstimate(flops=4 * T * hk * W, transcendentals=0,
                                      bytes_accessed=4 * (T * hk * W + E * W + 2 * T * W + 2 * T * hk)),
        compiler_params=pltpu.CompilerParams(dimension_semantics=("arbitrary",),
                                             vmem_limit_bytes=VMEM_LIMIT_BYTES),
    )(experts, w, ptab.reshape(E, sub, TC_LANES))
    return y.reshape(T, 2 * W)


def _act_kernel(dots_ref, gate_ref, w_ref, wp_ref):
    w = _gelu(dots_ref[...]) * gate_ref[...]
    w_ref[...] = w
    wp_ref[...] = _pack_splat(w)


def _act(dots, gates):
    T, hk = dots.shape
    tt = min(ROW_TILE, T)
    spec = pl.BlockSpec((tt, hk), lambda i: (i, 0))
    return pl.pallas_call(
        _act_kernel, grid=(T // tt,), in_specs=[spec, spec], out_specs=[spec, spec],
        out_shape=[jax.ShapeDtypeStruct((T, hk), F32), jax.ShapeDtypeStruct((T, hk), I32)],
        compiler_params=pltpu.CompilerParams(dimension_semantics=("arbitrary",)),
    )(dots, gates)


def _final_kernel(h_ref, y_ref, g_ref, *rest):
    o_ref = rest[-1]
    o_ref[...] = _rmsnorm(h_ref[...] + y_ref[...], g_ref[...])


def _final(h, y, gain, out, row0, total_rows):
    T, D = h.shape
    tt = min(ROW_TILE, T)
    assert row0 % tt == 0
    spec = pl.BlockSpec((tt, D), lambda i: (i, 0))
    out_spec = pl.BlockSpec((tt, D), lambda i: (i + row0 // tt, 0))
    in_specs = [spec, spec, pl.BlockSpec((1, D), lambda i: (0, 0))]
    args = (h, y, gain)
    aliases = {}
    if out is not None:
        in_specs.append(pl.BlockSpec(memory_space=pl.ANY))
        args += (out,)
        aliases = {3: 0}
    return pl.pallas_call(
        _final_kernel, grid=(T // tt,), in_specs=in_specs, out_specs=out_spec,
        out_shape=jax.ShapeDtypeStruct((total_rows, D), F32),
        input_output_aliases=aliases,
        compiler_params=pltpu.CompilerParams(dimension_semantics=("arbitrary",)),
    )(*args)


def kernel(x, mem, norm1_gain, w_in, pool_w, pool_scale, sgu_ln_gain, sgu_ln_bias, sgu_w_s, sgu_b_s,
           sgu_w_out, mem_norm_gain, xa_w_kv, xa_w_out, w_out, norm2_gain, peer_w_q, peer_keys1,
           peer_keys2, peer_u, peer_v, final_norm_gain):
    B, S, D = x.shape
    depth = w_in.shape[0]
    lo = lambda w: w.astype(MXU_DTYPE)
    row = lambda w: w.reshape(1, -1)
    assert depth == 1, "the final RMSNorm is fused with the last layer's residual add"
    l = 0
    weights = (row(norm1_gain[l]), lo(w_in[l]), lo(pool_w[l]), row(pool_scale[l]),
               row(sgu_ln_gain[l]), row(sgu_ln_bias[l]), sgu_w_s[l], sgu_b_s[l].T,
               lo(sgu_w_out[l]), lo(xa_w_out[l]), lo(w_out[l]))
    select_w = (row(norm2_gain[l]), lo(peer_w_q[l]), lo(peer_keys1[l]), lo(peer_keys2[l]))
    u_packed, v_packed = _pack_table(peer_u[l]), _pack_table(peer_v[l])
    k, v = _memkv(mem, row(mem_norm_gain[l]), lo(xa_w_kv[l]))
    fgain = row(final_norm_gain)
    chunks = BATCH_CHUNKS if sum(BATCH_CHUNKS) == B else (B,)
    out, b0, tc_sums, tc_h = None, 0, [], []
    for c, nb in enumerate(chunks):
        rows = nb * S
        h = _mixer(x, k, v, weights, b0, nb).reshape(rows, D)
        xn, experts, gates = _select(h, *select_w)
        dots = _sc_dots(xn, experts, u_packed)
        if c > 0:
            dots, tc_sums[c - 1] = lax.optimization_barrier((dots, tc_sums[c - 1]))
        w, wp = _act(dots, gates)
        n_tc = (rows * TC_WSUM_SHARE[0] // TC_WSUM_SHARE[1]) // SC_ROW_QUANTUM * SC_ROW_QUANTUM
        tc_sums.append(_tc_wsum(w[:n_tc], experts[:n_tc], v_packed))
        tc_h.append(h[:n_tc])
        y_sc = _sc_wsum(wp[n_tc:], experts[n_tc:], v_packed)
        out = _final(h[n_tc:], y_sc, fgain, out, b0 * S + n_tc, B * S)
        b0 += nb
    b0 = 0
    for nb, h_tc, y_tc in zip(chunks, tc_h, tc_sums):
        out = _final(h_tc, y_tc, fgain, out, b0 * S, B * S)
        b0 += nb
    return out.reshape(B, S, D)
```

```python
import functools
import math

import jax
import jax.numpy as jnp
from jax import lax
from jax.experimental import pallas as pl
from jax.experimental.pallas import tpu as pltpu
from jax.experimental.pallas import tpu_sc as plsc

F32 = jnp.float32
I32 = jnp.int32
MXU_DTYPE = jnp.bfloat16

RMS_EPS = 1e-6
LN_EPS = 1e-5
POOL_WINDOWS = (2, 4, 8, 16)
POOL_HALO = 16
SGU_CHUNK = 128
SGU_HEADS = 8
XA_HEADS = 4
PEER_HEADS = 8
PEER_N_KEYS = 128
PEER_TOPK = 16

V7X_VMEM_BYTES = 64 * 1024 * 1024
VMEM_LIMIT_BYTES = 56 * 1024 * 1024

SEQ_TILE = 256
SELECT_TILE = 256
ROW_TILE = 512
TC_WSUM_TILE = 32
TC_LANES = 128
BATCH_CHUNKS = (1, 1, 2, 2, 2)

SC_CORES = 2
SC_SUBCORES = 16
SC_LANES = 16
SC_WORKERS = SC_CORES * SC_SUBCORES
SC_TOKENS = 16
SC_CHUNK = 32
SC_NBUF = 4
HIGH_HALF = -65536
SC_ROW_QUANTUM = SC_WORKERS * SC_TOKENS
TC_WSUM_SHARE = (5, 8)


def _rmsnorm(x, gain):
    return x * lax.rsqrt(jnp.mean(x * x, axis=-1, keepdims=True) + RMS_EPS) * gain


def _gelu(x):
    return 0.5 * x * (1.0 + lax.erf(x * (1.0 / math.sqrt(2.0))))


def _mm(a, b):
    return jnp.dot(a.astype(MXU_DTYPE), b.astype(MXU_DTYPE), preferred_element_type=F32)


def _mm_nt(a, b):
    return lax.dot_general(a.astype(MXU_DTYPE), b.astype(MXU_DTYPE),
                           (((1,), (1,)), ((), ())), preferred_element_type=F32)


def _memkv_kernel(mem_ref, gain_ref, wkv_ref, k_ref, v_ref):
    width = k_ref.shape[-1]
    kv = _mm(_rmsnorm(mem_ref[0], gain_ref[...]), wkv_ref[...])
    k_ref[0] = kv[:, :width].astype(k_ref.dtype)
    v_ref[0] = kv[:, width:].astype(v_ref.dtype)


def _memkv(mem, gain, w_kv):
    B, M, D = mem.shape
    width = w_kv.shape[1] // 2
    const = lambda b: (0, 0)
    return pl.pallas_call(
        _memkv_kernel,
        grid=(B,),
        in_specs=[pl.BlockSpec((1, M, D), lambda b: (b, 0, 0)),
                  pl.BlockSpec((1, D), const),
                  pl.BlockSpec(w_kv.shape, const)],
        out_specs=[pl.BlockSpec((1, M, width), lambda b: (b, 0, 0))] * 2,
        out_shape=[jax.ShapeDtypeStruct((B, M, width), MXU_DTYPE)] * 2,
        compiler_params=pltpu.CompilerParams(dimension_semantics=("arbitrary",),
                                             vmem_limit_bytes=VMEM_LIMIT_BYTES),
    )(mem, gain, w_kv)


def _mixer_kernel(x_ref, k_ref, v_ref, n1g_ref, win_ref, poolw_ref, pools_ref, lng_ref, lnb_ref,
                  ws_ref, bst_ref, sguwo_ref, xawo_ref, wout_ref, h_ref, tail_ref):
    ts, d = x_ref.shape[1], x_ref.shape[2]
    s_idx = pl.program_id(1)
    x = x_ref[0]
    nb = _rmsnorm(x, n1g_ref[...]).astype(MXU_DTYPE)

    def proj(col):
        return jnp.dot(nb, win_ref[:, col * d:(col + 1) * d], preferred_element_type=F32)

    @pl.when(s_idx == 0)
    def _():
        tail_ref[...] = jnp.zeros_like(tail_ref)

    p = proj(0)
    ext = jnp.concatenate([tail_ref[...], p], axis=0)
    tail_ref[...] = p[ts - POOL_HALO:, :]
    pos = s_idx * ts + lax.broadcasted_iota(jnp.int32, (ts, 1), 0)
    group = d // len(POOL_WINDOWS)
    y_pool = []
    for g, w in enumerate(POOL_WINDOWS):
        acc = ext[:, g * group:(g + 1) * group]
        shift = 1
        while shift < w:
            acc = acc + pltpu.roll(acc, shift, 0)
            shift *= 2
        count = jnp.minimum(pos + 1, w).astype(F32)
        diff = acc[POOL_HALO:, :] / count - p[:, g * group:(g + 1) * group]
        y_pool.append(_mm(diff, poolw_ref[g]))
    y_pool = jnp.concatenate(y_pool, axis=1) * pools_ref[...]

    u = _gelu(proj(1))
    v = _gelu(proj(2))
    mu = jnp.mean(v, axis=-1, keepdims=True)
    vc = v - mu
    var = jnp.mean(vc * vc, axis=-1, keepdims=True)
    v = (vc * lax.rsqrt(var + LN_EPS) * lng_ref[...] + lnb_ref[...]).astype(MXU_DTYPE)
    hd = d // SGU_HEADS
    causal = (lax.broadcasted_iota(jnp.int32, (SGU_CHUNK, SGU_CHUNK), 0)
              >= lax.broadcasted_iota(jnp.int32, (SGU_CHUNK, SGU_CHUNK), 1))
    mixed_rows = []
    w_masked = [jnp.where(causal, ws_ref[h], 0.0).astype(MXU_DTYPE) for h in range(SGU_HEADS)]
    for c in range(ts // SGU_CHUNK):
        rows = slice(c * SGU_CHUNK, (c + 1) * SGU_CHUNK)
        heads = []
        for h in range(SGU_HEADS):
            mixed = jnp.dot(w_masked[h], v[rows, h * hd:(h + 1) * hd], preferred_element_type=F32)
            heads.append(mixed + bst_ref[:, h:h + 1])
        mixed_rows.append(jnp.concatenate(heads, axis=1))
    mixed = jnp.concatenate(mixed_rows, axis=0) if len(mixed_rows) > 1 else mixed_rows[0]
    y_sgu = _mm(u * mixed, sguwo_ref[...])

    q = proj(3).astype(MXU_DTYPE)
    xd = d // XA_HEADS
    outs = []
    for h in range(XA_HEADS):
        cols = slice(h * xd, (h + 1) * xd)
        s = _mm_nt(q[:, cols], k_ref[0, :, cols]) * (xd ** -0.5)
        e = jnp.exp(s - jnp.max(s, axis=-1, keepdims=True))
        probs = e / jnp.sum(e, axis=-1, keepdims=True)
        outs.append(_mm(probs, v_ref[0, :, cols]))
    y_xa = _mm(jnp.concatenate(outs, axis=1), xawo_ref[...])

    merged = (jax.nn.sigmoid(proj(4)) * y_pool + jax.nn.sigmoid(proj(5)) * y_sgu
              + jax.nn.sigmoid(proj(6)) * y_xa)
    h_ref[0] = x + _mm(merged, wout_ref[...])


def _resident(shape):
    zeros = (0,) * len(shape)
    return pl.BlockSpec(shape, lambda *_: zeros, pipeline_mode=pl.Buffered(1))


def _mixer(x, k, v, weights, b0, nb):
    _, S, D = x.shape
    ts = min(SEQ_TILE, S)
    M = k.shape[1]
    return pl.pallas_call(
        _mixer_kernel,
        grid=(nb, S // ts),
        in_specs=[pl.BlockSpec((1, ts, D), lambda b, s: (b + b0, s, 0)),
                  pl.BlockSpec((1, M, k.shape[2]), lambda b, s: (b + b0, 0, 0)),
                  pl.BlockSpec((1, M, v.shape[2]), lambda b, s: (b + b0, 0, 0))]
                 + [_resident(w.shape) for w in weights],
        out_specs=pl.BlockSpec((1, ts, D), lambda b, s: (b, s, 0)),
        out_shape=jax.ShapeDtypeStruct((nb, S, D), F32),
        scratch_shapes=[pltpu.VMEM((POOL_HALO, D), F32)],
        compiler_params=pltpu.CompilerParams(dimension_semantics=("arbitrary", "arbitrary"),
                                             vmem_limit_bytes=VMEM_LIMIT_BYTES),
    )(x, k, v, *weights)


def _topk_rows(vals, payload=None):
    n_rows = vals.shape[0]
    row = lax.broadcasted_iota(jnp.int32, vals.shape, 0).astype(F32)
    out_v, out_i = [], []
    for _ in range(PEER_TOPK):
        m = jnp.max(vals, axis=0, keepdims=True)
        first = jnp.min(jnp.where(vals == m, row, float(n_rows)), axis=0, keepdims=True)
        sel = row == first
        out_v.append(m)
        if payload is None:
            out_i.append(first)
        else:
            out_i.append(jnp.max(jnp.where(sel, payload, -1.0), axis=0, keepdims=True))
        vals = jnp.where(sel, -jnp.inf, vals)
    return jnp.concatenate(out_v, axis=0), jnp.concatenate(out_i, axis=0)


def _candidates(v1, i1, v2, i2):
    sub = 8
    jrow = lax.broadcasted_iota(jnp.int32, (sub, 1), 0)
    vals = [v1[0:1] + v2]
    ids = [i1[0:1] * float(PEER_N_KEYS) + i2]
    for i in range(1, sub):
        keep = jrow < (PEER_TOPK // (i + 1))
        vals.append(jnp.where(keep, v1[i:i + 1] + v2[0:sub], -jnp.inf))
        ids.append(i1[i:i + 1] * float(PEER_N_KEYS) + i2[0:sub])
    vals.append(v1[sub:] + v2[0:1])
    ids.append(i1[sub:] * float(PEER_N_KEYS) + i2[0:1])
    return jnp.concatenate(vals, axis=0), jnp.concatenate(ids, axis=0)


def _select_kernel(h_ref, n2g_ref, wq_ref, k1_ref, k2_ref, xn_ref, exp_ref, gate_ref):
    xn = _rmsnorm(h_ref[...], n2g_ref[...])
    xn_ref[...] = _pack_halves(xn)
    q = _mm(xn, wq_ref[...])
    half = k1_ref.shape[1]
    experts, gates = [], []
    for h in range(PEER_HEADS):
        q1 = q[:, (2 * h) * half:(2 * h + 1) * half]
        q2 = q[:, (2 * h + 1) * half:(2 * h + 2) * half]
        v1, i1 = _topk_rows(_mm_nt(k1_ref[...], q1))
        v2, i2 = _topk_rows(_mm_nt(k2_ref[...], q2))
        best, eid = _topk_rows(*_candidates(v1, i1, v2, i2))
        e = jnp.exp(best - jnp.max(best, axis=0, keepdims=True))
        gates.append(e / jnp.sum(e, axis=0, keepdims=True))
        experts.append(eid)
    exp_ref[...] = jnp.concatenate(experts, axis=0).T.astype(jnp.int32)
    gate_ref[...] = jnp.concatenate(gates, axis=0).T


def _select(h, n2g, w_q, keys1, keys2):
    T, D = h.shape
    tt = min(SELECT_TILE, T)
    hk = PEER_HEADS * PEER_TOPK
    return pl.pallas_call(
        _select_kernel,
        grid=(T // tt,),
        in_specs=[pl.BlockSpec((tt, D), lambda i: (i, 0)),
                  _resident(n2g.shape), _resident(w_q.shape),
                  _resident(keys1.shape), _resident(keys2.shape)],
        out_specs=[pl.BlockSpec((tt, D // 2), lambda i: (i, 0)),
                   pl.BlockSpec((tt, hk), lambda i: (i, 0)),
                   pl.BlockSpec((tt, hk), lambda i: (i, 0))],
        out_shape=[jax.ShapeDtypeStruct((T, D // 2), I32),
                   jax.ShapeDtypeStruct((T, hk), jnp.int32),
                   jax.ShapeDtypeStruct((T, hk), F32)],
        compiler_params=pltpu.CompilerParams(dimension_semantics=("arbitrary",),
                                             vmem_limit_bytes=VMEM_LIMIT_BYTES),
    )(h, n2g, w_q, keys1, keys2)


def _pack_table(table):
    half = table.shape[1] // 2
    bits = lax.bitcast_convert_type(table.astype(jnp.bfloat16), jnp.uint16).astype(jnp.uint32)
    return lax.bitcast_convert_type((bits[:, half:] << 16) | bits[:, :half], I32)


def _bf16_bits(x):
    return lax.bitcast_convert_type(x.astype(jnp.bfloat16).astype(F32), I32)


def _pack_halves(x):
    half = x.shape[1] // 2
    return (_bf16_bits(x[:, half:]) & HIGH_HALF) | lax.shift_right_logical(_bf16_bits(x[:, :half]), 16)


def _pack_splat(x):
    bits = _bf16_bits(x)
    return (bits & HIGH_HALF) | lax.shift_right_logical(bits, 16)


def _unpack_words(w):
    lo = lax.bitcast_convert_type(lax.shift_left(w, jnp.full(w.shape, 16, I32)), F32)
    hi = lax.bitcast_convert_type(lax.bitwise_and(w, jnp.full(w.shape, HIGH_HALF, I32)), F32)
    return lo, hi


def _as_bf16(w):
    return plsc.bitcast(w, jnp.bfloat16)


def _halves_f32(p):
    return _unpack_words(plsc.bitcast(p, I32))


def _sc_kernel(body, out_type, scratch_types):
    mesh = plsc.VectorSubcoreMesh(core_axis_name="c", subcore_axis_name="s")
    return pl.kernel(body, out_type=out_type, mesh=mesh, scratch_types=scratch_types,
                     compiler_params=pltpu.CompilerParams(needs_layout_passes=False))


def _sc_worker_base(tokens_per_worker):
    return (lax.axis_index("s") * SC_CORES + lax.axis_index("c")) * tokens_per_worker


def _sc_pipeline(nchunk, gather, compute):
    ahead = SC_NBUF - 1
    for i in range(ahead):
        gather(i, i).start()

    @pl.loop(0, nchunk)
    def _(ci):
        @pl.when(ci + ahead < nchunk)
        def _():
            gather(ci + ahead, (ci + ahead) % SC_NBUF).start()

        slot = ci % SC_NBUF
        gather(ci, slot).wait()
        compute(ci, slot)


def _sc_dots(xp, experts, ptab):
    T, W = xp.shape
    HK = experts.shape[1]
    L, TG, CH = SC_LANES, SC_TOKENS, SC_CHUNK
    tpw = T // SC_WORKERS
    cpt = HK // CH
    nchunk = TG * cpt
    assert W == ptab.shape[1] and T % (SC_WORKERS * TG) == 0 and HK % CH == 0 and CH % L == 0
    assert W % (2 * L) == 0 and nchunk >= SC_NBUF

    def body(x_hbm, idx_hbm, tab_hbm, out_hbm, x_v, idx_v, out_v, rows_v, acc_v, sem):
        base = _sc_worker_base(tpw)
        lane = lax.iota(I32, L)

        def gather(ci, slot):
            tok, c = ci // cpt, ci % cpt
            return pltpu.make_async_copy(tab_hbm.at[idx_v.at[tok, pl.ds(c * CH, CH)]],
                                         rows_v.at[slot], sem.at[slot])

        def compute(ci, slot):
            tok, c = ci // cpt, ci % cpt
            for eg in range(CH // L):
                accs = [jnp.zeros((L,), F32) for _ in range(L)]
                for j in range(0, W // L, 2):
                    xa = _as_bf16(x_v[tok, pl.ds(j * L, L)])
                    xb = _as_bf16(x_v[tok, pl.ds((j + 1) * L, L)])
                    for kk in range(L):
                        ra = _as_bf16(rows_v[slot, eg * L + kk, pl.ds(j * L, L)])
                        rb = _as_bf16(rows_v[slot, eg * L + kk, pl.ds((j + 1) * L, L)])
                        lo, hi = _halves_f32(ra * xa + rb * xb)
                        accs[kk] = accs[kk] + lo + hi
                for kk in range(L):
                    acc_v[kk, :] = accs[kk]

                s = jnp.zeros((L,), F32)
                for j in range(L):
                    s = s + plsc.load_gather(acc_v, [lane, jnp.full((L,), j, I32)])
                out_v[tok, pl.ds(c * CH + eg * L, L)] = s

        @pl.loop(0, tpw // TG)
        def _(g):
            t0 = base + g * TG
            pltpu.sync_copy(x_hbm.at[pl.ds(t0, TG)], x_v)
            pltpu.sync_copy(idx_hbm.at[pl.ds(t0, TG)], idx_v)
            _sc_pipeline(nchunk, gather, compute)
            pltpu.sync_copy(out_v, out_hbm.at[pl.ds(t0, TG)])

    return _sc_kernel(
        body, jax.ShapeDtypeStruct((T, HK), F32),
        [pltpu.VMEM((TG, W), I32), pltpu.VMEM((TG, HK), I32), pltpu.VMEM((TG, HK), F32),
         pltpu.VMEM((SC_NBUF, CH, W), I32), pltpu.VMEM((L, L), F32),
         pltpu.SemaphoreType.DMA((SC_NBUF,))],
    )(xp, experts, ptab)


def _sc_wsum(wp, experts, ptab):
    T, HK = wp.shape
    W = ptab.shape[1]
    D = 2 * W
    L, TG, CH = SC_LANES, SC_TOKENS, SC_CHUNK
    tpw = T // SC_WORKERS
    cpt = HK // CH
    nchunk = TG * cpt
    DG = 8
    assert T % (SC_WORKERS * TG) == 0 and HK % CH == 0 and CH % 2 == 0 and W % (DG * L) == 0
    assert nchunk >= SC_NBUF

    def body(w_hbm, idx_hbm, tab_hbm, out_hbm, w_v, idx_v, y_v, rows_v, sem):
        base = _sc_worker_base(tpw)

        def gather(ci, slot):
            tok, c = ci // cpt, ci % cpt
            return pltpu.make_async_copy(tab_hbm.at[idx_v.at[tok, pl.ds(c * CH, CH)]],
                                         rows_v.at[slot], sem.at[slot])

        def compute(ci, slot):
            tok, c = ci // cpt, ci % cpt
            tokv = jnp.full((L,), tok, I32)
            cont = jnp.full((L,), c, I32) != 0

            @pl.loop(0, W // (DG * L))
            def _(dg):
                off = dg * (DG * L)
                zero = jnp.zeros((L,), F32)
                alo = [jnp.where(cont, y_v[tok, pl.ds(off + j * L, L)], zero) for j in range(DG)]
                ahi = [jnp.where(cont, y_v[tok, pl.ds(W + off + j * L, L)], zero) for j in range(DG)]
                for kk in range(0, CH, 2):
                    wa = _as_bf16(plsc.load_gather(w_v, [tokv, jnp.full((L,), c * CH + kk, I32)]))
                    wb = _as_bf16(plsc.load_gather(w_v, [tokv, jnp.full((L,), c * CH + kk + 1, I32)]))
                    for j in range(DG):
                        ra = _as_bf16(rows_v[slot, kk, pl.ds(off + j * L, L)])
                        rb = _as_bf16(rows_v[slot, kk + 1, pl.ds(off + j * L, L)])
                        lo, hi = _halves_f32(wa * ra + wb * rb)
                        alo[j] = alo[j] + lo
                        ahi[j] = ahi[j] + hi
                for j in range(DG):
                    y_v[tok, pl.ds(off + j * L, L)] = alo[j]
                    y_v[tok, pl.ds(W + off + j * L, L)] = ahi[j]

        @pl.loop(0, tpw // TG)
        def _(g):
            t0 = base + g * TG
            pltpu.sync_copy(w_hbm.at[pl.ds(t0, TG)], w_v)
            pltpu.sync_copy(idx_hbm.at[pl.ds(t0, TG)], idx_v)
            _sc_pipeline(nchunk, gather, compute)
            pltpu.sync_copy(y_v, out_hbm.at[pl.ds(t0, TG)])

    return _sc_kernel(
        body, jax.ShapeDtypeStruct((T, D), F32),
        [pltpu.VMEM((TG, HK), I32), pltpu.VMEM((TG, HK), I32), pltpu.VMEM((TG, D), F32),
         pltpu.VMEM((SC_NBUF, CH, W), I32), pltpu.SemaphoreType.DMA((SC_NBUF,))],
    )(wp, experts, ptab)


def _tc_wsum_kernel(exp_ref, w_ref, tab_ref, y_ref):
    tb, hk = exp_ref.shape
    sub = tab_ref.shape[1]

    def token(t, carry):
        lo_acc = jnp.zeros(tab_ref.shape[1:], F32)
        hi_acc = jnp.zeros(tab_ref.shape[1:], F32)
        for k in range(hk):
            row = tab_ref[exp_ref[t, k]]
            w = w_ref[t, k]
            lo_acc = lo_acc + w * lax.bitcast_convert_type(lax.shift_left(row, 16), F32)
            hi_acc = hi_acc + w * lax.bitcast_convert_type(row & HIGH_HALF, F32)
        y_ref[t, pl.ds(0, sub), :] = lo_acc
        y_ref[t, pl.ds(sub, sub), :] = hi_acc
        return carry

    lax.fori_loop(0, tb, token, 0)


def _tc_wsum(w, experts, ptab):
    T, hk = w.shape
    E, W = ptab.shape
    sub = W // TC_LANES
    tb = min(TC_WSUM_TILE, T)
    smem = lambda: pl.BlockSpec((tb, hk), lambda i: (i, 0), memory_space=pltpu.SMEM)
    y = pl.pallas_call(
        _tc_wsum_kernel, grid=(T // tb,),
        in_specs=[smem(), smem(), _resident((E, sub, TC_LANES))],
        out_specs=pl.BlockSpec((tb, 2 * sub, TC_LANES), lambda i: (i, 0, 0)),
        out_shape=jax.ShapeDtypeStruct((T, 2 * sub, TC_LANES), F32),
        cost_estimate=pl.CostEstimate(flops=4 * T * hk * W, transcendentals=0,
                                      bytes_accessed=4 * (T * hk * W + E * W + 2 * T * W + 2 * T * hk)),
        compiler_params=pltpu.CompilerParams(dimension_semantics=("arbitrary",),
                                             vmem_limit_bytes=VMEM_LIMIT_BYTES),
    )(experts, w, ptab.reshape(E, sub, TC_LANES))
    return y.reshape(T, 2 * W)


def _act_kernel(dots_ref, gate_ref, w_ref, wp_ref):
    w = _gelu(dots_ref[...]) * gate_ref[...]
    w_ref[...] = w
    wp_ref[...] = _pack_splat(w)


def _act(dots, gates):
    T, hk = dots.shape
    tt = min(ROW_TILE, T)
    spec = pl.BlockSpec((tt, hk), lambda i: (i, 0))
    return pl.pallas_call(
        _act_kernel, grid=(T // tt,), in_specs=[spec, spec], out_specs=[spec, spec],
        out_shape=[jax.ShapeDtypeStruct((T, hk), F32), jax.ShapeDtypeStruct((T, hk), I32)],
        compiler_params=pltpu.CompilerParams(dimension_semantics=("arbitrary",)),
    )(dots, gates)


def _final_kernel(h_ref, y_ref, g_ref, *rest):
    o_ref = rest[-1]
    o_ref[...] = _rmsnorm(h_ref[...] + y_ref[...], g_ref[...])


def _final(h, y, gain, out, row0, total_rows):
    T, D = h.shape
    tt = min(ROW_TILE, T)
    assert row0 % tt == 0
    spec = pl.BlockSpec((tt, D), lambda i: (i, 0))
    out_spec = pl.BlockSpec((tt, D), lambda i: (i + row0 // tt, 0))
    in_specs = [spec, spec, pl.BlockSpec((1, D), lambda i: (0, 0))]
    args = (h, y, gain)
    aliases = {}
    if out is not None:
        in_specs.append(pl.BlockSpec(memory_space=pl.ANY))
        args += (out,)
        aliases = {3: 0}
    return pl.pallas_call(
        _final_kernel, grid=(T // tt,), in_specs=in_specs, out_specs=out_spec,
        out_shape=jax.ShapeDtypeStruct((total_rows, D), F32),
        input_output_aliases=aliases,
        compiler_params=pltpu.CompilerParams(dimension_semantics=("arbitrary",)),
    )(*args)


def kernel(x, mem, norm1_gain, w_in, pool_w, pool_scale, sgu_ln_gain, sgu_ln_bias, sgu_w_s, sgu_b_s,
           sgu_w_out, mem_norm_gain, xa_w_kv, xa_w_out, w_out, norm2_gain, peer_w_q, peer_keys1,
           peer_keys2, peer_u, peer_v, final_norm_gain):
    B, S, D = x.shape
    depth = w_in.shape[0]
    lo = lambda w: w.astype(MXU_DTYPE)
    row = lambda w: w.reshape(1, -1)
    assert depth == 1, "the final RMSNorm is fused with the last layer's residual add"
    l = 0
    weights = (row(norm1_gain[l]), lo(w_in[l]), lo(pool_w[l]), row(pool_scale[l]),
               row(sgu_ln_gain[l]), row(sgu_ln_bias[l]), sgu_w_s[l], sgu_b_s[l].T,
               lo(sgu_w_out[l]), lo(xa_w_out[l]), lo(w_out[l]))
    select_w = (row(norm2_gain[l]), lo(peer_w_q[l]), lo(peer_keys1[l]), lo(peer_keys2[l]))
    u_packed, v_packed = _pack_table(peer_u[l]), _pack_table(peer_v[l])
    k, v = _memkv(mem, row(mem_norm_gain[l]), lo(xa_w_kv[l]))
    fgain = row(final_norm_gain)
    chunks = BATCH_CHUNKS if sum(BATCH_CHUNKS) == B else (B,)
    b0, tc_sums, tc_parts, sc_parts = 0, [], [], []
    for c, nb in enumerate(chunks):
        rows = nb * S
        h = _mixer(x, k, v, weights, b0, nb).reshape(rows, D)
        xn, experts, gates = _select(h, *select_w)
        dots = _sc_dots(xn, experts, u_packed)
        if c > 0:
            dots, tc_sums[c - 1] = lax.optimization_barrier((dots, tc_sums[c - 1]))
        w, wp = _act(dots, gates)
        n_tc = (rows * TC_WSUM_SHARE[0] // TC_WSUM_SHARE[1]) // SC_ROW_QUANTUM * SC_ROW_QUANTUM
        tc_sums.append(_tc_wsum(w[:n_tc], experts[:n_tc], v_packed))
        tc_parts.append((h[:n_tc], b0 * S))
        sc_parts.append((h[n_tc:], _sc_wsum(wp[n_tc:], experts[n_tc:], v_packed), b0 * S + n_tc))
        b0 += nb
    out = None
    for (h_tc, row0), y_tc in zip(tc_parts, tc_sums):
        out = _final(h_tc, y_tc, fgain, out, row0, B * S)
    for h_sc, y_sc, row0 in sc_parts:
        out = _final(h_sc, y_sc, fgain, out, row0, B * S)
    return out.reshape(B, S, D)
```

```python
import functools
import math

import jax
import jax.numpy as jnp
from jax import lax
from jax.experimental import pallas as pl
from jax.experimental.pallas import tpu as pltpu
from jax.experimental.pallas import tpu_sc as plsc

F32 = jnp.float32
I32 = jnp.int32
MXU_DTYPE = jnp.bfloat16

RMS_EPS = 1e-6
LN_EPS = 1e-5
POOL_WINDOWS = (2, 4, 8, 16)
POOL_HALO = 16
SGU_CHUNK = 128
SGU_HEADS = 8
XA_HEADS = 4
PEER_HEADS = 8
PEER_N_KEYS = 128
PEER_TOPK = 16

V7X_VMEM_BYTES = 64 * 1024 * 1024
VMEM_LIMIT_BYTES = 56 * 1024 * 1024

SEQ_TILE = 256
SELECT_TILE = 256
ROW_TILE = 512
TC_WSUM_TILE = 8
TC_LANES = 128
BATCH_CHUNKS = (1, 1, 2, 2, 2)

SC_CORES = 2
SC_SUBCORES = 16
SC_LANES = 16
SC_WORKERS = SC_CORES * SC_SUBCORES
SC_TOKENS = 16
SC_CHUNK = 32
SC_NBUF = 4
HIGH_HALF = -65536
SC_ROW_QUANTUM = SC_WORKERS * SC_TOKENS
TC_WSUM_SHARE = (9, 16)


def _rmsnorm(x, gain):
    return x * lax.rsqrt(jnp.mean(x * x, axis=-1, keepdims=True) + RMS_EPS) * gain


def _gelu(x):
    return 0.5 * x * (1.0 + lax.erf(x * (1.0 / math.sqrt(2.0))))


def _mm(a, b):
    return jnp.dot(a.astype(MXU_DTYPE), b.astype(MXU_DTYPE), preferred_element_type=F32)


def _mm_nt(a, b):
    return lax.dot_general(a.astype(MXU_DTYPE), b.astype(MXU_DTYPE),
                           (((1,), (1,)), ((), ())), preferred_element_type=F32)


def _memkv_kernel(mem_ref, gain_ref, wkv_ref, k_ref, v_ref):
    width = k_ref.shape[-1]
    kv = _mm(_rmsnorm(mem_ref[0], gain_ref[...]), wkv_ref[...])
    k_ref[0] = kv[:, :width].astype(k_ref.dtype)
    v_ref[0] = kv[:, width:].astype(v_ref.dtype)


def _memkv(mem, gain, w_kv):
    B, M, D = mem.shape
    width = w_kv.shape[1] // 2
    const = lambda b: (0, 0)
    return pl.pallas_call(
        _memkv_kernel,
        grid=(B,),
        in_specs=[pl.BlockSpec((1, M, D), lambda b: (b, 0, 0)),
                  pl.BlockSpec((1, D), const),
                  pl.BlockSpec(w_kv.shape, const)],
        out_specs=[pl.BlockSpec((1, M, width), lambda b: (b, 0, 0))] * 2,
        out_shape=[jax.ShapeDtypeStruct((B, M, width), MXU_DTYPE)] * 2,
        compiler_params=pltpu.CompilerParams(dimension_semantics=("arbitrary",),
                                             vmem_limit_bytes=VMEM_LIMIT_BYTES),
    )(mem, gain, w_kv)


def _mixer_kernel(x_ref, k_ref, v_ref, n1g_ref, win_ref, poolw_ref, pools_ref, lng_ref, lnb_ref,
                  ws_ref, bst_ref, sguwo_ref, xawo_ref, wout_ref, h_ref, tail_ref):
    ts, d = x_ref.shape[1], x_ref.shape[2]
    s_idx = pl.program_id(1)
    x = x_ref[0]
    nb = _rmsnorm(x, n1g_ref[...]).astype(MXU_DTYPE)

    def proj(col):
        return jnp.dot(nb, win_ref[:, col * d:(col + 1) * d], preferred_element_type=F32)

    @pl.when(s_idx == 0)
    def _():
        tail_ref[...] = jnp.zeros_like(tail_ref)

    p = proj(0)
    ext = jnp.concatenate([tail_ref[...], p], axis=0)
    tail_ref[...] = p[ts - POOL_HALO:, :]
    pos = s_idx * ts + lax.broadcasted_iota(jnp.int32, (ts, 1), 0)
    group = d // len(POOL_WINDOWS)
    y_pool = []
    for g, w in enumerate(POOL_WINDOWS):
        acc = ext[:, g * group:(g + 1) * group]
        shift = 1
        while shift < w:
            acc = acc + pltpu.roll(acc, shift, 0)
            shift *= 2
        count = jnp.minimum(pos + 1, w).astype(F32)
        diff = acc[POOL_HALO:, :] / count - p[:, g * group:(g + 1) * group]
        y_pool.append(_mm(diff, poolw_ref[g]))
    y_pool = jnp.concatenate(y_pool, axis=1) * pools_ref[...]

    u = _gelu(proj(1))
    v = _gelu(proj(2))
    mu = jnp.mean(v, axis=-1, keepdims=True)
    vc = v - mu
    var = jnp.mean(vc * vc, axis=-1, keepdims=True)
    v = (vc * lax.rsqrt(var + LN_EPS) * lng_ref[...] + lnb_ref[...]).astype(MXU_DTYPE)
    hd = d // SGU_HEADS
    causal = (lax.broadcasted_iota(jnp.int32, (SGU_CHUNK, SGU_CHUNK), 0)
              >= lax.broadcasted_iota(jnp.int32, (SGU_CHUNK, SGU_CHUNK), 1))
    mixed_rows = []
    w_masked = [jnp.where(causal, ws_ref[h], 0.0).astype(MXU_DTYPE) for h in range(SGU_HEADS)]
    for c in range(ts // SGU_CHUNK):
        rows = slice(c * SGU_CHUNK, (c + 1) * SGU_CHUNK)
        heads = []
        for h in range(SGU_HEADS):
            mixed = jnp.dot(w_masked[h], v[rows, h * hd:(h + 1) * hd], preferred_element_type=F32)
            heads.append(mixed + bst_ref[:, h:h + 1])
        mixed_rows.append(jnp.concatenate(heads, axis=1))
    mixed = jnp.concatenate(mixed_rows, axis=0) if len(mixed_rows) > 1 else mixed_rows[0]
    y_sgu = _mm(u * mixed, sguwo_ref[...])

    q = proj(3).astype(MXU_DTYPE)
    xd = d // XA_HEADS
    outs = []
    for h in range(XA_HEADS):
        cols = slice(h * xd, (h + 1) * xd)
        s = _mm_nt(q[:, cols], k_ref[0, :, cols]) * (xd ** -0.5)
        e = jnp.exp(s - jnp.max(s, axis=-1, keepdims=True))
        probs = e / jnp.sum(e, axis=-1, keepdims=True)
        outs.append(_mm(probs, v_ref[0, :, cols]))
    y_xa = _mm(jnp.concatenate(outs, axis=1), xawo_ref[...])

    merged = (jax.nn.sigmoid(proj(4)) * y_pool + jax.nn.sigmoid(proj(5)) * y_sgu
              + jax.nn.sigmoid(proj(6)) * y_xa)
    h_ref[0] = x + _mm(merged, wout_ref[...])


def _resident(shape):
    zeros = (0,) * len(shape)
    return pl.BlockSpec(shape, lambda *_: zeros, pipeline_mode=pl.Buffered(1))


def _mixer(x, k, v, weights, b0, nb):
    _, S, D = x.shape
    ts = min(SEQ_TILE, S)
    M = k.shape[1]
    return pl.pallas_call(
        _mixer_kernel,
        grid=(nb, S // ts),
        in_specs=[pl.BlockSpec((1, ts, D), lambda b, s: (b + b0, s, 0)),
                  pl.BlockSpec((1, M, k.shape[2]), lambda b, s: (b + b0, 0, 0)),
                  pl.BlockSpec((1, M, v.shape[2]), lambda b, s: (b + b0, 0, 0))]
                 + [_resident(w.shape) for w in weights],
        out_specs=pl.BlockSpec((1, ts, D), lambda b, s: (b, s, 0)),
        out_shape=jax.ShapeDtypeStruct((nb, S, D), F32),
        scratch_shapes=[pltpu.VMEM((POOL_HALO, D), F32)],
        compiler_params=pltpu.CompilerParams(dimension_semantics=("arbitrary", "arbitrary"),
                                             vmem_limit_bytes=VMEM_LIMIT_BYTES),
    )(x, k, v, *weights)


def _topk_rows(vals, payload=None):
    n_rows = vals.shape[0]
    row = lax.broadcasted_iota(jnp.int32, vals.shape, 0).astype(F32)
    out_v, out_i = [], []
    for _ in range(PEER_TOPK):
        m = jnp.max(vals, axis=0, keepdims=True)
        first = jnp.min(jnp.where(vals == m, row, float(n_rows)), axis=0, keepdims=True)
        sel = row == first
        out_v.append(m)
        if payload is None:
            out_i.append(first)
        else:
            out_i.append(jnp.max(jnp.where(sel, payload, -1.0), axis=0, keepdims=True))
        vals = jnp.where(sel, -jnp.inf, vals)
    return jnp.concatenate(out_v, axis=0), jnp.concatenate(out_i, axis=0)


def _candidates(v1, i1, v2, i2):
    sub = 8
    jrow = lax.broadcasted_iota(jnp.int32, (sub, 1), 0)
    vals = [v1[0:1] + v2]
    ids = [i1[0:1] * float(PEER_N_KEYS) + i2]
    for i in range(1, sub):
        keep = jrow < (PEER_TOPK // (i + 1))
        vals.append(jnp.where(keep, v1[i:i + 1] + v2[0:sub], -jnp.inf))
        ids.append(i1[i:i + 1] * float(PEER_N_KEYS) + i2[0:sub])
    vals.append(v1[sub:] + v2[0:1])
    ids.append(i1[sub:] * float(PEER_N_KEYS) + i2[0:1])
    return jnp.concatenate(vals, axis=0), jnp.concatenate(ids, axis=0)


def _select_kernel(h_ref, n2g_ref, wq_ref, k1_ref, k2_ref, xn_ref, exp_ref, gate_ref):
    xn = _rmsnorm(h_ref[...], n2g_ref[...])
    xn_ref[...] = _pack_halves(xn)
    q = _mm(xn, wq_ref[...])
    half = k1_ref.shape[1]
    experts, gates = [], []
    for h in range(PEER_HEADS):
        q1 = q[:, (2 * h) * half:(2 * h + 1) * half]
        q2 = q[:, (2 * h + 1) * half:(2 * h + 2) * half]
        v1, i1 = _topk_rows(_mm_nt(k1_ref[...], q1))
        v2, i2 = _topk_rows(_mm_nt(k2_ref[...], q2))
        best, eid = _topk_rows(*_candidates(v1, i1, v2, i2))
        e = jnp.exp(best - jnp.max(best, axis=0, keepdims=True))
        gates.append(e / jnp.sum(e, axis=0, keepdims=True))
        experts.append(eid)
    exp_ref[...] = jnp.concatenate(experts, axis=0).T.astype(jnp.int32)
    gate_ref[...] = jnp.concatenate(gates, axis=0).T


def _select(h, n2g, w_q, keys1, keys2):
    T, D = h.shape
    tt = min(SELECT_TILE, T)
    hk = PEER_HEADS * PEER_TOPK
    return pl.pallas_call(
        _select_kernel,
        grid=(T // tt,),
        in_specs=[pl.BlockSpec((tt, D), lambda i: (i, 0)),
                  _resident(n2g.shape), _resident(w_q.shape),
                  _resident(keys1.shape), _resident(keys2.shape)],
        out_specs=[pl.BlockSpec((tt, D // 2), lambda i: (i, 0)),
                   pl.BlockSpec((tt, hk), lambda i: (i, 0)),
                   pl.BlockSpec((tt, hk), lambda i: (i, 0))],
        out_shape=[jax.ShapeDtypeStruct((T, D // 2), I32),
                   jax.ShapeDtypeStruct((T, hk), jnp.int32),
                   jax.ShapeDtypeStruct((T, hk), F32)],
        compiler_params=pltpu.CompilerParams(dimension_semantics=("arbitrary",),
                                             vmem_limit_bytes=VMEM_LIMIT_BYTES),
    )(h, n2g, w_q, keys1, keys2)


def _pack_table(table):
    half = table.shape[1] // 2
    bits = lax.bitcast_convert_type(table.astype(jnp.bfloat16), jnp.uint16).astype(jnp.uint32)
    return lax.bitcast_convert_type((bits[:, half:] << 16) | bits[:, :half], I32)


def _bf16_bits(x):
    return lax.bitcast_convert_type(x.astype(jnp.bfloat16).astype(F32), I32)


def _pack_halves(x):
    half = x.shape[1] // 2
    return (_bf16_bits(x[:, half:]) & HIGH_HALF) | lax.shift_right_logical(_bf16_bits(x[:, :half]), 16)


def _pack_splat(x):
    bits = _bf16_bits(x)
    return (bits & HIGH_HALF) | lax.shift_right_logical(bits, 16)


def _unpack_words(w):
    lo = lax.bitcast_convert_type(lax.shift_left(w, jnp.full(w.shape, 16, I32)), F32)
    hi = lax.bitcast_convert_type(lax.bitwise_and(w, jnp.full(w.shape, HIGH_HALF, I32)), F32)
    return lo, hi


def _as_bf16(w):
    return plsc.bitcast(w, jnp.bfloat16)


def _halves_f32(p):
    return _unpack_words(plsc.bitcast(p, I32))


def _sc_kernel(body, out_type, scratch_types):
    mesh = plsc.VectorSubcoreMesh(core_axis_name="c", subcore_axis_name="s")
    return pl.kernel(body, out_type=out_type, mesh=mesh, scratch_types=scratch_types,
                     compiler_params=pltpu.CompilerParams(needs_layout_passes=False))


def _sc_worker_base(tokens_per_worker):
    return (lax.axis_index("s") * SC_CORES + lax.axis_index("c")) * tokens_per_worker


def _sc_pipeline(nchunk, gather, compute):
    ahead = SC_NBUF - 1
    for i in range(ahead):
        gather(i, i).start()

    @pl.loop(0, nchunk)
    def _(ci):
        @pl.when(ci + ahead < nchunk)
        def _():
            gather(ci + ahead, (ci + ahead) % SC_NBUF).start()

        slot = ci % SC_NBUF
        gather(ci, slot).wait()
        compute(ci, slot)


def _sc_dots(xp, experts, ptab):
    T, W = xp.shape
    HK = experts.shape[1]
    L, TG, CH = SC_LANES, SC_TOKENS, SC_CHUNK
    tpw = T // SC_WORKERS
    cpt = HK // CH
    nchunk = TG * cpt
    assert W == ptab.shape[1] and T % (SC_WORKERS * TG) == 0 and HK % CH == 0 and CH % L == 0
    assert W % (2 * L) == 0 and nchunk >= SC_NBUF

    def body(x_hbm, idx_hbm, tab_hbm, out_hbm, x_v, idx_v, out_v, rows_v, acc_v, sem):
        base = _sc_worker_base(tpw)
        lane = lax.iota(I32, L)

        def gather(ci, slot):
            tok, c = ci // cpt, ci % cpt
            return pltpu.make_async_copy(tab_hbm.at[idx_v.at[tok, pl.ds(c * CH, CH)]],
                                         rows_v.at[slot], sem.at[slot])

        def compute(ci, slot):
            tok, c = ci // cpt, ci % cpt
            for eg in range(CH // L):
                accs = [jnp.zeros((L,), F32) for _ in range(L)]
                for j in range(0, W // L, 2):
                    xa = _as_bf16(x_v[tok, pl.ds(j * L, L)])
                    xb = _as_bf16(x_v[tok, pl.ds((j + 1) * L, L)])
                    for kk in range(L):
                        ra = _as_bf16(rows_v[slot, eg * L + kk, pl.ds(j * L, L)])
                        rb = _as_bf16(rows_v[slot, eg * L + kk, pl.ds((j + 1) * L, L)])
                        lo, hi = _halves_f32(ra * xa + rb * xb)
                        accs[kk] = accs[kk] + lo + hi
                for kk in range(L):
                    acc_v[kk, :] = accs[kk]

                s = jnp.zeros((L,), F32)
                for j in range(L):
                    s = s + plsc.load_gather(acc_v, [lane, jnp.full((L,), j, I32)])
                out_v[tok, pl.ds(c * CH + eg * L, L)] = s

        @pl.loop(0, tpw // TG)
        def _(g):
            t0 = base + g * TG
            pltpu.sync_copy(x_hbm.at[pl.ds(t0, TG)], x_v)
            pltpu.sync_copy(idx_hbm.at[pl.ds(t0, TG)], idx_v)
            _sc_pipeline(nchunk, gather, compute)
            pltpu.sync_copy(out_v, out_hbm.at[pl.ds(t0, TG)])

    return _sc_kernel(
        body, jax.ShapeDtypeStruct((T, HK), F32),
        [pltpu.VMEM((TG, W), I32), pltpu.VMEM((TG, HK), I32), pltpu.VMEM((TG, HK), F32),
         pltpu.VMEM((SC_NBUF, CH, W), I32), pltpu.VMEM((L, L), F32),
         pltpu.SemaphoreType.DMA((SC_NBUF,))],
    )(xp, experts, ptab)


def _sc_wsum(wp, experts, ptab):
    T, HK = wp.shape
    W = ptab.shape[1]
    D = 2 * W
    L, TG, CH = SC_LANES, SC_TOKENS, SC_CHUNK
    tpw = T // SC_WORKERS
    cpt = HK // CH
    nchunk = TG * cpt
    DG = 8
    assert T % (SC_WORKERS * TG) == 0 and HK % CH == 0 and CH % 2 == 0 and W % (DG * L) == 0
    assert nchunk >= SC_NBUF

    def body(w_hbm, idx_hbm, tab_hbm, out_hbm, w_v, idx_v, y_v, rows_v, sem):
        base = _sc_worker_base(tpw)

        def gather(ci, slot):
            tok, c = ci // cpt, ci % cpt
            return pltpu.make_async_copy(tab_hbm.at[idx_v.at[tok, pl.ds(c * CH, CH)]],
                                         rows_v.at[slot], sem.at[slot])

        def compute(ci, slot):
            tok, c = ci // cpt, ci % cpt
            tokv = jnp.full((L,), tok, I32)
            cont = jnp.full((L,), c, I32) != 0

            @pl.loop(0, W // (DG * L))
            def _(dg):
                off = dg * (DG * L)
                zero = jnp.zeros((L,), F32)
                alo = [jnp.where(cont, y_v[tok, pl.ds(off + j * L, L)], zero) for j in range(DG)]
                ahi = [jnp.where(cont, y_v[tok, pl.ds(W + off + j * L, L)], zero) for j in range(DG)]
                for kk in range(0, CH, 2):
                    wa = _as_bf16(plsc.load_gather(w_v, [tokv, jnp.full((L,), c * CH + kk, I32)]))
                    wb = _as_bf16(plsc.load_gather(w_v, [tokv, jnp.full((L,), c * CH + kk + 1, I32)]))
                    for j in range(DG):
                        ra = _as_bf16(rows_v[slot, kk, pl.ds(off + j * L, L)])
                        rb = _as_bf16(rows_v[slot, kk + 1, pl.ds(off + j * L, L)])
                        lo, hi = _halves_f32(wa * ra + wb * rb)
                        alo[j] = alo[j] + lo
                        ahi[j] = ahi[j] + hi
                for j in range(DG):
                    y_v[tok, pl.ds(off + j * L, L)] = alo[j]
                    y_v[tok, pl.ds(W + off + j * L, L)] = ahi[j]

        @pl.loop(0, tpw // TG)
        def _(g):
            t0 = base + g * TG
            pltpu.sync_copy(w_hbm.at[pl.ds(t0, TG)], w_v)
            pltpu.sync_copy(idx_hbm.at[pl.ds(t0, TG)], idx_v)
            _sc_pipeline(nchunk, gather, compute)
            pltpu.sync_copy(y_v, out_hbm.at[pl.ds(t0, TG)])

    return _sc_kernel(
        body, jax.ShapeDtypeStruct((T, D), F32),
        [pltpu.VMEM((TG, HK), I32), pltpu.VMEM((TG, HK), I32), pltpu.VMEM((TG, D), F32),
         pltpu.VMEM((SC_NBUF, CH, W), I32), pltpu.SemaphoreType.DMA((SC_NBUF,))],
    )(wp, experts, ptab)


def _tc_wsum_kernel(exp_ref, w_ref, tab_ref, y_ref):
    tb, hk = exp_ref.shape
    sub, lanes = tab_ref.shape[1:]
    for t in range(tb):
        lo_acc = jnp.zeros((sub, lanes), F32)
        hi_acc = jnp.zeros((sub, lanes), F32)
        for k in range(hk):
            row = tab_ref[exp_ref[t, k]]
            w = w_ref[t, k]
            lo_acc = lo_acc + w * lax.bitcast_convert_type(lax.shift_left(row, 16), F32)
            hi_acc = hi_acc + w * lax.bitcast_convert_type(row & HIGH_HALF, F32)
        y_ref[t, pl.ds(0, sub), :] = lo_acc
        y_ref[t, pl.ds(sub, sub), :] = hi_acc


def _tc_wsum(w, experts, ptab):
    T, hk = w.shape
    E, W = ptab.shape
    sub = W // TC_LANES
    tb = min(TC_WSUM_TILE, T)
    smem = pl.BlockSpec((tb, hk), lambda i: (i, 0), memory_space=pltpu.SMEM)
    y = pl.pallas_call(
        _tc_wsum_kernel, grid=(T // tb,),
        in_specs=[smem, smem, _resident((E, sub, TC_LANES))],
        out_specs=pl.BlockSpec((tb, 2 * sub, TC_LANES), lambda i: (i, 0, 0)),
        out_shape=jax.ShapeDtypeStruct((T, 2 * sub, TC_LANES), F32),
        cost_estimate=pl.CostEstimate(flops=4 * T * hk * W, transcendentals=0,
                                      bytes_accessed=4 * (T * hk * W + E * W + 2 * T * W + 2 * T * hk)),
        compiler_params=pltpu.CompilerParams(dimension_semantics=("arbitrary",),
                                             vmem_limit_bytes=VMEM_LIMIT_BYTES),
    )(experts, w, ptab.reshape(E, sub, TC_LANES))
    return y.reshape(T, 2 * W)


def _act_kernel(dots_ref, gate_ref, w_ref, wp_ref):
    w = _gelu(dots_ref[...]) * gate_ref[...]
    w_ref[...] = w
    wp_ref[...] = _pack_splat(w)


def _act(dots, gates):
    T, hk = dots.shape
    tt = min(ROW_TILE, T)
    spec = pl.BlockSpec((tt, hk), lambda i: (i, 0))
    return pl.pallas_call(
        _act_kernel, grid=(T // tt,), in_specs=[spec, spec], out_specs=[spec, spec],
        out_shape=[jax.ShapeDtypeStruct((T, hk), F32), jax.ShapeDtypeStruct((T, hk), I32)],
        compiler_params=pltpu.CompilerParams(dimension_semantics=("arbitrary",)),
    )(dots, gates)


def _final_kernel(h_ref, y_ref, g_ref, *rest):
    o_ref = rest[-1]
    o_ref[...] = _rmsnorm(h_ref[...] + y_ref[...], g_ref[...])


def _final(h, y, gain, out, row0, total_rows):
    T, D = h.shape
    tt = min(ROW_TILE, T)
    assert row0 % tt == 0
    spec = pl.BlockSpec((tt, D), lambda i: (i, 0))
    out_spec = pl.BlockSpec((tt, D), lambda i: (i + row0 // tt, 0))
    in_specs = [spec, spec, pl.BlockSpec((1, D), lambda i: (0, 0))]
    args = (h, y, gain)
    aliases = {}
    if out is not None:
        in_specs.append(pl.BlockSpec(memory_space=pl.ANY))
        args += (out,)
        aliases = {3: 0}
    return pl.pallas_call(
        _final_kernel, grid=(T // tt,), in_specs=in_specs, out_specs=out_spec,
        out_shape=jax.ShapeDtypeStruct((total_rows, D), F32),
        input_output_aliases=aliases,
        compiler_params=pltpu.CompilerParams(dimension_semantics=("arbitrary",)),
    )(*args)


def kernel(x, mem, norm1_gain, w_in, pool_w, pool_scale, sgu_ln_gain, sgu_ln_bias, sgu_w_s, sgu_b_s,
           sgu_w_out, mem_norm_gain, xa_w_kv, xa_w_out, w_out, norm2_gain, peer_w_q, peer_keys1,
           peer_keys2, peer_u, peer_v, final_norm_gain):
    B, S, D = x.shape
    depth = w_in.shape[0]
    lo = lambda w: w.astype(MXU_DTYPE)
    row = lambda w: w.reshape(1, -1)
    assert depth == 1, "the final RMSNorm is fused with the last layer's residual add"
    l = 0
    weights = (row(norm1_gain[l]), lo(w_in[l]), lo(pool_w[l]), row(pool_scale[l]),
               row(sgu_ln_gain[l]), row(sgu_ln_bias[l]), sgu_w_s[l], sgu_b_s[l].T,
               lo(sgu_w_out[l]), lo(xa_w_out[l]), lo(w_out[l]))
    select_w = (row(norm2_gain[l]), lo(peer_w_q[l]), lo(peer_keys1[l]), lo(peer_keys2[l]))
    u_packed, v_packed = _pack_table(peer_u[l]), _pack_table(peer_v[l])
    k, v = _memkv(mem, row(mem_norm_gain[l]), lo(xa_w_kv[l]))
    fgain = row(final_norm_gain)
    chunks = BATCH_CHUNKS if sum(BATCH_CHUNKS) == B else (B,)
    b0, tc_sums, tc_parts, sc_parts = 0, [], [], []
    for c, nb in enumerate(chunks):
        rows = nb * S
        h = _mixer(x, k, v, weights, b0, nb).reshape(rows, D)
        xn, experts, gates = _select(h, *select_w)
        dots = _sc_dots(xn, experts, u_packed)
        if c > 0:
            dots, tc_sums[c - 1] = lax.optimization_barrier((dots, tc_sums[c - 1]))
        w, wp = _act(dots, gates)
        n_tc = (rows * TC_WSUM_SHARE[0] // TC_WSUM_SHARE[1]) // SC_ROW_QUANTUM * SC_ROW_QUANTUM
        tc_sums.append(_tc_wsum(w[:n_tc], experts[:n_tc], v_packed))
        tc_parts.append((h[:n_tc], b0 * S))
        sc_parts.append((h[n_tc:], _sc_wsum(wp[n_tc:], experts[n_tc:], v_packed), b0 * S + n_tc))
        b0 += nb
    out = None
    for (h_tc, row0), y_tc in zip(tc_parts, tc_sums):
        out = _final(h_tc, y_tc, fgain, out, row0, B * S)
    for h_sc, y_sc, row0 in sc_parts:
        out = _final(h_sc, y_sc, fgain, out, row0, B * S)
    return out.reshape(B, S, D)
```

```python
import functools
import math

import jax
import jax.numpy as jnp
from jax import lax
from jax.experimental import pallas as pl
from jax.experimental.pallas import tpu as pltpu
from jax.experimental.pallas import tpu_sc as plsc

F32 = jnp.float32
I32 = jnp.int32
MXU_DTYPE = jnp.bfloat16

RMS_EPS = 1e-6
LN_EPS = 1e-5
POOL_WINDOWS = (2, 4, 8, 16)
POOL_HALO = 16
SGU_CHUNK = 128
SGU_HEADS = 8
XA_HEADS = 4
PEER_HEADS = 8
PEER_N_KEYS = 128
PEER_TOPK = 16

V7X_VMEM_BYTES = 64 * 1024 * 1024
VMEM_LIMIT_BYTES = 56 * 1024 * 1024

SEQ_TILE = 256
SELECT_TILE = 256
ROW_TILE = 512
TC_WSUM_TILE = 8
TC_LANES = 128
BATCH_CHUNKS = (1, 1, 2, 2, 2)

SC_CORES = 2
SC_SUBCORES = 16
SC_LANES = 16
SC_WORKERS = SC_CORES * SC_SUBCORES
SC_TOKENS = 32
SC_CHUNK = 32
SC_NBUF = 4
HIGH_HALF = -65536
SC_ROW_QUANTUM = SC_WORKERS * SC_TOKENS
TC_WSUM_SHARE = (1, 2)


def _rmsnorm(x, gain):
    return x * lax.rsqrt(jnp.mean(x * x, axis=-1, keepdims=True) + RMS_EPS) * gain


def _gelu(x):
    return 0.5 * x * (1.0 + lax.erf(x * (1.0 / math.sqrt(2.0))))


def _mm(a, b):
    return jnp.dot(a.astype(MXU_DTYPE), b.astype(MXU_DTYPE), preferred_element_type=F32)


def _mm_nt(a, b):
    return lax.dot_general(a.astype(MXU_DTYPE), b.astype(MXU_DTYPE),
                           (((1,), (1,)), ((), ())), preferred_element_type=F32)


def _memkv_kernel(mem_ref, gain_ref, wkv_ref, k_ref, v_ref):
    width = k_ref.shape[-1]
    kv = _mm(_rmsnorm(mem_ref[0], gain_ref[...]), wkv_ref[...])
    k_ref[0] = kv[:, :width].astype(k_ref.dtype)
    v_ref[0] = kv[:, width:].astype(v_ref.dtype)


def _memkv(mem, gain, w_kv):
    B, M, D = mem.shape
    width = w_kv.shape[1] // 2
    const = lambda b: (0, 0)
    return pl.pallas_call(
        _memkv_kernel,
        grid=(B,),
        in_specs=[pl.BlockSpec((1, M, D), lambda b: (b, 0, 0)),
                  pl.BlockSpec((1, D), const),
                  pl.BlockSpec(w_kv.shape, const)],
        out_specs=[pl.BlockSpec((1, M, width), lambda b: (b, 0, 0))] * 2,
        out_shape=[jax.ShapeDtypeStruct((B, M, width), MXU_DTYPE)] * 2,
        compiler_params=pltpu.CompilerParams(dimension_semantics=("arbitrary",),
                                             vmem_limit_bytes=VMEM_LIMIT_BYTES),
    )(mem, gain, w_kv)


def _mixer_kernel(x_ref, k_ref, v_ref, n1g_ref, win_ref, poolw_ref, pools_ref, lng_ref, lnb_ref,
                  ws_ref, bst_ref, sguwo_ref, xawo_ref, wout_ref, h_ref, tail_ref):
    ts, d = x_ref.shape[1], x_ref.shape[2]
    s_idx = pl.program_id(1)
    x = x_ref[0]
    nb = _rmsnorm(x, n1g_ref[...]).astype(MXU_DTYPE)

    def proj(col):
        return jnp.dot(nb, win_ref[:, col * d:(col + 1) * d], preferred_element_type=F32)

    @pl.when(s_idx == 0)
    def _():
        tail_ref[...] = jnp.zeros_like(tail_ref)

    p = proj(0)
    ext = jnp.concatenate([tail_ref[...], p], axis=0)
    tail_ref[...] = p[ts - POOL_HALO:, :]
    pos = s_idx * ts + lax.broadcasted_iota(jnp.int32, (ts, 1), 0)
    group = d // len(POOL_WINDOWS)
    y_pool = []
    for g, w in enumerate(POOL_WINDOWS):
        acc = ext[:, g * group:(g + 1) * group]
        shift = 1
        while shift < w:
            acc = acc + pltpu.roll(acc, shift, 0)
            shift *= 2
        count = jnp.minimum(pos + 1, w).astype(F32)
        diff = acc[POOL_HALO:, :] / count - p[:, g * group:(g + 1) * group]
        y_pool.append(_mm(diff, poolw_ref[g]))
    y_pool = jnp.concatenate(y_pool, axis=1) * pools_ref[...]

    u = _gelu(proj(1))
    v = _gelu(proj(2))
    mu = jnp.mean(v, axis=-1, keepdims=True)
    vc = v - mu
    var = jnp.mean(vc * vc, axis=-1, keepdims=True)
    v = (vc * lax.rsqrt(var + LN_EPS) * lng_ref[...] + lnb_ref[...]).astype(MXU_DTYPE)
    hd = d // SGU_HEADS
    causal = (lax.broadcasted_iota(jnp.int32, (SGU_CHUNK, SGU_CHUNK), 0)
              >= lax.broadcasted_iota(jnp.int32, (SGU_CHUNK, SGU_CHUNK), 1))
    mixed_rows = []
    w_masked = [jnp.where(causal, ws_ref[h], 0.0).astype(MXU_DTYPE) for h in range(SGU_HEADS)]
    for c in range(ts // SGU_CHUNK):
        rows = slice(c * SGU_CHUNK, (c + 1) * SGU_CHUNK)
        heads = []
        for h in range(SGU_HEADS):
            mixed = jnp.dot(w_masked[h], v[rows, h * hd:(h + 1) * hd], preferred_element_type=F32)
            heads.append(mixed + bst_ref[:, h:h + 1])
        mixed_rows.append(jnp.concatenate(heads, axis=1))
    mixed = jnp.concatenate(mixed_rows, axis=0) if len(mixed_rows) > 1 else mixed_rows[0]
    y_sgu = _mm(u * mixed, sguwo_ref[...])

    q = proj(3).astype(MXU_DTYPE)
    xd = d // XA_HEADS
    outs = []
    for h in range(XA_HEADS):
        cols = slice(h * xd, (h + 1) * xd)
        s = _mm_nt(q[:, cols], k_ref[0, :, cols]) * (xd ** -0.5)
        e = jnp.exp(s - jnp.max(s, axis=-1, keepdims=True))
        probs = e / jnp.sum(e, axis=-1, keepdims=True)
        outs.append(_mm(probs, v_ref[0, :, cols]))
    y_xa = _mm(jnp.concatenate(outs, axis=1), xawo_ref[...])

    merged = (jax.nn.sigmoid(proj(4)) * y_pool + jax.nn.sigmoid(proj(5)) * y_sgu
              + jax.nn.sigmoid(proj(6)) * y_xa)
    h_ref[0] = x + _mm(merged, wout_ref[...])


def _resident(shape):
    zeros = (0,) * len(shape)
    return pl.BlockSpec(shape, lambda *_: zeros, pipeline_mode=pl.Buffered(1))


def _mixer(x, k, v, weights, b0, nb):
    _, S, D = x.shape
    ts = min(SEQ_TILE, S)
    M = k.shape[1]
    return pl.pallas_call(
        _mixer_kernel,
        grid=(nb, S // ts),
        in_specs=[pl.BlockSpec((1, ts, D), lambda b, s: (b + b0, s, 0)),
                  pl.BlockSpec((1, M, k.shape[2]), lambda b, s: (b + b0, 0, 0)),
                  pl.BlockSpec((1, M, v.shape[2]), lambda b, s: (b + b0, 0, 0))]
                 + [_resident(w.shape) for w in weights],
        out_specs=pl.BlockSpec((1, ts, D), lambda b, s: (b, s, 0)),
        out_shape=jax.ShapeDtypeStruct((nb, S, D), F32),
        scratch_shapes=[pltpu.VMEM((POOL_HALO, D), F32)],
        compiler_params=pltpu.CompilerParams(dimension_semantics=("arbitrary", "arbitrary"),
                                             vmem_limit_bytes=VMEM_LIMIT_BYTES),
    )(x, k, v, *weights)


def _topk_rows(vals, payload=None):
    n_rows = vals.shape[0]
    row = lax.broadcasted_iota(jnp.int32, vals.shape, 0).astype(F32)
    out_v, out_i = [], []
    for _ in range(PEER_TOPK):
        m = jnp.max(vals, axis=0, keepdims=True)
        first = jnp.min(jnp.where(vals == m, row, float(n_rows)), axis=0, keepdims=True)
        sel = row == first
        out_v.append(m)
        if payload is None:
            out_i.append(first)
        else:
            out_i.append(jnp.max(jnp.where(sel, payload, -1.0), axis=0, keepdims=True))
        vals = jnp.where(sel, -jnp.inf, vals)
    return jnp.concatenate(out_v, axis=0), jnp.concatenate(out_i, axis=0)


def _candidates(v1, i1, v2, i2):
    sub = 8
    jrow = lax.broadcasted_iota(jnp.int32, (sub, 1), 0)
    vals = [v1[0:1] + v2]
    ids = [i1[0:1] * float(PEER_N_KEYS) + i2]
    for i in range(1, sub):
        keep = jrow < (PEER_TOPK // (i + 1))
        vals.append(jnp.where(keep, v1[i:i + 1] + v2[0:sub], -jnp.inf))
        ids.append(i1[i:i + 1] * float(PEER_N_KEYS) + i2[0:sub])
    vals.append(v1[sub:] + v2[0:1])
    ids.append(i1[sub:] * float(PEER_N_KEYS) + i2[0:1])
    return jnp.concatenate(vals, axis=0), jnp.concatenate(ids, axis=0)


def _select_kernel(h_ref, n2g_ref, wq_ref, k1_ref, k2_ref, xn_ref, exp_ref, gate_ref):
    xn = _rmsnorm(h_ref[...], n2g_ref[...])
    xn_ref[...] = _pack_halves(xn)
    q = _mm(xn, wq_ref[...])
    half = k1_ref.shape[1]
    experts, gates = [], []
    for h in range(PEER_HEADS):
        q1 = q[:, (2 * h) * half:(2 * h + 1) * half]
        q2 = q[:, (2 * h + 1) * half:(2 * h + 2) * half]
        v1, i1 = _topk_rows(_mm_nt(k1_ref[...], q1))
        v2, i2 = _topk_rows(_mm_nt(k2_ref[...], q2))
        best, eid = _topk_rows(*_candidates(v1, i1, v2, i2))
        e = jnp.exp(best - jnp.max(best, axis=0, keepdims=True))
        gates.append(e / jnp.sum(e, axis=0, keepdims=True))
        experts.append(eid)
    exp_ref[...] = jnp.concatenate(experts, axis=0).T.astype(jnp.int32)
    gate_ref[...] = jnp.concatenate(gates, axis=0).T


def _select(h, n2g, w_q, keys1, keys2):
    T, D = h.shape
    tt = min(SELECT_TILE, T)
    hk = PEER_HEADS * PEER_TOPK
    return pl.pallas_call(
        _select_kernel,
        grid=(T // tt,),
        in_specs=[pl.BlockSpec((tt, D), lambda i: (i, 0)),
                  _resident(n2g.shape), _resident(w_q.shape),
                  _resident(keys1.shape), _resident(keys2.shape)],
        out_specs=[pl.BlockSpec((tt, D // 2), lambda i: (i, 0)),
                   pl.BlockSpec((tt, hk), lambda i: (i, 0)),
                   pl.BlockSpec((tt, hk), lambda i: (i, 0))],
        out_shape=[jax.ShapeDtypeStruct((T, D // 2), I32),
                   jax.ShapeDtypeStruct((T, hk), jnp.int32),
                   jax.ShapeDtypeStruct((T, hk), F32)],
        compiler_params=pltpu.CompilerParams(dimension_semantics=("arbitrary",),
                                             vmem_limit_bytes=VMEM_LIMIT_BYTES),
    )(h, n2g, w_q, keys1, keys2)


def _pack_table(table):
    half = table.shape[1] // 2
    bits = lax.bitcast_convert_type(table.astype(jnp.bfloat16), jnp.uint16).astype(jnp.uint32)
    return lax.bitcast_convert_type((bits[:, half:] << 16) | bits[:, :half], I32)


def _bf16_bits(x):
    return lax.bitcast_convert_type(x.astype(jnp.bfloat16).astype(F32), I32)


def _pack_halves(x):
    half = x.shape[1] // 2
    return (_bf16_bits(x[:, half:]) & HIGH_HALF) | lax.shift_right_logical(_bf16_bits(x[:, :half]), 16)


def _pack_splat(x):
    bits = _bf16_bits(x)
    return (bits & HIGH_HALF) | lax.shift_right_logical(bits, 16)


def _unpack_words(w):
    lo = lax.bitcast_convert_type(lax.shift_left(w, jnp.full(w.shape, 16, I32)), F32)
    hi = lax.bitcast_convert_type(lax.bitwise_and(w, jnp.full(w.shape, HIGH_HALF, I32)), F32)
    return lo, hi


def _as_bf16(w):
    return plsc.bitcast(w, jnp.bfloat16)


def _halves_f32(p):
    return _unpack_words(plsc.bitcast(p, I32))


def _sc_kernel(body, out_type, scratch_types):
    mesh = plsc.VectorSubcoreMesh(core_axis_name="c", subcore_axis_name="s")
    return pl.kernel(body, out_type=out_type, mesh=mesh, scratch_types=scratch_types,
                     compiler_params=pltpu.CompilerParams(needs_layout_passes=False))


def _sc_worker_base(tokens_per_worker):
    return (lax.axis_index("s") * SC_CORES + lax.axis_index("c")) * tokens_per_worker


def _sc_pipeline(nchunk, gather, compute):
    ahead = SC_NBUF - 1
    for i in range(ahead):
        gather(i, i).start()

    @pl.loop(0, nchunk)
    def _(ci):
        @pl.when(ci + ahead < nchunk)
        def _():
            gather(ci + ahead, (ci + ahead) % SC_NBUF).start()

        slot = ci % SC_NBUF
        gather(ci, slot).wait()
        compute(ci, slot)


def _sc_dots(xp, experts, ptab):
    T, W = xp.shape
    HK = experts.shape[1]
    L, TG, CH = SC_LANES, SC_TOKENS, SC_CHUNK
    tpw = T // SC_WORKERS
    cpt = HK // CH
    nchunk = TG * cpt
    assert W == ptab.shape[1] and T % (SC_WORKERS * TG) == 0 and HK % CH == 0 and CH % L == 0
    assert W % (2 * L) == 0 and nchunk >= SC_NBUF

    def body(x_hbm, idx_hbm, tab_hbm, out_hbm, x_v, idx_v, out_v, rows_v, acc_v, sem):
        base = _sc_worker_base(tpw)
        lane = lax.iota(I32, L)

        def gather(ci, slot):
            tok, c = ci // cpt, ci % cpt
            return pltpu.make_async_copy(tab_hbm.at[idx_v.at[tok, pl.ds(c * CH, CH)]],
                                         rows_v.at[slot], sem.at[slot])

        def compute(ci, slot):
            tok, c = ci // cpt, ci % cpt
            for eg in range(CH // L):
                accs = [jnp.zeros((L,), F32) for _ in range(L)]
                for j in range(0, W // L, 2):
                    xa = _as_bf16(x_v[tok, pl.ds(j * L, L)])
                    xb = _as_bf16(x_v[tok, pl.ds((j + 1) * L, L)])
                    for kk in range(L):
                        ra = _as_bf16(rows_v[slot, eg * L + kk, pl.ds(j * L, L)])
                        rb = _as_bf16(rows_v[slot, eg * L + kk, pl.ds((j + 1) * L, L)])
                        lo, hi = _halves_f32(ra * xa + rb * xb)
                        accs[kk] = accs[kk] + lo + hi
                for kk in range(L):
                    acc_v[kk, :] = accs[kk]

                s = jnp.zeros((L,), F32)
                for j in range(L):
                    s = s + plsc.load_gather(acc_v, [lane, jnp.full((L,), j, I32)])
                out_v[tok, pl.ds(c * CH + eg * L, L)] = s

        @pl.loop(0, tpw // TG)
        def _(g):
            t0 = base + g * TG
            pltpu.sync_copy(x_hbm.at[pl.ds(t0, TG)], x_v)
            pltpu.sync_copy(idx_hbm.at[pl.ds(t0, TG)], idx_v)
            _sc_pipeline(nchunk, gather, compute)
            pltpu.sync_copy(out_v, out_hbm.at[pl.ds(t0, TG)])

    return _sc_kernel(
        body, jax.ShapeDtypeStruct((T, HK), F32),
        [pltpu.VMEM((TG, W), I32), pltpu.VMEM((TG, HK), I32), pltpu.VMEM((TG, HK), F32),
         pltpu.VMEM((SC_NBUF, CH, W), I32), pltpu.VMEM((L, L), F32),
         pltpu.SemaphoreType.DMA((SC_NBUF,))],
    )(xp, experts, ptab)


def _sc_wsum(wp, experts, ptab):
    T, HK = wp.shape
    W = ptab.shape[1]
    D = 2 * W
    L, TG, CH = SC_LANES, SC_TOKENS, SC_CHUNK
    tpw = T // SC_WORKERS
    cpt = HK // CH
    nchunk = TG * cpt
    DG = 8
    assert T % (SC_WORKERS * TG) == 0 and HK % CH == 0 and CH % 2 == 0 and W % (DG * L) == 0
    assert nchunk >= SC_NBUF

    def body(w_hbm, idx_hbm, tab_hbm, out_hbm, w_v, idx_v, y_v, rows_v, sem):
        base = _sc_worker_base(tpw)

        def gather(ci, slot):
            tok, c = ci // cpt, ci % cpt
            return pltpu.make_async_copy(tab_hbm.at[idx_v.at[tok, pl.ds(c * CH, CH)]],
                                         rows_v.at[slot], sem.at[slot])

        def compute(ci, slot):
            tok, c = ci // cpt, ci % cpt
            tokv = jnp.full((L,), tok, I32)
            cont = jnp.full((L,), c, I32) != 0

            @pl.loop(0, W // (DG * L))
            def _(dg):
                off = dg * (DG * L)
                zero = jnp.zeros((L,), F32)
                alo = [jnp.where(cont, y_v[tok, pl.ds(off + j * L, L)], zero) for j in range(DG)]
                ahi = [jnp.where(cont, y_v[tok, pl.ds(W + off + j * L, L)], zero) for j in range(DG)]
                for kk in range(0, CH, 2):
                    wa = _as_bf16(plsc.load_gather(w_v, [tokv, jnp.full((L,), c * CH + kk, I32)]))
                    wb = _as_bf16(plsc.load_gather(w_v, [tokv, jnp.full((L,), c * CH + kk + 1, I32)]))
                    for j in range(DG):
                        ra = _as_bf16(rows_v[slot, kk, pl.ds(off + j * L, L)])
                        rb = _as_bf16(rows_v[slot, kk + 1, pl.ds(off + j * L, L)])
                        lo, hi = _halves_f32(wa * ra + wb * rb)
                        alo[j] = alo[j] + lo
                        ahi[j] = ahi[j] + hi
                for j in range(DG):
                    y_v[tok, pl.ds(off + j * L, L)] = alo[j]
                    y_v[tok, pl.ds(W + off + j * L, L)] = ahi[j]

        @pl.loop(0, tpw // TG)
        def _(g):
            t0 = base + g * TG
            pltpu.sync_copy(w_hbm.at[pl.ds(t0, TG)], w_v)
            pltpu.sync_copy(idx_hbm.at[pl.ds(t0, TG)], idx_v)
            _sc_pipeline(nchunk, gather, compute)
            pltpu.sync_copy(y_v, out_hbm.at[pl.ds(t0, TG)])

    return _sc_kernel(
        body, jax.ShapeDtypeStruct((T, D), F32),
        [pltpu.VMEM((TG, HK), I32), pltpu.VMEM((TG, HK), I32), pltpu.VMEM((TG, D), F32),
         pltpu.VMEM((SC_NBUF, CH, W), I32), pltpu.SemaphoreType.DMA((SC_NBUF,))],
    )(wp, experts, ptab)


def _tc_wsum_kernel(exp_ref, w_ref, tab_ref, y_ref):
    tb, hk = exp_ref.shape
    sub, lanes = tab_ref.shape[1:]
    for t in range(tb):
        lo_acc = jnp.zeros((sub, lanes), F32)
        hi_acc = jnp.zeros((sub, lanes), F32)
        for k in range(hk):
            row = tab_ref[exp_ref[t, k]]
            w = w_ref[t, k]
            lo_acc = lo_acc + w * lax.bitcast_convert_type(lax.shift_left(row, 16), F32)
            hi_acc = hi_acc + w * lax.bitcast_convert_type(row & HIGH_HALF, F32)
        y_ref[t, pl.ds(0, sub), :] = lo_acc
        y_ref[t, pl.ds(sub, sub), :] = hi_acc


def _tc_wsum(w, experts, ptab):
    T, hk = w.shape
    E, W = ptab.shape
    sub = W // TC_LANES
    tb = min(TC_WSUM_TILE, T)
    smem = pl.BlockSpec((tb, hk), lambda i: (i, 0), memory_space=pltpu.SMEM)
    y = pl.pallas_call(
        _tc_wsum_kernel, grid=(T // tb,),
        in_specs=[smem, smem, _resident((E, sub, TC_LANES))],
        out_specs=pl.BlockSpec((tb, 2 * sub, TC_LANES), lambda i: (i, 0, 0)),
        out_shape=jax.ShapeDtypeStruct((T, 2 * sub, TC_LANES), F32),
        cost_estimate=pl.CostEstimate(flops=4 * T * hk * W, transcendentals=0,
                                      bytes_accessed=4 * (T * hk * W + E * W + 2 * T * W + 2 * T * hk)),
        compiler_params=pltpu.CompilerParams(dimension_semantics=("arbitrary",),
                                             vmem_limit_bytes=VMEM_LIMIT_BYTES),
    )(experts, w, ptab.reshape(E, sub, TC_LANES))
    return y.reshape(T, 2 * W)


def _act_kernel(dots_ref, gate_ref, w_ref, wp_ref):
    w = _gelu(dots_ref[...]) * gate_ref[...]
    w_ref[...] = w
    wp_ref[...] = _pack_splat(w)


def _act(dots, gates):
    T, hk = dots.shape
    tt = min(ROW_TILE, T)
    spec = pl.BlockSpec((tt, hk), lambda i: (i, 0))
    return pl.pallas_call(
        _act_kernel, grid=(T // tt,), in_specs=[spec, spec], out_specs=[spec, spec],
        out_shape=[jax.ShapeDtypeStruct((T, hk), F32), jax.ShapeDtypeStruct((T, hk), I32)],
        compiler_params=pltpu.CompilerParams(dimension_semantics=("arbitrary",)),
    )(dots, gates)


def _final_kernel(h_ref, y_ref, g_ref, *rest):
    o_ref = rest[-1]
    o_ref[...] = _rmsnorm(h_ref[...] + y_ref[...], g_ref[...])


def _final(h, y, gain, out, row0, total_rows):
    T, D = h.shape
    tt = min(ROW_TILE, T)
    assert row0 % tt == 0
    spec = pl.BlockSpec((tt, D), lambda i: (i, 0))
    out_spec = pl.BlockSpec((tt, D), lambda i: (i + row0 // tt, 0))
    in_specs = [spec, spec, pl.BlockSpec((1, D), lambda i: (0, 0))]
    args = (h, y, gain)
    aliases = {}
    if out is not None:
        in_specs.append(pl.BlockSpec(memory_space=pl.ANY))
        args += (out,)
        aliases = {3: 0}
    return pl.pallas_call(
        _final_kernel, grid=(T // tt,), in_specs=in_specs, out_specs=out_spec,
        out_shape=jax.ShapeDtypeStruct((total_rows, D), F32),
        input_output_aliases=aliases,
        compiler_params=pltpu.CompilerParams(dimension_semantics=("arbitrary",)),
    )(*args)


def kernel(x, mem, norm1_gain, w_in, pool_w, pool_scale, sgu_ln_gain, sgu_ln_bias, sgu_w_s, sgu_b_s,
           sgu_w_out, mem_norm_gain, xa_w_kv, xa_w_out, w_out, norm2_gain, peer_w_q, peer_keys1,
           peer_keys2, peer_u, peer_v, final_norm_gain):
    B, S, D = x.shape
    depth = w_in.shape[0]
    lo = lambda w: w.astype(MXU_DTYPE)
    row = lambda w: w.reshape(1, -1)
    assert depth == 1, "the final RMSNorm is fused with the last layer's residual add"
    l = 0
    weights = (row(norm1_gain[l]), lo(w_in[l]), lo(pool_w[l]), row(pool_scale[l]),
               row(sgu_ln_gain[l]), row(sgu_ln_bias[l]), sgu_w_s[l], sgu_b_s[l].T,
               lo(sgu_w_out[l]), lo(xa_w_out[l]), lo(w_out[l]))
    select_w = (row(norm2_gain[l]), lo(peer_w_q[l]), lo(peer_keys1[l]), lo(peer_keys2[l]))
    u_packed, v_packed = _pack_table(peer_u[l]), _pack_table(peer_v[l])
    k, v = _memkv(mem, row(mem_norm_gain[l]), lo(xa_w_kv[l]))
    fgain = row(final_norm_gain)
    chunks = BATCH_CHUNKS if sum(BATCH_CHUNKS) == B else (B,)
    b0, tc_sums, tc_parts, sc_parts = 0, [], [], []
    for c, nb in enumerate(chunks):
        rows = nb * S
        h = _mixer(x, k, v, weights, b0, nb).reshape(rows, D)
        xn, experts, gates = _select(h, *select_w)
        dots = _sc_dots(xn, experts, u_packed)
        if c > 0:
            dots, tc_sums[c - 1] = lax.optimization_barrier((dots, tc_sums[c - 1]))
        w, wp = _act(dots, gates)
        n_tc = (rows * TC_WSUM_SHARE[0] // TC_WSUM_SHARE[1]) // SC_ROW_QUANTUM * SC_ROW_QUANTUM
        tc_sums.append(_tc_wsum(w[:n_tc], experts[:n_tc], v_packed))
        tc_parts.append((h[:n_tc], b0 * S))
        sc_parts.append((h[n_tc:], _sc_wsum(wp[n_tc:], experts[n_tc:], v_packed), b0 * S + n_tc))
        b0 += nb
    out = None
    for (h_tc, row0), y_tc in zip(tc_parts, tc_sums):
        out = _final(h_tc, y_tc, fgain, out, row0, B * S)
    for h_sc, y_sc, row0 in sc_parts:
        out = _final(h_sc, y_sc, fgain, out, row0, B * S)
    return out.reshape(B, S, D)
```

```python
import functools
import math

import jax
import jax.numpy as jnp
from jax import lax
from jax.experimental import pallas as pl
from jax.experimental.pallas import tpu as pltpu
from jax.experimental.pallas import tpu_sc as plsc

F32 = jnp.float32
I32 = jnp.int32
MXU_DTYPE = jnp.bfloat16

RMS_EPS = 1e-6
LN_EPS = 1e-5
POOL_WINDOWS = (2, 4, 8, 16)
POOL_HALO = 16
SGU_CHUNK = 128
SGU_HEADS = 8
XA_HEADS = 4
PEER_HEADS = 8
PEER_N_KEYS = 128
PEER_TOPK = 16

V7X_VMEM_BYTES = 64 * 1024 * 1024
VMEM_LIMIT_BYTES = 56 * 1024 * 1024

SEQ_TILE = 256
SELECT_TILE = 256
ROW_TILE = 512
TC_WSUM_TILE = 8
TC_LANES = 128
BATCH_CHUNKS = (1, 1, 2, 2, 2)

SC_CORES = 2
SC_SUBCORES = 16
SC_LANES = 16
SC_WORKERS = SC_CORES * SC_SUBCORES
SC_TOKENS = 32
SC_WSUM_TOKENS = 16
SC_CHUNK = 32
SC_NBUF = 4
HIGH_HALF = -65536
SC_ROW_QUANTUM = SC_WORKERS * SC_WSUM_TOKENS
TC_WSUM_SHARE = (9, 16)


def _rmsnorm(x, gain):
    return x * lax.rsqrt(jnp.mean(x * x, axis=-1, keepdims=True) + RMS_EPS) * gain


def _gelu(x):
    return 0.5 * x * (1.0 + lax.erf(x * (1.0 / math.sqrt(2.0))))


def _mm(a, b):
    return jnp.dot(a.astype(MXU_DTYPE), b.astype(MXU_DTYPE), preferred_element_type=F32)


def _mm_nt(a, b):
    return lax.dot_general(a.astype(MXU_DTYPE), b.astype(MXU_DTYPE),
                           (((1,), (1,)), ((), ())), preferred_element_type=F32)


def _memkv_kernel(mem_ref, gain_ref, wkv_ref, k_ref, v_ref):
    width = k_ref.shape[-1]
    kv = _mm(_rmsnorm(mem_ref[0], gain_ref[...]), wkv_ref[...])
    k_ref[0] = kv[:, :width].astype(k_ref.dtype)
    v_ref[0] = kv[:, width:].astype(v_ref.dtype)


def _memkv(mem, gain, w_kv):
    B, M, D = mem.shape
    width = w_kv.shape[1] // 2
    const = lambda b: (0, 0)
    return pl.pallas_call(
        _memkv_kernel,
        grid=(B,),
        in_specs=[pl.BlockSpec((1, M, D), lambda b: (b, 0, 0)),
                  pl.BlockSpec((1, D), const),
                  pl.BlockSpec(w_kv.shape, const)],
        out_specs=[pl.BlockSpec((1, M, width), lambda b: (b, 0, 0))] * 2,
        out_shape=[jax.ShapeDtypeStruct((B, M, width), MXU_DTYPE)] * 2,
        compiler_params=pltpu.CompilerParams(dimension_semantics=("arbitrary",),
                                             vmem_limit_bytes=VMEM_LIMIT_BYTES),
    )(mem, gain, w_kv)


def _mixer_kernel(x_ref, k_ref, v_ref, n1g_ref, win_ref, poolw_ref, pools_ref, lng_ref, lnb_ref,
                  ws_ref, bst_ref, sguwo_ref, xawo_ref, wout_ref, h_ref, tail_ref):
    ts, d = x_ref.shape[1], x_ref.shape[2]
    s_idx = pl.program_id(1)
    x = x_ref[0]
    nb = _rmsnorm(x, n1g_ref[...]).astype(MXU_DTYPE)

    def proj(col):
        return jnp.dot(nb, win_ref[:, col * d:(col + 1) * d], preferred_element_type=F32)

    @pl.when(s_idx == 0)
    def _():
        tail_ref[...] = jnp.zeros_like(tail_ref)

    p = proj(0)
    ext = jnp.concatenate([tail_ref[...], p], axis=0)
    tail_ref[...] = p[ts - POOL_HALO:, :]
    pos = s_idx * ts + lax.broadcasted_iota(jnp.int32, (ts, 1), 0)
    group = d // len(POOL_WINDOWS)
    y_pool = []
    for g, w in enumerate(POOL_WINDOWS):
        acc = ext[:, g * group:(g + 1) * group]
        shift = 1
        while shift < w:
            acc = acc + pltpu.roll(acc, shift, 0)
            shift *= 2
        count = jnp.minimum(pos + 1, w).astype(F32)
        diff = acc[POOL_HALO:, :] / count - p[:, g * group:(g + 1) * group]
        y_pool.append(_mm(diff, poolw_ref[g]))
    y_pool = jnp.concatenate(y_pool, axis=1) * pools_ref[...]

    u = _gelu(proj(1))
    v = _gelu(proj(2))
    mu = jnp.mean(v, axis=-1, keepdims=True)
    vc = v - mu
    var = jnp.mean(vc * vc, axis=-1, keepdims=True)
    v = (vc * lax.rsqrt(var + LN_EPS) * lng_ref[...] + lnb_ref[...]).astype(MXU_DTYPE)
    hd = d // SGU_HEADS
    causal = (lax.broadcasted_iota(jnp.int32, (SGU_CHUNK, SGU_CHUNK), 0)
              >= lax.broadcasted_iota(jnp.int32, (SGU_CHUNK, SGU_CHUNK), 1))
    mixed_rows = []
    w_masked = [jnp.where(causal, ws_ref[h], 0.0).astype(MXU_DTYPE) for h in range(SGU_HEADS)]
    for c in range(ts // SGU_CHUNK):
        rows = slice(c * SGU_CHUNK, (c + 1) * SGU_CHUNK)
        heads = []
        for h in range(SGU_HEADS):
            mixed = jnp.dot(w_masked[h], v[rows, h * hd:(h + 1) * hd], preferred_element_type=F32)
            heads.append(mixed + bst_ref[:, h:h + 1])
        mixed_rows.append(jnp.concatenate(heads, axis=1))
    mixed = jnp.concatenate(mixed_rows, axis=0) if len(mixed_rows) > 1 else mixed_rows[0]
    y_sgu = _mm(u * mixed, sguwo_ref[...])

    q = proj(3).astype(MXU_DTYPE)
    xd = d // XA_HEADS
    outs = []
    for h in range(XA_HEADS):
        cols = slice(h * xd, (h + 1) * xd)
        s = _mm_nt(q[:, cols], k_ref[0, :, cols]) * (xd ** -0.5)
        e = jnp.exp(s - jnp.max(s, axis=-1, keepdims=True))
        probs = e / jnp.sum(e, axis=-1, keepdims=True)
        outs.append(_mm(probs, v_ref[0, :, cols]))
    y_xa = _mm(jnp.concatenate(outs, axis=1), xawo_ref[...])

    merged = (jax.nn.sigmoid(proj(4)) * y_pool + jax.nn.sigmoid(proj(5)) * y_sgu
              + jax.nn.sigmoid(proj(6)) * y_xa)
    h_ref[0] = x + _mm(merged, wout_ref[...])


def _resident(shape):
    zeros = (0,) * len(shape)
    return pl.BlockSpec(shape, lambda *_: zeros, pipeline_mode=pl.Buffered(1))


def _mixer(x, k, v, weights, b0, nb):
    _, S, D = x.shape
    ts = min(SEQ_TILE, S)
    M = k.shape[1]
    return pl.pallas_call(
        _mixer_kernel,
        grid=(nb, S // ts),
        in_specs=[pl.BlockSpec((1, ts, D), lambda b, s: (b + b0, s, 0)),
                  pl.BlockSpec((1, M, k.shape[2]), lambda b, s: (b + b0, 0, 0)),
                  pl.BlockSpec((1, M, v.shape[2]), lambda b, s: (b + b0, 0, 0))]
                 + [_resident(w.shape) for w in weights],
        out_specs=pl.BlockSpec((1, ts, D), lambda b, s: (b, s, 0)),
        out_shape=jax.ShapeDtypeStruct((nb, S, D), F32),
        scratch_shapes=[pltpu.VMEM((POOL_HALO, D), F32)],
        compiler_params=pltpu.CompilerParams(dimension_semantics=("arbitrary", "arbitrary"),
                                             vmem_limit_bytes=VMEM_LIMIT_BYTES),
    )(x, k, v, *weights)


def _topk_rows(vals, payload=None):
    n_rows = vals.shape[0]
    row = lax.broadcasted_iota(jnp.int32, vals.shape, 0).astype(F32)
    out_v, out_i = [], []
    for _ in range(PEER_TOPK):
        m = jnp.max(vals, axis=0, keepdims=True)
        first = jnp.min(jnp.where(vals == m, row, float(n_rows)), axis=0, keepdims=True)
        sel = row == first
        out_v.append(m)
        if payload is None:
            out_i.append(first)
        else:
            out_i.append(jnp.max(jnp.where(sel, payload, -1.0), axis=0, keepdims=True))
        vals = jnp.where(sel, -jnp.inf, vals)
    return jnp.concatenate(out_v, axis=0), jnp.concatenate(out_i, axis=0)


def _candidates(v1, i1, v2, i2):
    sub = 8
    jrow = lax.broadcasted_iota(jnp.int32, (sub, 1), 0)
    vals = [v1[0:1] + v2]
    ids = [i1[0:1] * float(PEER_N_KEYS) + i2]
    for i in range(1, sub):
        keep = jrow < (PEER_TOPK // (i + 1))
        vals.append(jnp.where(keep, v1[i:i + 1] + v2[0:sub], -jnp.inf))
        ids.append(i1[i:i + 1] * float(PEER_N_KEYS) + i2[0:sub])
    vals.append(v1[sub:] + v2[0:1])
    ids.append(i1[sub:] * float(PEER_N_KEYS) + i2[0:1])
    return jnp.concatenate(vals, axis=0), jnp.concatenate(ids, axis=0)


def _select_kernel(h_ref, n2g_ref, wq_ref, k1_ref, k2_ref, xn_ref, exp_ref, gate_ref):
    xn = _rmsnorm(h_ref[...], n2g_ref[...])
    xn_ref[...] = _pack_halves(xn)
    q = _mm(xn, wq_ref[...])
    half = k1_ref.shape[1]
    experts, gates = [], []
    for h in range(PEER_HEADS):
        q1 = q[:, (2 * h) * half:(2 * h + 1) * half]
        q2 = q[:, (2 * h + 1) * half:(2 * h + 2) * half]
        v1, i1 = _topk_rows(_mm_nt(k1_ref[...], q1))
        v2, i2 = _topk_rows(_mm_nt(k2_ref[...], q2))
        best, eid = _topk_rows(*_candidates(v1, i1, v2, i2))
        e = jnp.exp(best - jnp.max(best, axis=0, keepdims=True))
        gates.append(e / jnp.sum(e, axis=0, keepdims=True))
        experts.append(eid)
    exp_ref[...] = jnp.concatenate(experts, axis=0).T.astype(jnp.int32)
    gate_ref[...] = jnp.concatenate(gates, axis=0).T


def _select(h, n2g, w_q, keys1, keys2):
    T, D = h.shape
    tt = min(SELECT_TILE, T)
    hk = PEER_HEADS * PEER_TOPK
    return pl.pallas_call(
        _select_kernel,
        grid=(T // tt,),
        in_specs=[pl.BlockSpec((tt, D), lambda i: (i, 0)),
                  _resident(n2g.shape), _resident(w_q.shape),
                  _resident(keys1.shape), _resident(keys2.shape)],
        out_specs=[pl.BlockSpec((tt, D // 2), lambda i: (i, 0)),
                   pl.BlockSpec((tt, hk), lambda i: (i, 0)),
                   pl.BlockSpec((tt, hk), lambda i: (i, 0))],
        out_shape=[jax.ShapeDtypeStruct((T, D // 2), I32),
                   jax.ShapeDtypeStruct((T, hk), jnp.int32),
                   jax.ShapeDtypeStruct((T, hk), F32)],
        compiler_params=pltpu.CompilerParams(dimension_semantics=("arbitrary",),
                                             vmem_limit_bytes=VMEM_LIMIT_BYTES),
    )(h, n2g, w_q, keys1, keys2)


def _pack_table(table):
    half = table.shape[1] // 2
    bits = lax.bitcast_convert_type(table.astype(jnp.bfloat16), jnp.uint16).astype(jnp.uint32)
    return lax.bitcast_convert_type((bits[:, half:] << 16) | bits[:, :half], I32)


def _bf16_bits(x):
    return lax.bitcast_convert_type(x.astype(jnp.bfloat16).astype(F32), I32)


def _pack_halves(x):
    half = x.shape[1] // 2
    return (_bf16_bits(x[:, half:]) & HIGH_HALF) | lax.shift_right_logical(_bf16_bits(x[:, :half]), 16)


def _pack_splat(x):
    bits = _bf16_bits(x)
    return (bits & HIGH_HALF) | lax.shift_right_logical(bits, 16)


def _unpack_words(w):
    lo = lax.bitcast_convert_type(lax.shift_left(w, jnp.full(w.shape, 16, I32)), F32)
    hi = lax.bitcast_convert_type(lax.bitwise_and(w, jnp.full(w.shape, HIGH_HALF, I32)), F32)
    return lo, hi


def _as_bf16(w):
    return plsc.bitcast(w, jnp.bfloat16)


def _halves_f32(p):
    return _unpack_words(plsc.bitcast(p, I32))


def _sc_kernel(body, out_type, scratch_types):
    mesh = plsc.VectorSubcoreMesh(core_axis_name="c", subcore_axis_name="s")
    return pl.kernel(body, out_type=out_type, mesh=mesh, scratch_types=scratch_types,
                     compiler_params=pltpu.CompilerParams(needs_layout_passes=False))


def _sc_worker_base(tokens_per_worker):
    return (lax.axis_index("s") * SC_CORES + lax.axis_index("c")) * tokens_per_worker


def _sc_pipeline(nchunk, gather, compute):
    ahead = SC_NBUF - 1
    for i in range(ahead):
        gather(i, i).start()

    @pl.loop(0, nchunk)
    def _(ci):
        @pl.when(ci + ahead < nchunk)
        def _():
            gather(ci + ahead, (ci + ahead) % SC_NBUF).start()

        slot = ci % SC_NBUF
        gather(ci, slot).wait()
        compute(ci, slot)


def _sc_dots(xp, experts, ptab):
    T, W = xp.shape
    HK = experts.shape[1]
    L, TG, CH = SC_LANES, SC_TOKENS, SC_CHUNK
    tpw = T // SC_WORKERS
    cpt = HK // CH
    nchunk = TG * cpt
    assert W == ptab.shape[1] and T % (SC_WORKERS * TG) == 0 and HK % CH == 0 and CH % L == 0
    assert W % (2 * L) == 0 and nchunk >= SC_NBUF

    def body(x_hbm, idx_hbm, tab_hbm, out_hbm, x_v, idx_v, out_v, rows_v, acc_v, sem):
        base = _sc_worker_base(tpw)
        lane = lax.iota(I32, L)

        def gather(ci, slot):
            tok, c = ci // cpt, ci % cpt
            return pltpu.make_async_copy(tab_hbm.at[idx_v.at[tok, pl.ds(c * CH, CH)]],
                                         rows_v.at[slot], sem.at[slot])

        def compute(ci, slot):
            tok, c = ci // cpt, ci % cpt
            for eg in range(CH // L):
                accs = [jnp.zeros((L,), F32) for _ in range(L)]
                for j in range(0, W // L, 2):
                    xa = _as_bf16(x_v[tok, pl.ds(j * L, L)])
                    xb = _as_bf16(x_v[tok, pl.ds((j + 1) * L, L)])
                    for kk in range(L):
                        ra = _as_bf16(rows_v[slot, eg * L + kk, pl.ds(j * L, L)])
                        rb = _as_bf16(rows_v[slot, eg * L + kk, pl.ds((j + 1) * L, L)])
                        lo, hi = _halves_f32(ra * xa + rb * xb)
                        accs[kk] = accs[kk] + lo + hi
                for kk in range(L):
                    acc_v[kk, :] = accs[kk]

                s = jnp.zeros((L,), F32)
                for j in range(L):
                    s = s + plsc.load_gather(acc_v, [lane, jnp.full((L,), j, I32)])
                out_v[tok, pl.ds(c * CH + eg * L, L)] = s

        @pl.loop(0, tpw // TG)
        def _(g):
            t0 = base + g * TG
            pltpu.sync_copy(x_hbm.at[pl.ds(t0, TG)], x_v)
            pltpu.sync_copy(idx_hbm.at[pl.ds(t0, TG)], idx_v)
            _sc_pipeline(nchunk, gather, compute)
            pltpu.sync_copy(out_v, out_hbm.at[pl.ds(t0, TG)])

    return _sc_kernel(
        body, jax.ShapeDtypeStruct((T, HK), F32),
        [pltpu.VMEM((TG, W), I32), pltpu.VMEM((TG, HK), I32), pltpu.VMEM((TG, HK), F32),
         pltpu.VMEM((SC_NBUF, CH, W), I32), pltpu.VMEM((L, L), F32),
         pltpu.SemaphoreType.DMA((SC_NBUF,))],
    )(xp, experts, ptab)


def _sc_wsum(wp, experts, ptab):
    T, HK = wp.shape
    W = ptab.shape[1]
    D = 2 * W
    L, TG, CH = SC_LANES, SC_WSUM_TOKENS, SC_CHUNK
    tpw = T // SC_WORKERS
    cpt = HK // CH
    nchunk = TG * cpt
    DG = 8
    assert T % (SC_WORKERS * TG) == 0 and HK % CH == 0 and CH % 2 == 0 and W % (DG * L) == 0
    assert nchunk >= SC_NBUF

    def body(w_hbm, idx_hbm, tab_hbm, out_hbm, w_v, idx_v, y_v, rows_v, sem):
        base = _sc_worker_base(tpw)

        def gather(ci, slot):
            tok, c = ci // cpt, ci % cpt
            return pltpu.make_async_copy(tab_hbm.at[idx_v.at[tok, pl.ds(c * CH, CH)]],
                                         rows_v.at[slot], sem.at[slot])

        def compute(ci, slot):
            tok, c = ci // cpt, ci % cpt
            tokv = jnp.full((L,), tok, I32)
            cont = jnp.full((L,), c, I32) != 0

            @pl.loop(0, W // (DG * L))
            def _(dg):
                off = dg * (DG * L)
                zero = jnp.zeros((L,), F32)
                alo = [jnp.where(cont, y_v[tok, pl.ds(off + j * L, L)], zero) for j in range(DG)]
                ahi = [jnp.where(cont, y_v[tok, pl.ds(W + off + j * L, L)], zero) for j in range(DG)]
                for kk in range(0, CH, 2):
                    wa = _as_bf16(plsc.load_gather(w_v, [tokv, jnp.full((L,), c * CH + kk, I32)]))
                    wb = _as_bf16(plsc.load_gather(w_v, [tokv, jnp.full((L,), c * CH + kk + 1, I32)]))
                    for j in range(DG):
                        ra = _as_bf16(rows_v[slot, kk, pl.ds(off + j * L, L)])
                        rb = _as_bf16(rows_v[slot, kk + 1, pl.ds(off + j * L, L)])
                        lo, hi = _halves_f32(wa * ra + wb * rb)
                        alo[j] = alo[j] + lo
                        ahi[j] = ahi[j] + hi
                for j in range(DG):
                    y_v[tok, pl.ds(off + j * L, L)] = alo[j]
                    y_v[tok, pl.ds(W + off + j * L, L)] = ahi[j]

        @pl.loop(0, tpw // TG)
        def _(g):
            t0 = base + g * TG
            pltpu.sync_copy(w_hbm.at[pl.ds(t0, TG)], w_v)
            pltpu.sync_copy(idx_hbm.at[pl.ds(t0, TG)], idx_v)
            _sc_pipeline(nchunk, gather, compute)
            pltpu.sync_copy(y_v, out_hbm.at[pl.ds(t0, TG)])

    return _sc_kernel(
        body, jax.ShapeDtypeStruct((T, D), F32),
        [pltpu.VMEM((TG, HK), I32), pltpu.VMEM((TG, HK), I32), pltpu.VMEM((TG, D), F32),
         pltpu.VMEM((SC_NBUF, CH, W), I32), pltpu.SemaphoreType.DMA((SC_NBUF,))],
    )(wp, experts, ptab)


def _tc_wsum_kernel(exp_ref, w_ref, tab_ref, y_ref):
    tb, hk = exp_ref.shape
    sub, lanes = tab_ref.shape[1:]
    for t in range(tb):
        lo_acc = jnp.zeros((sub, lanes), F32)
        hi_acc = jnp.zeros((sub, lanes), F32)
        for k in range(hk):
            row = tab_ref[exp_ref[t, k]]
            w = w_ref[t, k]
            lo_acc = lo_acc + w * lax.bitcast_convert_type(lax.shift_left(row, 16), F32)
            hi_acc = hi_acc + w * lax.bitcast_convert_type(row & HIGH_HALF, F32)
        y_ref[t, pl.ds(0, sub), :] = lo_acc
        y_ref[t, pl.ds(sub, sub), :] = hi_acc


def _tc_wsum(w, experts, ptab):
    T, hk = w.shape
    E, W = ptab.shape
    sub = W // TC_LANES
    tb = min(TC_WSUM_TILE, T)
    smem = pl.BlockSpec((tb, hk), lambda i: (i, 0), memory_space=pltpu.SMEM)
    y = pl.pallas_call(
        _tc_wsum_kernel, grid=(T // tb,),
        in_specs=[smem, smem, _resident((E, sub, TC_LANES))],
        out_specs=pl.BlockSpec((tb, 2 * sub, TC_LANES), lambda i: (i, 0, 0)),
        out_shape=jax.ShapeDtypeStruct((T, 2 * sub, TC_LANES), F32),
        cost_estimate=pl.CostEstimate(flops=4 * T * hk * W, transcendentals=0,
                                      bytes_accessed=4 * (T * hk * W + E * W + 2 * T * W + 2 * T * hk)),
        compiler_params=pltpu.CompilerParams(dimension_semantics=("arbitrary",),
                                             vmem_limit_bytes=VMEM_LIMIT_BYTES),
    )(experts, w, ptab.reshape(E, sub, TC_LANES))
    return y.reshape(T, 2 * W)


def _act_kernel(dots_ref, gate_ref, w_ref, wp_ref):
    w = _gelu(dots_ref[...]) * gate_ref[...]
    w_ref[...] = w
    wp_ref[...] = _pack_splat(w)


def _act(dots, gates):
    T, hk = dots.shape
    tt = min(ROW_TILE, T)
    spec = pl.BlockSpec((tt, hk), lambda i: (i, 0))
    return pl.pallas_call(
        _act_kernel, grid=(T // tt,), in_specs=[spec, spec], out_specs=[spec, spec],
        out_shape=[jax.ShapeDtypeStruct((T, hk), F32), jax.ShapeDtypeStruct((T, hk), I32)],
        compiler_params=pltpu.CompilerParams(dimension_semantics=("arbitrary",)),
    )(dots, gates)


def _final_kernel(h_ref, y_ref, g_ref, *rest):
    o_ref = rest[-1]
    o_ref[...] = _rmsnorm(h_ref[...] + y_ref[...], g_ref[...])


def _final(h, y, gain, out, row0, total_rows):
    T, D = h.shape
    tt = min(ROW_TILE, T)
    assert row0 % tt == 0
    spec = pl.BlockSpec((tt, D), lambda i: (i, 0))
    out_spec = pl.BlockSpec((tt, D), lambda i: (i + row0 // tt, 0))
    in_specs = [spec, spec, pl.BlockSpec((1, D), lambda i: (0, 0))]
    args = (h, y, gain)
    aliases = {}
    if out is not None:
        in_specs.append(pl.BlockSpec(memory_space=pl.ANY))
        args += (out,)
        aliases = {3: 0}
    return pl.pallas_call(
        _final_kernel, grid=(T // tt,), in_specs=in_specs, out_specs=out_spec,
        out_shape=jax.ShapeDtypeStruct((total_rows, D), F32),
        input_output_aliases=aliases,
        compiler_params=pltpu.CompilerParams(dimension_semantics=("arbitrary",)),
    )(*args)


def kernel(x, mem, norm1_gain, w_in, pool_w, pool_scale, sgu_ln_gain, sgu_ln_bias, sgu_w_s, sgu_b_s,
           sgu_w_out, mem_norm_gain, xa_w_kv, xa_w_out, w_out, norm2_gain, peer_w_q, peer_keys1,
           peer_keys2, peer_u, peer_v, final_norm_gain):
    B, S, D = x.shape
    depth = w_in.shape[0]
    lo = lambda w: w.astype(MXU_DTYPE)
    row = lambda w: w.reshape(1, -1)
    assert depth == 1, "the final RMSNorm is fused with the last layer's residual add"
    l = 0
    weights = (row(norm1_gain[l]), lo(w_in[l]), lo(pool_w[l]), row(pool_scale[l]),
               row(sgu_ln_gain[l]), row(sgu_ln_bias[l]), sgu_w_s[l], sgu_b_s[l].T,
               lo(sgu_w_out[l]), lo(xa_w_out[l]), lo(w_out[l]))
    select_w = (row(norm2_gain[l]), lo(peer_w_q[l]), lo(peer_keys1[l]), lo(peer_keys2[l]))
    u_packed, v_packed = _pack_table(peer_u[l]), _pack_table(peer_v[l])
    k, v = _memkv(mem, row(mem_norm_gain[l]), lo(xa_w_kv[l]))
    fgain = row(final_norm_gain)
    chunks = BATCH_CHUNKS if sum(BATCH_CHUNKS) == B else (B,)
    b0, tc_sums, tc_parts, sc_parts = 0, [], [], []
    for c, nb in enumerate(chunks):
        rows = nb * S
        h = _mixer(x, k, v, weights, b0, nb).reshape(rows, D)
        xn, experts, gates = _select(h, *select_w)
        dots = _sc_dots(xn, experts, u_packed)
        if c > 0:
            dots, tc_sums[c - 1] = lax.optimization_barrier((dots, tc_sums[c - 1]))
        w, wp = _act(dots, gates)
        n_tc = (rows * TC_WSUM_SHARE[0] // TC_WSUM_SHARE[1]) // SC_ROW_QUANTUM * SC_ROW_QUANTUM
        tc_sums.append(_tc_wsum(w[:n_tc], experts[:n_tc], v_packed))
        tc_parts.append((h[:n_tc], b0 * S))
        sc_parts.append((h[n_tc:], _sc_wsum(wp[n_tc:], experts[n_tc:], v_packed), b0 * S + n_tc))
        b0 += nb
    out = None
    for (h_tc, row0), y_tc in zip(tc_parts, tc_sums):
        out = _final(h_tc, y_tc, fgain, out, row0, B * S)
    for h_sc, y_sc, row0 in sc_parts:
        out = _final(h_sc, y_sc, fgain, out, row0, B * S)
    return out.reshape(B, S, D)
```

```python
import functools
import math

import jax
import jax.numpy as jnp
from jax import lax
from jax.experimental import pallas as pl
from jax.experimental.pallas import tpu as pltpu
from jax.experimental.pallas import tpu_sc as plsc

F32 = jnp.float32
I32 = jnp.int32
MXU_DTYPE = jnp.bfloat16

RMS_EPS = 1e-6
LN_EPS = 1e-5
POOL_WINDOWS = (2, 4, 8, 16)
POOL_HALO = 16
SGU_CHUNK = 128
SGU_HEADS = 8
XA_HEADS = 4
PEER_HEADS = 8
PEER_N_KEYS = 128
PEER_TOPK = 16

V7X_VMEM_BYTES = 64 * 1024 * 1024
VMEM_LIMIT_BYTES = 56 * 1024 * 1024

SEQ_TILE = 256
SELECT_TILE = 256
ROW_TILE = 512
TC_WSUM_TILE = 8
TC_LANES = 128
BATCH_CHUNKS = (1, 1, 2, 2, 2)

SC_CORES = 2
SC_SUBCORES = 16
SC_LANES = 16
SC_WORKERS = SC_CORES * SC_SUBCORES
SC_TOKENS = 32
SC_WSUM_TOKENS = 16
SC_CHUNK = 32
SC_NBUF = 4
HIGH_HALF = -65536
SC_ROW_QUANTUM = SC_WORKERS * SC_WSUM_TOKENS
TC_WSUM_SHARE = (9, 16)


def _rmsnorm(x, gain):
    return x * lax.rsqrt(jnp.mean(x * x, axis=-1, keepdims=True) + RMS_EPS) * gain


def _gelu(x):
    return 0.5 * x * (1.0 + lax.erf(x * (1.0 / math.sqrt(2.0))))


def _mm(a, b):
    return jnp.dot(a.astype(MXU_DTYPE), b.astype(MXU_DTYPE), preferred_element_type=F32)


def _mm_nt(a, b):
    return lax.dot_general(a.astype(MXU_DTYPE), b.astype(MXU_DTYPE),
                           (((1,), (1,)), ((), ())), preferred_element_type=F32)


def _memkv_kernel(mem_ref, gain_ref, wkv_ref, k_ref, v_ref):
    width = k_ref.shape[-1]
    kv = _mm(_rmsnorm(mem_ref[0], gain_ref[...]), wkv_ref[...])
    k_ref[0] = kv[:, :width].astype(k_ref.dtype)
    v_ref[0] = kv[:, width:].astype(v_ref.dtype)


def _memkv(mem, gain, w_kv):
    B, M, D = mem.shape
    width = w_kv.shape[1] // 2
    const = lambda b: (0, 0)
    return pl.pallas_call(
        _memkv_kernel,
        grid=(B,),
        in_specs=[pl.BlockSpec((1, M, D), lambda b: (b, 0, 0)),
                  pl.BlockSpec((1, D), const),
                  pl.BlockSpec(w_kv.shape, const)],
        out_specs=[pl.BlockSpec((1, M, width), lambda b: (b, 0, 0))] * 2,
        out_shape=[jax.ShapeDtypeStruct((B, M, width), MXU_DTYPE)] * 2,
        compiler_params=pltpu.CompilerParams(dimension_semantics=("arbitrary",),
                                             vmem_limit_bytes=VMEM_LIMIT_BYTES),
    )(mem, gain, w_kv)


def _mixer_kernel(x_ref, k_ref, v_ref, n1g_ref, win_ref, poolw_ref, pools_ref, lng_ref, lnb_ref,
                  ws_ref, bst_ref, sguwo_ref, xawo_ref, wout_ref, h_ref, tail_ref):
    ts, d = x_ref.shape[1], x_ref.shape[2]
    s_idx = pl.program_id(1)
    x = x_ref[0]
    nb = _rmsnorm(x, n1g_ref[...]).astype(MXU_DTYPE)

    def proj(col):
        return jnp.dot(nb, win_ref[:, col * d:(col + 1) * d], preferred_element_type=F32)

    @pl.when(s_idx == 0)
    def _():
        tail_ref[...] = jnp.zeros_like(tail_ref)

    p = proj(0)
    ext = jnp.concatenate([tail_ref[...], p], axis=0)
    tail_ref[...] = p[ts - POOL_HALO:, :]
    pos = s_idx * ts + lax.broadcasted_iota(jnp.int32, (ts, 1), 0)
    group = d // len(POOL_WINDOWS)
    y_pool = []
    for g, w in enumerate(POOL_WINDOWS):
        acc = ext[:, g * group:(g + 1) * group]
        shift = 1
        while shift < w:
            acc = acc + pltpu.roll(acc, shift, 0)
            shift *= 2
        count = jnp.minimum(pos + 1, w).astype(F32)
        diff = acc[POOL_HALO:, :] / count - p[:, g * group:(g + 1) * group]
        y_pool.append(_mm(diff, poolw_ref[g]))
    y_pool = jnp.concatenate(y_pool, axis=1) * pools_ref[...]

    u = _gelu(proj(1))
    v = _gelu(proj(2))
    mu = jnp.mean(v, axis=-1, keepdims=True)
    vc = v - mu
    var = jnp.mean(vc * vc, axis=-1, keepdims=True)
    v = (vc * lax.rsqrt(var + LN_EPS) * lng_ref[...] + lnb_ref[...]).astype(MXU_DTYPE)
    hd = d // SGU_HEADS
    causal = (lax.broadcasted_iota(jnp.int32, (SGU_CHUNK, SGU_CHUNK), 0)
              >= lax.broadcasted_iota(jnp.int32, (SGU_CHUNK, SGU_CHUNK), 1))
    mixed_rows = []
    w_masked = [jnp.where(causal, ws_ref[h], 0.0).astype(MXU_DTYPE) for h in range(SGU_HEADS)]
    for c in range(ts // SGU_CHUNK):
        rows = slice(c * SGU_CHUNK, (c + 1) * SGU_CHUNK)
        heads = []
        for h in range(SGU_HEADS):
            mixed = jnp.dot(w_masked[h], v[rows, h * hd:(h + 1) * hd], preferred_element_type=F32)
            heads.append(mixed + bst_ref[:, h:h + 1])
        mixed_rows.append(jnp.concatenate(heads, axis=1))
    mixed = jnp.concatenate(mixed_rows, axis=0) if len(mixed_rows) > 1 else mixed_rows[0]
    y_sgu = _mm(u * mixed, sguwo_ref[...])

    q = proj(3).astype(MXU_DTYPE)
    xd = d // XA_HEADS
    outs = []
    for h in range(XA_HEADS):
        cols = slice(h * xd, (h + 1) * xd)
        s = _mm_nt(q[:, cols], k_ref[0, :, cols]) * (xd ** -0.5)
        e = jnp.exp(s - jnp.max(s, axis=-1, keepdims=True))
        probs = e / jnp.sum(e, axis=-1, keepdims=True)
        outs.append(_mm(probs, v_ref[0, :, cols]))
    y_xa = _mm(jnp.concatenate(outs, axis=1), xawo_ref[...])

    merged = (jax.nn.sigmoid(proj(4)) * y_pool + jax.nn.sigmoid(proj(5)) * y_sgu
              + jax.nn.sigmoid(proj(6)) * y_xa)
    h_ref[0] = x + _mm(merged, wout_ref[...])


def _resident(shape):
    zeros = (0,) * len(shape)
    return pl.BlockSpec(shape, lambda *_: zeros, pipeline_mode=pl.Buffered(1))


def _mixer(x, k, v, weights, b0, nb):
    _, S, D = x.shape
    ts = min(SEQ_TILE, S)
    M = k.shape[1]
    return pl.pallas_call(
        _mixer_kernel,
        grid=(nb, S // ts),
        in_specs=[pl.BlockSpec((1, ts, D), lambda b, s: (b + b0, s, 0)),
                  pl.BlockSpec((1, M, k.shape[2]), lambda b, s: (b + b0, 0, 0)),
                  pl.BlockSpec((1, M, v.shape[2]), lambda b, s: (b + b0, 0, 0))]
                 + [_resident(w.shape) for w in weights],
        out_specs=pl.BlockSpec((1, ts, D), lambda b, s: (b, s, 0)),
        out_shape=jax.ShapeDtypeStruct((nb, S, D), F32),
        scratch_shapes=[pltpu.VMEM((POOL_HALO, D), F32)],
        compiler_params=pltpu.CompilerParams(dimension_semantics=("arbitrary", "arbitrary"),
                                             vmem_limit_bytes=VMEM_LIMIT_BYTES),
    )(x, k, v, *weights)


def _topk_rows(vals, payload=None):
    n_rows = vals.shape[0]
    row = lax.broadcasted_iota(jnp.int32, vals.shape, 0).astype(F32)
    out_v, out_i = [], []
    for _ in range(PEER_TOPK):
        m = jnp.max(vals, axis=0, keepdims=True)
        first = jnp.min(jnp.where(vals == m, row, float(n_rows)), axis=0, keepdims=True)
        sel = row == first
        out_v.append(m)
        if payload is None:
            out_i.append(first)
        else:
            out_i.append(jnp.max(jnp.where(sel, payload, -1.0), axis=0, keepdims=True))
        vals = jnp.where(sel, -jnp.inf, vals)
    return jnp.concatenate(out_v, axis=0), jnp.concatenate(out_i, axis=0)


def _candidates(v1, i1, v2, i2):
    sub = 8
    jrow = lax.broadcasted_iota(jnp.int32, (sub, 1), 0)
    vals = [v1[0:1] + v2]
    ids = [i1[0:1] * float(PEER_N_KEYS) + i2]
    for i in range(1, sub):
        keep = jrow < (PEER_TOPK // (i + 1))
        vals.append(jnp.where(keep, v1[i:i + 1] + v2[0:sub], -jnp.inf))
        ids.append(i1[i:i + 1] * float(PEER_N_KEYS) + i2[0:sub])
    vals.append(v1[sub:] + v2[0:1])
    ids.append(i1[sub:] * float(PEER_N_KEYS) + i2[0:1])
    return jnp.concatenate(vals, axis=0), jnp.concatenate(ids, axis=0)


def _select_kernel(h_ref, n2g_ref, wq_ref, k1_ref, k2_ref, xn_ref, exp_ref, gate_ref):
    xn = _rmsnorm(h_ref[...], n2g_ref[...])
    xn_ref[...] = _pack_halves(xn)
    q = _mm(xn, wq_ref[...])
    half = k1_ref.shape[1]
    experts, gates = [], []
    for h in range(PEER_HEADS):
        q1 = q[:, (2 * h) * half:(2 * h + 1) * half]
        q2 = q[:, (2 * h + 1) * half:(2 * h + 2) * half]
        v1, i1 = _topk_rows(_mm_nt(k1_ref[...], q1))
        v2, i2 = _topk_rows(_mm_nt(k2_ref[...], q2))
        best, eid = _topk_rows(*_candidates(v1, i1, v2, i2))
        e = jnp.exp(best - jnp.max(best, axis=0, keepdims=True))
        gates.append(e / jnp.sum(e, axis=0, keepdims=True))
        experts.append(eid)
    exp_ref[...] = jnp.concatenate(experts, axis=0).T.astype(jnp.int32)
    gate_ref[...] = jnp.concatenate(gates, axis=0).T


def _select(h, n2g, w_q, keys1, keys2):
    T, D = h.shape
    tt = min(SELECT_TILE, T)
    hk = PEER_HEADS * PEER_TOPK
    return pl.pallas_call(
        _select_kernel,
        grid=(T // tt,),
        in_specs=[pl.BlockSpec((tt, D), lambda i: (i, 0)),
                  _resident(n2g.shape), _resident(w_q.shape),
                  _resident(keys1.shape), _resident(keys2.shape)],
        out_specs=[pl.BlockSpec((tt, D // 2), lambda i: (i, 0)),
                   pl.BlockSpec((tt, hk), lambda i: (i, 0)),
                   pl.BlockSpec((tt, hk), lambda i: (i, 0))],
        out_shape=[jax.ShapeDtypeStruct((T, D // 2), I32),
                   jax.ShapeDtypeStruct((T, hk), jnp.int32),
                   jax.ShapeDtypeStruct((T, hk), F32)],
        compiler_params=pltpu.CompilerParams(dimension_semantics=("arbitrary",),
                                             vmem_limit_bytes=VMEM_LIMIT_BYTES),
    )(h, n2g, w_q, keys1, keys2)


def _pack_table(table):
    half = table.shape[1] // 2
    bits = lax.bitcast_convert_type(table.astype(jnp.bfloat16), jnp.uint16).astype(jnp.uint32)
    return lax.bitcast_convert_type((bits[:, half:] << 16) | bits[:, :half], I32)


def _bf16_bits(x):
    return lax.bitcast_convert_type(x.astype(jnp.bfloat16).astype(F32), I32)


def _pack_halves(x):
    half = x.shape[1] // 2
    return (_bf16_bits(x[:, half:]) & HIGH_HALF) | lax.shift_right_logical(_bf16_bits(x[:, :half]), 16)


def _pack_splat(x):
    bits = _bf16_bits(x)
    return (bits & HIGH_HALF) | lax.shift_right_logical(bits, 16)


def _unpack_words(w):
    lo = lax.bitcast_convert_type(lax.shift_left(w, jnp.full(w.shape, 16, I32)), F32)
    hi = lax.bitcast_convert_type(lax.bitwise_and(w, jnp.full(w.shape, HIGH_HALF, I32)), F32)
    return lo, hi


def _as_bf16(w):
    return plsc.bitcast(w, jnp.bfloat16)


def _halves_f32(p):
    return _unpack_words(plsc.bitcast(p, I32))


def _sc_kernel(body, out_type, scratch_types):
    mesh = plsc.VectorSubcoreMesh(core_axis_name="c", subcore_axis_name="s")
    return pl.kernel(body, out_type=out_type, mesh=mesh, scratch_types=scratch_types,
                     compiler_params=pltpu.CompilerParams(needs_layout_passes=False))


def _sc_worker_base(tokens_per_worker):
    return (lax.axis_index("s") * SC_CORES + lax.axis_index("c")) * tokens_per_worker


def _sc_pipeline(nchunk, gather, compute):
    ahead = SC_NBUF - 1
    for i in range(ahead):
        gather(i, i).start()

    @pl.loop(0, nchunk)
    def _(ci):
        @pl.when(ci + ahead < nchunk)
        def _():
            gather(ci + ahead, (ci + ahead) % SC_NBUF).start()

        slot = ci % SC_NBUF
        gather(ci, slot).wait()
        compute(ci, slot)


def _sc_dots(xp, experts, ptab):
    T, W = xp.shape
    HK = experts.shape[1]
    L, TG, CH = SC_LANES, SC_TOKENS, SC_CHUNK
    tpw = T // SC_WORKERS
    cpt = HK // CH
    nchunk = TG * cpt
    assert W == ptab.shape[1] and T % (SC_WORKERS * TG) == 0 and HK % CH == 0 and CH % L == 0
    assert W % (2 * L) == 0 and nchunk >= SC_NBUF

    def body(x_hbm, idx_hbm, tab_hbm, out_hbm, x_v, idx_v, out_v, rows_v, acc_v, sem):
        base = _sc_worker_base(tpw)
        lane = lax.iota(I32, L)

        def gather(ci, slot):
            tok, c = ci // cpt, ci % cpt
            return pltpu.make_async_copy(tab_hbm.at[idx_v.at[tok, pl.ds(c * CH, CH)]],
                                         rows_v.at[slot], sem.at[slot])

        def compute(ci, slot):
            tok, c = ci // cpt, ci % cpt
            for eg in range(CH // L):
                accs = [jnp.zeros((L,), F32) for _ in range(L)]
                for j in range(0, W // L, 2):
                    xa = _as_bf16(x_v[tok, pl.ds(j * L, L)])
                    xb = _as_bf16(x_v[tok, pl.ds((j + 1) * L, L)])
                    for kk in range(L):
                        ra = _as_bf16(rows_v[slot, eg * L + kk, pl.ds(j * L, L)])
                        rb = _as_bf16(rows_v[slot, eg * L + kk, pl.ds((j + 1) * L, L)])
                        lo, hi = _halves_f32(ra * xa + rb * xb)
                        accs[kk] = accs[kk] + lo + hi
                for kk in range(L):
                    acc_v[kk, :] = accs[kk]

                s = jnp.zeros((L,), F32)
                for j in range(L):
                    s = s + plsc.load_gather(acc_v, [lane, jnp.full((L,), j, I32)])
                out_v[tok, pl.ds(c * CH + eg * L, L)] = s

        @pl.loop(0, tpw // TG)
        def _(g):
            t0 = base + g * TG
            pltpu.sync_copy(x_hbm.at[pl.ds(t0, TG)], x_v)
            pltpu.sync_copy(idx_hbm.at[pl.ds(t0, TG)], idx_v)
            _sc_pipeline(nchunk, gather, compute)
            pltpu.sync_copy(out_v, out_hbm.at[pl.ds(t0, TG)])

    return _sc_kernel(
        body, jax.ShapeDtypeStruct((T, HK), F32),
        [pltpu.VMEM((TG, W), I32), pltpu.VMEM((TG, HK), I32), pltpu.VMEM((TG, HK), F32),
         pltpu.VMEM((SC_NBUF, CH, W), I32), pltpu.VMEM((L, L), F32),
         pltpu.SemaphoreType.DMA((SC_NBUF,))],
    )(xp, experts, ptab)


def _sc_wsum(wp, experts, ptab):
    T, HK = wp.shape
    W = ptab.shape[1]
    D = 2 * W
    L, TG, CH = SC_LANES, SC_WSUM_TOKENS, SC_CHUNK
    tpw = T // SC_WORKERS
    cpt = HK // CH
    nchunk = TG * cpt
    DG = 8
    assert T % (SC_WORKERS * TG) == 0 and HK % CH == 0 and CH % 2 == 0 and W % (DG * L) == 0
    assert nchunk >= SC_NBUF

    def body(w_hbm, idx_hbm, tab_hbm, out_hbm, w_v, idx_v, y_v, rows_v, sem):
        base = _sc_worker_base(tpw)

        def gather(ci, slot):
            tok, c = ci // cpt, ci % cpt
            return pltpu.make_async_copy(tab_hbm.at[idx_v.at[tok, pl.ds(c * CH, CH)]],
                                         rows_v.at[slot], sem.at[slot])

        def compute(ci, slot):
            tok, c = ci // cpt, ci % cpt
            tokv = jnp.full((L,), tok, I32)
            cont = jnp.full((L,), c, I32) != 0

            @pl.loop(0, W // (DG * L))
            def _(dg):
                off = dg * (DG * L)
                zero = jnp.zeros((L,), F32)
                alo = [jnp.where(cont, y_v[tok, pl.ds(off + j * L, L)], zero) for j in range(DG)]
                ahi = [jnp.where(cont, y_v[tok, pl.ds(W + off + j * L, L)], zero) for j in range(DG)]
                for kk in range(0, CH, 2):
                    wa = _as_bf16(plsc.load_gather(w_v, [tokv, jnp.full((L,), c * CH + kk, I32)]))
                    wb = _as_bf16(plsc.load_gather(w_v, [tokv, jnp.full((L,), c * CH + kk + 1, I32)]))
                    for j in range(DG):
                        ra = _as_bf16(rows_v[slot, kk, pl.ds(off + j * L, L)])
                        rb = _as_bf16(rows_v[slot, kk + 1, pl.ds(off + j * L, L)])
                        lo, hi = _halves_f32(wa * ra + wb * rb)
                        alo[j] = alo[j] + lo
                        ahi[j] = ahi[j] + hi
                for j in range(DG):
                    y_v[tok, pl.ds(off + j * L, L)] = alo[j]
                    y_v[tok, pl.ds(W + off + j * L, L)] = ahi[j]

        @pl.loop(0, tpw // TG)
        def _(g):
            t0 = base + g * TG
            pltpu.sync_copy(w_hbm.at[pl.ds(t0, TG)], w_v)
            pltpu.sync_copy(idx_hbm.at[pl.ds(t0, TG)], idx_v)
            _sc_pipeline(nchunk, gather, compute)
            pltpu.sync_copy(y_v, out_hbm.at[pl.ds(t0, TG)])

    return _sc_kernel(
        body, jax.ShapeDtypeStruct((T, D), F32),
        [pltpu.VMEM((TG, HK), I32), pltpu.VMEM((TG, HK), I32), pltpu.VMEM((TG, D), F32),
         pltpu.VMEM((SC_NBUF, CH, W), I32), pltpu.SemaphoreType.DMA((SC_NBUF,))],
    )(wp, experts, ptab)


def _tc_wsum_kernel(exp_ref, w_ref, tab_ref, y_ref):
    tb, hk = exp_ref.shape
    sub, lanes = tab_ref.shape[1:]
    for t in range(tb):
        lo_acc = jnp.zeros((sub, lanes), F32)
        hi_acc = jnp.zeros((sub, lanes), F32)
        for k in range(hk):
            row = tab_ref[exp_ref[t, k]]
            w = w_ref[t, k]
            lo_acc = lo_acc + w * lax.bitcast_convert_type(lax.shift_left(row, 16), F32)
            hi_acc = hi_acc + w * lax.bitcast_convert_type(row & HIGH_HALF, F32)
        y_ref[t, pl.ds(0, sub), :] = lo_acc
        y_ref[t, pl.ds(sub, sub), :] = hi_acc


def _tc_wsum(w, experts, ptab):
    T, hk = w.shape
    E, W = ptab.shape
    sub = W // TC_LANES
    tb = min(TC_WSUM_TILE, T)
    smem = pl.BlockSpec((tb, hk), lambda i: (i, 0), memory_space=pltpu.SMEM)
    y = pl.pallas_call(
        _tc_wsum_kernel, grid=(T // tb,),
        in_specs=[smem, smem, _resident((E, sub, TC_LANES))],
        out_specs=pl.BlockSpec((tb, 2 * sub, TC_LANES), lambda i: (i, 0, 0)),
        out_shape=jax.ShapeDtypeStruct((T, 2 * sub, TC_LANES), F32),
        cost_estimate=pl.CostEstimate(flops=4 * T * hk * W, transcendentals=0,
                                      bytes_accessed=4 * (T * hk * W + E * W + 2 * T * W + 2 * T * hk)),
        compiler_params=pltpu.CompilerParams(dimension_semantics=("arbitrary",),
                                             vmem_limit_bytes=VMEM_LIMIT_BYTES),
    )(experts, w, ptab.reshape(E, sub, TC_LANES))
    return y.reshape(T, 2 * W)


def _act_kernel(dots_ref, gate_ref, w_ref, wp_ref):
    w = _gelu(dots_ref[...]) * gate_ref[...]
    w_ref[...] = w
    wp_ref[...] = _pack_splat(w)


def _act(dots, gates):
    T, hk = dots.shape
    tt = min(ROW_TILE, T)
    spec = pl.BlockSpec((tt, hk), lambda i: (i, 0))
    return pl.pallas_call(
        _act_kernel, grid=(T // tt,), in_specs=[spec, spec], out_specs=[spec, spec],
        out_shape=[jax.ShapeDtypeStruct((T, hk), F32), jax.ShapeDtypeStruct((T, hk), I32)],
        compiler_params=pltpu.CompilerParams(dimension_semantics=("arbitrary",)),
    )(dots, gates)


def _final_kernel(h_ref, y_ref, g_ref, *rest):
    o_ref = rest[-1]
    o_ref[...] = _rmsnorm(h_ref[...] + y_ref[...], g_ref[...])


def _final(h, y, gain, out, row0, total_rows):
    T, D = h.shape
    tt = min(ROW_TILE, T)
    assert row0 % tt == 0
    spec = pl.BlockSpec((tt, D), lambda i: (i, 0))
    out_spec = pl.BlockSpec((tt, D), lambda i: (i + row0 // tt, 0))
    in_specs = [spec, spec, pl.BlockSpec((1, D), lambda i: (0, 0))]
    args = (h, y, gain)
    aliases = {}
    if out is not None:
        in_specs.append(pl.BlockSpec(memory_space=pl.ANY))
        args += (out,)
        aliases = {3: 0}
    return pl.pallas_call(
        _final_kernel, grid=(T // tt,), in_specs=in_specs, out_specs=out_spec,
        out_shape=jax.ShapeDtypeStruct((total_rows, D), F32),
        input_output_aliases=aliases,
        compiler_params=pltpu.CompilerParams(dimension_semantics=("arbitrary",)),
    )(*args)


def kernel(x, mem, norm1_gain, w_in, pool_w, pool_scale, sgu_ln_gain, sgu_ln_bias, sgu_w_s, sgu_b_s,
           sgu_w_out, mem_norm_gain, xa_w_kv, xa_w_out, w_out, norm2_gain, peer_w_q, peer_keys1,
           peer_keys2, peer_u, peer_v, final_norm_gain):
    B, S, D = x.shape
    depth = w_in.shape[0]
    lo = lambda w: w.astype(MXU_DTYPE)
    row = lambda w: w.reshape(1, -1)
    assert depth == 1, "the final RMSNorm is fused with the last layer's residual add"
    l = 0
    weights = (row(norm1_gain[l]), lo(w_in[l]), lo(pool_w[l]), row(pool_scale[l]),
               row(sgu_ln_gain[l]), row(sgu_ln_bias[l]), sgu_w_s[l], sgu_b_s[l].T,
               lo(sgu_w_out[l]), lo(xa_w_out[l]), lo(w_out[l]))
    select_w = (row(norm2_gain[l]), lo(peer_w_q[l]), lo(peer_keys1[l]), lo(peer_keys2[l]))
    u_packed, v_packed = _pack_table(peer_u[l]), _pack_table(peer_v[l])
    k, v = _memkv(mem, row(mem_norm_gain[l]), lo(xa_w_kv[l]))
    fgain = row(final_norm_gain)
    chunks = BATCH_CHUNKS if sum(BATCH_CHUNKS) == B else (B,)
    b0, tc_sums, tc_parts, sc_parts = 0, [], [], []
    for c, nb in enumerate(chunks):
        rows = nb * S
        h = _mixer(x, k, v, weights, b0, nb).reshape(rows, D)
        xn, experts, gates = _select(h, *select_w)
        dots = _sc_dots(xn, experts, u_packed)
        if c > 0:
            dots, tc_sums[c - 1] = lax.optimization_barrier((dots, tc_sums[c - 1]))
        w, wp = _act(dots, gates)
        n_tc = (rows * TC_WSUM_SHARE[0] // TC_WSUM_SHARE[1]) // SC_ROW_QUANTUM * SC_ROW_QUANTUM
        tc_sums.append(_tc_wsum(w[:n_tc], experts[:n_tc], v_packed))
        tc_parts.append((h[:n_tc], b0 * S))
        sc_parts.append((h[n_tc:], _sc_wsum(wp[n_tc:], experts[n_tc:], v_packed), b0 * S + n_tc))
        b0 += nb
    parts = [(h_tc, y_tc, row0) for (h_tc, row0), y_tc in zip(tc_parts, tc_sums)]
    order = parts[:-1] + sc_parts[:-1] + parts[-1:] + sc_parts[-1:]
    out = None
    for h_part, y_part, row0 in order:
        out = _final(h_part, y_part, fgain, out, row0, B * S)
    return out.reshape(B, S, D)
```

```python
import functools
import math

import jax
import jax.numpy as jnp
from jax import lax
from jax.experimental import pallas as pl
from jax.experimental.pallas import tpu as pltpu
from jax.experimental.pallas import tpu_sc as plsc

F32 = jnp.float32
I32 = jnp.int32
MXU_DTYPE = jnp.bfloat16

RMS_EPS = 1e-6
LN_EPS = 1e-5
POOL_WINDOWS = (2, 4, 8, 16)
POOL_HALO = 16
SGU_CHUNK = 128
SGU_HEADS = 8
XA_HEADS = 4
PEER_HEADS = 8
PEER_N_KEYS = 128
PEER_TOPK = 16

V7X_VMEM_BYTES = 64 * 1024 * 1024
VMEM_LIMIT_BYTES = 56 * 1024 * 1024

SEQ_TILE = 256
SELECT_TILE = 256
ROW_TILE = 512
TC_WSUM_TILE = 8
TC_LANES = 128
TC_SUBLANES = 8
BATCH_CHUNKS = (1, 1, 2, 2, 2)

SC_CORES = 2
SC_SUBCORES = 16
SC_LANES = 16
SC_WORKERS = SC_CORES * SC_SUBCORES
SC_TOKENS = 32
SC_WSUM_TOKENS = 16
SC_CHUNK = 32
SC_NBUF = 4
HIGH_HALF = -65536
SC_ROW_QUANTUM = SC_WORKERS * SC_WSUM_TOKENS
TC_WSUM_SHARE = (9, 16)


def _rmsnorm(x, gain):
    return x * lax.rsqrt(jnp.mean(x * x, axis=-1, keepdims=True) + RMS_EPS) * gain


def _gelu(x):
    return 0.5 * x * (1.0 + lax.erf(x * (1.0 / math.sqrt(2.0))))


def _mm(a, b):
    return jnp.dot(a.astype(MXU_DTYPE), b.astype(MXU_DTYPE), preferred_element_type=F32)


def _mm_nt(a, b):
    return lax.dot_general(a.astype(MXU_DTYPE), b.astype(MXU_DTYPE),
                           (((1,), (1,)), ((), ())), preferred_element_type=F32)


def _memkv_kernel(mem_ref, gain_ref, wkv_ref, k_ref, v_ref):
    width = k_ref.shape[-1]
    kv = _mm(_rmsnorm(mem_ref[0], gain_ref[...]), wkv_ref[...])
    k_ref[0] = kv[:, :width].astype(k_ref.dtype)
    v_ref[0] = kv[:, width:].astype(v_ref.dtype)


def _memkv(mem, gain, w_kv):
    B, M, D = mem.shape
    width = w_kv.shape[1] // 2
    const = lambda b: (0, 0)
    return pl.pallas_call(
        _memkv_kernel,
        grid=(B,),
        in_specs=[pl.BlockSpec((1, M, D), lambda b: (b, 0, 0)),
                  pl.BlockSpec((1, D), const),
                  pl.BlockSpec(w_kv.shape, const)],
        out_specs=[pl.BlockSpec((1, M, width), lambda b: (b, 0, 0))] * 2,
        out_shape=[jax.ShapeDtypeStruct((B, M, width), MXU_DTYPE)] * 2,
        compiler_params=pltpu.CompilerParams(dimension_semantics=("arbitrary",),
                                             vmem_limit_bytes=VMEM_LIMIT_BYTES),
    )(mem, gain, w_kv)


def _mixer_kernel(x_ref, k_ref, v_ref, n1g_ref, win_ref, poolw_ref, pools_ref, lng_ref, lnb_ref,
                  ws_ref, bst_ref, sguwo_ref, xawo_ref, wout_ref, h_ref, tail_ref):
    ts, d = x_ref.shape[1], x_ref.shape[2]
    s_idx = pl.program_id(1)
    x = x_ref[0]
    nb = _rmsnorm(x, n1g_ref[...]).astype(MXU_DTYPE)

    def proj(col):
        return jnp.dot(nb, win_ref[:, col * d:(col + 1) * d], preferred_element_type=F32)

    @pl.when(s_idx == 0)
    def _():
        tail_ref[...] = jnp.zeros_like(tail_ref)

    p = proj(0)
    ext = jnp.concatenate([tail_ref[...], p], axis=0)
    tail_ref[...] = p[ts - POOL_HALO:, :]
    pos = s_idx * ts + lax.broadcasted_iota(jnp.int32, (ts, 1), 0)
    group = d // len(POOL_WINDOWS)
    y_pool = []
    for g, w in enumerate(POOL_WINDOWS):
        acc = ext[:, g * group:(g + 1) * group]
        shift = 1
        while shift < w:
            acc = acc + pltpu.roll(acc, shift, 0)
            shift *= 2
        count = jnp.minimum(pos + 1, w).astype(F32)
        diff = acc[POOL_HALO:, :] / count - p[:, g * group:(g + 1) * group]
        y_pool.append(_mm(diff, poolw_ref[g]))
    y_pool = jnp.concatenate(y_pool, axis=1) * pools_ref[...]

    u = _gelu(proj(1))
    v = _gelu(proj(2))
    mu = jnp.mean(v, axis=-1, keepdims=True)
    vc = v - mu
    var = jnp.mean(vc * vc, axis=-1, keepdims=True)
    v = (vc * lax.rsqrt(var + LN_EPS) * lng_ref[...] + lnb_ref[...]).astype(MXU_DTYPE)
    hd = d // SGU_HEADS
    causal = (lax.broadcasted_iota(jnp.int32, (SGU_CHUNK, SGU_CHUNK), 0)
              >= lax.broadcasted_iota(jnp.int32, (SGU_CHUNK, SGU_CHUNK), 1))
    mixed_rows = []
    w_masked = [jnp.where(causal, ws_ref[h], 0.0).astype(MXU_DTYPE) for h in range(SGU_HEADS)]
    for c in range(ts // SGU_CHUNK):
        rows = slice(c * SGU_CHUNK, (c + 1) * SGU_CHUNK)
        heads = []
        for h in range(SGU_HEADS):
            mixed = jnp.dot(w_masked[h], v[rows, h * hd:(h + 1) * hd], preferred_element_type=F32)
            heads.append(mixed + bst_ref[:, h:h + 1])
        mixed_rows.append(jnp.concatenate(heads, axis=1))
    mixed = jnp.concatenate(mixed_rows, axis=0) if len(mixed_rows) > 1 else mixed_rows[0]
    y_sgu = _mm(u * mixed, sguwo_ref[...])

    q = proj(3).astype(MXU_DTYPE)
    xd = d // XA_HEADS
    outs = []
    for h in range(XA_HEADS):
        cols = slice(h * xd, (h + 1) * xd)
        s = _mm_nt(q[:, cols], k_ref[0, :, cols]) * (xd ** -0.5)
        e = jnp.exp(s - jnp.max(s, axis=-1, keepdims=True))
        probs = e / jnp.sum(e, axis=-1, keepdims=True)
        outs.append(_mm(probs, v_ref[0, :, cols]))
    y_xa = _mm(jnp.concatenate(outs, axis=1), xawo_ref[...])

    merged = (jax.nn.sigmoid(proj(4)) * y_pool + jax.nn.sigmoid(proj(5)) * y_sgu
              + jax.nn.sigmoid(proj(6)) * y_xa)
    h_ref[0] = x + _mm(merged, wout_ref[...])


def _resident(shape):
    zeros = (0,) * len(shape)
    return pl.BlockSpec(shape, lambda *_: zeros, pipeline_mode=pl.Buffered(1))


def _mixer(x, k, v, weights, b0, nb):
    _, S, D = x.shape
    ts = min(SEQ_TILE, S)
    M = k.shape[1]
    return pl.pallas_call(
        _mixer_kernel,
        grid=(nb, S // ts),
        in_specs=[pl.BlockSpec((1, ts, D), lambda b, s: (b + b0, s, 0)),
                  pl.BlockSpec((1, M, k.shape[2]), lambda b, s: (b + b0, 0, 0)),
                  pl.BlockSpec((1, M, v.shape[2]), lambda b, s: (b + b0, 0, 0))]
                 + [_resident(w.shape) for w in weights],
        out_specs=pl.BlockSpec((1, ts, D), lambda b, s: (b, s, 0)),
        out_shape=jax.ShapeDtypeStruct((nb, S, D), F32),
        scratch_shapes=[pltpu.VMEM((POOL_HALO, D), F32)],
        compiler_params=pltpu.CompilerParams(dimension_semantics=("arbitrary", "arbitrary"),
                                             vmem_limit_bytes=VMEM_LIMIT_BYTES),
    )(x, k, v, *weights)


def _topk_rows(vals, payload=None):
    n_rows, width = vals.shape
    sub = TC_SUBLANES
    n_tiles = n_rows // sub
    assert n_rows % sub == 0
    tiles = [vals[j * sub:(j + 1) * sub, :] for j in range(n_tiles)]
    pay = None if payload is None else [payload[j * sub:(j + 1) * sub, :] for j in range(n_tiles)]
    sub_row = lax.broadcasted_iota(jnp.int32, (sub, width), 0).astype(F32)
    rows = [sub_row + float(j * sub) for j in range(n_tiles)]

    def best_of(items):
        while len(items) > 1:
            merged = []
            for a, b in zip(items[0::2], items[1::2]):
                keep_a = a[0] >= b[0]
                merged.append(tuple(jnp.where(keep_a, xa, xb) for xa, xb in zip(a, b)))
            if len(items) % 2:
                merged.append(items[-1])
            items = merged
        return items[0]

    out_v, out_i = [], []
    for _ in range(PEER_TOPK):
        if pay is None:
            v8, j8 = best_of([(tiles[j], float(j)) for j in range(n_tiles)])
        else:
            v8, j8, p8 = best_of([(tiles[j], float(j), pay[j]) for j in range(n_tiles)])
        r8 = j8 * float(sub) + sub_row
        m = jnp.max(v8, axis=0, keepdims=True)
        first = jnp.min(jnp.where(v8 == m, r8, float(n_rows)), axis=0, keepdims=True)
        out_v.append(m)
        if pay is None:
            out_i.append(first)
        else:
            out_i.append(jnp.max(jnp.where(r8 == first, p8, -1.0), axis=0, keepdims=True))
        tiles = [jnp.where(rows[j] == first, -jnp.inf, tiles[j]) for j in range(n_tiles)]
    return jnp.concatenate(out_v, axis=0), jnp.concatenate(out_i, axis=0)


def _candidates(v1, i1, v2, i2):
    sub = 8
    jrow = lax.broadcasted_iota(jnp.int32, (sub, 1), 0)
    vals = [v1[0:1] + v2]
    ids = [i1[0:1] * float(PEER_N_KEYS) + i2]
    for i in range(1, sub):
        keep = jrow < (PEER_TOPK // (i + 1))
        vals.append(jnp.where(keep, v1[i:i + 1] + v2[0:sub], -jnp.inf))
        ids.append(i1[i:i + 1] * float(PEER_N_KEYS) + i2[0:sub])
    vals.append(v1[sub:] + v2[0:1])
    ids.append(i1[sub:] * float(PEER_N_KEYS) + i2[0:1])
    return jnp.concatenate(vals, axis=0), jnp.concatenate(ids, axis=0)


def _select_kernel(h_ref, n2g_ref, wq_ref, k1_ref, k2_ref, xn_ref, exp_ref, gate_ref):
    xn = _rmsnorm(h_ref[...], n2g_ref[...])
    xn_ref[...] = _pack_halves(xn)
    q = _mm(xn, wq_ref[...])
    half = k1_ref.shape[1]
    experts, gates = [], []
    for h in range(PEER_HEADS):
        q1 = q[:, (2 * h) * half:(2 * h + 1) * half]
        q2 = q[:, (2 * h + 1) * half:(2 * h + 2) * half]
        v1, i1 = _topk_rows(_mm_nt(k1_ref[...], q1))
        v2, i2 = _topk_rows(_mm_nt(k2_ref[...], q2))
        best, eid = _topk_rows(*_candidates(v1, i1, v2, i2))
        e = jnp.exp(best - jnp.max(best, axis=0, keepdims=True))
        gates.append(e / jnp.sum(e, axis=0, keepdims=True))
        experts.append(eid)
    exp_ref[...] = jnp.concatenate(experts, axis=0).T.astype(jnp.int32)
    gate_ref[...] = jnp.concatenate(gates, axis=0).T


def _select(h, n2g, w_q, keys1, keys2):
    T, D = h.shape
    tt = min(SELECT_TILE, T)
    hk = PEER_HEADS * PEER_TOPK
    return pl.pallas_call(
        _select_kernel,
        grid=(T // tt,),
        in_specs=[pl.BlockSpec((tt, D), lambda i: (i, 0)),
                  _resident(n2g.shape), _resident(w_q.shape),
                  _resident(keys1.shape), _resident(keys2.shape)],
        out_specs=[pl.BlockSpec((tt, D // 2), lambda i: (i, 0)),
                   pl.BlockSpec((tt, hk), lambda i: (i, 0)),
                   pl.BlockSpec((tt, hk), lambda i: (i, 0))],
        out_shape=[jax.ShapeDtypeStruct((T, D // 2), I32),
                   jax.ShapeDtypeStruct((T, hk), jnp.int32),
                   jax.ShapeDtypeStruct((T, hk), F32)],
        compiler_params=pltpu.CompilerParams(dimension_semantics=("arbitrary",),
                                             vmem_limit_bytes=VMEM_LIMIT_BYTES),
    )(h, n2g, w_q, keys1, keys2)


def _pack_table(table):
    half = table.shape[1] // 2
    bits = lax.bitcast_convert_type(table.astype(jnp.bfloat16), jnp.uint16).astype(jnp.uint32)
    return lax.bitcast_convert_type((bits[:, half:] << 16) | bits[:, :half], I32)


def _bf16_bits(x):
    return lax.bitcast_convert_type(x.astype(jnp.bfloat16).astype(F32), I32)


def _pack_halves(x):
    half = x.shape[1] // 2
    return (_bf16_bits(x[:, half:]) & HIGH_HALF) | lax.shift_right_logical(_bf16_bits(x[:, :half]), 16)


def _pack_splat(x):
    bits = _bf16_bits(x)
    return (bits & HIGH_HALF) | lax.shift_right_logical(bits, 16)


def _unpack_words(w):
    lo = lax.bitcast_convert_type(lax.shift_left(w, jnp.full(w.shape, 16, I32)), F32)
    hi = lax.bitcast_convert_type(lax.bitwise_and(w, jnp.full(w.shape, HIGH_HALF, I32)), F32)
    return lo, hi


def _as_bf16(w):
    return plsc.bitcast(w, jnp.bfloat16)


def _halves_f32(p):
    return _unpack_words(plsc.bitcast(p, I32))


def _sc_kernel(body, out_type, scratch_types):
    mesh = plsc.VectorSubcoreMesh(core_axis_name="c", subcore_axis_name="s")
    return pl.kernel(body, out_type=out_type, mesh=mesh, scratch_types=scratch_types,
                     compiler_params=pltpu.CompilerParams(needs_layout_passes=False))


def _sc_worker_base(tokens_per_worker):
    return (lax.axis_index("s") * SC_CORES + lax.axis_index("c")) * tokens_per_worker


def _sc_pipeline(nchunk, gather, compute):
    ahead = SC_NBUF - 1
    for i in range(ahead):
        gather(i, i).start()

    @pl.loop(0, nchunk)
    def _(ci):
        @pl.when(ci + ahead < nchunk)
        def _():
            gather(ci + ahead, (ci + ahead) % SC_NBUF).start()

        slot = ci % SC_NBUF
        gather(ci, slot).wait()
        compute(ci, slot)


def _sc_dots(xp, experts, ptab):
    T, W = xp.shape
    HK = experts.shape[1]
    L, TG, CH = SC_LANES, SC_TOKENS, SC_CHUNK
    tpw = T // SC_WORKERS
    cpt = HK // CH
    nchunk = TG * cpt
    assert W == ptab.shape[1] and T % (SC_WORKERS * TG) == 0 and HK % CH == 0 and CH % L == 0
    assert W % (2 * L) == 0 and nchunk >= SC_NBUF

    def body(x_hbm, idx_hbm, tab_hbm, out_hbm, x_v, idx_v, out_v, rows_v, acc_v, sem):
        base = _sc_worker_base(tpw)
        lane = lax.iota(I32, L)

        def gather(ci, slot):
            tok, c = ci // cpt, ci % cpt
            return pltpu.make_async_copy(tab_hbm.at[idx_v.at[tok, pl.ds(c * CH, CH)]],
                                         rows_v.at[slot], sem.at[slot])

        def compute(ci, slot):
            tok, c = ci // cpt, ci % cpt
            for eg in range(CH // L):
                accs = [jnp.zeros((L,), F32) for _ in range(L)]
                for j in range(0, W // L, 2):
                    xa = _as_bf16(x_v[tok, pl.ds(j * L, L)])
                    xb = _as_bf16(x_v[tok, pl.ds((j + 1) * L, L)])
                    for kk in range(L):
                        ra = _as_bf16(rows_v[slot, eg * L + kk, pl.ds(j * L, L)])
                        rb = _as_bf16(rows_v[slot, eg * L + kk, pl.ds((j + 1) * L, L)])
                        lo, hi = _halves_f32(ra * xa + rb * xb)
                        accs[kk] = accs[kk] + lo + hi
                for kk in range(L):
                    acc_v[kk, :] = accs[kk]

                s = jnp.zeros((L,), F32)
                for j in range(L):
                    s = s + plsc.load_gather(acc_v, [lane, jnp.full((L,), j, I32)])
                out_v[tok, pl.ds(c * CH + eg * L, L)] = s

        @pl.loop(0, tpw // TG)
        def _(g):
            t0 = base + g * TG
            pltpu.sync_copy(x_hbm.at[pl.ds(t0, TG)], x_v)
            pltpu.sync_copy(idx_hbm.at[pl.ds(t0, TG)], idx_v)
            _sc_pipeline(nchunk, gather, compute)
            pltpu.sync_copy(out_v, out_hbm.at[pl.ds(t0, TG)])

    return _sc_kernel(
        body, jax.ShapeDtypeStruct((T, HK), F32),
        [pltpu.VMEM((TG, W), I32), pltpu.VMEM((TG, HK), I32), pltpu.VMEM((TG, HK), F32),
         pltpu.VMEM((SC_NBUF, CH, W), I32), pltpu.VMEM((L, L), F32),
         pltpu.SemaphoreType.DMA((SC_NBUF,))],
    )(xp, experts, ptab)


def _sc_wsum(wp, experts, ptab):
    T, HK = wp.shape
    W = ptab.shape[1]
    D = 2 * W
    L, TG, CH = SC_LANES, SC_WSUM_TOKENS, SC_CHUNK
    tpw = T // SC_WORKERS
    cpt = HK // CH
    nchunk = TG * cpt
    DG = 8
    assert T % (SC_WORKERS * TG) == 0 and HK % CH == 0 and CH % 2 == 0 and W % (DG * L) == 0
    assert nchunk >= SC_NBUF

    def body(w_hbm, idx_hbm, tab_hbm, out_hbm, w_v, idx_v, y_v, rows_v, sem):
        base = _sc_worker_base(tpw)

        def gather(ci, slot):
            tok, c = ci // cpt, ci % cpt
            return pltpu.make_async_copy(tab_hbm.at[idx_v.at[tok, pl.ds(c * CH, CH)]],
                                         rows_v.at[slot], sem.at[slot])

        def compute(ci, slot):
            tok, c = ci // cpt, ci % cpt
            tokv = jnp.full((L,), tok, I32)
            cont = jnp.full((L,), c, I32) != 0

            @pl.loop(0, W // (DG * L))
            def _(dg):
                off = dg * (DG * L)
                zero = jnp.zeros((L,), F32)
                alo = [jnp.where(cont, y_v[tok, pl.ds(off + j * L, L)], zero) for j in range(DG)]
                ahi = [jnp.where(cont, y_v[tok, pl.ds(W + off + j * L, L)], zero) for j in range(DG)]
                for kk in range(0, CH, 2):
                    wa = _as_bf16(plsc.load_gather(w_v, [tokv, jnp.full((L,), c * CH + kk, I32)]))
                    wb = _as_bf16(plsc.load_gather(w_v, [tokv, jnp.full((L,), c * CH + kk + 1, I32)]))
                    for j in range(DG):
                        ra = _as_bf16(rows_v[slot, kk, pl.ds(off + j * L, L)])
                        rb = _as_bf16(rows_v[slot, kk + 1, pl.ds(off + j * L, L)])
                        lo, hi = _halves_f32(wa * ra + wb * rb)
                        alo[j] = alo[j] + lo
                        ahi[j] = ahi[j] + hi
                for j in range(DG):
                    y_v[tok, pl.ds(off + j * L, L)] = alo[j]
                    y_v[tok, pl.ds(W + off + j * L, L)] = ahi[j]

        @pl.loop(0, tpw // TG)
        def _(g):
            t0 = base + g * TG
            pltpu.sync_copy(w_hbm.at[pl.ds(t0, TG)], w_v)
            pltpu.sync_copy(idx_hbm.at[pl.ds(t0, TG)], idx_v)
            _sc_pipeline(nchunk, gather, compute)
            pltpu.sync_copy(y_v, out_hbm.at[pl.ds(t0, TG)])

    return _sc_kernel(
        body, jax.ShapeDtypeStruct((T, D), F32),
        [pltpu.VMEM((TG, HK), I32), pltpu.VMEM((TG, HK), I32), pltpu.VMEM((TG, D), F32),
         pltpu.VMEM((SC_NBUF, CH, W), I32), pltpu.SemaphoreType.DMA((SC_NBUF,))],
    )(wp, experts, ptab)


def _tc_wsum_kernel(exp_ref, w_ref, tab_ref, y_ref):
    tb, hk = exp_ref.shape
    sub, lanes = tab_ref.shape[1:]
    for t in range(tb):
        lo_acc = jnp.zeros((sub, lanes), F32)
        hi_acc = jnp.zeros((sub, lanes), F32)
        for k in range(hk):
            row = tab_ref[exp_ref[t, k]]
            w = w_ref[t, k]
            lo_acc = lo_acc + w * lax.bitcast_convert_type(lax.shift_left(row, 16), F32)
            hi_acc = hi_acc + w * lax.bitcast_convert_type(row & HIGH_HALF, F32)
        y_ref[t, pl.ds(0, sub), :] = lo_acc
        y_ref[t, pl.ds(sub, sub), :] = hi_acc


def _tc_wsum(w, experts, ptab):
    T, hk = w.shape
    E, W = ptab.shape
    sub = W // TC_LANES
    tb = min(TC_WSUM_TILE, T)
    smem = pl.BlockSpec((tb, hk), lambda i: (i, 0), memory_space=pltpu.SMEM)
    y = pl.pallas_call(
        _tc_wsum_kernel, grid=(T // tb,),
        in_specs=[smem, smem, _resident((E, sub, TC_LANES))],
        out_specs=pl.BlockSpec((tb, 2 * sub, TC_LANES), lambda i: (i, 0, 0)),
        out_shape=jax.ShapeDtypeStruct((T, 2 * sub, TC_LANES), F32),
        cost_estimate=pl.CostEstimate(flops=4 * T * hk * W, transcendentals=0,
                                      bytes_accessed=4 * (T * hk * W + E * W + 2 * T * W + 2 * T * hk)),
        compiler_params=pltpu.CompilerParams(dimension_semantics=("arbitrary",),
                                             vmem_limit_bytes=VMEM_LIMIT_BYTES),
    )(experts, w, ptab.reshape(E, sub, TC_LANES))
    return y.reshape(T, 2 * W)


def _act_kernel(dots_ref, gate_ref, w_ref, wp_ref):
    w = _gelu(dots_ref[...]) * gate_ref[...]
    w_ref[...] = w
    wp_ref[...] = _pack_splat(w)


def _act(dots, gates):
    T, hk = dots.shape
    tt = min(ROW_TILE, T)
    spec = pl.BlockSpec((tt, hk), lambda i: (i, 0))
    return pl.pallas_call(
        _act_kernel, grid=(T // tt,), in_specs=[spec, spec], out_specs=[spec, spec],
        out_shape=[jax.ShapeDtypeStruct((T, hk), F32), jax.ShapeDtypeStruct((T, hk), I32)],
        compiler_params=pltpu.CompilerParams(dimension_semantics=("arbitrary",)),
    )(dots, gates)


def _final_kernel(h_ref, y_ref, g_ref, *rest):
    o_ref = rest[-1]
    o_ref[...] = _rmsnorm(h_ref[...] + y_ref[...], g_ref[...])


def _final(h, y, gain, out, row0, total_rows):
    T, D = h.shape
    tt = min(ROW_TILE, T)
    assert row0 % tt == 0
    spec = pl.BlockSpec((tt, D), lambda i: (i, 0))
    out_spec = pl.BlockSpec((tt, D), lambda i: (i + row0 // tt, 0))
    in_specs = [spec, spec, pl.BlockSpec((1, D), lambda i: (0, 0))]
    args = (h, y, gain)
    aliases = {}
    if out is not None:
        in_specs.append(pl.BlockSpec(memory_space=pl.ANY))
        args += (out,)
        aliases = {3: 0}
    return pl.pallas_call(
        _final_kernel, grid=(T // tt,), in_specs=in_specs, out_specs=out_spec,
        out_shape=jax.ShapeDtypeStruct((total_rows, D), F32),
        input_output_aliases=aliases,
        compiler_params=pltpu.CompilerParams(dimension_semantics=("arbitrary",)),
    )(*args)


def kernel(x, mem, norm1_gain, w_in, pool_w, pool_scale, sgu_ln_gain, sgu_ln_bias, sgu_w_s, sgu_b_s,
           sgu_w_out, mem_norm_gain, xa_w_kv, xa_w_out, w_out, norm2_gain, peer_w_q, peer_keys1,
           peer_keys2, peer_u, peer_v, final_norm_gain):
    B, S, D = x.shape
    depth = w_in.shape[0]
    lo = lambda w: w.astype(MXU_DTYPE)
    row = lambda w: w.reshape(1, -1)
    assert depth == 1, "the final RMSNorm is fused with the last layer's residual add"
    l = 0
    weights = (row(norm1_gain[l]), lo(w_in[l]), lo(pool_w[l]), row(pool_scale[l]),
               row(sgu_ln_gain[l]), row(sgu_ln_bias[l]), sgu_w_s[l], sgu_b_s[l].T,
               lo(sgu_w_out[l]), lo(xa_w_out[l]), lo(w_out[l]))
    select_w = (row(norm2_gain[l]), lo(peer_w_q[l]), lo(peer_keys1[l]), lo(peer_keys2[l]))
    u_packed, v_packed = _pack_table(peer_u[l]), _pack_table(peer_v[l])
    k, v = _memkv(mem, row(mem_norm_gain[l]), lo(xa_w_kv[l]))
    fgain = row(final_norm_gain)
    chunks = BATCH_CHUNKS if sum(BATCH_CHUNKS) == B else (B,)
    b0, tc_sums, tc_parts, sc_parts = 0, [], [], []
    for c, nb in enumerate(chunks):
        rows = nb * S
        h = _mixer(x, k, v, weights, b0, nb).reshape(rows, D)
        xn, experts, gates = _select(h, *select_w)
        dots = _sc_dots(xn, experts, u_packed)
        if c > 0:
            dots, tc_sums[c - 1] = lax.optimization_barrier((dots, tc_sums[c - 1]))
        w, wp = _act(dots, gates)
        n_tc = (rows * TC_WSUM_SHARE[0] // TC_WSUM_SHARE[1]) // SC_ROW_QUANTUM * SC_ROW_QUANTUM
        tc_sums.append(_tc_wsum(w[:n_tc], experts[:n_tc], v_packed))
        tc_parts.append((h[:n_tc], b0 * S))
        sc_parts.append((h[n_tc:], _sc_wsum(wp[n_tc:], experts[n_tc:], v_packed), b0 * S + n_tc))
        b0 += nb
    parts = [(h_tc, y_tc, row0) for (h_tc, row0), y_tc in zip(tc_parts, tc_sums)]
    order = parts[:-1] + sc_parts[:-1] + parts[-1:] + sc_parts[-1:]
    out = None
    for h_part, y_part, row0 in order:
        out = _final(h_part, y_part, fgain, out, row0, B * S)
    return out.reshape(B, S, D)
```

```python
import functools
import math

import jax
import jax.numpy as jnp
from jax import lax
from jax.experimental import pallas as pl
from jax.experimental.pallas import tpu as pltpu
from jax.experimental.pallas import tpu_sc as plsc

F32 = jnp.float32
I32 = jnp.int32
MXU_DTYPE = jnp.bfloat16

RMS_EPS = 1e-6
LN_EPS = 1e-5
POOL_WINDOWS = (2, 4, 8, 16)
POOL_HALO = 16
SGU_CHUNK = 128
SGU_HEADS = 8
XA_HEADS = 4
PEER_HEADS = 8
PEER_N_KEYS = 128
PEER_TOPK = 16

V7X_VMEM_BYTES = 64 * 1024 * 1024
VMEM_LIMIT_BYTES = 56 * 1024 * 1024

SEQ_TILE = 256
SELECT_TILE = 256
ROW_TILE = 512
TC_WSUM_TILE = 8
TC_LANES = 128
TC_SUBLANES = 8
BATCH_CHUNKS = (1, 1, 2, 2, 2)

SC_CORES = 2
SC_SUBCORES = 16
SC_LANES = 16
SC_WORKERS = SC_CORES * SC_SUBCORES
SC_TOKENS = 32
SC_WSUM_TOKENS = 16
SC_CHUNK = 32
SC_NBUF = 4
HIGH_HALF = -65536
SC_ROW_QUANTUM = SC_WORKERS * SC_WSUM_TOKENS
TC_WSUM_SHARE = {1: (5, 8), 2: (9, 16)}
TC_WSUM_SHARE_DEFAULT = (9, 16)


def _rmsnorm(x, gain):
    return x * lax.rsqrt(jnp.mean(x * x, axis=-1, keepdims=True) + RMS_EPS) * gain


def _gelu(x):
    return 0.5 * x * (1.0 + lax.erf(x * (1.0 / math.sqrt(2.0))))


def _mm(a, b):
    return jnp.dot(a.astype(MXU_DTYPE), b.astype(MXU_DTYPE), preferred_element_type=F32)


def _mm_nt(a, b):
    return lax.dot_general(a.astype(MXU_DTYPE), b.astype(MXU_DTYPE),
                           (((1,), (1,)), ((), ())), preferred_element_type=F32)


def _memkv_kernel(mem_ref, gain_ref, wkv_ref, k_ref, v_ref):
    width = k_ref.shape[-1]
    kv = _mm(_rmsnorm(mem_ref[0], gain_ref[...]), wkv_ref[...])
    k_ref[0] = kv[:, :width].astype(k_ref.dtype)
    v_ref[0] = kv[:, width:].astype(v_ref.dtype)


def _memkv(mem, gain, w_kv):
    B, M, D = mem.shape
    width = w_kv.shape[1] // 2
    const = lambda b: (0, 0)
    return pl.pallas_call(
        _memkv_kernel,
        grid=(B,),
        in_specs=[pl.BlockSpec((1, M, D), lambda b: (b, 0, 0)),
                  pl.BlockSpec((1, D), const),
                  pl.BlockSpec(w_kv.shape, const)],
        out_specs=[pl.BlockSpec((1, M, width), lambda b: (b, 0, 0))] * 2,
        out_shape=[jax.ShapeDtypeStruct((B, M, width), MXU_DTYPE)] * 2,
        compiler_params=pltpu.CompilerParams(dimension_semantics=("arbitrary",),
                                             vmem_limit_bytes=VMEM_LIMIT_BYTES),
    )(mem, gain, w_kv)


def _mixer_kernel(x_ref, k_ref, v_ref, n1g_ref, win_ref, poolw_ref, pools_ref, lng_ref, lnb_ref,
                  ws_ref, bst_ref, sguwo_ref, xawo_ref, wout_ref, h_ref, tail_ref):
    ts, d = x_ref.shape[1], x_ref.shape[2]
    s_idx = pl.program_id(1)
    x = x_ref[0]
    nb = _rmsnorm(x, n1g_ref[...]).astype(MXU_DTYPE)

    def proj(col):
        return jnp.dot(nb, win_ref[:, col * d:(col + 1) * d], preferred_element_type=F32)

    @pl.when(s_idx == 0)
    def _():
        tail_ref[...] = jnp.zeros_like(tail_ref)

    p = proj(0)
    ext = jnp.concatenate([tail_ref[...], p], axis=0)
    tail_ref[...] = p[ts - POOL_HALO:, :]
    pos = s_idx * ts + lax.broadcasted_iota(jnp.int32, (ts, 1), 0)
    group = d // len(POOL_WINDOWS)
    y_pool = []
    for g, w in enumerate(POOL_WINDOWS):
        acc = ext[:, g * group:(g + 1) * group]
        shift = 1
        while shift < w:
            acc = acc + pltpu.roll(acc, shift, 0)
            shift *= 2
        count = jnp.minimum(pos + 1, w).astype(F32)
        diff = acc[POOL_HALO:, :] / count - p[:, g * group:(g + 1) * group]
        y_pool.append(_mm(diff, poolw_ref[g]))
    y_pool = jnp.concatenate(y_pool, axis=1) * pools_ref[...]

    u = _gelu(proj(1))
    v = _gelu(proj(2))
    mu = jnp.mean(v, axis=-1, keepdims=True)
    vc = v - mu
    var = jnp.mean(vc * vc, axis=-1, keepdims=True)
    v = (vc * lax.rsqrt(var + LN_EPS) * lng_ref[...] + lnb_ref[...]).astype(MXU_DTYPE)
    hd = d // SGU_HEADS
    causal = (lax.broadcasted_iota(jnp.int32, (SGU_CHUNK, SGU_CHUNK), 0)
              >= lax.broadcasted_iota(jnp.int32, (SGU_CHUNK, SGU_CHUNK), 1))
    mixed_rows = []
    w_masked = [jnp.where(causal, ws_ref[h], 0.0).astype(MXU_DTYPE) for h in range(SGU_HEADS)]
    for c in range(ts // SGU_CHUNK):
        rows = slice(c * SGU_CHUNK, (c + 1) * SGU_CHUNK)
        heads = []
        for h in range(SGU_HEADS):
            mixed = jnp.dot(w_masked[h], v[rows, h * hd:(h + 1) * hd], preferred_element_type=F32)
            heads.append(mixed + bst_ref[:, h:h + 1])
        mixed_rows.append(jnp.concatenate(heads, axis=1))
    mixed = jnp.concatenate(mixed_rows, axis=0) if len(mixed_rows) > 1 else mixed_rows[0]
    y_sgu = _mm(u * mixed, sguwo_ref[...])

    q = proj(3).astype(MXU_DTYPE)
    xd = d // XA_HEADS
    outs = []
    for h in range(XA_HEADS):
        cols = slice(h * xd, (h + 1) * xd)
        s = _mm_nt(q[:, cols], k_ref[0, :, cols]) * (xd ** -0.5)
        e = jnp.exp(s - jnp.max(s, axis=-1, keepdims=True))
        probs = e / jnp.sum(e, axis=-1, keepdims=True)
        outs.append(_mm(probs, v_ref[0, :, cols]))
    y_xa = _mm(jnp.concatenate(outs, axis=1), xawo_ref[...])

    merged = (jax.nn.sigmoid(proj(4)) * y_pool + jax.nn.sigmoid(proj(5)) * y_sgu
              + jax.nn.sigmoid(proj(6)) * y_xa)
    h_ref[0] = x + _mm(merged, wout_ref[...])


def _resident(shape):
    zeros = (0,) * len(shape)
    return pl.BlockSpec(shape, lambda *_: zeros, pipeline_mode=pl.Buffered(1))


def _mixer(x, k, v, weights, b0, nb):
    _, S, D = x.shape
    ts = min(SEQ_TILE, S)
    M = k.shape[1]
    return pl.pallas_call(
        _mixer_kernel,
        grid=(nb, S // ts),
        in_specs=[pl.BlockSpec((1, ts, D), lambda b, s: (b + b0, s, 0)),
                  pl.BlockSpec((1, M, k.shape[2]), lambda b, s: (b + b0, 0, 0)),
                  pl.BlockSpec((1, M, v.shape[2]), lambda b, s: (b + b0, 0, 0))]
                 + [_resident(w.shape) for w in weights],
        out_specs=pl.BlockSpec((1, ts, D), lambda b, s: (b, s, 0)),
        out_shape=jax.ShapeDtypeStruct((nb, S, D), F32),
        scratch_shapes=[pltpu.VMEM((POOL_HALO, D), F32)],
        compiler_params=pltpu.CompilerParams(dimension_semantics=("arbitrary", "arbitrary"),
                                             vmem_limit_bytes=VMEM_LIMIT_BYTES),
    )(x, k, v, *weights)


def _topk_rows(vals, payload=None):
    n_rows, width = vals.shape
    sub = TC_SUBLANES
    n_tiles = n_rows // sub
    assert n_rows % sub == 0
    tiles = [vals[j * sub:(j + 1) * sub, :] for j in range(n_tiles)]
    pay = None if payload is None else [payload[j * sub:(j + 1) * sub, :] for j in range(n_tiles)]
    sub_row = lax.broadcasted_iota(jnp.int32, (sub, width), 0).astype(F32)
    rows = [sub_row + float(j * sub) for j in range(n_tiles)]

    def best_of(items):
        while len(items) > 1:
            merged = []
            for a, b in zip(items[0::2], items[1::2]):
                keep_a = a[0] >= b[0]
                merged.append(tuple(jnp.where(keep_a, xa, xb) for xa, xb in zip(a, b)))
            if len(items) % 2:
                merged.append(items[-1])
            items = merged
        return items[0]

    out_v, out_i = [], []
    for _ in range(PEER_TOPK):
        if pay is None:
            v8, j8 = best_of([(tiles[j], float(j)) for j in range(n_tiles)])
        else:
            v8, j8, p8 = best_of([(tiles[j], float(j), pay[j]) for j in range(n_tiles)])
        r8 = j8 * float(sub) + sub_row
        m = jnp.max(v8, axis=0, keepdims=True)
        first = jnp.min(jnp.where(v8 == m, r8, float(n_rows)), axis=0, keepdims=True)
        out_v.append(m)
        if pay is None:
            out_i.append(first)
        else:
            out_i.append(jnp.max(jnp.where(r8 == first, p8, -1.0), axis=0, keepdims=True))
        tiles = [jnp.where(rows[j] == first, -jnp.inf, tiles[j]) for j in range(n_tiles)]
    return jnp.concatenate(out_v, axis=0), jnp.concatenate(out_i, axis=0)


def _candidates(v1, i1, v2, i2):
    sub = 8
    jrow = lax.broadcasted_iota(jnp.int32, (sub, 1), 0)
    vals = [v1[0:1] + v2]
    ids = [i1[0:1] * float(PEER_N_KEYS) + i2]
    for i in range(1, sub):
        keep = jrow < (PEER_TOPK // (i + 1))
        vals.append(jnp.where(keep, v1[i:i + 1] + v2[0:sub], -jnp.inf))
        ids.append(i1[i:i + 1] * float(PEER_N_KEYS) + i2[0:sub])
    vals.append(v1[sub:] + v2[0:1])
    ids.append(i1[sub:] * float(PEER_N_KEYS) + i2[0:1])
    return jnp.concatenate(vals, axis=0), jnp.concatenate(ids, axis=0)


def _select_kernel(h_ref, n2g_ref, wq_ref, k1_ref, k2_ref, xn_ref, exp_ref, gate_ref):
    xn = _rmsnorm(h_ref[...], n2g_ref[...])
    xn_ref[...] = _pack_halves(xn)
    q = _mm(xn, wq_ref[...])
    half = k1_ref.shape[1]
    experts, gates = [], []
    for h in range(PEER_HEADS):
        q1 = q[:, (2 * h) * half:(2 * h + 1) * half]
        q2 = q[:, (2 * h + 1) * half:(2 * h + 2) * half]
        v1, i1 = _topk_rows(_mm_nt(k1_ref[...], q1))
        v2, i2 = _topk_rows(_mm_nt(k2_ref[...], q2))
        best, eid = _topk_rows(*_candidates(v1, i1, v2, i2))
        e = jnp.exp(best - jnp.max(best, axis=0, keepdims=True))
        gates.append(e / jnp.sum(e, axis=0, keepdims=True))
        experts.append(eid)
    exp_ref[...] = jnp.concatenate(experts, axis=0).T.astype(jnp.int32)
    gate_ref[...] = jnp.concatenate(gates, axis=0).T


def _select(h, n2g, w_q, keys1, keys2):
    T, D = h.shape
    tt = min(SELECT_TILE, T)
    hk = PEER_HEADS * PEER_TOPK
    return pl.pallas_call(
        _select_kernel,
        grid=(T // tt,),
        in_specs=[pl.BlockSpec((tt, D), lambda i: (i, 0)),
                  _resident(n2g.shape), _resident(w_q.shape),
                  _resident(keys1.shape), _resident(keys2.shape)],
        out_specs=[pl.BlockSpec((tt, D // 2), lambda i: (i, 0)),
                   pl.BlockSpec((tt, hk), lambda i: (i, 0)),
                   pl.BlockSpec((tt, hk), lambda i: (i, 0))],
        out_shape=[jax.ShapeDtypeStruct((T, D // 2), I32),
                   jax.ShapeDtypeStruct((T, hk), jnp.int32),
                   jax.ShapeDtypeStruct((T, hk), F32)],
        compiler_params=pltpu.CompilerParams(dimension_semantics=("arbitrary",),
                                             vmem_limit_bytes=VMEM_LIMIT_BYTES),
    )(h, n2g, w_q, keys1, keys2)


def _pack_table(table):
    half = table.shape[1] // 2
    bits = lax.bitcast_convert_type(table.astype(jnp.bfloat16), jnp.uint16).astype(jnp.uint32)
    return lax.bitcast_convert_type((bits[:, half:] << 16) | bits[:, :half], I32)


def _bf16_bits(x):
    return lax.bitcast_convert_type(x.astype(jnp.bfloat16).astype(F32), I32)


def _pack_halves(x):
    half = x.shape[1] // 2
    return (_bf16_bits(x[:, half:]) & HIGH_HALF) | lax.shift_right_logical(_bf16_bits(x[:, :half]), 16)


def _pack_splat(x):
    bits = _bf16_bits(x)
    return (bits & HIGH_HALF) | lax.shift_right_logical(bits, 16)


def _unpack_words(w):
    lo = lax.bitcast_convert_type(lax.shift_left(w, jnp.full(w.shape, 16, I32)), F32)
    hi = lax.bitcast_convert_type(lax.bitwise_and(w, jnp.full(w.shape, HIGH_HALF, I32)), F32)
    return lo, hi


def _as_bf16(w):
    return plsc.bitcast(w, jnp.bfloat16)


def _halves_f32(p):
    return _unpack_words(plsc.bitcast(p, I32))


def _sc_kernel(body, out_type, scratch_types):
    mesh = plsc.VectorSubcoreMesh(core_axis_name="c", subcore_axis_name="s")
    return pl.kernel(body, out_type=out_type, mesh=mesh, scratch_types=scratch_types,
                     compiler_params=pltpu.CompilerParams(needs_layout_passes=False))


def _sc_worker_base(tokens_per_worker):
    return (lax.axis_index("s") * SC_CORES + lax.axis_index("c")) * tokens_per_worker


def _sc_pipeline(nchunk, gather, compute):
    ahead = SC_NBUF - 1
    for i in range(ahead):
        gather(i, i).start()

    @pl.loop(0, nchunk)
    def _(ci):
        @pl.when(ci + ahead < nchunk)
        def _():
            gather(ci + ahead, (ci + ahead) % SC_NBUF).start()

        slot = ci % SC_NBUF
        gather(ci, slot).wait()
        compute(ci, slot)


def _sc_dots(xp, experts, ptab):
    T, W = xp.shape
    HK = experts.shape[1]
    L, TG, CH = SC_LANES, SC_TOKENS, SC_CHUNK
    tpw = T // SC_WORKERS
    cpt = HK // CH
    nchunk = TG * cpt
    assert W == ptab.shape[1] and T % (SC_WORKERS * TG) == 0 and HK % CH == 0 and CH % L == 0
    assert W % (2 * L) == 0 and nchunk >= SC_NBUF

    def body(x_hbm, idx_hbm, tab_hbm, out_hbm, x_v, idx_v, out_v, rows_v, acc_v, sem):
        base = _sc_worker_base(tpw)
        lane = lax.iota(I32, L)

        def gather(ci, slot):
            tok, c = ci // cpt, ci % cpt
            return pltpu.make_async_copy(tab_hbm.at[idx_v.at[tok, pl.ds(c * CH, CH)]],
                                         rows_v.at[slot], sem.at[slot])

        def compute(ci, slot):
            tok, c = ci // cpt, ci % cpt
            for eg in range(CH // L):
                accs = [jnp.zeros((L,), F32) for _ in range(L)]
                for j in range(0, W // L, 2):
                    xa = _as_bf16(x_v[tok, pl.ds(j * L, L)])
                    xb = _as_bf16(x_v[tok, pl.ds((j + 1) * L, L)])
                    for kk in range(L):
                        ra = _as_bf16(rows_v[slot, eg * L + kk, pl.ds(j * L, L)])
                        rb = _as_bf16(rows_v[slot, eg * L + kk, pl.ds((j + 1) * L, L)])
                        lo, hi = _halves_f32(ra * xa + rb * xb)
                        accs[kk] = accs[kk] + lo + hi
                for kk in range(L):
                    acc_v[kk, :] = accs[kk]

                s = jnp.zeros((L,), F32)
                for j in range(L):
                    s = s + plsc.load_gather(acc_v, [lane, jnp.full((L,), j, I32)])
                out_v[tok, pl.ds(c * CH + eg * L, L)] = s

        @pl.loop(0, tpw // TG)
        def _(g):
            t0 = base + g * TG
            pltpu.sync_copy(x_hbm.at[pl.ds(t0, TG)], x_v)
            pltpu.sync_copy(idx_hbm.at[pl.ds(t0, TG)], idx_v)
            _sc_pipeline(nchunk, gather, compute)
            pltpu.sync_copy(out_v, out_hbm.at[pl.ds(t0, TG)])

    return _sc_kernel(
        body, jax.ShapeDtypeStruct((T, HK), F32),
        [pltpu.VMEM((TG, W), I32), pltpu.VMEM((TG, HK), I32), pltpu.VMEM((TG, HK), F32),
         pltpu.VMEM((SC_NBUF, CH, W), I32), pltpu.VMEM((L, L), F32),
         pltpu.SemaphoreType.DMA((SC_NBUF,))],
    )(xp, experts, ptab)


def _sc_wsum(wp, experts, ptab):
    T, HK = wp.shape
    W = ptab.shape[1]
    D = 2 * W
    L, TG, CH = SC_LANES, SC_WSUM_TOKENS, SC_CHUNK
    tpw = T // SC_WORKERS
    cpt = HK // CH
    nchunk = TG * cpt
    DG = 8
    assert T % (SC_WORKERS * TG) == 0 and HK % CH == 0 and CH % 2 == 0 and W % (DG * L) == 0
    assert nchunk >= SC_NBUF

    def body(w_hbm, idx_hbm, tab_hbm, out_hbm, w_v, idx_v, y_v, rows_v, sem):
        base = _sc_worker_base(tpw)

        def gather(ci, slot):
            tok, c = ci // cpt, ci % cpt
            return pltpu.make_async_copy(tab_hbm.at[idx_v.at[tok, pl.ds(c * CH, CH)]],
                                         rows_v.at[slot], sem.at[slot])

        def compute(ci, slot):
            tok, c = ci // cpt, ci % cpt
            tokv = jnp.full((L,), tok, I32)
            cont = jnp.full((L,), c, I32) != 0

            @pl.loop(0, W // (DG * L))
            def _(dg):
                off = dg * (DG * L)
                zero = jnp.zeros((L,), F32)
                alo = [jnp.where(cont, y_v[tok, pl.ds(off + j * L, L)], zero) for j in range(DG)]
                ahi = [jnp.where(cont, y_v[tok, pl.ds(W + off + j * L, L)], zero) for j in range(DG)]
                for kk in range(0, CH, 2):
                    wa = _as_bf16(plsc.load_gather(w_v, [tokv, jnp.full((L,), c * CH + kk, I32)]))
                    wb = _as_bf16(plsc.load_gather(w_v, [tokv, jnp.full((L,), c * CH + kk + 1, I32)]))
                    for j in range(DG):
                        ra = _as_bf16(rows_v[slot, kk, pl.ds(off + j * L, L)])
                        rb = _as_bf16(rows_v[slot, kk + 1, pl.ds(off + j * L, L)])
                        lo, hi = _halves_f32(wa * ra + wb * rb)
                        alo[j] = alo[j] + lo
                        ahi[j] = ahi[j] + hi
                for j in range(DG):
                    y_v[tok, pl.ds(off + j * L, L)] = alo[j]
                    y_v[tok, pl.ds(W + off + j * L, L)] = ahi[j]

        @pl.loop(0, tpw // TG)
        def _(g):
            t0 = base + g * TG
            pltpu.sync_copy(w_hbm.at[pl.ds(t0, TG)], w_v)
            pltpu.sync_copy(idx_hbm.at[pl.ds(t0, TG)], idx_v)
            _sc_pipeline(nchunk, gather, compute)
            pltpu.sync_copy(y_v, out_hbm.at[pl.ds(t0, TG)])

    return _sc_kernel(
        body, jax.ShapeDtypeStruct((T, D), F32),
        [pltpu.VMEM((TG, HK), I32), pltpu.VMEM((TG, HK), I32), pltpu.VMEM((TG, D), F32),
         pltpu.VMEM((SC_NBUF, CH, W), I32), pltpu.SemaphoreType.DMA((SC_NBUF,))],
    )(wp, experts, ptab)


def _tc_wsum_kernel(exp_ref, w_ref, tab_ref, y_ref):
    tb, hk = exp_ref.shape
    sub, lanes = tab_ref.shape[1:]
    for t in range(tb):
        lo_acc = jnp.zeros((sub, lanes), F32)
        hi_acc = jnp.zeros((sub, lanes), F32)
        for k in range(hk):
            row = tab_ref[exp_ref[t, k]]
            w = w_ref[t, k]
            lo_acc = lo_acc + w * lax.bitcast_convert_type(lax.shift_left(row, 16), F32)
            hi_acc = hi_acc + w * lax.bitcast_convert_type(row & HIGH_HALF, F32)
        y_ref[t, pl.ds(0, sub), :] = lo_acc
        y_ref[t, pl.ds(sub, sub), :] = hi_acc


def _tc_wsum(w, experts, ptab):
    T, hk = w.shape
    E, W = ptab.shape
    sub = W // TC_LANES
    tb = min(TC_WSUM_TILE, T)
    smem = pl.BlockSpec((tb, hk), lambda i: (i, 0), memory_space=pltpu.SMEM)
    y = pl.pallas_call(
        _tc_wsum_kernel, grid=(T // tb,),
        in_specs=[smem, smem, _resident((E, sub, TC_LANES))],
        out_specs=pl.BlockSpec((tb, 2 * sub, TC_LANES), lambda i: (i, 0, 0)),
        out_shape=jax.ShapeDtypeStruct((T, 2 * sub, TC_LANES), F32),
        cost_estimate=pl.CostEstimate(flops=4 * T * hk * W, transcendentals=0,
                                      bytes_accessed=4 * (T * hk * W + E * W + 2 * T * W + 2 * T * hk)),
        compiler_params=pltpu.CompilerParams(dimension_semantics=("arbitrary",),
                                             vmem_limit_bytes=VMEM_LIMIT_BYTES),
    )(experts, w, ptab.reshape(E, sub, TC_LANES))
    return y.reshape(T, 2 * W)


def _act_kernel(dots_ref, gate_ref, w_ref, wp_ref):
    w = _gelu(dots_ref[...]) * gate_ref[...]
    w_ref[...] = w
    wp_ref[...] = _pack_splat(w)


def _act(dots, gates):
    T, hk = dots.shape
    tt = min(ROW_TILE, T)
    spec = pl.BlockSpec((tt, hk), lambda i: (i, 0))
    return pl.pallas_call(
        _act_kernel, grid=(T // tt,), in_specs=[spec, spec], out_specs=[spec, spec],
        out_shape=[jax.ShapeDtypeStruct((T, hk), F32), jax.ShapeDtypeStruct((T, hk), I32)],
        compiler_params=pltpu.CompilerParams(dimension_semantics=("arbitrary",)),
    )(dots, gates)


def _final_kernel(h_ref, y_ref, g_ref, *rest):
    o_ref = rest[-1]
    o_ref[...] = _rmsnorm(h_ref[...] + y_ref[...], g_ref[...])


def _final(h, y, gain, out, row0, total_rows):
    T, D = h.shape
    tt = min(ROW_TILE, T)
    assert row0 % tt == 0
    spec = pl.BlockSpec((tt, D), lambda i: (i, 0))
    out_spec = pl.BlockSpec((tt, D), lambda i: (i + row0 // tt, 0))
    in_specs = [spec, spec, pl.BlockSpec((1, D), lambda i: (0, 0))]
    args = (h, y, gain)
    aliases = {}
    if out is not None:
        in_specs.append(pl.BlockSpec(memory_space=pl.ANY))
        args += (out,)
        aliases = {3: 0}
    return pl.pallas_call(
        _final_kernel, grid=(T // tt,), in_specs=in_specs, out_specs=out_spec,
        out_shape=jax.ShapeDtypeStruct((total_rows, D), F32),
        input_output_aliases=aliases,
        compiler_params=pltpu.CompilerParams(dimension_semantics=("arbitrary",)),
    )(*args)


def kernel(x, mem, norm1_gain, w_in, pool_w, pool_scale, sgu_ln_gain, sgu_ln_bias, sgu_w_s, sgu_b_s,
           sgu_w_out, mem_norm_gain, xa_w_kv, xa_w_out, w_out, norm2_gain, peer_w_q, peer_keys1,
           peer_keys2, peer_u, peer_v, final_norm_gain):
    B, S, D = x.shape
    depth = w_in.shape[0]
    lo = lambda w: w.astype(MXU_DTYPE)
    row = lambda w: w.reshape(1, -1)
    assert depth == 1, "the final RMSNorm is fused with the last layer's residual add"
    l = 0
    weights = (row(norm1_gain[l]), lo(w_in[l]), lo(pool_w[l]), row(pool_scale[l]),
               row(sgu_ln_gain[l]), row(sgu_ln_bias[l]), sgu_w_s[l], sgu_b_s[l].T,
               lo(sgu_w_out[l]), lo(xa_w_out[l]), lo(w_out[l]))
    select_w = (row(norm2_gain[l]), lo(peer_w_q[l]), lo(peer_keys1[l]), lo(peer_keys2[l]))
    u_packed, v_packed = _pack_table(peer_u[l]), _pack_table(peer_v[l])
    k, v = _memkv(mem, row(mem_norm_gain[l]), lo(xa_w_kv[l]))
    fgain = row(final_norm_gain)
    chunks = BATCH_CHUNKS if sum(BATCH_CHUNKS) == B else (B,)
    b0, tc_sums, tc_parts, sc_parts = 0, [], [], []
    for c, nb in enumerate(chunks):
        rows = nb * S
        h = _mixer(x, k, v, weights, b0, nb).reshape(rows, D)
        xn, experts, gates = _select(h, *select_w)
        dots = _sc_dots(xn, experts, u_packed)
        if c > 0:
            dots, tc_sums[c - 1] = lax.optimization_barrier((dots, tc_sums[c - 1]))
        w, wp = _act(dots, gates)
        num, den = TC_WSUM_SHARE.get(nb, TC_WSUM_SHARE_DEFAULT)
        n_tc = (rows * num // den) // SC_ROW_QUANTUM * SC_ROW_QUANTUM
        tc_sums.append(_tc_wsum(w[:n_tc], experts[:n_tc], v_packed))
        tc_parts.append((h[:n_tc], b0 * S))
        sc_parts.append((h[n_tc:], _sc_wsum(wp[n_tc:], experts[n_tc:], v_packed), b0 * S + n_tc))
        b0 += nb
    parts = [(h_tc, y_tc, row0) for (h_tc, row0), y_tc in zip(tc_parts, tc_sums)]
    order = parts[:-1] + sc_parts[:-1] + parts[-1:] + sc_parts[-1:]
    out = None
    for h_part, y_part, row0 in order:
        out = _final(h_part, y_part, fgain, out, row0, B * S)
    return out.reshape(B, S, D)
```

```python
import functools
import math

import jax
import jax.numpy as jnp
from jax import lax
from jax.experimental import pallas as pl
from jax.experimental.pallas import tpu as pltpu
from jax.experimental.pallas import tpu_sc as plsc

F32 = jnp.float32
I32 = jnp.int32
MXU_DTYPE = jnp.bfloat16

RMS_EPS = 1e-6
LN_EPS = 1e-5
POOL_WINDOWS = (2, 4, 8, 16)
POOL_HALO = 16
SGU_CHUNK = 128
SGU_HEADS = 8
XA_HEADS = 4
PEER_HEADS = 8
PEER_N_KEYS = 128
PEER_TOPK = 16

V7X_VMEM_BYTES = 64 * 1024 * 1024
VMEM_LIMIT_BYTES = 56 * 1024 * 1024

SEQ_TILE = 256
SELECT_TILE = 256
ROW_TILE = 512
TC_WSUM_TILE = 8
TC_LANES = 128
TC_SUBLANES = 8
BATCH_CHUNKS = (1, 1, 3, 3)

SC_CORES = 2
SC_SUBCORES = 16
SC_LANES = 16
SC_WORKERS = SC_CORES * SC_SUBCORES
SC_TOKENS = 32
SC_WSUM_TOKENS = 16
SC_CHUNK = 32
SC_NBUF = 4
HIGH_HALF = -65536
SC_ROW_QUANTUM = SC_WORKERS * SC_WSUM_TOKENS
TC_WSUM_SHARE = (9, 16)


def _rmsnorm(x, gain):
    return x * lax.rsqrt(jnp.mean(x * x, axis=-1, keepdims=True) + RMS_EPS) * gain


def _gelu(x):
    return 0.5 * x * (1.0 + lax.erf(x * (1.0 / math.sqrt(2.0))))


def _mm(a, b):
    return jnp.dot(a.astype(MXU_DTYPE), b.astype(MXU_DTYPE), preferred_element_type=F32)


def _mm_nt(a, b):
    return lax.dot_general(a.astype(MXU_DTYPE), b.astype(MXU_DTYPE),
                           (((1,), (1,)), ((), ())), preferred_element_type=F32)


def _memkv_kernel(mem_ref, gain_ref, wkv_ref, k_ref, v_ref):
    width = k_ref.shape[-1]
    kv = _mm(_rmsnorm(mem_ref[0], gain_ref[...]), wkv_ref[...])
    k_ref[0] = kv[:, :width].astype(k_ref.dtype)
    v_ref[0] = kv[:, width:].astype(v_ref.dtype)


def _memkv(mem, gain, w_kv):
    B, M, D = mem.shape
    width = w_kv.shape[1] // 2
    const = lambda b: (0, 0)
    return pl.pallas_call(
        _memkv_kernel,
        grid=(B,),
        in_specs=[pl.BlockSpec((1, M, D), lambda b: (b, 0, 0)),
                  pl.BlockSpec((1, D), const),
                  pl.BlockSpec(w_kv.shape, const)],
        out_specs=[pl.BlockSpec((1, M, width), lambda b: (b, 0, 0))] * 2,
        out_shape=[jax.ShapeDtypeStruct((B, M, width), MXU_DTYPE)] * 2,
        compiler_params=pltpu.CompilerParams(dimension_semantics=("arbitrary",),
                                             vmem_limit_bytes=VMEM_LIMIT_BYTES),
    )(mem, gain, w_kv)


def _mixer_kernel(x_ref, k_ref, v_ref, n1g_ref, win_ref, poolw_ref, pools_ref, lng_ref, lnb_ref,
                  ws_ref, bst_ref, sguwo_ref, xawo_ref, wout_ref, h_ref, tail_ref):
    ts, d = x_ref.shape[1], x_ref.shape[2]
    s_idx = pl.program_id(1)
    x = x_ref[0]
    nb = _rmsnorm(x, n1g_ref[...]).astype(MXU_DTYPE)

    def proj(col):
        return jnp.dot(nb, win_ref[:, col * d:(col + 1) * d], preferred_element_type=F32)

    @pl.when(s_idx == 0)
    def _():
        tail_ref[...] = jnp.zeros_like(tail_ref)

    p = proj(0)
    ext = jnp.concatenate([tail_ref[...], p], axis=0)
    tail_ref[...] = p[ts - POOL_HALO:, :]
    pos = s_idx * ts + lax.broadcasted_iota(jnp.int32, (ts, 1), 0)
    group = d // len(POOL_WINDOWS)
    y_pool = []
    for g, w in enumerate(POOL_WINDOWS):
        acc = ext[:, g * group:(g + 1) * group]
        shift = 1
        while shift < w:
            acc = acc + pltpu.roll(acc, shift, 0)
            shift *= 2
        count = jnp.minimum(pos + 1, w).astype(F32)
        diff = acc[POOL_HALO:, :] / count - p[:, g * group:(g + 1) * group]
        y_pool.append(_mm(diff, poolw_ref[g]))
    y_pool = jnp.concatenate(y_pool, axis=1) * pools_ref[...]

    u = _gelu(proj(1))
    v = _gelu(proj(2))
    mu = jnp.mean(v, axis=-1, keepdims=True)
    vc = v - mu
    var = jnp.mean(vc * vc, axis=-1, keepdims=True)
    v = (vc * lax.rsqrt(var + LN_EPS) * lng_ref[...] + lnb_ref[...]).astype(MXU_DTYPE)
    hd = d // SGU_HEADS
    causal = (lax.broadcasted_iota(jnp.int32, (SGU_CHUNK, SGU_CHUNK), 0)
              >= lax.broadcasted_iota(jnp.int32, (SGU_CHUNK, SGU_CHUNK), 1))
    mixed_rows = []
    w_masked = [jnp.where(causal, ws_ref[h], 0.0).astype(MXU_DTYPE) for h in range(SGU_HEADS)]
    for c in range(ts // SGU_CHUNK):
        rows = slice(c * SGU_CHUNK, (c + 1) * SGU_CHUNK)
        heads = []
        for h in range(SGU_HEADS):
            mixed = jnp.dot(w_masked[h], v[rows, h * hd:(h + 1) * hd], preferred_element_type=F32)
            heads.append(mixed + bst_ref[:, h:h + 1])
        mixed_rows.append(jnp.concatenate(heads, axis=1))
    mixed = jnp.concatenate(mixed_rows, axis=0) if len(mixed_rows) > 1 else mixed_rows[0]
    y_sgu = _mm(u * mixed, sguwo_ref[...])

    q = proj(3).astype(MXU_DTYPE)
    xd = d // XA_HEADS
    outs = []
    for h in range(XA_HEADS):
        cols = slice(h * xd, (h + 1) * xd)
        s = _mm_nt(q[:, cols], k_ref[0, :, cols]) * (xd ** -0.5)
        e = jnp.exp(s - jnp.max(s, axis=-1, keepdims=True))
        probs = e / jnp.sum(e, axis=-1, keepdims=True)
        outs.append(_mm(probs, v_ref[0, :, cols]))
    y_xa = _mm(jnp.concatenate(outs, axis=1), xawo_ref[...])

    merged = (jax.nn.sigmoid(proj(4)) * y_pool + jax.nn.sigmoid(proj(5)) * y_sgu
              + jax.nn.sigmoid(proj(6)) * y_xa)
    h_ref[0] = x + _mm(merged, wout_ref[...])


def _resident(shape):
    zeros = (0,) * len(shape)
    return pl.BlockSpec(shape, lambda *_: zeros, pipeline_mode=pl.Buffered(1))


def _mixer(x, k, v, weights, b0, nb):
    _, S, D = x.shape
    ts = min(SEQ_TILE, S)
    M = k.shape[1]
    return pl.pallas_call(
        _mixer_kernel,
        grid=(nb, S // ts),
        in_specs=[pl.BlockSpec((1, ts, D), lambda b, s: (b + b0, s, 0)),
                  pl.BlockSpec((1, M, k.shape[2]), lambda b, s: (b + b0, 0, 0)),
                  pl.BlockSpec((1, M, v.shape[2]), lambda b, s: (b + b0, 0, 0))]
                 + [_resident(w.shape) for w in weights],
        out_specs=pl.BlockSpec((1, ts, D), lambda b, s: (b, s, 0)),
        out_shape=jax.ShapeDtypeStruct((nb, S, D), F32),
        scratch_shapes=[pltpu.VMEM((POOL_HALO, D), F32)],
        compiler_params=pltpu.CompilerParams(dimension_semantics=("arbitrary", "arbitrary"),
                                             vmem_limit_bytes=VMEM_LIMIT_BYTES),
    )(x, k, v, *weights)


def _topk_rows(vals, payload=None):
    n_rows, width = vals.shape
    sub = TC_SUBLANES
    n_tiles = n_rows // sub
    assert n_rows % sub == 0
    tiles = [vals[j * sub:(j + 1) * sub, :] for j in range(n_tiles)]
    pay = None if payload is None else [payload[j * sub:(j + 1) * sub, :] for j in range(n_tiles)]
    sub_row = lax.broadcasted_iota(jnp.int32, (sub, width), 0).astype(F32)
    rows = [sub_row + float(j * sub) for j in range(n_tiles)]

    def best_of(items):
        while len(items) > 1:
            merged = []
            for a, b in zip(items[0::2], items[1::2]):
                keep_a = a[0] >= b[0]
                merged.append(tuple(jnp.where(keep_a, xa, xb) for xa, xb in zip(a, b)))
            if len(items) % 2:
                merged.append(items[-1])
            items = merged
        return items[0]

    out_v, out_i = [], []
    for _ in range(PEER_TOPK):
        if pay is None:
            v8, j8 = best_of([(tiles[j], float(j)) for j in range(n_tiles)])
        else:
            v8, j8, p8 = best_of([(tiles[j], float(j), pay[j]) for j in range(n_tiles)])
        r8 = j8 * float(sub) + sub_row
        m = jnp.max(v8, axis=0, keepdims=True)
        first = jnp.min(jnp.where(v8 == m, r8, float(n_rows)), axis=0, keepdims=True)
        out_v.append(m)
        if pay is None:
            out_i.append(first)
        else:
            out_i.append(jnp.max(jnp.where(r8 == first, p8, -1.0), axis=0, keepdims=True))
        tiles = [jnp.where(rows[j] == first, -jnp.inf, tiles[j]) for j in range(n_tiles)]
    return jnp.concatenate(out_v, axis=0), jnp.concatenate(out_i, axis=0)


def _candidates(v1, i1, v2, i2):
    sub = 8
    jrow = lax.broadcasted_iota(jnp.int32, (sub, 1), 0)
    vals = [v1[0:1] + v2]
    ids = [i1[0:1] * float(PEER_N_KEYS) + i2]
    for i in range(1, sub):
        keep = jrow < (PEER_TOPK // (i + 1))
        vals.append(jnp.where(keep, v1[i:i + 1] + v2[0:sub], -jnp.inf))
        ids.append(i1[i:i + 1] * float(PEER_N_KEYS) + i2[0:sub])
    vals.append(v1[sub:] + v2[0:1])
    ids.append(i1[sub:] * float(PEER_N_KEYS) + i2[0:1])
    return jnp.concatenate(vals, axis=0), jnp.concatenate(ids, axis=0)


def _select_kernel(h_ref, n2g_ref, wq_ref, k1_ref, k2_ref, xn_ref, exp_ref, gate_ref):
    xn = _rmsnorm(h_ref[...], n2g_ref[...])
    xn_ref[...] = _pack_halves(xn)
    q = _mm(xn, wq_ref[...])
    half = k1_ref.shape[1]
    experts, gates = [], []
    for h in range(PEER_HEADS):
        q1 = q[:, (2 * h) * half:(2 * h + 1) * half]
        q2 = q[:, (2 * h + 1) * half:(2 * h + 2) * half]
        v1, i1 = _topk_rows(_mm_nt(k1_ref[...], q1))
        v2, i2 = _topk_rows(_mm_nt(k2_ref[...], q2))
        best, eid = _topk_rows(*_candidates(v1, i1, v2, i2))
        e = jnp.exp(best - jnp.max(best, axis=0, keepdims=True))
        gates.append(e / jnp.sum(e, axis=0, keepdims=True))
        experts.append(eid)
    exp_ref[...] = jnp.concatenate(experts, axis=0).T.astype(jnp.int32)
    gate_ref[...] = jnp.concatenate(gates, axis=0).T


def _select(h, n2g, w_q, keys1, keys2):
    T, D = h.shape
    tt = min(SELECT_TILE, T)
    hk = PEER_HEADS * PEER_TOPK
    return pl.pallas_call(
        _select_kernel,
        grid=(T // tt,),
        in_specs=[pl.BlockSpec((tt, D), lambda i: (i, 0)),
                  _resident(n2g.shape), _resident(w_q.shape),
                  _resident(keys1.shape), _resident(keys2.shape)],
        out_specs=[pl.BlockSpec((tt, D // 2), lambda i: (i, 0)),
                   pl.BlockSpec((tt, hk), lambda i: (i, 0)),
                   pl.BlockSpec((tt, hk), lambda i: (i, 0))],
        out_shape=[jax.ShapeDtypeStruct((T, D // 2), I32),
                   jax.ShapeDtypeStruct((T, hk), jnp.int32),
                   jax.ShapeDtypeStruct((T, hk), F32)],
        compiler_params=pltpu.CompilerParams(dimension_semantics=("arbitrary",),
                                             vmem_limit_bytes=VMEM_LIMIT_BYTES),
    )(h, n2g, w_q, keys1, keys2)


def _pack_table(table):
    half = table.shape[1] // 2
    bits = lax.bitcast_convert_type(table.astype(jnp.bfloat16), jnp.uint16).astype(jnp.uint32)
    return lax.bitcast_convert_type((bits[:, half:] << 16) | bits[:, :half], I32)


def _bf16_bits(x):
    return lax.bitcast_convert_type(x.astype(jnp.bfloat16).astype(F32), I32)


def _pack_halves(x):
    half = x.shape[1] // 2
    return (_bf16_bits(x[:, half:]) & HIGH_HALF) | lax.shift_right_logical(_bf16_bits(x[:, :half]), 16)


def _pack_splat(x):
    bits = _bf16_bits(x)
    return (bits & HIGH_HALF) | lax.shift_right_logical(bits, 16)


def _unpack_words(w):
    lo = lax.bitcast_convert_type(lax.shift_left(w, jnp.full(w.shape, 16, I32)), F32)
    hi = lax.bitcast_convert_type(lax.bitwise_and(w, jnp.full(w.shape, HIGH_HALF, I32)), F32)
    return lo, hi


def _as_bf16(w):
    return plsc.bitcast(w, jnp.bfloat16)


def _halves_f32(p):
    return _unpack_words(plsc.bitcast(p, I32))


def _sc_kernel(body, out_type, scratch_types):
    mesh = plsc.VectorSubcoreMesh(core_axis_name="c", subcore_axis_name="s")
    return pl.kernel(body, out_type=out_type, mesh=mesh, scratch_types=scratch_types,
                     compiler_params=pltpu.CompilerParams(needs_layout_passes=False))


def _sc_worker_base(tokens_per_worker):
    return (lax.axis_index("s") * SC_CORES + lax.axis_index("c")) * tokens_per_worker


def _sc_pipeline(nchunk, gather, compute):
    ahead = SC_NBUF - 1
    for i in range(ahead):
        gather(i, i).start()

    @pl.loop(0, nchunk)
    def _(ci):
        @pl.when(ci + ahead < nchunk)
        def _():
            gather(ci + ahead, (ci + ahead) % SC_NBUF).start()

        slot = ci % SC_NBUF
        gather(ci, slot).wait()
        compute(ci, slot)


def _sc_dots(xp, experts, ptab):
    T, W = xp.shape
    HK = experts.shape[1]
    L, TG, CH = SC_LANES, SC_TOKENS, SC_CHUNK
    tpw = T // SC_WORKERS
    cpt = HK // CH
    nchunk = TG * cpt
    assert W == ptab.shape[1] and T % (SC_WORKERS * TG) == 0 and HK % CH == 0 and CH % L == 0
    assert W % (2 * L) == 0 and nchunk >= SC_NBUF

    def body(x_hbm, idx_hbm, tab_hbm, out_hbm, x_v, idx_v, out_v, rows_v, acc_v, sem):
        base = _sc_worker_base(tpw)
        lane = lax.iota(I32, L)

        def gather(ci, slot):
            tok, c = ci // cpt, ci % cpt
            return pltpu.make_async_copy(tab_hbm.at[idx_v.at[tok, pl.ds(c * CH, CH)]],
                                         rows_v.at[slot], sem.at[slot])

        def compute(ci, slot):
            tok, c = ci // cpt, ci % cpt
            for eg in range(CH // L):
                accs = [jnp.zeros((L,), F32) for _ in range(L)]
                for j in range(0, W // L, 2):
                    xa = _as_bf16(x_v[tok, pl.ds(j * L, L)])
                    xb = _as_bf16(x_v[tok, pl.ds((j + 1) * L, L)])
                    for kk in range(L):
                        ra = _as_bf16(rows_v[slot, eg * L + kk, pl.ds(j * L, L)])
                        rb = _as_bf16(rows_v[slot, eg * L + kk, pl.ds((j + 1) * L, L)])
                        lo, hi = _halves_f32(ra * xa + rb * xb)
                        accs[kk] = accs[kk] + lo + hi
                for kk in range(L):
                    acc_v[kk, :] = accs[kk]

                s = jnp.zeros((L,), F32)
                for j in range(L):
                    s = s + plsc.load_gather(acc_v, [lane, jnp.full((L,), j, I32)])
                out_v[tok, pl.ds(c * CH + eg * L, L)] = s

        @pl.loop(0, tpw // TG)
        def _(g):
            t0 = base + g * TG
            pltpu.sync_copy(x_hbm.at[pl.ds(t0, TG)], x_v)
            pltpu.sync_copy(idx_hbm.at[pl.ds(t0, TG)], idx_v)
            _sc_pipeline(nchunk, gather, compute)
            pltpu.sync_copy(out_v, out_hbm.at[pl.ds(t0, TG)])

    return _sc_kernel(
        body, jax.ShapeDtypeStruct((T, HK), F32),
        [pltpu.VMEM((TG, W), I32), pltpu.VMEM((TG, HK), I32), pltpu.VMEM((TG, HK), F32),
         pltpu.VMEM((SC_NBUF, CH, W), I32), pltpu.VMEM((L, L), F32),
         pltpu.SemaphoreType.DMA((SC_NBUF,))],
    )(xp, experts, ptab)


def _sc_wsum(wp, experts, ptab):
    T, HK = wp.shape
    W = ptab.shape[1]
    D = 2 * W
    L, TG, CH = SC_LANES, SC_WSUM_TOKENS, SC_CHUNK
    tpw = T // SC_WORKERS
    cpt = HK // CH
    nchunk = TG * cpt
    DG = 8
    assert T % (SC_WORKERS * TG) == 0 and HK % CH == 0 and CH % 2 == 0 and W % (DG * L) == 0
    assert nchunk >= SC_NBUF

    def body(w_hbm, idx_hbm, tab_hbm, out_hbm, w_v, idx_v, y_v, rows_v, sem):
        base = _sc_worker_base(tpw)

        def gather(ci, slot):
            tok, c = ci // cpt, ci % cpt
            return pltpu.make_async_copy(tab_hbm.at[idx_v.at[tok, pl.ds(c * CH, CH)]],
                                         rows_v.at[slot], sem.at[slot])

        def compute(ci, slot):
            tok, c = ci // cpt, ci % cpt
            tokv = jnp.full((L,), tok, I32)
            cont = jnp.full((L,), c, I32) != 0

            @pl.loop(0, W // (DG * L))
            def _(dg):
                off = dg * (DG * L)
                zero = jnp.zeros((L,), F32)
                alo = [jnp.where(cont, y_v[tok, pl.ds(off + j * L, L)], zero) for j in range(DG)]
                ahi = [jnp.where(cont, y_v[tok, pl.ds(W + off + j * L, L)], zero) for j in range(DG)]
                for kk in range(0, CH, 2):
                    wa = _as_bf16(plsc.load_gather(w_v, [tokv, jnp.full((L,), c * CH + kk, I32)]))
                    wb = _as_bf16(plsc.load_gather(w_v, [tokv, jnp.full((L,), c * CH + kk + 1, I32)]))
                    for j in range(DG):
                        ra = _as_bf16(rows_v[slot, kk, pl.ds(off + j * L, L)])
                        rb = _as_bf16(rows_v[slot, kk + 1, pl.ds(off + j * L, L)])
                        lo, hi = _halves_f32(wa * ra + wb * rb)
                        alo[j] = alo[j] + lo
                        ahi[j] = ahi[j] + hi
                for j in range(DG):
                    y_v[tok, pl.ds(off + j * L, L)] = alo[j]
                    y_v[tok, pl.ds(W + off + j * L, L)] = ahi[j]

        @pl.loop(0, tpw // TG)
        def _(g):
            t0 = base + g * TG
            pltpu.sync_copy(w_hbm.at[pl.ds(t0, TG)], w_v)
            pltpu.sync_copy(idx_hbm.at[pl.ds(t0, TG)], idx_v)
            _sc_pipeline(nchunk, gather, compute)
            pltpu.sync_copy(y_v, out_hbm.at[pl.ds(t0, TG)])

    return _sc_kernel(
        body, jax.ShapeDtypeStruct((T, D), F32),
        [pltpu.VMEM((TG, HK), I32), pltpu.VMEM((TG, HK), I32), pltpu.VMEM((TG, D), F32),
         pltpu.VMEM((SC_NBUF, CH, W), I32), pltpu.SemaphoreType.DMA((SC_NBUF,))],
    )(wp, experts, ptab)


def _tc_wsum_kernel(exp_ref, w_ref, tab_ref, y_ref):
    tb, hk = exp_ref.shape
    sub, lanes = tab_ref.shape[1:]
    for t in range(tb):
        lo_acc = jnp.zeros((sub, lanes), F32)
        hi_acc = jnp.zeros((sub, lanes), F32)
        for k in range(hk):
            row = tab_ref[exp_ref[t, k]]
            w = w_ref[t, k]
            lo_acc = lo_acc + w * lax.bitcast_convert_type(lax.shift_left(row, 16), F32)
            hi_acc = hi_acc + w * lax.bitcast_convert_type(row & HIGH_HALF, F32)
        y_ref[t, pl.ds(0, sub), :] = lo_acc
        y_ref[t, pl.ds(sub, sub), :] = hi_acc


def _tc_wsum(w, experts, ptab):
    T, hk = w.shape
    E, W = ptab.shape
    sub = W // TC_LANES
    tb = min(TC_WSUM_TILE, T)
    smem = pl.BlockSpec((tb, hk), lambda i: (i, 0), memory_space=pltpu.SMEM)
    y = pl.pallas_call(
        _tc_wsum_kernel, grid=(T // tb,),
        in_specs=[smem, smem, _resident((E, sub, TC_LANES))],
        out_specs=pl.BlockSpec((tb, 2 * sub, TC_LANES), lambda i: (i, 0, 0)),
        out_shape=jax.ShapeDtypeStruct((T, 2 * sub, TC_LANES), F32),
        cost_estimate=pl.CostEstimate(flops=4 * T * hk * W, transcendentals=0,
                                      bytes_accessed=4 * (T * hk * W + E * W + 2 * T * W + 2 * T * hk)),
        compiler_params=pltpu.CompilerParams(dimension_semantics=("arbitrary",),
                                             vmem_limit_bytes=VMEM_LIMIT_BYTES),
    )(experts, w, ptab.reshape(E, sub, TC_LANES))
    return y.reshape(T, 2 * W)


def _act_kernel(dots_ref, gate_ref, w_ref, wp_ref):
    w = _gelu(dots_ref[...]) * gate_ref[...]
    w_ref[...] = w
    wp_ref[...] = _pack_splat(w)


def _act(dots, gates):
    T, hk = dots.shape
    tt = min(ROW_TILE, T)
    spec = pl.BlockSpec((tt, hk), lambda i: (i, 0))
    return pl.pallas_call(
        _act_kernel, grid=(T // tt,), in_specs=[spec, spec], out_specs=[spec, spec],
        out_shape=[jax.ShapeDtypeStruct((T, hk), F32), jax.ShapeDtypeStruct((T, hk), I32)],
        compiler_params=pltpu.CompilerParams(dimension_semantics=("arbitrary",)),
    )(dots, gates)


def _final_kernel(h_ref, y_ref, g_ref, *rest):
    o_ref = rest[-1]
    o_ref[...] = _rmsnorm(h_ref[...] + y_ref[...], g_ref[...])


def _final(h, y, gain, out, row0, total_rows):
    T, D = h.shape
    tt = min(ROW_TILE, T)
    assert row0 % tt == 0
    spec = pl.BlockSpec((tt, D), lambda i: (i, 0))
    out_spec = pl.BlockSpec((tt, D), lambda i: (i + row0 // tt, 0))
    in_specs = [spec, spec, pl.BlockSpec((1, D), lambda i: (0, 0))]
    args = (h, y, gain)
    aliases = {}
    if out is not None:
        in_specs.append(pl.BlockSpec(memory_space=pl.ANY))
        args += (out,)
        aliases = {3: 0}
    return pl.pallas_call(
        _final_kernel, grid=(T // tt,), in_specs=in_specs, out_specs=out_spec,
        out_shape=jax.ShapeDtypeStruct((total_rows, D), F32),
        input_output_aliases=aliases,
        compiler_params=pltpu.CompilerParams(dimension_semantics=("arbitrary",)),
    )(*args)


def kernel(x, mem, norm1_gain, w_in, pool_w, pool_scale, sgu_ln_gain, sgu_ln_bias, sgu_w_s, sgu_b_s,
           sgu_w_out, mem_norm_gain, xa_w_kv, xa_w_out, w_out, norm2_gain, peer_w_q, peer_keys1,
           peer_keys2, peer_u, peer_v, final_norm_gain):
    B, S, D = x.shape
    depth = w_in.shape[0]
    lo = lambda w: w.astype(MXU_DTYPE)
    row = lambda w: w.reshape(1, -1)
    assert depth == 1, "the final RMSNorm is fused with the last layer's residual add"
    l = 0
    weights = (row(norm1_gain[l]), lo(w_in[l]), lo(pool_w[l]), row(pool_scale[l]),
               row(sgu_ln_gain[l]), row(sgu_ln_bias[l]), sgu_w_s[l], sgu_b_s[l].T,
               lo(sgu_w_out[l]), lo(xa_w_out[l]), lo(w_out[l]))
    select_w = (row(norm2_gain[l]), lo(peer_w_q[l]), lo(peer_keys1[l]), lo(peer_keys2[l]))
    u_packed, v_packed = _pack_table(peer_u[l]), _pack_table(peer_v[l])
    k, v = _memkv(mem, row(mem_norm_gain[l]), lo(xa_w_kv[l]))
    fgain = row(final_norm_gain)
    chunks = BATCH_CHUNKS if sum(BATCH_CHUNKS) == B else (B,)
    b0, tc_sums, tc_parts, sc_parts = 0, [], [], []
    for c, nb in enumerate(chunks):
        rows = nb * S
        h = _mixer(x, k, v, weights, b0, nb).reshape(rows, D)
        xn, experts, gates = _select(h, *select_w)
        dots = _sc_dots(xn, experts, u_packed)
        if c > 0:
            dots, tc_sums[c - 1] = lax.optimization_barrier((dots, tc_sums[c - 1]))
        w, wp = _act(dots, gates)
        n_tc = (rows * TC_WSUM_SHARE[0] // TC_WSUM_SHARE[1]) // SC_ROW_QUANTUM * SC_ROW_QUANTUM
        tc_sums.append(_tc_wsum(w[:n_tc], experts[:n_tc], v_packed))
        tc_parts.append((h[:n_tc], b0 * S))
        sc_parts.append((h[n_tc:], _sc_wsum(wp[n_tc:], experts[n_tc:], v_packed), b0 * S + n_tc))
        b0 += nb
    parts = [(h_tc, y_tc, row0) for (h_tc, row0), y_tc in zip(tc_parts, tc_sums)]
    order = parts[:-1] + sc_parts[:-1] + parts[-1:] + sc_parts[-1:]
    out = None
    for h_part, y_part, row0 in order:
        out = _final(h_part, y_part, fgain, out, row0, B * S)
    return out.reshape(B, S, D)
```

```python
import functools
import math

import jax
import jax.numpy as jnp
from jax import lax
from jax.experimental import pallas as pl
from jax.experimental.pallas import tpu as pltpu
from jax.experimental.pallas import tpu_sc as plsc

F32 = jnp.float32
I32 = jnp.int32
MXU_DTYPE = jnp.bfloat16

RMS_EPS = 1e-6
LN_EPS = 1e-5
POOL_WINDOWS = (2, 4, 8, 16)
POOL_HALO = 16
SGU_CHUNK = 128
SGU_HEADS = 8
XA_HEADS = 4
PEER_HEADS = 8
PEER_N_KEYS = 128
PEER_TOPK = 16

V7X_VMEM_BYTES = 64 * 1024 * 1024
VMEM_LIMIT_BYTES = 56 * 1024 * 1024

SEQ_TILE = 256
SELECT_TILE = 256
ROW_TILE = 512
TC_WSUM_TILE = 8
TC_LANES = 128
TC_SUBLANES = 8
BATCH_CHUNKS = (1, 2, 2, 3)

SC_CORES = 2
SC_SUBCORES = 16
SC_LANES = 16
SC_WORKERS = SC_CORES * SC_SUBCORES
SC_TOKENS = 32
SC_WSUM_TOKENS = 16
SC_CHUNK = 32
SC_NBUF = 4
HIGH_HALF = -65536
SC_ROW_QUANTUM = SC_WORKERS * SC_WSUM_TOKENS
TC_WSUM_SHARE = (9, 16)


def _rmsnorm(x, gain):
    return x * lax.rsqrt(jnp.mean(x * x, axis=-1, keepdims=True) + RMS_EPS) * gain


def _gelu(x):
    return 0.5 * x * (1.0 + lax.erf(x * (1.0 / math.sqrt(2.0))))


def _mm(a, b):
    return jnp.dot(a.astype(MXU_DTYPE), b.astype(MXU_DTYPE), preferred_element_type=F32)


def _mm_nt(a, b):
    return lax.dot_general(a.astype(MXU_DTYPE), b.astype(MXU_DTYPE),
                           (((1,), (1,)), ((), ())), preferred_element_type=F32)


def _memkv_kernel(mem_ref, gain_ref, wkv_ref, k_ref, v_ref):
    width = k_ref.shape[-1]
    kv = _mm(_rmsnorm(mem_ref[0], gain_ref[...]), wkv_ref[...])
    k_ref[0] = kv[:, :width].astype(k_ref.dtype)
    v_ref[0] = kv[:, width:].astype(v_ref.dtype)


def _memkv(mem, gain, w_kv):
    B, M, D = mem.shape
    width = w_kv.shape[1] // 2
    const = lambda b: (0, 0)
    return pl.pallas_call(
        _memkv_kernel,
        grid=(B,),
        in_specs=[pl.BlockSpec((1, M, D), lambda b: (b, 0, 0)),
                  pl.BlockSpec((1, D), const),
                  pl.BlockSpec(w_kv.shape, const)],
        out_specs=[pl.BlockSpec((1, M, width), lambda b: (b, 0, 0))] * 2,
        out_shape=[jax.ShapeDtypeStruct((B, M, width), MXU_DTYPE)] * 2,
        compiler_params=pltpu.CompilerParams(dimension_semantics=("arbitrary",),
                                             vmem_limit_bytes=VMEM_LIMIT_BYTES),
    )(mem, gain, w_kv)


def _mixer_kernel(x_ref, k_ref, v_ref, n1g_ref, win_ref, poolw_ref, pools_ref, lng_ref, lnb_ref,
                  ws_ref, bst_ref, sguwo_ref, xawo_ref, wout_ref, h_ref, tail_ref):
    ts, d = x_ref.shape[1], x_ref.shape[2]
    s_idx = pl.program_id(1)
    x = x_ref[0]
    nb = _rmsnorm(x, n1g_ref[...]).astype(MXU_DTYPE)

    def proj(col):
        return jnp.dot(nb, win_ref[:, col * d:(col + 1) * d], preferred_element_type=F32)

    @pl.when(s_idx == 0)
    def _():
        tail_ref[...] = jnp.zeros_like(tail_ref)

    p = proj(0)
    ext = jnp.concatenate([tail_ref[...], p], axis=0)
    tail_ref[...] = p[ts - POOL_HALO:, :]
    pos = s_idx * ts + lax.broadcasted_iota(jnp.int32, (ts, 1), 0)
    group = d // len(POOL_WINDOWS)
    y_pool = []
    for g, w in enumerate(POOL_WINDOWS):
        acc = ext[:, g * group:(g + 1) * group]
        shift = 1
        while shift < w:
            acc = acc + pltpu.roll(acc, shift, 0)
            shift *= 2
        count = jnp.minimum(pos + 1, w).astype(F32)
        diff = acc[POOL_HALO:, :] / count - p[:, g * group:(g + 1) * group]
        y_pool.append(_mm(diff, poolw_ref[g]))
    y_pool = jnp.concatenate(y_pool, axis=1) * pools_ref[...]

    u = _gelu(proj(1))
    v = _gelu(proj(2))
    mu = jnp.mean(v, axis=-1, keepdims=True)
    vc = v - mu
    var = jnp.mean(vc * vc, axis=-1, keepdims=True)
    v = (vc * lax.rsqrt(var + LN_EPS) * lng_ref[...] + lnb_ref[...]).astype(MXU_DTYPE)
    hd = d // SGU_HEADS
    causal = (lax.broadcasted_iota(jnp.int32, (SGU_CHUNK, SGU_CHUNK), 0)
              >= lax.broadcasted_iota(jnp.int32, (SGU_CHUNK, SGU_CHUNK), 1))
    mixed_rows = []
    w_masked = [jnp.where(causal, ws_ref[h], 0.0).astype(MXU_DTYPE) for h in range(SGU_HEADS)]
    for c in range(ts // SGU_CHUNK):
        rows = slice(c * SGU_CHUNK, (c + 1) * SGU_CHUNK)
        heads = []
        for h in range(SGU_HEADS):
            mixed = jnp.dot(w_masked[h], v[rows, h * hd:(h + 1) * hd], preferred_element_type=F32)
            heads.append(mixed + bst_ref[:, h:h + 1])
        mixed_rows.append(jnp.concatenate(heads, axis=1))
    mixed = jnp.concatenate(mixed_rows, axis=0) if len(mixed_rows) > 1 else mixed_rows[0]
    y_sgu = _mm(u * mixed, sguwo_ref[...])

    q = proj(3).astype(MXU_DTYPE)
    xd = d // XA_HEADS
    outs = []
    for h in range(XA_HEADS):
        cols = slice(h * xd, (h + 1) * xd)
        s = _mm_nt(q[:, cols], k_ref[0, :, cols]) * (xd ** -0.5)
        e = jnp.exp(s - jnp.max(s, axis=-1, keepdims=True))
        probs = e / jnp.sum(e, axis=-1, keepdims=True)
        outs.append(_mm(probs, v_ref[0, :, cols]))
    y_xa = _mm(jnp.concatenate(outs, axis=1), xawo_ref[...])

    merged = (jax.nn.sigmoid(proj(4)) * y_pool + jax.nn.sigmoid(proj(5)) * y_sgu
              + jax.nn.sigmoid(proj(6)) * y_xa)
    h_ref[0] = x + _mm(merged, wout_ref[...])


def _resident(shape):
    zeros = (0,) * len(shape)
    return pl.BlockSpec(shape, lambda *_: zeros, pipeline_mode=pl.Buffered(1))


def _mixer(x, k, v, weights, b0, nb):
    _, S, D = x.shape
    ts = min(SEQ_TILE, S)
    M = k.shape[1]
    return pl.pallas_call(
        _mixer_kernel,
        grid=(nb, S // ts),
        in_specs=[pl.BlockSpec((1, ts, D), lambda b, s: (b + b0, s, 0)),
                  pl.BlockSpec((1, M, k.shape[2]), lambda b, s: (b + b0, 0, 0)),
                  pl.BlockSpec((1, M, v.shape[2]), lambda b, s: (b + b0, 0, 0))]
                 + [_resident(w.shape) for w in weights],
        out_specs=pl.BlockSpec((1, ts, D), lambda b, s: (b, s, 0)),
        out_shape=jax.ShapeDtypeStruct((nb, S, D), F32),
        scratch_shapes=[pltpu.VMEM((POOL_HALO, D), F32)],
        compiler_params=pltpu.CompilerParams(dimension_semantics=("arbitrary", "arbitrary"),
                                             vmem_limit_bytes=VMEM_LIMIT_BYTES),
    )(x, k, v, *weights)


def _topk_rows(vals, payload=None):
    n_rows, width = vals.shape
    sub = TC_SUBLANES
    n_tiles = n_rows // sub
    assert n_rows % sub == 0
    tiles = [vals[j * sub:(j + 1) * sub, :] for j in range(n_tiles)]
    pay = None if payload is None else [payload[j * sub:(j + 1) * sub, :] for j in range(n_tiles)]
    sub_row = lax.broadcasted_iota(jnp.int32, (sub, width), 0).astype(F32)
    rows = [sub_row + float(j * sub) for j in range(n_tiles)]

    def best_of(items):
        while len(items) > 1:
            merged = []
            for a, b in zip(items[0::2], items[1::2]):
                keep_a = a[0] >= b[0]
                merged.append(tuple(jnp.where(keep_a, xa, xb) for xa, xb in zip(a, b)))
            if len(items) % 2:
                merged.append(items[-1])
            items = merged
        return items[0]

    out_v, out_i = [], []
    for _ in range(PEER_TOPK):
        if pay is None:
            v8, j8 = best_of([(tiles[j], float(j)) for j in range(n_tiles)])
        else:
            v8, j8, p8 = best_of([(tiles[j], float(j), pay[j]) for j in range(n_tiles)])
        r8 = j8 * float(sub) + sub_row
        m = jnp.max(v8, axis=0, keepdims=True)
        first = jnp.min(jnp.where(v8 == m, r8, float(n_rows)), axis=0, keepdims=True)
        out_v.append(m)
        if pay is None:
            out_i.append(first)
        else:
            out_i.append(jnp.max(jnp.where(r8 == first, p8, -1.0), axis=0, keepdims=True))
        tiles = [jnp.where(rows[j] == first, -jnp.inf, tiles[j]) for j in range(n_tiles)]
    return jnp.concatenate(out_v, axis=0), jnp.concatenate(out_i, axis=0)


def _candidates(v1, i1, v2, i2):
    sub = 8
    jrow = lax.broadcasted_iota(jnp.int32, (sub, 1), 0)
    vals = [v1[0:1] + v2]
    ids = [i1[0:1] * float(PEER_N_KEYS) + i2]
    for i in range(1, sub):
        keep = jrow < (PEER_TOPK // (i + 1))
        vals.append(jnp.where(keep, v1[i:i + 1] + v2[0:sub], -jnp.inf))
        ids.append(i1[i:i + 1] * float(PEER_N_KEYS) + i2[0:sub])
    vals.append(v1[sub:] + v2[0:1])
    ids.append(i1[sub:] * float(PEER_N_KEYS) + i2[0:1])
    return jnp.concatenate(vals, axis=0), jnp.concatenate(ids, axis=0)


def _select_kernel(h_ref, n2g_ref, wq_ref, k1_ref, k2_ref, xn_ref, exp_ref, gate_ref):
    xn = _rmsnorm(h_ref[...], n2g_ref[...])
    xn_ref[...] = _pack_halves(xn)
    q = _mm(xn, wq_ref[...])
    half = k1_ref.shape[1]
    experts, gates = [], []
    for h in range(PEER_HEADS):
        q1 = q[:, (2 * h) * half:(2 * h + 1) * half]
        q2 = q[:, (2 * h + 1) * half:(2 * h + 2) * half]
        v1, i1 = _topk_rows(_mm_nt(k1_ref[...], q1))
        v2, i2 = _topk_rows(_mm_nt(k2_ref[...], q2))
        best, eid = _topk_rows(*_candidates(v1, i1, v2, i2))
        e = jnp.exp(best - jnp.max(best, axis=0, keepdims=True))
        gates.append(e / jnp.sum(e, axis=0, keepdims=True))
        experts.append(eid)
    exp_ref[...] = jnp.concatenate(experts, axis=0).T.astype(jnp.int32)
    gate_ref[...] = jnp.concatenate(gates, axis=0).T


def _select(h, n2g, w_q, keys1, keys2):
    T, D = h.shape
    tt = min(SELECT_TILE, T)
    hk = PEER_HEADS * PEER_TOPK
    return pl.pallas_call(
        _select_kernel,
        grid=(T // tt,),
        in_specs=[pl.BlockSpec((tt, D), lambda i: (i, 0)),
                  _resident(n2g.shape), _resident(w_q.shape),
                  _resident(keys1.shape), _resident(keys2.shape)],
        out_specs=[pl.BlockSpec((tt, D // 2), lambda i: (i, 0)),
                   pl.BlockSpec((tt, hk), lambda i: (i, 0)),
                   pl.BlockSpec((tt, hk), lambda i: (i, 0))],
        out_shape=[jax.ShapeDtypeStruct((T, D // 2), I32),
                   jax.ShapeDtypeStruct((T, hk), jnp.int32),
                   jax.ShapeDtypeStruct((T, hk), F32)],
        compiler_params=pltpu.CompilerParams(dimension_semantics=("arbitrary",),
                                             vmem_limit_bytes=VMEM_LIMIT_BYTES),
    )(h, n2g, w_q, keys1, keys2)


def _pack_table(table):
    half = table.shape[1] // 2
    bits = lax.bitcast_convert_type(table.astype(jnp.bfloat16), jnp.uint16).astype(jnp.uint32)
    return lax.bitcast_convert_type((bits[:, half:] << 16) | bits[:, :half], I32)


def _bf16_bits(x):
    return lax.bitcast_convert_type(x.astype(jnp.bfloat16).astype(F32), I32)


def _pack_halves(x):
    half = x.shape[1] // 2
    return (_bf16_bits(x[:, half:]) & HIGH_HALF) | lax.shift_right_logical(_bf16_bits(x[:, :half]), 16)


def _pack_splat(x):
    bits = _bf16_bits(x)
    return (bits & HIGH_HALF) | lax.shift_right_logical(bits, 16)


def _unpack_words(w):
    lo = lax.bitcast_convert_type(lax.shift_left(w, jnp.full(w.shape, 16, I32)), F32)
    hi = lax.bitcast_convert_type(lax.bitwise_and(w, jnp.full(w.shape, HIGH_HALF, I32)), F32)
    return lo, hi


def _as_bf16(w):
    return plsc.bitcast(w, jnp.bfloat16)


def _halves_f32(p):
    return _unpack_words(plsc.bitcast(p, I32))


def _sc_kernel(body, out_type, scratch_types):
    mesh = plsc.VectorSubcoreMesh(core_axis_name="c", subcore_axis_name="s")
    return pl.kernel(body, out_type=out_type, mesh=mesh, scratch_types=scratch_types,
                     compiler_params=pltpu.CompilerParams(needs_layout_passes=False))


def _sc_worker_base(tokens_per_worker):
    return (lax.axis_index("s") * SC_CORES + lax.axis_index("c")) * tokens_per_worker


def _sc_pipeline(nchunk, gather, compute):
    ahead = SC_NBUF - 1
    for i in range(ahead):
        gather(i, i).start()

    @pl.loop(0, nchunk)
    def _(ci):
        @pl.when(ci + ahead < nchunk)
        def _():
            gather(ci + ahead, (ci + ahead) % SC_NBUF).start()

        slot = ci % SC_NBUF
        gather(ci, slot).wait()
        compute(ci, slot)


def _sc_dots(xp, experts, ptab):
    T, W = xp.shape
    HK = experts.shape[1]
    L, TG, CH = SC_LANES, SC_TOKENS, SC_CHUNK
    tpw = T // SC_WORKERS
    cpt = HK // CH
    nchunk = TG * cpt
    assert W == ptab.shape[1] and T % (SC_WORKERS * TG) == 0 and HK % CH == 0 and CH % L == 0
    assert W % (2 * L) == 0 and nchunk >= SC_NBUF

    def body(x_hbm, idx_hbm, tab_hbm, out_hbm, x_v, idx_v, out_v, rows_v, acc_v, sem):
        base = _sc_worker_base(tpw)
        lane = lax.iota(I32, L)

        def gather(ci, slot):
            tok, c = ci // cpt, ci % cpt
            return pltpu.make_async_copy(tab_hbm.at[idx_v.at[tok, pl.ds(c * CH, CH)]],
                                         rows_v.at[slot], sem.at[slot])

        def compute(ci, slot):
            tok, c = ci // cpt, ci % cpt
            for eg in range(CH // L):
                accs = [jnp.zeros((L,), F32) for _ in range(L)]
                for j in range(0, W // L, 2):
                    xa = _as_bf16(x_v[tok, pl.ds(j * L, L)])
                    xb = _as_bf16(x_v[tok, pl.ds((j + 1) * L, L)])
                    for kk in range(L):
                        ra = _as_bf16(rows_v[slot, eg * L + kk, pl.ds(j * L, L)])
                        rb = _as_bf16(rows_v[slot, eg * L + kk, pl.ds((j + 1) * L, L)])
                        lo, hi = _halves_f32(ra * xa + rb * xb)
                        accs[kk] = accs[kk] + lo + hi
                for kk in range(L):
                    acc_v[kk, :] = accs[kk]

                s = jnp.zeros((L,), F32)
                for j in range(L):
                    s = s + plsc.load_gather(acc_v, [lane, jnp.full((L,), j, I32)])
                out_v[tok, pl.ds(c * CH + eg * L, L)] = s

        @pl.loop(0, tpw // TG)
        def _(g):
            t0 = base + g * TG
            pltpu.sync_copy(x_hbm.at[pl.ds(t0, TG)], x_v)
            pltpu.sync_copy(idx_hbm.at[pl.ds(t0, TG)], idx_v)
            _sc_pipeline(nchunk, gather, compute)
            pltpu.sync_copy(out_v, out_hbm.at[pl.ds(t0, TG)])

    return _sc_kernel(
        body, jax.ShapeDtypeStruct((T, HK), F32),
        [pltpu.VMEM((TG, W), I32), pltpu.VMEM((TG, HK), I32), pltpu.VMEM((TG, HK), F32),
         pltpu.VMEM((SC_NBUF, CH, W), I32), pltpu.VMEM((L, L), F32),
         pltpu.SemaphoreType.DMA((SC_NBUF,))],
    )(xp, experts, ptab)


def _sc_wsum(wp, experts, ptab):
    T, HK = wp.shape
    W = ptab.shape[1]
    D = 2 * W
    L, TG, CH = SC_LANES, SC_WSUM_TOKENS, SC_CHUNK
    tpw = T // SC_WORKERS
    cpt = HK // CH
    nchunk = TG * cpt
    DG = 8
    assert T % (SC_WORKERS * TG) == 0 and HK % CH == 0 and CH % 2 == 0 and W % (DG * L) == 0
    assert nchunk >= SC_NBUF

    def body(w_hbm, idx_hbm, tab_hbm, out_hbm, w_v, idx_v, y_v, rows_v, sem):
        base = _sc_worker_base(tpw)

        def gather(ci, slot):
            tok, c = ci // cpt, ci % cpt
            return pltpu.make_async_copy(tab_hbm.at[idx_v.at[tok, pl.ds(c * CH, CH)]],
                                         rows_v.at[slot], sem.at[slot])

        def compute(ci, slot):
            tok, c = ci // cpt, ci % cpt
            tokv = jnp.full((L,), tok, I32)
            cont = jnp.full((L,), c, I32) != 0

            @pl.loop(0, W // (DG * L))
            def _(dg):
                off = dg * (DG * L)
                zero = jnp.zeros((L,), F32)
                alo = [jnp.where(cont, y_v[tok, pl.ds(off + j * L, L)], zero) for j in range(DG)]
                ahi = [jnp.where(cont, y_v[tok, pl.ds(W + off + j * L, L)], zero) for j in range(DG)]
                for kk in range(0, CH, 2):
                    wa = _as_bf16(plsc.load_gather(w_v, [tokv, jnp.full((L,), c * CH + kk, I32)]))
                    wb = _as_bf16(plsc.load_gather(w_v, [tokv, jnp.full((L,), c * CH + kk + 1, I32)]))
                    for j in range(DG):
                        ra = _as_bf16(rows_v[slot, kk, pl.ds(off + j * L, L)])
                        rb = _as_bf16(rows_v[slot, kk + 1, pl.ds(off + j * L, L)])
                        lo, hi = _halves_f32(wa * ra + wb * rb)
                        alo[j] = alo[j] + lo
                        ahi[j] = ahi[j] + hi
                for j in range(DG):
                    y_v[tok, pl.ds(off + j * L, L)] = alo[j]
                    y_v[tok, pl.ds(W + off + j * L, L)] = ahi[j]

        @pl.loop(0, tpw // TG)
        def _(g):
            t0 = base + g * TG
            pltpu.sync_copy(w_hbm.at[pl.ds(t0, TG)], w_v)
            pltpu.sync_copy(idx_hbm.at[pl.ds(t0, TG)], idx_v)
            _sc_pipeline(nchunk, gather, compute)
            pltpu.sync_copy(y_v, out_hbm.at[pl.ds(t0, TG)])

    return _sc_kernel(
        body, jax.ShapeDtypeStruct((T, D), F32),
        [pltpu.VMEM((TG, HK), I32), pltpu.VMEM((TG, HK), I32), pltpu.VMEM((TG, D), F32),
         pltpu.VMEM((SC_NBUF, CH, W), I32), pltpu.SemaphoreType.DMA((SC_NBUF,))],
    )(wp, experts, ptab)


def _tc_wsum_kernel(exp_ref, w_ref, tab_ref, y_ref):
    tb, hk = exp_ref.shape
    sub, lanes = tab_ref.shape[1:]
    for t in range(tb):
        lo_acc = jnp.zeros((sub, lanes), F32)
        hi_acc = jnp.zeros((sub, lanes), F32)
        for k in range(hk):
            row = tab_ref[exp_ref[t, k]]
            w = w_ref[t, k]
            lo_acc = lo_acc + w * lax.bitcast_convert_type(lax.shift_left(row, 16), F32)
            hi_acc = hi_acc + w * lax.bitcast_convert_type(row & HIGH_HALF, F32)
        y_ref[t, pl.ds(0, sub), :] = lo_acc
        y_ref[t, pl.ds(sub, sub), :] = hi_acc


def _tc_wsum(w, experts, ptab):
    T, hk = w.shape
    E, W = ptab.shape
    sub = W // TC_LANES
    tb = min(TC_WSUM_TILE, T)
    smem = pl.BlockSpec((tb, hk), lambda i: (i, 0), memory_space=pltpu.SMEM)
    y = pl.pallas_call(
        _tc_wsum_kernel, grid=(T // tb,),
        in_specs=[smem, smem, _resident((E, sub, TC_LANES))],
        out_specs=pl.BlockSpec((tb, 2 * sub, TC_LANES), lambda i: (i, 0, 0)),
        out_shape=jax.ShapeDtypeStruct((T, 2 * sub, TC_LANES), F32),
        cost_estimate=pl.CostEstimate(flops=4 * T * hk * W, transcendentals=0,
                                      bytes_accessed=4 * (T * hk * W + E * W + 2 * T * W + 2 * T * hk)),
        compiler_params=pltpu.CompilerParams(dimension_semantics=("arbitrary",),
                                             vmem_limit_bytes=VMEM_LIMIT_BYTES),
    )(experts, w, ptab.reshape(E, sub, TC_LANES))
    return y.reshape(T, 2 * W)


def _act_kernel(dots_ref, gate_ref, w_ref, wp_ref):
    w = _gelu(dots_ref[...]) * gate_ref[...]
    w_ref[...] = w
    wp_ref[...] = _pack_splat(w)


def _act(dots, gates):
    T, hk = dots.shape
    tt = min(ROW_TILE, T)
    spec = pl.BlockSpec((tt, hk), lambda i: (i, 0))
    return pl.pallas_call(
        _act_kernel, grid=(T // tt,), in_specs=[spec, spec], out_specs=[spec, spec],
        out_shape=[jax.ShapeDtypeStruct((T, hk), F32), jax.ShapeDtypeStruct((T, hk), I32)],
        compiler_params=pltpu.CompilerParams(dimension_semantics=("arbitrary",)),
    )(dots, gates)


def _final_kernel(h_ref, y_ref, g_ref, *rest):
    o_ref = rest[-1]
    o_ref[...] = _rmsnorm(h_ref[...] + y_ref[...], g_ref[...])


def _final(h, y, gain, out, row0, total_rows):
    T, D = h.shape
    tt = min(ROW_TILE, T)
    assert row0 % tt == 0
    spec = pl.BlockSpec((tt, D), lambda i: (i, 0))
    out_spec = pl.BlockSpec((tt, D), lambda i: (i + row0 // tt, 0))
    in_specs = [spec, spec, pl.BlockSpec((1, D), lambda i: (0, 0))]
    args = (h, y, gain)
    aliases = {}
    if out is not None:
        in_specs.append(pl.BlockSpec(memory_space=pl.ANY))
        args += (out,)
        aliases = {3: 0}
    return pl.pallas_call(
        _final_kernel, grid=(T // tt,), in_specs=in_specs, out_specs=out_spec,
        out_shape=jax.ShapeDtypeStruct((total_rows, D), F32),
        input_output_aliases=aliases,
        compiler_params=pltpu.CompilerParams(dimension_semantics=("arbitrary",)),
    )(*args)


def kernel(x, mem, norm1_gain, w_in, pool_w, pool_scale, sgu_ln_gain, sgu_ln_bias, sgu_w_s, sgu_b_s,
           sgu_w_out, mem_norm_gain, xa_w_kv, xa_w_out, w_out, norm2_gain, peer_w_q, peer_keys1,
           peer_keys2, peer_u, peer_v, final_norm_gain):
    B, S, D = x.shape
    depth = w_in.shape[0]
    lo = lambda w: w.astype(MXU_DTYPE)
    row = lambda w: w.reshape(1, -1)
    assert depth == 1, "the final RMSNorm is fused with the last layer's residual add"
    l = 0
    weights = (row(norm1_gain[l]), lo(w_in[l]), lo(pool_w[l]), row(pool_scale[l]),
               row(sgu_ln_gain[l]), row(sgu_ln_bias[l]), sgu_w_s[l], sgu_b_s[l].T,
               lo(sgu_w_out[l]), lo(xa_w_out[l]), lo(w_out[l]))
    select_w = (row(norm2_gain[l]), lo(peer_w_q[l]), lo(peer_keys1[l]), lo(peer_keys2[l]))
    u_packed, v_packed = _pack_table(peer_u[l]), _pack_table(peer_v[l])
    k, v = _memkv(mem, row(mem_norm_gain[l]), lo(xa_w_kv[l]))
    fgain = row(final_norm_gain)
    chunks = BATCH_CHUNKS if sum(BATCH_CHUNKS) == B else (B,)
    b0, tc_sums, tc_parts, sc_parts = 0, [], [], []
    for c, nb in enumerate(chunks):
        rows = nb * S
        h = _mixer(x, k, v, weights, b0, nb).reshape(rows, D)
        xn, experts, gates = _select(h, *select_w)
        dots = _sc_dots(xn, experts, u_packed)
        if c > 0:
            dots, tc_sums[c - 1] = lax.optimization_barrier((dots, tc_sums[c - 1]))
        w, wp = _act(dots, gates)
        n_tc = (rows * TC_WSUM_SHARE[0] // TC_WSUM_SHARE[1]) // SC_ROW_QUANTUM * SC_ROW_QUANTUM
        tc_sums.append(_tc_wsum(w[:n_tc], experts[:n_tc], v_packed))
        tc_parts.append((h[:n_tc], b0 * S))
        sc_parts.append((h[n_tc:], _sc_wsum(wp[n_tc:], experts[n_tc:], v_packed), b0 * S + n_tc))
        b0 += nb
    parts = [(h_tc, y_tc, row0) for (h_tc, row0), y_tc in zip(tc_parts, tc_sums)]
    order = parts[:-1] + sc_parts[:-1] + parts[-1:] + sc_parts[-1:]
    out = None
    for h_part, y_part, row0 in order:
        out = _final(h_part, y_part, fgain, out, row0, B * S)
    return out.reshape(B, S, D)
```

```python
import math

import jax
import jax.numpy as jnp
from jax import lax
from jax.experimental import pallas as pl
from jax.experimental.pallas import tpu as pltpu
from jax.experimental.pallas import tpu_sc as plsc

F32 = jnp.float32
I32 = jnp.int32
MXU_DTYPE = jnp.bfloat16

RMS_EPS = 1e-6
LN_EPS = 1e-5
POOL_WINDOWS = (2, 4, 8, 16)
POOL_HALO = 16
SGU_CHUNK = 128
SGU_HEADS = 8
XA_HEADS = 4
PEER_HEADS = 8
PEER_N_KEYS = 128
PEER_TOPK = 16

V7X_VMEM_BYTES = 64 * 1024 * 1024
VMEM_LIMIT_BYTES = V7X_VMEM_BYTES - 8 * 1024 * 1024

SEQ_TILE = 256
SELECT_TILE = 256
ROW_TILE = 512
TC_WSUM_TILE = 8
TC_LANES = 128
TC_SUBLANES = 8
BATCH_CHUNKS = (1, 1, 2, 2, 2)

SC_CORES = 2
SC_SUBCORES = 16
SC_LANES = 16
SC_WORKERS = SC_CORES * SC_SUBCORES
SC_TOKENS = 32
SC_WSUM_TOKENS = 16
SC_CHUNK = 32
SC_DOTS_NBUF = 5
SC_WSUM_NBUF = 4
HIGH_HALF = -65536
SC_ROW_QUANTUM = SC_WORKERS * SC_WSUM_TOKENS
TC_WSUM_SHARE = (9, 16)


def _rmsnorm(x, gain):
    return x * lax.rsqrt(jnp.mean(x * x, axis=-1, keepdims=True) + RMS_EPS) * gain


def _gelu(x):
    return 0.5 * x * (1.0 + lax.erf(x * (1.0 / math.sqrt(2.0))))


def _mm(a, b):
    return jnp.dot(a.astype(MXU_DTYPE), b.astype(MXU_DTYPE), preferred_element_type=F32)


def _mm_nt(a, b):
    return lax.dot_general(a.astype(MXU_DTYPE), b.astype(MXU_DTYPE),
                           (((1,), (1,)), ((), ())), preferred_element_type=F32)


def _memkv_kernel(mem_ref, gain_ref, wkv_ref, k_ref, v_ref):
    width = k_ref.shape[-1]
    kv = _mm(_rmsnorm(mem_ref[0], gain_ref[...]), wkv_ref[...])
    k_ref[0] = kv[:, :width].astype(k_ref.dtype)
    v_ref[0] = kv[:, width:].astype(v_ref.dtype)


def _memkv(mem, gain, w_kv):
    B, M, D = mem.shape
    width = w_kv.shape[1] // 2
    const = lambda b: (0, 0)
    return pl.pallas_call(
        _memkv_kernel,
        grid=(B,),
        in_specs=[pl.BlockSpec((1, M, D), lambda b: (b, 0, 0)),
                  pl.BlockSpec((1, D), const),
                  pl.BlockSpec(w_kv.shape, const)],
        out_specs=[pl.BlockSpec((1, M, width), lambda b: (b, 0, 0))] * 2,
        out_shape=[jax.ShapeDtypeStruct((B, M, width), MXU_DTYPE)] * 2,
        compiler_params=pltpu.CompilerParams(dimension_semantics=("arbitrary",),
                                             vmem_limit_bytes=VMEM_LIMIT_BYTES),
    )(mem, gain, w_kv)


def _mixer_kernel(x_ref, k_ref, v_ref, n1g_ref, win_ref, poolw_ref, pools_ref, lng_ref, lnb_ref,
                  ws_ref, bst_ref, sguwo_ref, xawo_ref, wout_ref, h_ref, tail_ref):
    ts, d = x_ref.shape[1], x_ref.shape[2]
    s_idx = pl.program_id(1)
    x = x_ref[0]
    nb = _rmsnorm(x, n1g_ref[...]).astype(MXU_DTYPE)

    def proj(col):
        return jnp.dot(nb, win_ref[:, col * d:(col + 1) * d], preferred_element_type=F32)

    @pl.when(s_idx == 0)
    def _():
        tail_ref[...] = jnp.zeros_like(tail_ref)

    p = proj(0)
    ext = jnp.concatenate([tail_ref[...], p], axis=0)
    tail_ref[...] = p[ts - POOL_HALO:, :]
    pos = s_idx * ts + lax.broadcasted_iota(jnp.int32, (ts, 1), 0)
    group = d // len(POOL_WINDOWS)
    y_pool = []
    for g, w in enumerate(POOL_WINDOWS):
        acc = ext[:, g * group:(g + 1) * group]
        shift = 1
        while shift < w:
            acc = acc + pltpu.roll(acc, shift, 0)
            shift *= 2
        count = jnp.minimum(pos + 1, w).astype(F32)
        diff = acc[POOL_HALO:, :] / count - p[:, g * group:(g + 1) * group]
        y_pool.append(_mm(diff, poolw_ref[g]))
    y_pool = jnp.concatenate(y_pool, axis=1) * pools_ref[...]

    u = _gelu(proj(1))
    v = _gelu(proj(2))
    mu = jnp.mean(v, axis=-1, keepdims=True)
    vc = v - mu
    var = jnp.mean(vc * vc, axis=-1, keepdims=True)
    v = (vc * lax.rsqrt(var + LN_EPS) * lng_ref[...] + lnb_ref[...]).astype(MXU_DTYPE)
    hd = d // SGU_HEADS
    causal = (lax.broadcasted_iota(jnp.int32, (SGU_CHUNK, SGU_CHUNK), 0)
              >= lax.broadcasted_iota(jnp.int32, (SGU_CHUNK, SGU_CHUNK), 1))
    mixed_rows = []
    w_masked = [jnp.where(causal, ws_ref[h], 0.0).astype(MXU_DTYPE) for h in range(SGU_HEADS)]
    for c in range(ts // SGU_CHUNK):
        rows = slice(c * SGU_CHUNK, (c + 1) * SGU_CHUNK)
        heads = []
        for h in range(SGU_HEADS):
            mixed = jnp.dot(w_masked[h], v[rows, h * hd:(h + 1) * hd], preferred_element_type=F32)
            heads.append(mixed + bst_ref[:, h:h + 1])
        mixed_rows.append(jnp.concatenate(heads, axis=1))
    mixed = jnp.concatenate(mixed_rows, axis=0) if len(mixed_rows) > 1 else mixed_rows[0]
    y_sgu = _mm(u * mixed, sguwo_ref[...])

    q = proj(3).astype(MXU_DTYPE)
    xd = d // XA_HEADS
    outs = []
    for h in range(XA_HEADS):
        cols = slice(h * xd, (h + 1) * xd)
        s = _mm_nt(q[:, cols], k_ref[0, :, cols]) * (xd ** -0.5)
        e = jnp.exp(s - jnp.max(s, axis=-1, keepdims=True))
        probs = e / jnp.sum(e, axis=-1, keepdims=True)
        outs.append(_mm(probs, v_ref[0, :, cols]))
    y_xa = _mm(jnp.concatenate(outs, axis=1), xawo_ref[...])

    merged = (jax.nn.sigmoid(proj(4)) * y_pool + jax.nn.sigmoid(proj(5)) * y_sgu
              + jax.nn.sigmoid(proj(6)) * y_xa)
    h_ref[0] = x + _mm(merged, wout_ref[...])


def _resident(shape):
    zeros = (0,) * len(shape)
    return pl.BlockSpec(shape, lambda *_: zeros, pipeline_mode=pl.Buffered(1))


def _mixer(x, k, v, weights, b0, nb):
    _, S, D = x.shape
    ts = min(SEQ_TILE, S)
    M = k.shape[1]
    return pl.pallas_call(
        _mixer_kernel,
        grid=(nb, S // ts),
        in_specs=[pl.BlockSpec((1, ts, D), lambda b, s: (b + b0, s, 0)),
                  pl.BlockSpec((1, M, k.shape[2]), lambda b, s: (b + b0, 0, 0)),
                  pl.BlockSpec((1, M, v.shape[2]), lambda b, s: (b + b0, 0, 0))]
                 + [_resident(w.shape) for w in weights],
        out_specs=pl.BlockSpec((1, ts, D), lambda b, s: (b, s, 0)),
        out_shape=jax.ShapeDtypeStruct((nb, S, D), F32),
        scratch_shapes=[pltpu.VMEM((POOL_HALO, D), F32)],
        compiler_params=pltpu.CompilerParams(dimension_semantics=("arbitrary", "arbitrary"),
                                             vmem_limit_bytes=VMEM_LIMIT_BYTES),
    )(x, k, v, *weights)


def _topk_rows(vals, payload=None):
    n_rows, width = vals.shape
    sub = TC_SUBLANES
    n_tiles = n_rows // sub
    assert n_rows % sub == 0
    tiles = [vals[j * sub:(j + 1) * sub, :] for j in range(n_tiles)]
    pay = None if payload is None else [payload[j * sub:(j + 1) * sub, :] for j in range(n_tiles)]
    sub_row = lax.broadcasted_iota(jnp.int32, (sub, width), 0).astype(F32)
    rows = [sub_row + float(j * sub) for j in range(n_tiles)]

    def best_of(items):
        while len(items) > 1:
            merged = []
            for a, b in zip(items[0::2], items[1::2]):
                keep_a = a[0] >= b[0]
                merged.append(tuple(jnp.where(keep_a, xa, xb) for xa, xb in zip(a, b)))
            if len(items) % 2:
                merged.append(items[-1])
            items = merged
        return items[0]

    out_v, out_i = [], []
    for _ in range(PEER_TOPK):
        if pay is None:
            v8, j8 = best_of([(tiles[j], float(j)) for j in range(n_tiles)])
        else:
            v8, j8, p8 = best_of([(tiles[j], float(j), pay[j]) for j in range(n_tiles)])
        r8 = j8 * float(sub) + sub_row
        m = jnp.max(v8, axis=0, keepdims=True)
        first = jnp.min(jnp.where(v8 == m, r8, float(n_rows)), axis=0, keepdims=True)
        out_v.append(m)
        if pay is None:
            out_i.append(first)
        else:
            out_i.append(jnp.max(jnp.where(r8 == first, p8, -1.0), axis=0, keepdims=True))
        tiles = [jnp.where(rows[j] == first, -jnp.inf, tiles[j]) for j in range(n_tiles)]
    return jnp.concatenate(out_v, axis=0), jnp.concatenate(out_i, axis=0)


def _candidates(v1, i1, v2, i2):
    sub = 8
    jrow = lax.broadcasted_iota(jnp.int32, (sub, 1), 0)
    vals = [v1[0:1] + v2]
    ids = [i1[0:1] * float(PEER_N_KEYS) + i2]
    for i in range(1, sub):
        keep = jrow < (PEER_TOPK // (i + 1))
        vals.append(jnp.where(keep, v1[i:i + 1] + v2[0:sub], -jnp.inf))
        ids.append(i1[i:i + 1] * float(PEER_N_KEYS) + i2[0:sub])
    vals.append(v1[sub:] + v2[0:1])
    ids.append(i1[sub:] * float(PEER_N_KEYS) + i2[0:1])
    return jnp.concatenate(vals, axis=0), jnp.concatenate(ids, axis=0)


def _select_kernel(h_ref, n2g_ref, wq_ref, k1_ref, k2_ref, xn_ref, exp_ref, gate_ref):
    xn = _rmsnorm(h_ref[...], n2g_ref[...])
    xn_ref[...] = _pack_halves(xn)
    q = _mm(xn, wq_ref[...])
    half = k1_ref.shape[1]
    experts, gates = [], []
    for h in range(PEER_HEADS):
        q1 = q[:, (2 * h) * half:(2 * h + 1) * half]
        q2 = q[:, (2 * h + 1) * half:(2 * h + 2) * half]
        v1, i1 = _topk_rows(_mm_nt(k1_ref[...], q1))
        v2, i2 = _topk_rows(_mm_nt(k2_ref[...], q2))
        best, eid = _topk_rows(*_candidates(v1, i1, v2, i2))
        e = jnp.exp(best - jnp.max(best, axis=0, keepdims=True))
        gates.append(e / jnp.sum(e, axis=0, keepdims=True))
        experts.append(eid)
    exp_ref[...] = jnp.concatenate(experts, axis=0).T.astype(jnp.int32)
    gate_ref[...] = jnp.concatenate(gates, axis=0).T


def _select(h, n2g, w_q, keys1, keys2):
    T, D = h.shape
    tt = min(SELECT_TILE, T)
    hk = PEER_HEADS * PEER_TOPK
    return pl.pallas_call(
        _select_kernel,
        grid=(T // tt,),
        in_specs=[pl.BlockSpec((tt, D), lambda i: (i, 0)),
                  _resident(n2g.shape), _resident(w_q.shape),
                  _resident(keys1.shape), _resident(keys2.shape)],
        out_specs=[pl.BlockSpec((tt, D // 2), lambda i: (i, 0)),
                   pl.BlockSpec((tt, hk), lambda i: (i, 0)),
                   pl.BlockSpec((tt, hk), lambda i: (i, 0))],
        out_shape=[jax.ShapeDtypeStruct((T, D // 2), I32),
                   jax.ShapeDtypeStruct((T, hk), jnp.int32),
                   jax.ShapeDtypeStruct((T, hk), F32)],
        compiler_params=pltpu.CompilerParams(dimension_semantics=("arbitrary",),
                                             vmem_limit_bytes=VMEM_LIMIT_BYTES),
    )(h, n2g, w_q, keys1, keys2)


def _pack_table(table):
    half = table.shape[1] // 2
    bits = lax.bitcast_convert_type(table.astype(jnp.bfloat16), jnp.uint16).astype(jnp.uint32)
    return lax.bitcast_convert_type((bits[:, half:] << 16) | bits[:, :half], I32)


def _bf16_bits(x):
    return lax.bitcast_convert_type(x.astype(jnp.bfloat16).astype(F32), I32)


def _pack_halves(x):
    half = x.shape[1] // 2
    return (_bf16_bits(x[:, half:]) & HIGH_HALF) | lax.shift_right_logical(_bf16_bits(x[:, :half]), 16)


def _pack_splat(x):
    bits = _bf16_bits(x)
    return (bits & HIGH_HALF) | lax.shift_right_logical(bits, 16)


def _unpack_words(w):
    lo = lax.bitcast_convert_type(lax.shift_left(w, jnp.full(w.shape, 16, I32)), F32)
    hi = lax.bitcast_convert_type(lax.bitwise_and(w, jnp.full(w.shape, HIGH_HALF, I32)), F32)
    return lo, hi


def _as_bf16(w):
    return plsc.bitcast(w, jnp.bfloat16)


def _halves_f32(p):
    return _unpack_words(plsc.bitcast(p, I32))


def _sc_kernel(body, out_type, scratch_types):
    mesh = plsc.VectorSubcoreMesh(core_axis_name="c", subcore_axis_name="s")
    return pl.kernel(body, out_type=out_type, mesh=mesh, scratch_types=scratch_types,
                     compiler_params=pltpu.CompilerParams(needs_layout_passes=False))


def _sc_worker_base(tokens_per_worker):
    return (lax.axis_index("s") * SC_CORES + lax.axis_index("c")) * tokens_per_worker


def _sc_pipeline(nchunk, nbuf, gather, compute):
    assert nchunk >= nbuf
    ahead = nbuf - 1
    for i in range(ahead):
        gather(i, i).start()

    @pl.loop(0, nchunk)
    def _(ci):
        @pl.when(ci + ahead < nchunk)
        def _():
            gather(ci + ahead, (ci + ahead) % nbuf).start()

        slot = ci % nbuf
        gather(ci, slot).wait()
        compute(ci, slot)


def _sc_dots(xp, experts, ptab):
    T, W = xp.shape
    HK = experts.shape[1]
    L, TG, CH = SC_LANES, SC_TOKENS, SC_CHUNK
    tpw = T // SC_WORKERS
    cpt = HK // CH
    nchunk = TG * cpt
    assert W == ptab.shape[1] and T % (SC_WORKERS * TG) == 0 and HK % CH == 0 and CH % L == 0
    assert W % (2 * L) == 0
    nbuf = SC_DOTS_NBUF

    def body(x_hbm, idx_hbm, tab_hbm, out_hbm, x_v, idx_v, out_v, rows_v, acc_v, sem):
        base = _sc_worker_base(tpw)
        lane = lax.iota(I32, L)

        def gather(ci, slot):
            tok, c = ci // cpt, ci % cpt
            return pltpu.make_async_copy(tab_hbm.at[idx_v.at[tok, pl.ds(c * CH, CH)]],
                                         rows_v.at[slot], sem.at[slot])

        def compute(ci, slot):
            tok, c = ci // cpt, ci % cpt
            for eg in range(CH // L):
                accs = [jnp.zeros((L,), F32) for _ in range(L)]
                for j in range(0, W // L, 2):
                    xa = _as_bf16(x_v[tok, pl.ds(j * L, L)])
                    xb = _as_bf16(x_v[tok, pl.ds((j + 1) * L, L)])
                    for kk in range(L):
                        ra = _as_bf16(rows_v[slot, eg * L + kk, pl.ds(j * L, L)])
                        rb = _as_bf16(rows_v[slot, eg * L + kk, pl.ds((j + 1) * L, L)])
                        lo, hi = _halves_f32(ra * xa + rb * xb)
                        accs[kk] = accs[kk] + lo + hi
                for kk in range(L):
                    acc_v[kk, :] = accs[kk]

                s = jnp.zeros((L,), F32)
                for j in range(L):
                    s = s + plsc.load_gather(acc_v, [lane, jnp.full((L,), j, I32)])
                out_v[tok, pl.ds(c * CH + eg * L, L)] = s

        @pl.loop(0, tpw // TG)
        def _(g):
            t0 = base + g * TG
            pltpu.sync_copy(x_hbm.at[pl.ds(t0, TG)], x_v)
            pltpu.sync_copy(idx_hbm.at[pl.ds(t0, TG)], idx_v)
            _sc_pipeline(nchunk, nbuf, gather, compute)
            pltpu.sync_copy(out_v, out_hbm.at[pl.ds(t0, TG)])

    return _sc_kernel(
        body, jax.ShapeDtypeStruct((T, HK), F32),
        [pltpu.VMEM((TG, W), I32), pltpu.VMEM((TG, HK), I32), pltpu.VMEM((TG, HK), F32),
         pltpu.VMEM((nbuf, CH, W), I32), pltpu.VMEM((L, L), F32),
         pltpu.SemaphoreType.DMA((nbuf,))],
    )(xp, experts, ptab)


def _sc_wsum(wp, experts, ptab):
    T, HK = wp.shape
    W = ptab.shape[1]
    D = 2 * W
    L, TG, CH = SC_LANES, SC_WSUM_TOKENS, SC_CHUNK
    tpw = T // SC_WORKERS
    cpt = HK // CH
    nchunk = TG * cpt
    DG = 8
    assert T % (SC_WORKERS * TG) == 0 and HK % CH == 0 and CH % 2 == 0 and W % (DG * L) == 0
    nbuf = SC_WSUM_NBUF

    def body(w_hbm, idx_hbm, tab_hbm, out_hbm, w_v, idx_v, y_v, rows_v, sem):
        base = _sc_worker_base(tpw)

        def gather(ci, slot):
            tok, c = ci // cpt, ci % cpt
            return pltpu.make_async_copy(tab_hbm.at[idx_v.at[tok, pl.ds(c * CH, CH)]],
                                         rows_v.at[slot], sem.at[slot])

        def compute(ci, slot):
            tok, c = ci // cpt, ci % cpt
            tokv = jnp.full((L,), tok, I32)
            cont = jnp.full((L,), c, I32) != 0

            @pl.loop(0, W // (DG * L))
            def _(dg):
                off = dg * (DG * L)
                zero = jnp.zeros((L,), F32)
                alo = [jnp.where(cont, y_v[tok, pl.ds(off + j * L, L)], zero) for j in range(DG)]
                ahi = [jnp.where(cont, y_v[tok, pl.ds(W + off + j * L, L)], zero) for j in range(DG)]
                for kk in range(0, CH, 2):
                    wa = _as_bf16(plsc.load_gather(w_v, [tokv, jnp.full((L,), c * CH + kk, I32)]))
                    wb = _as_bf16(plsc.load_gather(w_v, [tokv, jnp.full((L,), c * CH + kk + 1, I32)]))
                    for j in range(DG):
                        ra = _as_bf16(rows_v[slot, kk, pl.ds(off + j * L, L)])
                        rb = _as_bf16(rows_v[slot, kk + 1, pl.ds(off + j * L, L)])
                        lo, hi = _halves_f32(wa * ra + wb * rb)
                        alo[j] = alo[j] + lo
                        ahi[j] = ahi[j] + hi
                for j in range(DG):
                    y_v[tok, pl.ds(off + j * L, L)] = alo[j]
                    y_v[tok, pl.ds(W + off + j * L, L)] = ahi[j]

        @pl.loop(0, tpw // TG)
        def _(g):
            t0 = base + g * TG
            pltpu.sync_copy(w_hbm.at[pl.ds(t0, TG)], w_v)
            pltpu.sync_copy(idx_hbm.at[pl.ds(t0, TG)], idx_v)
            _sc_pipeline(nchunk, nbuf, gather, compute)
            pltpu.sync_copy(y_v, out_hbm.at[pl.ds(t0, TG)])

    return _sc_kernel(
        body, jax.ShapeDtypeStruct((T, D), F32),
        [pltpu.VMEM((TG, HK), I32), pltpu.VMEM((TG, HK), I32), pltpu.VMEM((TG, D), F32),
         pltpu.VMEM((nbuf, CH, W), I32), pltpu.SemaphoreType.DMA((nbuf,))],
    )(wp, experts, ptab)


def _tc_wsum_kernel(exp_ref, w_ref, tab_ref, y_ref):
    tb, hk = exp_ref.shape
    sub, lanes = tab_ref.shape[1:]
    for t in range(tb):
        lo_acc = jnp.zeros((sub, lanes), F32)
        hi_acc = jnp.zeros((sub, lanes), F32)
        for k in range(hk):
            row = tab_ref[exp_ref[t, k]]
            w = w_ref[t, k]
            lo_acc = lo_acc + w * lax.bitcast_convert_type(lax.shift_left(row, 16), F32)
            hi_acc = hi_acc + w * lax.bitcast_convert_type(row & HIGH_HALF, F32)
        y_ref[t, pl.ds(0, sub), :] = lo_acc
        y_ref[t, pl.ds(sub, sub), :] = hi_acc


def _tc_wsum(w, experts, ptab):
    T, hk = w.shape
    E, W = ptab.shape
    sub = W // TC_LANES
    tb = min(TC_WSUM_TILE, T)
    smem = pl.BlockSpec((tb, hk), lambda i: (i, 0), memory_space=pltpu.SMEM)
    y = pl.pallas_call(
        _tc_wsum_kernel, grid=(T // tb,),
        in_specs=[smem, smem, _resident((E, sub, TC_LANES))],
        out_specs=pl.BlockSpec((tb, 2 * sub, TC_LANES), lambda i: (i, 0, 0)),
        out_shape=jax.ShapeDtypeStruct((T, 2 * sub, TC_LANES), F32),
        cost_estimate=pl.CostEstimate(flops=4 * T * hk * W, transcendentals=0,
                                      bytes_accessed=4 * (T * hk * W + E * W + 2 * T * W + 2 * T * hk)),
        compiler_params=pltpu.CompilerParams(dimension_semantics=("arbitrary",),
                                             vmem_limit_bytes=VMEM_LIMIT_BYTES),
    )(experts, w, ptab.reshape(E, sub, TC_LANES))
    return y.reshape(T, 2 * W)


def _act_kernel(dots_ref, gate_ref, w_ref, wp_ref):
    w = _gelu(dots_ref[...]) * gate_ref[...]
    w_ref[...] = w
    wp_ref[...] = _pack_splat(w)


def _act(dots, gates):
    T, hk = dots.shape
    tt = min(ROW_TILE, T)
    spec = pl.BlockSpec((tt, hk), lambda i: (i, 0))
    return pl.pallas_call(
        _act_kernel, grid=(T // tt,), in_specs=[spec, spec], out_specs=[spec, spec],
        out_shape=[jax.ShapeDtypeStruct((T, hk), F32), jax.ShapeDtypeStruct((T, hk), I32)],
        compiler_params=pltpu.CompilerParams(dimension_semantics=("arbitrary",)),
    )(dots, gates)


def _final_kernel(h_ref, y_ref, g_ref, *rest):
    o_ref = rest[-1]
    o_ref[...] = _rmsnorm(h_ref[...] + y_ref[...], g_ref[...])


def _final(h, y, gain, out, row0, total_rows):
    T, D = h.shape
    tt = min(ROW_TILE, T)
    assert row0 % tt == 0
    spec = pl.BlockSpec((tt, D), lambda i: (i, 0))
    out_spec = pl.BlockSpec((tt, D), lambda i: (i + row0 // tt, 0))
    in_specs = [spec, spec, pl.BlockSpec((1, D), lambda i: (0, 0))]
    args = (h, y, gain)
    aliases = {}
    if out is not None:
        in_specs.append(pl.BlockSpec(memory_space=pl.ANY))
        args += (out,)
        aliases = {3: 0}
    return pl.pallas_call(
        _final_kernel, grid=(T // tt,), in_specs=in_specs, out_specs=out_spec,
        out_shape=jax.ShapeDtypeStruct((total_rows, D), F32),
        input_output_aliases=aliases,
        compiler_params=pltpu.CompilerParams(dimension_semantics=("arbitrary",)),
    )(*args)


def kernel(x, mem, norm1_gain, w_in, pool_w, pool_scale, sgu_ln_gain, sgu_ln_bias, sgu_w_s, sgu_b_s,
           sgu_w_out, mem_norm_gain, xa_w_kv, xa_w_out, w_out, norm2_gain, peer_w_q, peer_keys1,
           peer_keys2, peer_u, peer_v, final_norm_gain):
    B, S, D = x.shape
    depth = w_in.shape[0]
    lo = lambda w: w.astype(MXU_DTYPE)
    row = lambda w: w.reshape(1, -1)
    assert depth == 1, "the final RMSNorm is fused with the last layer's residual add"
    l = 0
    weights = (row(norm1_gain[l]), lo(w_in[l]), lo(pool_w[l]), row(pool_scale[l]),
               row(sgu_ln_gain[l]), row(sgu_ln_bias[l]), sgu_w_s[l], sgu_b_s[l].T,
               lo(sgu_w_out[l]), lo(xa_w_out[l]), lo(w_out[l]))
    select_w = (row(norm2_gain[l]), lo(peer_w_q[l]), lo(peer_keys1[l]), lo(peer_keys2[l]))
    u_packed, v_packed = _pack_table(peer_u[l]), _pack_table(peer_v[l])
    k, v = _memkv(mem, row(mem_norm_gain[l]), lo(xa_w_kv[l]))
    fgain = row(final_norm_gain)
    chunks = BATCH_CHUNKS if sum(BATCH_CHUNKS) == B else (B,)
    b0, tc_sums, tc_parts, sc_parts = 0, [], [], []
    for c, nb in enumerate(chunks):
        rows = nb * S
        h = _mixer(x, k, v, weights, b0, nb).reshape(rows, D)
        xn, experts, gates = _select(h, *select_w)
        dots = _sc_dots(xn, experts, u_packed)
        if c > 0:
            dots, tc_sums[c - 1] = lax.optimization_barrier((dots, tc_sums[c - 1]))
        w, wp = _act(dots, gates)
        n_tc = (rows * TC_WSUM_SHARE[0] // TC_WSUM_SHARE[1]) // SC_ROW_QUANTUM * SC_ROW_QUANTUM
        tc_sums.append(_tc_wsum(w[:n_tc], experts[:n_tc], v_packed))
        tc_parts.append((h[:n_tc], b0 * S))
        sc_parts.append((h[n_tc:], _sc_wsum(wp[n_tc:], experts[n_tc:], v_packed), b0 * S + n_tc))
        b0 += nb
    parts = [(h_tc, y_tc, row0) for (h_tc, row0), y_tc in zip(tc_parts, tc_sums)]
    order = parts[:-1] + sc_parts[:-1] + parts[-1:] + sc_parts[-1:]
    out = None
    for h_part, y_part, row0 in order:
        out = _final(h_part, y_part, fgain, out, row0, B * S)
    return out.reshape(B, S, D)
```

```python
import math

import jax
import jax.numpy as jnp
from jax import lax
from jax.experimental import pallas as pl
from jax.experimental.pallas import tpu as pltpu
from jax.experimental.pallas import tpu_sc as plsc

F32 = jnp.float32
I32 = jnp.int32
MXU_DTYPE = jnp.bfloat16

RMS_EPS = 1e-6
LN_EPS = 1e-5
POOL_WINDOWS = (2, 4, 8, 16)
POOL_HALO = 16
SGU_CHUNK = 128
SGU_HEADS = 8
XA_HEADS = 4
PEER_HEADS = 8
PEER_N_KEYS = 128
PEER_TOPK = 16

V7X_VMEM_BYTES = 64 * 1024 * 1024
VMEM_LIMIT_BYTES = V7X_VMEM_BYTES - 8 * 1024 * 1024

SEQ_TILE = 256
SELECT_TILE = 256
ROW_TILE = 512
TC_WSUM_TILE = 8
TC_LANES = 128
TC_SUBLANES = 8
BATCH_CHUNKS = (1, 1, 2, 2, 2)

SC_CORES = 2
SC_SUBCORES = 16
SC_LANES = 16
SC_WORKERS = SC_CORES * SC_SUBCORES
SC_TOKENS = 32
SC_WSUM_TOKENS = 16
SC_CHUNK = 32
SC_DOTS_NBUF = 5
SC_WSUM_NBUF = 4
HIGH_HALF = -65536
SC_ROW_QUANTUM = SC_WORKERS * SC_WSUM_TOKENS
TC_WSUM_SHARE = ((1, 2), (1, 2), (5, 8), (9, 16), (5, 8))


def _rmsnorm(x, gain):
    return x * lax.rsqrt(jnp.mean(x * x, axis=-1, keepdims=True) + RMS_EPS) * gain


def _gelu(x):
    return 0.5 * x * (1.0 + lax.erf(x * (1.0 / math.sqrt(2.0))))


def _mm(a, b):
    return jnp.dot(a.astype(MXU_DTYPE), b.astype(MXU_DTYPE), preferred_element_type=F32)


def _mm_nt(a, b):
    return lax.dot_general(a.astype(MXU_DTYPE), b.astype(MXU_DTYPE),
                           (((1,), (1,)), ((), ())), preferred_element_type=F32)


def _memkv_kernel(mem_ref, gain_ref, wkv_ref, k_ref, v_ref):
    width = k_ref.shape[-1]
    kv = _mm(_rmsnorm(mem_ref[0], gain_ref[...]), wkv_ref[...])
    k_ref[0] = kv[:, :width].astype(k_ref.dtype)
    v_ref[0] = kv[:, width:].astype(v_ref.dtype)


def _memkv(mem, gain, w_kv):
    B, M, D = mem.shape
    width = w_kv.shape[1] // 2
    const = lambda b: (0, 0)
    return pl.pallas_call(
        _memkv_kernel,
        grid=(B,),
        in_specs=[pl.BlockSpec((1, M, D), lambda b: (b, 0, 0)),
                  pl.BlockSpec((1, D), const),
                  pl.BlockSpec(w_kv.shape, const)],
        out_specs=[pl.BlockSpec((1, M, width), lambda b: (b, 0, 0))] * 2,
        out_shape=[jax.ShapeDtypeStruct((B, M, width), MXU_DTYPE)] * 2,
        compiler_params=pltpu.CompilerParams(dimension_semantics=("arbitrary",),
                                             vmem_limit_bytes=VMEM_LIMIT_BYTES),
    )(mem, gain, w_kv)


def _mixer_kernel(x_ref, k_ref, v_ref, n1g_ref, win_ref, poolw_ref, pools_ref, lng_ref, lnb_ref,
                  ws_ref, bst_ref, sguwo_ref, xawo_ref, wout_ref, h_ref, tail_ref):
    ts, d = x_ref.shape[1], x_ref.shape[2]
    s_idx = pl.program_id(1)
    x = x_ref[0]
    nb = _rmsnorm(x, n1g_ref[...]).astype(MXU_DTYPE)

    def proj(col):
        return jnp.dot(nb, win_ref[:, col * d:(col + 1) * d], preferred_element_type=F32)

    @pl.when(s_idx == 0)
    def _():
        tail_ref[...] = jnp.zeros_like(tail_ref)

    p = proj(0)
    ext = jnp.concatenate([tail_ref[...], p], axis=0)
    tail_ref[...] = p[ts - POOL_HALO:, :]
    pos = s_idx * ts + lax.broadcasted_iota(jnp.int32, (ts, 1), 0)
    group = d // len(POOL_WINDOWS)
    y_pool = []
    for g, w in enumerate(POOL_WINDOWS):
        acc = ext[:, g * group:(g + 1) * group]
        shift = 1
        while shift < w:
            acc = acc + pltpu.roll(acc, shift, 0)
            shift *= 2
        count = jnp.minimum(pos + 1, w).astype(F32)
        diff = acc[POOL_HALO:, :] / count - p[:, g * group:(g + 1) * group]
        y_pool.append(_mm(diff, poolw_ref[g]))
    y_pool = jnp.concatenate(y_pool, axis=1) * pools_ref[...]

    u = _gelu(proj(1))
    v = _gelu(proj(2))
    mu = jnp.mean(v, axis=-1, keepdims=True)
    vc = v - mu
    var = jnp.mean(vc * vc, axis=-1, keepdims=True)
    v = (vc * lax.rsqrt(var + LN_EPS) * lng_ref[...] + lnb_ref[...]).astype(MXU_DTYPE)
    hd = d // SGU_HEADS
    causal = (lax.broadcasted_iota(jnp.int32, (SGU_CHUNK, SGU_CHUNK), 0)
              >= lax.broadcasted_iota(jnp.int32, (SGU_CHUNK, SGU_CHUNK), 1))
    mixed_rows = []
    w_masked = [jnp.where(causal, ws_ref[h], 0.0).astype(MXU_DTYPE) for h in range(SGU_HEADS)]
    for c in range(ts // SGU_CHUNK):
        rows = slice(c * SGU_CHUNK, (c + 1) * SGU_CHUNK)
        heads = []
        for h in range(SGU_HEADS):
            mixed = jnp.dot(w_masked[h], v[rows, h * hd:(h + 1) * hd], preferred_element_type=F32)
            heads.append(mixed + bst_ref[:, h:h + 1])
        mixed_rows.append(jnp.concatenate(heads, axis=1))
    mixed = jnp.concatenate(mixed_rows, axis=0) if len(mixed_rows) > 1 else mixed_rows[0]
    y_sgu = _mm(u * mixed, sguwo_ref[...])

    q = proj(3).astype(MXU_DTYPE)
    xd = d // XA_HEADS
    outs = []
    for h in range(XA_HEADS):
        cols = slice(h * xd, (h + 1) * xd)
        s = _mm_nt(q[:, cols], k_ref[0, :, cols]) * (xd ** -0.5)
        e = jnp.exp(s - jnp.max(s, axis=-1, keepdims=True))
        probs = e / jnp.sum(e, axis=-1, keepdims=True)
        outs.append(_mm(probs, v_ref[0, :, cols]))
    y_xa = _mm(jnp.concatenate(outs, axis=1), xawo_ref[...])

    merged = (jax.nn.sigmoid(proj(4)) * y_pool + jax.nn.sigmoid(proj(5)) * y_sgu
              + jax.nn.sigmoid(proj(6)) * y_xa)
    h_ref[0] = x + _mm(merged, wout_ref[...])


def _resident(shape):
    zeros = (0,) * len(shape)
    return pl.BlockSpec(shape, lambda *_: zeros, pipeline_mode=pl.Buffered(1))


def _mixer(x, k, v, weights, b0, nb):
    _, S, D = x.shape
    ts = min(SEQ_TILE, S)
    M = k.shape[1]
    return pl.pallas_call(
        _mixer_kernel,
        grid=(nb, S // ts),
        in_specs=[pl.BlockSpec((1, ts, D), lambda b, s: (b + b0, s, 0)),
                  pl.BlockSpec((1, M, k.shape[2]), lambda b, s: (b + b0, 0, 0)),
                  pl.BlockSpec((1, M, v.shape[2]), lambda b, s: (b + b0, 0, 0))]
                 + [_resident(w.shape) for w in weights],
        out_specs=pl.BlockSpec((1, ts, D), lambda b, s: (b, s, 0)),
        out_shape=jax.ShapeDtypeStruct((nb, S, D), F32),
        scratch_shapes=[pltpu.VMEM((POOL_HALO, D), F32)],
        compiler_params=pltpu.CompilerParams(dimension_semantics=("arbitrary", "arbitrary"),
                                             vmem_limit_bytes=VMEM_LIMIT_BYTES),
    )(x, k, v, *weights)


def _topk_rows(vals, payload=None):
    n_rows, width = vals.shape
    sub = TC_SUBLANES
    n_tiles = n_rows // sub
    assert n_rows % sub == 0
    tiles = [vals[j * sub:(j + 1) * sub, :] for j in range(n_tiles)]
    pay = None if payload is None else [payload[j * sub:(j + 1) * sub, :] for j in range(n_tiles)]
    sub_row = lax.broadcasted_iota(jnp.int32, (sub, width), 0).astype(F32)
    rows = [sub_row + float(j * sub) for j in range(n_tiles)]

    def best_of(items):
        while len(items) > 1:
            merged = []
            for a, b in zip(items[0::2], items[1::2]):
                keep_a = a[0] >= b[0]
                merged.append(tuple(jnp.where(keep_a, xa, xb) for xa, xb in zip(a, b)))
            if len(items) % 2:
                merged.append(items[-1])
            items = merged
        return items[0]

    out_v, out_i = [], []
    for _ in range(PEER_TOPK):
        if pay is None:
            v8, j8 = best_of([(tiles[j], float(j)) for j in range(n_tiles)])
        else:
            v8, j8, p8 = best_of([(tiles[j], float(j), pay[j]) for j in range(n_tiles)])
        r8 = j8 * float(sub) + sub_row
        m = jnp.max(v8, axis=0, keepdims=True)
        first = jnp.min(jnp.where(v8 == m, r8, float(n_rows)), axis=0, keepdims=True)
        out_v.append(m)
        if pay is None:
            out_i.append(first)
        else:
            out_i.append(jnp.max(jnp.where(r8 == first, p8, -1.0), axis=0, keepdims=True))
        tiles = [jnp.where(rows[j] == first, -jnp.inf, tiles[j]) for j in range(n_tiles)]
    return jnp.concatenate(out_v, axis=0), jnp.concatenate(out_i, axis=0)


def _candidates(v1, i1, v2, i2):
    sub = 8
    jrow = lax.broadcasted_iota(jnp.int32, (sub, 1), 0)
    vals = [v1[0:1] + v2]
    ids = [i1[0:1] * float(PEER_N_KEYS) + i2]
    for i in range(1, sub):
        keep = jrow < (PEER_TOPK // (i + 1))
        vals.append(jnp.where(keep, v1[i:i + 1] + v2[0:sub], -jnp.inf))
        ids.append(i1[i:i + 1] * float(PEER_N_KEYS) + i2[0:sub])
    vals.append(v1[sub:] + v2[0:1])
    ids.append(i1[sub:] * float(PEER_N_KEYS) + i2[0:1])
    return jnp.concatenate(vals, axis=0), jnp.concatenate(ids, axis=0)


def _select_kernel(h_ref, n2g_ref, wq_ref, k1_ref, k2_ref, xn_ref, exp_ref, gate_ref):
    xn = _rmsnorm(h_ref[...], n2g_ref[...])
    xn_ref[...] = _pack_halves(xn)
    q = _mm(xn, wq_ref[...])
    half = k1_ref.shape[1]
    experts, gates = [], []
    for h in range(PEER_HEADS):
        q1 = q[:, (2 * h) * half:(2 * h + 1) * half]
        q2 = q[:, (2 * h + 1) * half:(2 * h + 2) * half]
        v1, i1 = _topk_rows(_mm_nt(k1_ref[...], q1))
        v2, i2 = _topk_rows(_mm_nt(k2_ref[...], q2))
        best, eid = _topk_rows(*_candidates(v1, i1, v2, i2))
        e = jnp.exp(best - jnp.max(best, axis=0, keepdims=True))
        gates.append(e / jnp.sum(e, axis=0, keepdims=True))
        experts.append(eid)
    exp_ref[...] = jnp.concatenate(experts, axis=0).T.astype(jnp.int32)
    gate_ref[...] = jnp.concatenate(gates, axis=0).T


def _select(h, n2g, w_q, keys1, keys2):
    T, D = h.shape
    tt = min(SELECT_TILE, T)
    hk = PEER_HEADS * PEER_TOPK
    return pl.pallas_call(
        _select_kernel,
        grid=(T // tt,),
        in_specs=[pl.BlockSpec((tt, D), lambda i: (i, 0)),
                  _resident(n2g.shape), _resident(w_q.shape),
                  _resident(keys1.shape), _resident(keys2.shape)],
        out_specs=[pl.BlockSpec((tt, D // 2), lambda i: (i, 0)),
                   pl.BlockSpec((tt, hk), lambda i: (i, 0)),
                   pl.BlockSpec((tt, hk), lambda i: (i, 0))],
        out_shape=[jax.ShapeDtypeStruct((T, D // 2), I32),
                   jax.ShapeDtypeStruct((T, hk), jnp.int32),
                   jax.ShapeDtypeStruct((T, hk), F32)],
        compiler_params=pltpu.CompilerParams(dimension_semantics=("arbitrary",),
                                             vmem_limit_bytes=VMEM_LIMIT_BYTES),
    )(h, n2g, w_q, keys1, keys2)


def _pack_table(table):
    half = table.shape[1] // 2
    bits = lax.bitcast_convert_type(table.astype(jnp.bfloat16), jnp.uint16).astype(jnp.uint32)
    return lax.bitcast_convert_type((bits[:, half:] << 16) | bits[:, :half], I32)


def _bf16_bits(x):
    return lax.bitcast_convert_type(x.astype(jnp.bfloat16).astype(F32), I32)


def _pack_halves(x):
    half = x.shape[1] // 2
    return (_bf16_bits(x[:, half:]) & HIGH_HALF) | lax.shift_right_logical(_bf16_bits(x[:, :half]), 16)


def _pack_splat(x):
    bits = _bf16_bits(x)
    return (bits & HIGH_HALF) | lax.shift_right_logical(bits, 16)


def _unpack_words(w):
    lo = lax.bitcast_convert_type(lax.shift_left(w, jnp.full(w.shape, 16, I32)), F32)
    hi = lax.bitcast_convert_type(lax.bitwise_and(w, jnp.full(w.shape, HIGH_HALF, I32)), F32)
    return lo, hi


def _as_bf16(w):
    return plsc.bitcast(w, jnp.bfloat16)


def _halves_f32(p):
    return _unpack_words(plsc.bitcast(p, I32))


def _sc_kernel(body, out_type, scratch_types):
    mesh = plsc.VectorSubcoreMesh(core_axis_name="c", subcore_axis_name="s")
    return pl.kernel(body, out_type=out_type, mesh=mesh, scratch_types=scratch_types,
                     compiler_params=pltpu.CompilerParams(needs_layout_passes=False))


def _sc_worker_base(tokens_per_worker):
    return (lax.axis_index("s") * SC_CORES + lax.axis_index("c")) * tokens_per_worker


def _sc_pipeline(nchunk, nbuf, gather, compute):
    assert nchunk >= nbuf
    ahead = nbuf - 1
    for i in range(ahead):
        gather(i, i).start()

    @pl.loop(0, nchunk)
    def _(ci):
        @pl.when(ci + ahead < nchunk)
        def _():
            gather(ci + ahead, (ci + ahead) % nbuf).start()

        slot = ci % nbuf
        gather(ci, slot).wait()
        compute(ci, slot)


def _sc_dots(xp, experts, ptab):
    T, W = xp.shape
    HK = experts.shape[1]
    L, TG, CH = SC_LANES, SC_TOKENS, SC_CHUNK
    tpw = T // SC_WORKERS
    cpt = HK // CH
    nchunk = TG * cpt
    assert W == ptab.shape[1] and T % (SC_WORKERS * TG) == 0 and HK % CH == 0 and CH % L == 0
    assert W % (2 * L) == 0
    nbuf = SC_DOTS_NBUF

    def body(x_hbm, idx_hbm, tab_hbm, out_hbm, x_v, idx_v, out_v, rows_v, acc_v, sem):
        base = _sc_worker_base(tpw)
        lane = lax.iota(I32, L)

        def gather(ci, slot):
            tok, c = ci // cpt, ci % cpt
            return pltpu.make_async_copy(tab_hbm.at[idx_v.at[tok, pl.ds(c * CH, CH)]],
                                         rows_v.at[slot], sem.at[slot])

        def compute(ci, slot):
            tok, c = ci // cpt, ci % cpt
            for eg in range(CH // L):
                accs = [jnp.zeros((L,), F32) for _ in range(L)]
                for j in range(0, W // L, 2):
                    xa = _as_bf16(x_v[tok, pl.ds(j * L, L)])
                    xb = _as_bf16(x_v[tok, pl.ds((j + 1) * L, L)])
                    for kk in range(L):
                        ra = _as_bf16(rows_v[slot, eg * L + kk, pl.ds(j * L, L)])
                        rb = _as_bf16(rows_v[slot, eg * L + kk, pl.ds((j + 1) * L, L)])
                        lo, hi = _halves_f32(ra * xa + rb * xb)
                        accs[kk] = accs[kk] + lo + hi
                for kk in range(L):
                    acc_v[kk, :] = accs[kk]

                s = jnp.zeros((L,), F32)
                for j in range(L):
                    s = s + plsc.load_gather(acc_v, [lane, jnp.full((L,), j, I32)])
                out_v[tok, pl.ds(c * CH + eg * L, L)] = s

        @pl.loop(0, tpw // TG)
        def _(g):
            t0 = base + g * TG
            pltpu.sync_copy(x_hbm.at[pl.ds(t0, TG)], x_v)
            pltpu.sync_copy(idx_hbm.at[pl.ds(t0, TG)], idx_v)
            _sc_pipeline(nchunk, nbuf, gather, compute)
            pltpu.sync_copy(out_v, out_hbm.at[pl.ds(t0, TG)])

    return _sc_kernel(
        body, jax.ShapeDtypeStruct((T, HK), F32),
        [pltpu.VMEM((TG, W), I32), pltpu.VMEM((TG, HK), I32), pltpu.VMEM((TG, HK), F32),
         pltpu.VMEM((nbuf, CH, W), I32), pltpu.VMEM((L, L), F32),
         pltpu.SemaphoreType.DMA((nbuf,))],
    )(xp, experts, ptab)


def _sc_wsum(wp, experts, ptab):
    T, HK = wp.shape
    W = ptab.shape[1]
    D = 2 * W
    L, TG, CH = SC_LANES, SC_WSUM_TOKENS, SC_CHUNK
    tpw = T // SC_WORKERS
    cpt = HK // CH
    nchunk = TG * cpt
    DG = 8
    assert T % (SC_WORKERS * TG) == 0 and HK % CH == 0 and CH % 2 == 0 and W % (DG * L) == 0
    nbuf = SC_WSUM_NBUF

    def body(w_hbm, idx_hbm, tab_hbm, out_hbm, w_v, idx_v, y_v, rows_v, sem):
        base = _sc_worker_base(tpw)

        def gather(ci, slot):
            tok, c = ci // cpt, ci % cpt
            return pltpu.make_async_copy(tab_hbm.at[idx_v.at[tok, pl.ds(c * CH, CH)]],
                                         rows_v.at[slot], sem.at[slot])

        def compute(ci, slot):
            tok, c = ci // cpt, ci % cpt
            tokv = jnp.full((L,), tok, I32)
            cont = jnp.full((L,), c, I32) != 0

            @pl.loop(0, W // (DG * L))
            def _(dg):
                off = dg * (DG * L)
                zero = jnp.zeros((L,), F32)
                alo = [jnp.where(cont, y_v[tok, pl.ds(off + j * L, L)], zero) for j in range(DG)]
                ahi = [jnp.where(cont, y_v[tok, pl.ds(W + off + j * L, L)], zero) for j in range(DG)]
                for kk in range(0, CH, 2):
                    wa = _as_bf16(plsc.load_gather(w_v, [tokv, jnp.full((L,), c * CH + kk, I32)]))
                    wb = _as_bf16(plsc.load_gather(w_v, [tokv, jnp.full((L,), c * CH + kk + 1, I32)]))
                    for j in range(DG):
                        ra = _as_bf16(rows_v[slot, kk, pl.ds(off + j * L, L)])
                        rb = _as_bf16(rows_v[slot, kk + 1, pl.ds(off + j * L, L)])
                        lo, hi = _halves_f32(wa * ra + wb * rb)
                        alo[j] = alo[j] + lo
                        ahi[j] = ahi[j] + hi
                for j in range(DG):
                    y_v[tok, pl.ds(off + j * L, L)] = alo[j]
                    y_v[tok, pl.ds(W + off + j * L, L)] = ahi[j]

        @pl.loop(0, tpw // TG)
        def _(g):
            t0 = base + g * TG
            pltpu.sync_copy(w_hbm.at[pl.ds(t0, TG)], w_v)
            pltpu.sync_copy(idx_hbm.at[pl.ds(t0, TG)], idx_v)
            _sc_pipeline(nchunk, nbuf, gather, compute)
            pltpu.sync_copy(y_v, out_hbm.at[pl.ds(t0, TG)])

    return _sc_kernel(
        body, jax.ShapeDtypeStruct((T, D), F32),
        [pltpu.VMEM((TG, HK), I32), pltpu.VMEM((TG, HK), I32), pltpu.VMEM((TG, D), F32),
         pltpu.VMEM((nbuf, CH, W), I32), pltpu.SemaphoreType.DMA((nbuf,))],
    )(wp, experts, ptab)


def _tc_wsum_kernel(exp_ref, w_ref, tab_ref, y_ref):
    tb, hk = exp_ref.shape
    sub, lanes = tab_ref.shape[1:]
    for t in range(tb):
        lo_acc = jnp.zeros((sub, lanes), F32)
        hi_acc = jnp.zeros((sub, lanes), F32)
        for k in range(hk):
            row = tab_ref[exp_ref[t, k]]
            w = w_ref[t, k]
            lo_acc = lo_acc + w * lax.bitcast_convert_type(lax.shift_left(row, 16), F32)
            hi_acc = hi_acc + w * lax.bitcast_convert_type(row & HIGH_HALF, F32)
        y_ref[t, pl.ds(0, sub), :] = lo_acc
        y_ref[t, pl.ds(sub, sub), :] = hi_acc


def _tc_wsum(w, experts, ptab):
    T, hk = w.shape
    E, W = ptab.shape
    sub = W // TC_LANES
    tb = min(TC_WSUM_TILE, T)
    smem = pl.BlockSpec((tb, hk), lambda i: (i, 0), memory_space=pltpu.SMEM)
    y = pl.pallas_call(
        _tc_wsum_kernel, grid=(T // tb,),
        in_specs=[smem, smem, _resident((E, sub, TC_LANES))],
        out_specs=pl.BlockSpec((tb, 2 * sub, TC_LANES), lambda i: (i, 0, 0)),
        out_shape=jax.ShapeDtypeStruct((T, 2 * sub, TC_LANES), F32),
        cost_estimate=pl.CostEstimate(flops=4 * T * hk * W, transcendentals=0,
                                      bytes_accessed=4 * (T * hk * W + E * W + 2 * T * W + 2 * T * hk)),
        compiler_params=pltpu.CompilerParams(dimension_semantics=("arbitrary",),
                                             vmem_limit_bytes=VMEM_LIMIT_BYTES),
    )(experts, w, ptab.reshape(E, sub, TC_LANES))
    return y.reshape(T, 2 * W)


def _act_kernel(dots_ref, gate_ref, w_ref, wp_ref):
    w = _gelu(dots_ref[...]) * gate_ref[...]
    w_ref[...] = w
    wp_ref[...] = _pack_splat(w)


def _act(dots, gates):
    T, hk = dots.shape
    tt = min(ROW_TILE, T)
    spec = pl.BlockSpec((tt, hk), lambda i: (i, 0))
    return pl.pallas_call(
        _act_kernel, grid=(T // tt,), in_specs=[spec, spec], out_specs=[spec, spec],
        out_shape=[jax.ShapeDtypeStruct((T, hk), F32), jax.ShapeDtypeStruct((T, hk), I32)],
        compiler_params=pltpu.CompilerParams(dimension_semantics=("arbitrary",)),
    )(dots, gates)


def _final_kernel(h_ref, y_ref, g_ref, *rest):
    o_ref = rest[-1]
    o_ref[...] = _rmsnorm(h_ref[...] + y_ref[...], g_ref[...])


def _final(h, y, gain, out, row0, total_rows):
    T, D = h.shape
    tt = min(ROW_TILE, T)
    assert row0 % tt == 0
    spec = pl.BlockSpec((tt, D), lambda i: (i, 0))
    out_spec = pl.BlockSpec((tt, D), lambda i: (i + row0 // tt, 0))
    in_specs = [spec, spec, pl.BlockSpec((1, D), lambda i: (0, 0))]
    args = (h, y, gain)
    aliases = {}
    if out is not None:
        in_specs.append(pl.BlockSpec(memory_space=pl.ANY))
        args += (out,)
        aliases = {3: 0}
    return pl.pallas_call(
        _final_kernel, grid=(T // tt,), in_specs=in_specs, out_specs=out_spec,
        out_shape=jax.ShapeDtypeStruct((total_rows, D), F32),
        input_output_aliases=aliases,
        compiler_params=pltpu.CompilerParams(dimension_semantics=("arbitrary",)),
    )(*args)


def kernel(x, mem, norm1_gain, w_in, pool_w, pool_scale, sgu_ln_gain, sgu_ln_bias, sgu_w_s, sgu_b_s,
           sgu_w_out, mem_norm_gain, xa_w_kv, xa_w_out, w_out, norm2_gain, peer_w_q, peer_keys1,
           peer_keys2, peer_u, peer_v, final_norm_gain):
    B, S, D = x.shape
    depth = w_in.shape[0]
    lo = lambda w: w.astype(MXU_DTYPE)
    row = lambda w: w.reshape(1, -1)
    assert depth == 1, "the final RMSNorm is fused with the last layer's residual add"
    l = 0
    weights = (row(norm1_gain[l]), lo(w_in[l]), lo(pool_w[l]), row(pool_scale[l]),
               row(sgu_ln_gain[l]), row(sgu_ln_bias[l]), sgu_w_s[l], sgu_b_s[l].T,
               lo(sgu_w_out[l]), lo(xa_w_out[l]), lo(w_out[l]))
    select_w = (row(norm2_gain[l]), lo(peer_w_q[l]), lo(peer_keys1[l]), lo(peer_keys2[l]))
    u_packed, v_packed = _pack_table(peer_u[l]), _pack_table(peer_v[l])
    k, v = _memkv(mem, row(mem_norm_gain[l]), lo(xa_w_kv[l]))
    fgain = row(final_norm_gain)
    chunks = BATCH_CHUNKS if sum(BATCH_CHUNKS) == B else (B,)
    b0, tc_sums, tc_parts, sc_parts = 0, [], [], []
    for c, nb in enumerate(chunks):
        rows = nb * S
        h = _mixer(x, k, v, weights, b0, nb).reshape(rows, D)
        xn, experts, gates = _select(h, *select_w)
        dots = _sc_dots(xn, experts, u_packed)
        if c > 0:
            dots, tc_sums[c - 1] = lax.optimization_barrier((dots, tc_sums[c - 1]))
        w, wp = _act(dots, gates)
        num, den = TC_WSUM_SHARE[c] if len(chunks) == len(TC_WSUM_SHARE) else TC_WSUM_SHARE[-1]
        n_tc = (rows * num // den) // SC_ROW_QUANTUM * SC_ROW_QUANTUM
        tc_sums.append(_tc_wsum(w[:n_tc], experts[:n_tc], v_packed))
        tc_parts.append((h[:n_tc], b0 * S))
        sc_parts.append((h[n_tc:], _sc_wsum(wp[n_tc:], experts[n_tc:], v_packed), b0 * S + n_tc))
        b0 += nb
    parts = [(h_tc, y_tc, row0) for (h_tc, row0), y_tc in zip(tc_parts, tc_sums)]
    order = parts[:-1] + sc_parts[:-1] + parts[-1:] + sc_parts[-1:]
    out = None
    for h_part, y_part, row0 in order:
        out = _final(h_part, y_part, fgain, out, row0, B * S)
    return out.reshape(B, S, D)
```

```python
import math

import jax
import jax.numpy as jnp
from jax import lax
from jax.experimental import pallas as pl
from jax.experimental.pallas import tpu as pltpu
from jax.experimental.pallas import tpu_sc as plsc

F32 = jnp.float32
I32 = jnp.int32
MXU_DTYPE = jnp.bfloat16

RMS_EPS = 1e-6
LN_EPS = 1e-5
POOL_WINDOWS = (2, 4, 8, 16)
POOL_HALO = 16
SGU_CHUNK = 128
SGU_HEADS = 8
XA_HEADS = 4
PEER_HEADS = 8
PEER_N_KEYS = 128
PEER_TOPK = 16

V7X_VMEM_BYTES = 64 * 1024 * 1024
VMEM_LIMIT_BYTES = V7X_VMEM_BYTES - 8 * 1024 * 1024

SEQ_TILE = 256
SELECT_TILE = 256
ROW_TILE = 512
TC_WSUM_TILE = 8
TC_LANES = 128
TC_SUBLANES = 8
BATCH_CHUNKS = (1, 1, 2, 2, 2)

SC_CORES = 2
SC_SUBCORES = 16
SC_LANES = 16
SC_WORKERS = SC_CORES * SC_SUBCORES
SC_TOKENS = 32
SC_WSUM_TOKENS = 16
SC_CHUNK = 32
SC_DOTS_NBUF = 5
SC_WSUM_NBUF = 4
HIGH_HALF = -65536
SC_ROW_QUANTUM = SC_WORKERS * SC_WSUM_TOKENS
TC_WSUM_SHARE = (9, 16)


def _rmsnorm(x, gain):
    return x * lax.rsqrt(jnp.mean(x * x, axis=-1, keepdims=True) + RMS_EPS) * gain


def _gelu(x):
    return 0.5 * x * (1.0 + lax.erf(x * (1.0 / math.sqrt(2.0))))


def _mm(a, b):
    return jnp.dot(a.astype(MXU_DTYPE), b.astype(MXU_DTYPE), preferred_element_type=F32)


def _mm_nt(a, b):
    return lax.dot_general(a.astype(MXU_DTYPE), b.astype(MXU_DTYPE),
                           (((1,), (1,)), ((), ())), preferred_element_type=F32)


def _memkv_kernel(mem_ref, gain_ref, wkv_ref, k_ref, v_ref):
    width = k_ref.shape[-1]
    kv = _mm(_rmsnorm(mem_ref[0], gain_ref[...]), wkv_ref[...])
    k_ref[0] = kv[:, :width].astype(k_ref.dtype)
    v_ref[0] = kv[:, width:].astype(v_ref.dtype)


def _memkv(mem, gain, w_kv):
    B, M, D = mem.shape
    width = w_kv.shape[1] // 2
    const = lambda b: (0, 0)
    return pl.pallas_call(
        _memkv_kernel,
        grid=(B,),
        in_specs=[pl.BlockSpec((1, M, D), lambda b: (b, 0, 0)),
                  pl.BlockSpec((1, D), const),
                  pl.BlockSpec(w_kv.shape, const)],
        out_specs=[pl.BlockSpec((1, M, width), lambda b: (b, 0, 0))] * 2,
        out_shape=[jax.ShapeDtypeStruct((B, M, width), MXU_DTYPE)] * 2,
        compiler_params=pltpu.CompilerParams(dimension_semantics=("arbitrary",),
                                             vmem_limit_bytes=VMEM_LIMIT_BYTES),
    )(mem, gain, w_kv)


def _mixer_kernel(x_ref, k_ref, v_ref, n1g_ref, win_ref, poolw_ref, pools_ref, lng_ref, lnb_ref,
                  ws_ref, bst_ref, sguwo_ref, xawo_ref, wout_ref, h_ref, tail_ref):
    ts, d = x_ref.shape[1], x_ref.shape[2]
    s_idx = pl.program_id(1)
    x = x_ref[0]
    nb = _rmsnorm(x, n1g_ref[...]).astype(MXU_DTYPE)

    def proj(col):
        return jnp.dot(nb, win_ref[:, col * d:(col + 1) * d], preferred_element_type=F32)

    @pl.when(s_idx == 0)
    def _():
        tail_ref[...] = jnp.zeros_like(tail_ref)

    p = proj(0)
    ext = jnp.concatenate([tail_ref[...], p], axis=0)
    tail_ref[...] = p[ts - POOL_HALO:, :]
    pos = s_idx * ts + lax.broadcasted_iota(jnp.int32, (ts, 1), 0)
    group = d // len(POOL_WINDOWS)
    y_pool = []
    for g, w in enumerate(POOL_WINDOWS):
        acc = ext[:, g * group:(g + 1) * group]
        shift = 1
        while shift < w:
            acc = acc + pltpu.roll(acc, shift, 0)
            shift *= 2
        count = jnp.minimum(pos + 1, w).astype(F32)
        diff = acc[POOL_HALO:, :] / count - p[:, g * group:(g + 1) * group]
        y_pool.append(_mm(diff, poolw_ref[g]))
    y_pool = jnp.concatenate(y_pool, axis=1) * pools_ref[...]

    u = _gelu(proj(1))
    v = _gelu(proj(2))
    mu = jnp.mean(v, axis=-1, keepdims=True)
    vc = v - mu
    var = jnp.mean(vc * vc, axis=-1, keepdims=True)
    v = (vc * lax.rsqrt(var + LN_EPS) * lng_ref[...] + lnb_ref[...]).astype(MXU_DTYPE)
    hd = d // SGU_HEADS
    causal = (lax.broadcasted_iota(jnp.int32, (SGU_CHUNK, SGU_CHUNK), 0)
              >= lax.broadcasted_iota(jnp.int32, (SGU_CHUNK, SGU_CHUNK), 1))
    mixed_rows = []
    w_masked = [jnp.where(causal, ws_ref[h], 0.0).astype(MXU_DTYPE) for h in range(SGU_HEADS)]
    for c in range(ts // SGU_CHUNK):
        rows = slice(c * SGU_CHUNK, (c + 1) * SGU_CHUNK)
        heads = []
        for h in range(SGU_HEADS):
            mixed = jnp.dot(w_masked[h], v[rows, h * hd:(h + 1) * hd], preferred_element_type=F32)
            heads.append(mixed + bst_ref[:, h:h + 1])
        mixed_rows.append(jnp.concatenate(heads, axis=1))
    mixed = jnp.concatenate(mixed_rows, axis=0) if len(mixed_rows) > 1 else mixed_rows[0]
    y_sgu = _mm(u * mixed, sguwo_ref[...])

    q = proj(3).astype(MXU_DTYPE)
    xd = d // XA_HEADS
    outs = []
    for h in range(XA_HEADS):
        cols = slice(h * xd, (h + 1) * xd)
        s = _mm_nt(q[:, cols], k_ref[0, :, cols]) * (xd ** -0.5)
        e = jnp.exp(s - jnp.max(s, axis=-1, keepdims=True))
        probs = e / jnp.sum(e, axis=-1, keepdims=True)
        outs.append(_mm(probs, v_ref[0, :, cols]))
    y_xa = _mm(jnp.concatenate(outs, axis=1), xawo_ref[...])

    merged = (jax.nn.sigmoid(proj(4)) * y_pool + jax.nn.sigmoid(proj(5)) * y_sgu
              + jax.nn.sigmoid(proj(6)) * y_xa)
    h_ref[0] = x + _mm(merged, wout_ref[...])


def _resident(shape):
    zeros = (0,) * len(shape)
    return pl.BlockSpec(shape, lambda *_: zeros, pipeline_mode=pl.Buffered(1))


def _mixer(x, k, v, weights, b0, nb):
    _, S, D = x.shape
    ts = min(SEQ_TILE, S)
    M = k.shape[1]
    return pl.pallas_call(
        _mixer_kernel,
        grid=(nb, S // ts),
        in_specs=[pl.BlockSpec((1, ts, D), lambda b, s: (b + b0, s, 0)),
                  pl.BlockSpec((1, M, k.shape[2]), lambda b, s: (b + b0, 0, 0)),
                  pl.BlockSpec((1, M, v.shape[2]), lambda b, s: (b + b0, 0, 0))]
                 + [_resident(w.shape) for w in weights],
        out_specs=pl.BlockSpec((1, ts, D), lambda b, s: (b, s, 0)),
        out_shape=jax.ShapeDtypeStruct((nb, S, D), F32),
        scratch_shapes=[pltpu.VMEM((POOL_HALO, D), F32)],
        compiler_params=pltpu.CompilerParams(dimension_semantics=("arbitrary", "arbitrary"),
                                             vmem_limit_bytes=VMEM_LIMIT_BYTES),
    )(x, k, v, *weights)


def _topk_rows(vals, payload=None):
    n_rows, width = vals.shape
    sub = TC_SUBLANES
    n_tiles = n_rows // sub
    assert n_rows % sub == 0
    tiles = [vals[j * sub:(j + 1) * sub, :] for j in range(n_tiles)]
    pay = None if payload is None else [payload[j * sub:(j + 1) * sub, :] for j in range(n_tiles)]
    sub_row = lax.broadcasted_iota(jnp.int32, (sub, width), 0).astype(F32)
    rows = [sub_row + float(j * sub) for j in range(n_tiles)]

    def best_of(items):
        while len(items) > 1:
            merged = []
            for a, b in zip(items[0::2], items[1::2]):
                keep_a = a[0] >= b[0]
                merged.append(tuple(jnp.where(keep_a, xa, xb) for xa, xb in zip(a, b)))
            if len(items) % 2:
                merged.append(items[-1])
            items = merged
        return items[0]

    out_v, out_i = [], []
    for _ in range(PEER_TOPK):
        if pay is None:
            v8, j8 = best_of([(tiles[j], float(j)) for j in range(n_tiles)])
        else:
            v8, j8, p8 = best_of([(tiles[j], float(j), pay[j]) for j in range(n_tiles)])
        r8 = j8 * float(sub) + sub_row
        m = jnp.max(v8, axis=0, keepdims=True)
        first = jnp.min(jnp.where(v8 == m, r8, float(n_rows)), axis=0, keepdims=True)
        out_v.append(m)
        if pay is None:
            out_i.append(first)
        else:
            out_i.append(jnp.max(jnp.where(r8 == first, p8, -1.0), axis=0, keepdims=True))
        tiles = [jnp.where(rows[j] == first, -jnp.inf, tiles[j]) for j in range(n_tiles)]
    return jnp.concatenate(out_v, axis=0), jnp.concatenate(out_i, axis=0)


def _candidates(v1, i1, v2, i2):
    sub = 8
    jrow = lax.broadcasted_iota(jnp.int32, (sub, 1), 0)
    vals = [v1[0:1] + v2]
    ids = [i1[0:1] * float(PEER_N_KEYS) + i2]
    for i in range(1, sub):
        keep = jrow < (PEER_TOPK // (i + 1))
        vals.append(jnp.where(keep, v1[i:i + 1] + v2[0:sub], -jnp.inf))
        ids.append(i1[i:i + 1] * float(PEER_N_KEYS) + i2[0:sub])
    vals.append(v1[sub:] + v2[0:1])
    ids.append(i1[sub:] * float(PEER_N_KEYS) + i2[0:1])
    return jnp.concatenate(vals, axis=0), jnp.concatenate(ids, axis=0)


def _select_kernel(h_ref, n2g_ref, wq_ref, k1_ref, k2_ref, xn_ref, exp_ref, gate_ref):
    xn = _rmsnorm(h_ref[...], n2g_ref[...])
    xn_ref[...] = _pack_halves(xn)
    q = _mm(xn, wq_ref[...])
    half = k1_ref.shape[1]
    experts, gates = [], []
    for h in range(PEER_HEADS):
        q1 = q[:, (2 * h) * half:(2 * h + 1) * half]
        q2 = q[:, (2 * h + 1) * half:(2 * h + 2) * half]
        v1, i1 = _topk_rows(_mm_nt(k1_ref[...], q1))
        v2, i2 = _topk_rows(_mm_nt(k2_ref[...], q2))
        best, eid = _topk_rows(*_candidates(v1, i1, v2, i2))
        e = jnp.exp(best - jnp.max(best, axis=0, keepdims=True))
        gates.append(e / jnp.sum(e, axis=0, keepdims=True))
        experts.append(eid)
    exp_ref[...] = jnp.concatenate(experts, axis=0).T.astype(jnp.int32)
    gate_ref[...] = jnp.concatenate(gates, axis=0).T


def _select(h, n2g, w_q, keys1, keys2):
    T, D = h.shape
    tt = min(SELECT_TILE, T)
    hk = PEER_HEADS * PEER_TOPK
    return pl.pallas_call(
        _select_kernel,
        grid=(T // tt,),
        in_specs=[pl.BlockSpec((tt, D), lambda i: (i, 0)),
                  _resident(n2g.shape), _resident(w_q.shape),
                  _resident(keys1.shape), _resident(keys2.shape)],
        out_specs=[pl.BlockSpec((tt, D // 2), lambda i: (i, 0)),
                   pl.BlockSpec((tt, hk), lambda i: (i, 0)),
                   pl.BlockSpec((tt, hk), lambda i: (i, 0))],
        out_shape=[jax.ShapeDtypeStruct((T, D // 2), I32),
                   jax.ShapeDtypeStruct((T, hk), jnp.int32),
                   jax.ShapeDtypeStruct((T, hk), F32)],
        compiler_params=pltpu.CompilerParams(dimension_semantics=("arbitrary",),
                                             vmem_limit_bytes=VMEM_LIMIT_BYTES),
    )(h, n2g, w_q, keys1, keys2)


def _pack_table(table):
    half = table.shape[1] // 2
    bits = lax.bitcast_convert_type(table.astype(jnp.bfloat16), jnp.uint16).astype(jnp.uint32)
    return lax.bitcast_convert_type((bits[:, half:] << 16) | bits[:, :half], I32)


def _bf16_bits(x):
    return lax.bitcast_convert_type(x.astype(jnp.bfloat16).astype(F32), I32)


def _pack_halves(x):
    half = x.shape[1] // 2
    return (_bf16_bits(x[:, half:]) & HIGH_HALF) | lax.shift_right_logical(_bf16_bits(x[:, :half]), 16)


def _pack_splat(x):
    bits = _bf16_bits(x)
    return (bits & HIGH_HALF) | lax.shift_right_logical(bits, 16)


def _unpack_words(w):
    lo = lax.bitcast_convert_type(lax.shift_left(w, jnp.full(w.shape, 16, I32)), F32)
    hi = lax.bitcast_convert_type(lax.bitwise_and(w, jnp.full(w.shape, HIGH_HALF, I32)), F32)
    return lo, hi


def _as_bf16(w):
    return plsc.bitcast(w, jnp.bfloat16)


def _halves_f32(p):
    return _unpack_words(plsc.bitcast(p, I32))


def _sc_kernel(body, out_type, scratch_types):
    mesh = plsc.VectorSubcoreMesh(core_axis_name="c", subcore_axis_name="s")
    return pl.kernel(body, out_type=out_type, mesh=mesh, scratch_types=scratch_types,
                     compiler_params=pltpu.CompilerParams(needs_layout_passes=False))


def _sc_worker_base(tokens_per_worker):
    return (lax.axis_index("s") * SC_CORES + lax.axis_index("c")) * tokens_per_worker


def _sc_pipeline(nchunk, nbuf, gather, compute):
    assert nchunk >= nbuf
    ahead = nbuf - 1
    for i in range(ahead):
        gather(i, i).start()

    @pl.loop(0, nchunk)
    def _(ci):
        @pl.when(ci + ahead < nchunk)
        def _():
            gather(ci + ahead, (ci + ahead) % nbuf).start()

        slot = ci % nbuf
        gather(ci, slot).wait()
        compute(ci, slot)


def _sc_dots(xp, experts, ptab):
    T, W = xp.shape
    HK = experts.shape[1]
    L, TG, CH = SC_LANES, SC_TOKENS, SC_CHUNK
    tpw = T // SC_WORKERS
    cpt = HK // CH
    nchunk = TG * cpt
    assert W == ptab.shape[1] and T % (SC_WORKERS * TG) == 0 and HK % CH == 0 and CH % L == 0
    assert W % (2 * L) == 0
    nbuf = SC_DOTS_NBUF
    ahead = nbuf - 1
    ngroups = tpw // TG
    assert nchunk > ahead

    def body(x_hbm, idx_hbm, tab_hbm, out_hbm, x_v, idx_v, out_v, rows_v, acc_v, sem, sem_in):
        base = _sc_worker_base(tpw)
        lane = lax.iota(I32, L)

        def stage(g, gs):
            t0 = base + g * TG
            return (pltpu.make_async_copy(x_hbm.at[pl.ds(t0, TG)], x_v.at[gs], sem_in.at[0, gs]),
                    pltpu.make_async_copy(idx_hbm.at[pl.ds(t0, TG)], idx_v.at[gs], sem_in.at[1, gs]))

        def gather(gs, ci, slot):
            tok, c = ci // cpt, ci % cpt
            return pltpu.make_async_copy(tab_hbm.at[idx_v.at[gs, tok, pl.ds(c * CH, CH)]],
                                         rows_v.at[slot], sem.at[slot])

        def compute(gs, ci, slot):
            tok, c = ci // cpt, ci % cpt
            for eg in range(CH // L):
                accs = [jnp.zeros((L,), F32) for _ in range(L)]
                for j in range(0, W // L, 2):
                    xa = _as_bf16(x_v[gs, tok, pl.ds(j * L, L)])
                    xb = _as_bf16(x_v[gs, tok, pl.ds((j + 1) * L, L)])
                    for kk in range(L):
                        ra = _as_bf16(rows_v[slot, eg * L + kk, pl.ds(j * L, L)])
                        rb = _as_bf16(rows_v[slot, eg * L + kk, pl.ds((j + 1) * L, L)])
                        lo, hi = _halves_f32(ra * xa + rb * xb)
                        accs[kk] = accs[kk] + lo + hi
                for kk in range(L):
                    acc_v[kk, :] = accs[kk]

                s = jnp.zeros((L,), F32)
                for j in range(L):
                    s = s + plsc.load_gather(acc_v, [lane, jnp.full((L,), j, I32)])
                out_v[tok, pl.ds(c * CH + eg * L, L)] = s

        for cp in stage(0, 0):
            cp.start()
        for cp in stage(0, 0):
            cp.wait()
        for i in range(ahead):
            gather(0, i, i).start()

        @pl.loop(0, ngroups)
        def _(g):
            gs = g % 2
            more = g + 1 < ngroups

            @pl.when(more)
            def _():
                for cp in stage(g + 1, 1 - gs):
                    cp.start()

            @pl.loop(0, nchunk)
            def _(ci):
                gid = g * nchunk + ci
                nxt = ci + ahead

                @pl.when(nxt < nchunk)
                def _():
                    gather(gs, nxt, (gid + ahead) % nbuf).start()

                @pl.when(jnp.logical_and(nxt >= nchunk, more))
                def _():
                    @pl.when(nxt == nchunk)
                    def _():
                        for cp in stage(g + 1, 1 - gs):
                            cp.wait()

                    gather(1 - gs, nxt - nchunk, (gid + ahead) % nbuf).start()

                slot = gid % nbuf
                gather(gs, ci, slot).wait()
                compute(gs, ci, slot)

            pltpu.sync_copy(out_v, out_hbm.at[pl.ds(base + g * TG, TG)])

    return _sc_kernel(
        body, jax.ShapeDtypeStruct((T, HK), F32),
        [pltpu.VMEM((2, TG, W), I32), pltpu.VMEM((2, TG, HK), I32), pltpu.VMEM((TG, HK), F32),
         pltpu.VMEM((nbuf, CH, W), I32), pltpu.VMEM((L, L), F32),
         pltpu.SemaphoreType.DMA((nbuf,)), pltpu.SemaphoreType.DMA((2, 2))],
    )(xp, experts, ptab)


def _sc_wsum(wp, experts, ptab):
    T, HK = wp.shape
    W = ptab.shape[1]
    D = 2 * W
    L, TG, CH = SC_LANES, SC_WSUM_TOKENS, SC_CHUNK
    tpw = T // SC_WORKERS
    cpt = HK // CH
    nchunk = TG * cpt
    DG = 8
    assert T % (SC_WORKERS * TG) == 0 and HK % CH == 0 and CH % 2 == 0 and W % (DG * L) == 0
    nbuf = SC_WSUM_NBUF

    def body(w_hbm, idx_hbm, tab_hbm, out_hbm, w_v, idx_v, y_v, rows_v, sem):
        base = _sc_worker_base(tpw)

        def gather(ci, slot):
            tok, c = ci // cpt, ci % cpt
            return pltpu.make_async_copy(tab_hbm.at[idx_v.at[tok, pl.ds(c * CH, CH)]],
                                         rows_v.at[slot], sem.at[slot])

        def compute(ci, slot):
            tok, c = ci // cpt, ci % cpt
            tokv = jnp.full((L,), tok, I32)
            cont = jnp.full((L,), c, I32) != 0

            @pl.loop(0, W // (DG * L))
            def _(dg):
                off = dg * (DG * L)
                zero = jnp.zeros((L,), F32)
                alo = [jnp.where(cont, y_v[tok, pl.ds(off + j * L, L)], zero) for j in range(DG)]
                ahi = [jnp.where(cont, y_v[tok, pl.ds(W + off + j * L, L)], zero) for j in range(DG)]
                for kk in range(0, CH, 2):
                    wa = _as_bf16(plsc.load_gather(w_v, [tokv, jnp.full((L,), c * CH + kk, I32)]))
                    wb = _as_bf16(plsc.load_gather(w_v, [tokv, jnp.full((L,), c * CH + kk + 1, I32)]))
                    for j in range(DG):
                        ra = _as_bf16(rows_v[slot, kk, pl.ds(off + j * L, L)])
                        rb = _as_bf16(rows_v[slot, kk + 1, pl.ds(off + j * L, L)])
                        lo, hi = _halves_f32(wa * ra + wb * rb)
                        alo[j] = alo[j] + lo
                        ahi[j] = ahi[j] + hi
                for j in range(DG):
                    y_v[tok, pl.ds(off + j * L, L)] = alo[j]
                    y_v[tok, pl.ds(W + off + j * L, L)] = ahi[j]

        @pl.loop(0, tpw // TG)
        def _(g):
            t0 = base + g * TG
            pltpu.sync_copy(w_hbm.at[pl.ds(t0, TG)], w_v)
            pltpu.sync_copy(idx_hbm.at[pl.ds(t0, TG)], idx_v)
            _sc_pipeline(nchunk, nbuf, gather, compute)
            pltpu.sync_copy(y_v, out_hbm.at[pl.ds(t0, TG)])

    return _sc_kernel(
        body, jax.ShapeDtypeStruct((T, D), F32),
        [pltpu.VMEM((TG, HK), I32), pltpu.VMEM((TG, HK), I32), pltpu.VMEM((TG, D), F32),
         pltpu.VMEM((nbuf, CH, W), I32), pltpu.SemaphoreType.DMA((nbuf,))],
    )(wp, experts, ptab)


def _tc_wsum_kernel(exp_ref, w_ref, tab_ref, y_ref):
    tb, hk = exp_ref.shape
    sub, lanes = tab_ref.shape[1:]
    for t in range(tb):
        lo_acc = jnp.zeros((sub, lanes), F32)
        hi_acc = jnp.zeros((sub, lanes), F32)
        for k in range(hk):
            row = tab_ref[exp_ref[t, k]]
            w = w_ref[t, k]
            lo_acc = lo_acc + w * lax.bitcast_convert_type(lax.shift_left(row, 16), F32)
            hi_acc = hi_acc + w * lax.bitcast_convert_type(row & HIGH_HALF, F32)
        y_ref[t, pl.ds(0, sub), :] = lo_acc
        y_ref[t, pl.ds(sub, sub), :] = hi_acc


def _tc_wsum(w, experts, ptab):
    T, hk = w.shape
    E, W = ptab.shape
    sub = W // TC_LANES
    tb = min(TC_WSUM_TILE, T)
    smem = pl.BlockSpec((tb, hk), lambda i: (i, 0), memory_space=pltpu.SMEM)
    y = pl.pallas_call(
        _tc_wsum_kernel, grid=(T // tb,),
        in_specs=[smem, smem, _resident((E, sub, TC_LANES))],
        out_specs=pl.BlockSpec((tb, 2 * sub, TC_LANES), lambda i: (i, 0, 0)),
        out_shape=jax.ShapeDtypeStruct((T, 2 * sub, TC_LANES), F32),
        cost_estimate=pl.CostEstimate(flops=4 * T * hk * W, transcendentals=0,
                                      bytes_accessed=4 * (T * hk * W + E * W + 2 * T * W + 2 * T * hk)),
        compiler_params=pltpu.CompilerParams(dimension_semantics=("arbitrary",),
                                             vmem_limit_bytes=VMEM_LIMIT_BYTES),
    )(experts, w, ptab.reshape(E, sub, TC_LANES))
    return y.reshape(T, 2 * W)


def _act_kernel(dots_ref, gate_ref, w_ref, wp_ref):
    w = _gelu(dots_ref[...]) * gate_ref[...]
    w_ref[...] = w
    wp_ref[...] = _pack_splat(w)


def _act(dots, gates):
    T, hk = dots.shape
    tt = min(ROW_TILE, T)
    spec = pl.BlockSpec((tt, hk), lambda i: (i, 0))
    return pl.pallas_call(
        _act_kernel, grid=(T // tt,), in_specs=[spec, spec], out_specs=[spec, spec],
        out_shape=[jax.ShapeDtypeStruct((T, hk), F32), jax.ShapeDtypeStruct((T, hk), I32)],
        compiler_params=pltpu.CompilerParams(dimension_semantics=("arbitrary",)),
    )(dots, gates)


def _final_kernel(h_ref, y_ref, g_ref, *rest):
    o_ref = rest[-1]
    o_ref[...] = _rmsnorm(h_ref[...] + y_ref[...], g_ref[...])


def _final(h, y, gain, out, row0, total_rows):
    T, D = h.shape
    tt = min(ROW_TILE, T)
    assert row0 % tt == 0
    spec = pl.BlockSpec((tt, D), lambda i: (i, 0))
    out_spec = pl.BlockSpec((tt, D), lambda i: (i + row0 // tt, 0))
    in_specs = [spec, spec, pl.BlockSpec((1, D), lambda i: (0, 0))]
    args = (h, y, gain)
    aliases = {}
    if out is not None:
        in_specs.append(pl.BlockSpec(memory_space=pl.ANY))
        args += (out,)
        aliases = {3: 0}
    return pl.pallas_call(
        _final_kernel, grid=(T // tt,), in_specs=in_specs, out_specs=out_spec,
        out_shape=jax.ShapeDtypeStruct((total_rows, D), F32),
        input_output_aliases=aliases,
        compiler_params=pltpu.CompilerParams(dimension_semantics=("arbitrary",)),
    )(*args)


def kernel(x, mem, norm1_gain, w_in, pool_w, pool_scale, sgu_ln_gain, sgu_ln_bias, sgu_w_s, sgu_b_s,
           sgu_w_out, mem_norm_gain, xa_w_kv, xa_w_out, w_out, norm2_gain, peer_w_q, peer_keys1,
           peer_keys2, peer_u, peer_v, final_norm_gain):
    B, S, D = x.shape
    depth = w_in.shape[0]
    lo = lambda w: w.astype(MXU_DTYPE)
    row = lambda w: w.reshape(1, -1)
    assert depth == 1, "the final RMSNorm is fused with the last layer's residual add"
    l = 0
    weights = (row(norm1_gain[l]), lo(w_in[l]), lo(pool_w[l]), row(pool_scale[l]),
               row(sgu_ln_gain[l]), row(sgu_ln_bias[l]), sgu_w_s[l], sgu_b_s[l].T,
               lo(sgu_w_out[l]), lo(xa_w_out[l]), lo(w_out[l]))
    select_w = (row(norm2_gain[l]), lo(peer_w_q[l]), lo(peer_keys1[l]), lo(peer_keys2[l]))
    u_packed, v_packed = _pack_table(peer_u[l]), _pack_table(peer_v[l])
    k, v = _memkv(mem, row(mem_norm_gain[l]), lo(xa_w_kv[l]))
    fgain = row(final_norm_gain)
    chunks = BATCH_CHUNKS if sum(BATCH_CHUNKS) == B else (B,)
    b0, tc_sums, tc_parts, sc_parts = 0, [], [], []
    for c, nb in enumerate(chunks):
        rows = nb * S
        h = _mixer(x, k, v, weights, b0, nb).reshape(rows, D)
        xn, experts, gates = _select(h, *select_w)
        dots = _sc_dots(xn, experts, u_packed)
        if c > 0:
            dots, tc_sums[c - 1] = lax.optimization_barrier((dots, tc_sums[c - 1]))
        w, wp = _act(dots, gates)
        n_tc = (rows * TC_WSUM_SHARE[0] // TC_WSUM_SHARE[1]) // SC_ROW_QUANTUM * SC_ROW_QUANTUM
        tc_sums.append(_tc_wsum(w[:n_tc], experts[:n_tc], v_packed))
        tc_parts.append((h[:n_tc], b0 * S))
        sc_parts.append((h[n_tc:], _sc_wsum(wp[n_tc:], experts[n_tc:], v_packed), b0 * S + n_tc))
        b0 += nb
    parts = [(h_tc, y_tc, row0) for (h_tc, row0), y_tc in zip(tc_parts, tc_sums)]
    order = parts[:-1] + sc_parts[:-1] + parts[-1:] + sc_parts[-1:]
    out = None
    for h_part, y_part, row0 in order:
        out = _final(h_part, y_part, fgain, out, row0, B * S)
    return out.reshape(B, S, D)
```

```python
import math

import jax
import jax.numpy as jnp
from jax import lax
from jax.experimental import pallas as pl
from jax.experimental.pallas import tpu as pltpu
from jax.experimental.pallas import tpu_sc as plsc

F32 = jnp.float32
I32 = jnp.int32
MXU_DTYPE = jnp.bfloat16

RMS_EPS = 1e-6
LN_EPS = 1e-5
POOL_WINDOWS = (2, 4, 8, 16)
POOL_HALO = 16
SGU_CHUNK = 128
SGU_HEADS = 8
XA_HEADS = 4
PEER_HEADS = 8
PEER_N_KEYS = 128
PEER_TOPK = 16

V7X_VMEM_BYTES = 64 * 1024 * 1024
VMEM_LIMIT_BYTES = V7X_VMEM_BYTES - 8 * 1024 * 1024

SEQ_TILE = 256
SELECT_TILE = 256
ROW_TILE = 512
TC_WSUM_TILE = 8
TC_LANES = 128
TC_SUBLANES = 8
BATCH_CHUNKS = (1, 1, 2, 2, 2)

SC_CORES = 2
SC_SUBCORES = 16
SC_LANES = 16
SC_WORKERS = SC_CORES * SC_SUBCORES
SC_TOKENS = 32
SC_WSUM_TOKENS = 16
SC_CHUNK = 32
SC_DOTS_NBUF = 5
SC_WSUM_NBUF = 4
HIGH_HALF = -65536
SC_ROW_QUANTUM = SC_WORKERS * SC_WSUM_TOKENS
TC_WSUM_SHARE = (9, 16)
TC_WSUM_SHARE_LAST = (1, 2)


def _rmsnorm(x, gain):
    return x * lax.rsqrt(jnp.mean(x * x, axis=-1, keepdims=True) + RMS_EPS) * gain


def _gelu(x):
    return 0.5 * x * (1.0 + lax.erf(x * (1.0 / math.sqrt(2.0))))


def _mm(a, b):
    return jnp.dot(a.astype(MXU_DTYPE), b.astype(MXU_DTYPE), preferred_element_type=F32)


def _mm_nt(a, b):
    return lax.dot_general(a.astype(MXU_DTYPE), b.astype(MXU_DTYPE),
                           (((1,), (1,)), ((), ())), preferred_element_type=F32)


def _memkv_kernel(mem_ref, gain_ref, wkv_ref, k_ref, v_ref):
    width = k_ref.shape[-1]
    kv = _mm(_rmsnorm(mem_ref[0], gain_ref[...]), wkv_ref[...])
    k_ref[0] = kv[:, :width].astype(k_ref.dtype)
    v_ref[0] = kv[:, width:].astype(v_ref.dtype)


def _memkv(mem, gain, w_kv):
    B, M, D = mem.shape
    width = w_kv.shape[1] // 2
    const = lambda b: (0, 0)
    return pl.pallas_call(
        _memkv_kernel,
        grid=(B,),
        in_specs=[pl.BlockSpec((1, M, D), lambda b: (b, 0, 0)),
                  pl.BlockSpec((1, D), const),
                  pl.BlockSpec(w_kv.shape, const)],
        out_specs=[pl.BlockSpec((1, M, width), lambda b: (b, 0, 0))] * 2,
        out_shape=[jax.ShapeDtypeStruct((B, M, width), MXU_DTYPE)] * 2,
        compiler_params=pltpu.CompilerParams(dimension_semantics=("arbitrary",),
                                             vmem_limit_bytes=VMEM_LIMIT_BYTES),
    )(mem, gain, w_kv)


def _mixer_kernel(x_ref, k_ref, v_ref, n1g_ref, win_ref, poolw_ref, pools_ref, lng_ref, lnb_ref,
                  ws_ref, bst_ref, sguwo_ref, xawo_ref, wout_ref, h_ref, tail_ref):
    ts, d = x_ref.shape[1], x_ref.shape[2]
    s_idx = pl.program_id(1)
    x = x_ref[0]
    nb = _rmsnorm(x, n1g_ref[...]).astype(MXU_DTYPE)

    def proj(col):
        return jnp.dot(nb, win_ref[:, col * d:(col + 1) * d], preferred_element_type=F32)

    @pl.when(s_idx == 0)
    def _():
        tail_ref[...] = jnp.zeros_like(tail_ref)

    p = proj(0)
    ext = jnp.concatenate([tail_ref[...], p], axis=0)
    tail_ref[...] = p[ts - POOL_HALO:, :]
    pos = s_idx * ts + lax.broadcasted_iota(jnp.int32, (ts, 1), 0)
    group = d // len(POOL_WINDOWS)
    y_pool = []
    for g, w in enumerate(POOL_WINDOWS):
        acc = ext[:, g * group:(g + 1) * group]
        shift = 1
        while shift < w:
            acc = acc + pltpu.roll(acc, shift, 0)
            shift *= 2
        count = jnp.minimum(pos + 1, w).astype(F32)
        diff = acc[POOL_HALO:, :] / count - p[:, g * group:(g + 1) * group]
        y_pool.append(_mm(diff, poolw_ref[g]))
    y_pool = jnp.concatenate(y_pool, axis=1) * pools_ref[...]

    u = _gelu(proj(1))
    v = _gelu(proj(2))
    mu = jnp.mean(v, axis=-1, keepdims=True)
    vc = v - mu
    var = jnp.mean(vc * vc, axis=-1, keepdims=True)
    v = (vc * lax.rsqrt(var + LN_EPS) * lng_ref[...] + lnb_ref[...]).astype(MXU_DTYPE)
    hd = d // SGU_HEADS
    causal = (lax.broadcasted_iota(jnp.int32, (SGU_CHUNK, SGU_CHUNK), 0)
              >= lax.broadcasted_iota(jnp.int32, (SGU_CHUNK, SGU_CHUNK), 1))
    mixed_rows = []
    w_masked = [jnp.where(causal, ws_ref[h], 0.0).astype(MXU_DTYPE) for h in range(SGU_HEADS)]
    for c in range(ts // SGU_CHUNK):
        rows = slice(c * SGU_CHUNK, (c + 1) * SGU_CHUNK)
        heads = []
        for h in range(SGU_HEADS):
            mixed = jnp.dot(w_masked[h], v[rows, h * hd:(h + 1) * hd], preferred_element_type=F32)
            heads.append(mixed + bst_ref[:, h:h + 1])
        mixed_rows.append(jnp.concatenate(heads, axis=1))
    mixed = jnp.concatenate(mixed_rows, axis=0) if len(mixed_rows) > 1 else mixed_rows[0]
    y_sgu = _mm(u * mixed, sguwo_ref[...])

    q = proj(3).astype(MXU_DTYPE)
    xd = d // XA_HEADS
    outs = []
    for h in range(XA_HEADS):
        cols = slice(h * xd, (h + 1) * xd)
        s = _mm_nt(q[:, cols], k_ref[0, :, cols]) * (xd ** -0.5)
        e = jnp.exp(s - jnp.max(s, axis=-1, keepdims=True))
        probs = e / jnp.sum(e, axis=-1, keepdims=True)
        outs.append(_mm(probs, v_ref[0, :, cols]))
    y_xa = _mm(jnp.concatenate(outs, axis=1), xawo_ref[...])

    merged = (jax.nn.sigmoid(proj(4)) * y_pool + jax.nn.sigmoid(proj(5)) * y_sgu
              + jax.nn.sigmoid(proj(6)) * y_xa)
    h_ref[0] = x + _mm(merged, wout_ref[...])


def _resident(shape):
    zeros = (0,) * len(shape)
    return pl.BlockSpec(shape, lambda *_: zeros, pipeline_mode=pl.Buffered(1))


def _mixer(x, k, v, weights, b0, nb):
    _, S, D = x.shape
    ts = min(SEQ_TILE, S)
    M = k.shape[1]
    return pl.pallas_call(
        _mixer_kernel,
        grid=(nb, S // ts),
        in_specs=[pl.BlockSpec((1, ts, D), lambda b, s: (b + b0, s, 0)),
                  pl.BlockSpec((1, M, k.shape[2]), lambda b, s: (b + b0, 0, 0)),
                  pl.BlockSpec((1, M, v.shape[2]), lambda b, s: (b + b0, 0, 0))]
                 + [_resident(w.shape) for w in weights],
        out_specs=pl.BlockSpec((1, ts, D), lambda b, s: (b, s, 0)),
        out_shape=jax.ShapeDtypeStruct((nb, S, D), F32),
        scratch_shapes=[pltpu.VMEM((POOL_HALO, D), F32)],
        compiler_params=pltpu.CompilerParams(dimension_semantics=("arbitrary", "arbitrary"),
                                             vmem_limit_bytes=VMEM_LIMIT_BYTES),
    )(x, k, v, *weights)


def _topk_rows(vals, payload=None):
    n_rows, width = vals.shape
    sub = TC_SUBLANES
    n_tiles = n_rows // sub
    assert n_rows % sub == 0
    tiles = [vals[j * sub:(j + 1) * sub, :] for j in range(n_tiles)]
    pay = None if payload is None else [payload[j * sub:(j + 1) * sub, :] for j in range(n_tiles)]
    sub_row = lax.broadcasted_iota(jnp.int32, (sub, width), 0).astype(F32)
    rows = [sub_row + float(j * sub) for j in range(n_tiles)]

    def best_of(items):
        while len(items) > 1:
            merged = []
            for a, b in zip(items[0::2], items[1::2]):
                keep_a = a[0] >= b[0]
                merged.append(tuple(jnp.where(keep_a, xa, xb) for xa, xb in zip(a, b)))
            if len(items) % 2:
                merged.append(items[-1])
            items = merged
        return items[0]

    out_v, out_i = [], []
    for _ in range(PEER_TOPK):
        if pay is None:
            v8, j8 = best_of([(tiles[j], float(j)) for j in range(n_tiles)])
        else:
            v8, j8, p8 = best_of([(tiles[j], float(j), pay[j]) for j in range(n_tiles)])
        r8 = j8 * float(sub) + sub_row
        m = jnp.max(v8, axis=0, keepdims=True)
        first = jnp.min(jnp.where(v8 == m, r8, float(n_rows)), axis=0, keepdims=True)
        out_v.append(m)
        if pay is None:
            out_i.append(first)
        else:
            out_i.append(jnp.max(jnp.where(r8 == first, p8, -1.0), axis=0, keepdims=True))
        tiles = [jnp.where(rows[j] == first, -jnp.inf, tiles[j]) for j in range(n_tiles)]
    return jnp.concatenate(out_v, axis=0), jnp.concatenate(out_i, axis=0)


def _candidates(v1, i1, v2, i2):
    sub = 8
    jrow = lax.broadcasted_iota(jnp.int32, (sub, 1), 0)
    vals = [v1[0:1] + v2]
    ids = [i1[0:1] * float(PEER_N_KEYS) + i2]
    for i in range(1, sub):
        keep = jrow < (PEER_TOPK // (i + 1))
        vals.append(jnp.where(keep, v1[i:i + 1] + v2[0:sub], -jnp.inf))
        ids.append(i1[i:i + 1] * float(PEER_N_KEYS) + i2[0:sub])
    vals.append(v1[sub:] + v2[0:1])
    ids.append(i1[sub:] * float(PEER_N_KEYS) + i2[0:1])
    return jnp.concatenate(vals, axis=0), jnp.concatenate(ids, axis=0)


def _select_kernel(h_ref, n2g_ref, wq_ref, k1_ref, k2_ref, xn_ref, exp_ref, gate_ref):
    xn = _rmsnorm(h_ref[...], n2g_ref[...])
    xn_ref[...] = _pack_halves(xn)
    q = _mm(xn, wq_ref[...])
    half = k1_ref.shape[1]
    experts, gates = [], []
    for h in range(PEER_HEADS):
        q1 = q[:, (2 * h) * half:(2 * h + 1) * half]
        q2 = q[:, (2 * h + 1) * half:(2 * h + 2) * half]
        v1, i1 = _topk_rows(_mm_nt(k1_ref[...], q1))
        v2, i2 = _topk_rows(_mm_nt(k2_ref[...], q2))
        best, eid = _topk_rows(*_candidates(v1, i1, v2, i2))
        e = jnp.exp(best - jnp.max(best, axis=0, keepdims=True))
        gates.append(e / jnp.sum(e, axis=0, keepdims=True))
        experts.append(eid)
    exp_ref[...] = jnp.concatenate(experts, axis=0).T.astype(jnp.int32)
    gate_ref[...] = jnp.concatenate(gates, axis=0).T


def _select(h, n2g, w_q, keys1, keys2):
    T, D = h.shape
    tt = min(SELECT_TILE, T)
    hk = PEER_HEADS * PEER_TOPK
    return pl.pallas_call(
        _select_kernel,
        grid=(T // tt,),
        in_specs=[pl.BlockSpec((tt, D), lambda i: (i, 0)),
                  _resident(n2g.shape), _resident(w_q.shape),
                  _resident(keys1.shape), _resident(keys2.shape)],
        out_specs=[pl.BlockSpec((tt, D // 2), lambda i: (i, 0)),
                   pl.BlockSpec((tt, hk), lambda i: (i, 0)),
                   pl.BlockSpec((tt, hk), lambda i: (i, 0))],
        out_shape=[jax.ShapeDtypeStruct((T, D // 2), I32),
                   jax.ShapeDtypeStruct((T, hk), jnp.int32),
                   jax.ShapeDtypeStruct((T, hk), F32)],
        compiler_params=pltpu.CompilerParams(dimension_semantics=("arbitrary",),
                                             vmem_limit_bytes=VMEM_LIMIT_BYTES),
    )(h, n2g, w_q, keys1, keys2)


def _pack_table(table):
    half = table.shape[1] // 2
    bits = lax.bitcast_convert_type(table.astype(jnp.bfloat16), jnp.uint16).astype(jnp.uint32)
    return lax.bitcast_convert_type((bits[:, half:] << 16) | bits[:, :half], I32)


def _bf16_bits(x):
    return lax.bitcast_convert_type(x.astype(jnp.bfloat16).astype(F32), I32)


def _pack_halves(x):
    half = x.shape[1] // 2
    return (_bf16_bits(x[:, half:]) & HIGH_HALF) | lax.shift_right_logical(_bf16_bits(x[:, :half]), 16)


def _pack_splat(x):
    bits = _bf16_bits(x)
    return (bits & HIGH_HALF) | lax.shift_right_logical(bits, 16)


def _unpack_words(w):
    lo = lax.bitcast_convert_type(lax.shift_left(w, jnp.full(w.shape, 16, I32)), F32)
    hi = lax.bitcast_convert_type(lax.bitwise_and(w, jnp.full(w.shape, HIGH_HALF, I32)), F32)
    return lo, hi


def _as_bf16(w):
    return plsc.bitcast(w, jnp.bfloat16)


def _halves_f32(p):
    return _unpack_words(plsc.bitcast(p, I32))


def _sc_kernel(body, out_type, scratch_types):
    mesh = plsc.VectorSubcoreMesh(core_axis_name="c", subcore_axis_name="s")
    return pl.kernel(body, out_type=out_type, mesh=mesh, scratch_types=scratch_types,
                     compiler_params=pltpu.CompilerParams(needs_layout_passes=False))


def _sc_worker_base(tokens_per_worker):
    return (lax.axis_index("s") * SC_CORES + lax.axis_index("c")) * tokens_per_worker


def _sc_pipeline(nchunk, nbuf, gather, compute):
    assert nchunk >= nbuf
    ahead = nbuf - 1
    for i in range(ahead):
        gather(i, i).start()

    @pl.loop(0, nchunk)
    def _(ci):
        @pl.when(ci + ahead < nchunk)
        def _():
            gather(ci + ahead, (ci + ahead) % nbuf).start()

        slot = ci % nbuf
        gather(ci, slot).wait()
        compute(ci, slot)


def _sc_dots(xp, experts, ptab):
    T, W = xp.shape
    HK = experts.shape[1]
    L, TG, CH = SC_LANES, SC_TOKENS, SC_CHUNK
    tpw = T // SC_WORKERS
    cpt = HK // CH
    nchunk = TG * cpt
    assert W == ptab.shape[1] and T % (SC_WORKERS * TG) == 0 and HK % CH == 0 and CH % L == 0
    assert W % (2 * L) == 0
    nbuf = SC_DOTS_NBUF
    ahead = nbuf - 1
    ngroups = tpw // TG
    assert nchunk > ahead

    def body(x_hbm, idx_hbm, tab_hbm, out_hbm, x_v, idx_v, out_v, rows_v, acc_v, sem, sem_in):
        base = _sc_worker_base(tpw)
        lane = lax.iota(I32, L)

        def stage(g, gs):
            t0 = base + g * TG
            return (pltpu.make_async_copy(x_hbm.at[pl.ds(t0, TG)], x_v.at[gs], sem_in.at[0, gs]),
                    pltpu.make_async_copy(idx_hbm.at[pl.ds(t0, TG)], idx_v.at[gs], sem_in.at[1, gs]))

        def gather(gs, ci, slot):
            tok, c = ci // cpt, ci % cpt
            return pltpu.make_async_copy(tab_hbm.at[idx_v.at[gs, tok, pl.ds(c * CH, CH)]],
                                         rows_v.at[slot], sem.at[slot])

        def compute(gs, ci, slot):
            tok, c = ci // cpt, ci % cpt
            for eg in range(CH // L):
                accs = [jnp.zeros((L,), F32) for _ in range(L)]
                for j in range(0, W // L, 2):
                    xa = _as_bf16(x_v[gs, tok, pl.ds(j * L, L)])
                    xb = _as_bf16(x_v[gs, tok, pl.ds((j + 1) * L, L)])
                    for kk in range(L):
                        ra = _as_bf16(rows_v[slot, eg * L + kk, pl.ds(j * L, L)])
                        rb = _as_bf16(rows_v[slot, eg * L + kk, pl.ds((j + 1) * L, L)])
                        lo, hi = _halves_f32(ra * xa + rb * xb)
                        accs[kk] = accs[kk] + lo + hi
                for kk in range(L):
                    acc_v[kk, :] = accs[kk]

                s = jnp.zeros((L,), F32)
                for j in range(L):
                    s = s + plsc.load_gather(acc_v, [lane, jnp.full((L,), j, I32)])
                out_v[tok, pl.ds(c * CH + eg * L, L)] = s

        for cp in stage(0, 0):
            cp.start()
        for cp in stage(0, 0):
            cp.wait()
        for i in range(ahead):
            gather(0, i, i).start()

        @pl.loop(0, ngroups)
        def _(g):
            gs = g % 2
            more = g + 1 < ngroups

            @pl.when(more)
            def _():
                for cp in stage(g + 1, 1 - gs):
                    cp.start()

            @pl.loop(0, nchunk)
            def _(ci):
                gid = g * nchunk + ci
                nxt = ci + ahead

                @pl.when(nxt < nchunk)
                def _():
                    gather(gs, nxt, (gid + ahead) % nbuf).start()

                @pl.when(jnp.logical_and(nxt >= nchunk, more))
                def _():
                    @pl.when(nxt == nchunk)
                    def _():
                        for cp in stage(g + 1, 1 - gs):
                            cp.wait()

                    gather(1 - gs, nxt - nchunk, (gid + ahead) % nbuf).start()

                slot = gid % nbuf
                gather(gs, ci, slot).wait()
                compute(gs, ci, slot)

            pltpu.sync_copy(out_v, out_hbm.at[pl.ds(base + g * TG, TG)])

    return _sc_kernel(
        body, jax.ShapeDtypeStruct((T, HK), F32),
        [pltpu.VMEM((2, TG, W), I32), pltpu.VMEM((2, TG, HK), I32), pltpu.VMEM((TG, HK), F32),
         pltpu.VMEM((nbuf, CH, W), I32), pltpu.VMEM((L, L), F32),
         pltpu.SemaphoreType.DMA((nbuf,)), pltpu.SemaphoreType.DMA((2, 2))],
    )(xp, experts, ptab)


def _sc_wsum(wp, experts, ptab):
    T, HK = wp.shape
    W = ptab.shape[1]
    D = 2 * W
    L, TG, CH = SC_LANES, SC_WSUM_TOKENS, SC_CHUNK
    tpw = T // SC_WORKERS
    cpt = HK // CH
    nchunk = TG * cpt
    DG = 8
    assert T % (SC_WORKERS * TG) == 0 and HK % CH == 0 and CH % 2 == 0 and W % (DG * L) == 0
    nbuf = SC_WSUM_NBUF

    def body(w_hbm, idx_hbm, tab_hbm, out_hbm, w_v, idx_v, y_v, rows_v, sem):
        base = _sc_worker_base(tpw)

        def gather(ci, slot):
            tok, c = ci // cpt, ci % cpt
            return pltpu.make_async_copy(tab_hbm.at[idx_v.at[tok, pl.ds(c * CH, CH)]],
                                         rows_v.at[slot], sem.at[slot])

        def compute(ci, slot):
            tok, c = ci // cpt, ci % cpt
            tokv = jnp.full((L,), tok, I32)
            cont = jnp.full((L,), c, I32) != 0

            @pl.loop(0, W // (DG * L))
            def _(dg):
                off = dg * (DG * L)
                zero = jnp.zeros((L,), F32)
                alo = [jnp.where(cont, y_v[tok, pl.ds(off + j * L, L)], zero) for j in range(DG)]
                ahi = [jnp.where(cont, y_v[tok, pl.ds(W + off + j * L, L)], zero) for j in range(DG)]
                for kk in range(0, CH, 2):
                    wa = _as_bf16(plsc.load_gather(w_v, [tokv, jnp.full((L,), c * CH + kk, I32)]))
                    wb = _as_bf16(plsc.load_gather(w_v, [tokv, jnp.full((L,), c * CH + kk + 1, I32)]))
                    for j in range(DG):
                        ra = _as_bf16(rows_v[slot, kk, pl.ds(off + j * L, L)])
                        rb = _as_bf16(rows_v[slot, kk + 1, pl.ds(off + j * L, L)])
                        lo, hi = _halves_f32(wa * ra + wb * rb)
                        alo[j] = alo[j] + lo
                        ahi[j] = ahi[j] + hi
                for j in range(DG):
                    y_v[tok, pl.ds(off + j * L, L)] = alo[j]
                    y_v[tok, pl.ds(W + off + j * L, L)] = ahi[j]

        @pl.loop(0, tpw // TG)
        def _(g):
            t0 = base + g * TG
            pltpu.sync_copy(w_hbm.at[pl.ds(t0, TG)], w_v)
            pltpu.sync_copy(idx_hbm.at[pl.ds(t0, TG)], idx_v)
            _sc_pipeline(nchunk, nbuf, gather, compute)
            pltpu.sync_copy(y_v, out_hbm.at[pl.ds(t0, TG)])

    return _sc_kernel(
        body, jax.ShapeDtypeStruct((T, D), F32),
        [pltpu.VMEM((TG, HK), I32), pltpu.VMEM((TG, HK), I32), pltpu.VMEM((TG, D), F32),
         pltpu.VMEM((nbuf, CH, W), I32), pltpu.SemaphoreType.DMA((nbuf,))],
    )(wp, experts, ptab)


def _tc_wsum_kernel(exp_ref, w_ref, tab_ref, y_ref):
    tb, hk = exp_ref.shape
    sub, lanes = tab_ref.shape[1:]
    for t in range(tb):
        lo_acc = jnp.zeros((sub, lanes), F32)
        hi_acc = jnp.zeros((sub, lanes), F32)
        for k in range(hk):
            row = tab_ref[exp_ref[t, k]]
            w = w_ref[t, k]
            lo_acc = lo_acc + w * lax.bitcast_convert_type(lax.shift_left(row, 16), F32)
            hi_acc = hi_acc + w * lax.bitcast_convert_type(row & HIGH_HALF, F32)
        y_ref[t, pl.ds(0, sub), :] = lo_acc
        y_ref[t, pl.ds(sub, sub), :] = hi_acc


def _tc_wsum(w, experts, ptab):
    T, hk = w.shape
    E, W = ptab.shape
    sub = W // TC_LANES
    tb = min(TC_WSUM_TILE, T)
    smem = pl.BlockSpec((tb, hk), lambda i: (i, 0), memory_space=pltpu.SMEM)
    y = pl.pallas_call(
        _tc_wsum_kernel, grid=(T // tb,),
        in_specs=[smem, smem, _resident((E, sub, TC_LANES))],
        out_specs=pl.BlockSpec((tb, 2 * sub, TC_LANES), lambda i: (i, 0, 0)),
        out_shape=jax.ShapeDtypeStruct((T, 2 * sub, TC_LANES), F32),
        cost_estimate=pl.CostEstimate(flops=4 * T * hk * W, transcendentals=0,
                                      bytes_accessed=4 * (T * hk * W + E * W + 2 * T * W + 2 * T * hk)),
        compiler_params=pltpu.CompilerParams(dimension_semantics=("arbitrary",),
                                             vmem_limit_bytes=VMEM_LIMIT_BYTES),
    )(experts, w, ptab.reshape(E, sub, TC_LANES))
    return y.reshape(T, 2 * W)


def _act_kernel(dots_ref, gate_ref, w_ref, wp_ref):
    w = _gelu(dots_ref[...]) * gate_ref[...]
    w_ref[...] = w
    wp_ref[...] = _pack_splat(w)


def _act(dots, gates):
    T, hk = dots.shape
    tt = min(ROW_TILE, T)
    spec = pl.BlockSpec((tt, hk), lambda i: (i, 0))
    return pl.pallas_call(
        _act_kernel, grid=(T // tt,), in_specs=[spec, spec], out_specs=[spec, spec],
        out_shape=[jax.ShapeDtypeStruct((T, hk), F32), jax.ShapeDtypeStruct((T, hk), I32)],
        compiler_params=pltpu.CompilerParams(dimension_semantics=("arbitrary",)),
    )(dots, gates)


def _final_kernel(h_ref, y_ref, g_ref, *rest):
    o_ref = rest[-1]
    o_ref[...] = _rmsnorm(h_ref[...] + y_ref[...], g_ref[...])


def _final(h, y, gain, out, row0, total_rows):
    T, D = h.shape
    tt = min(ROW_TILE, T)
    assert row0 % tt == 0
    spec = pl.BlockSpec((tt, D), lambda i: (i, 0))
    out_spec = pl.BlockSpec((tt, D), lambda i: (i + row0 // tt, 0))
    in_specs = [spec, spec, pl.BlockSpec((1, D), lambda i: (0, 0))]
    args = (h, y, gain)
    aliases = {}
    if out is not None:
        in_specs.append(pl.BlockSpec(memory_space=pl.ANY))
        args += (out,)
        aliases = {3: 0}
    return pl.pallas_call(
        _final_kernel, grid=(T // tt,), in_specs=in_specs, out_specs=out_spec,
        out_shape=jax.ShapeDtypeStruct((total_rows, D), F32),
        input_output_aliases=aliases,
        compiler_params=pltpu.CompilerParams(dimension_semantics=("arbitrary",)),
    )(*args)


def kernel(x, mem, norm1_gain, w_in, pool_w, pool_scale, sgu_ln_gain, sgu_ln_bias, sgu_w_s, sgu_b_s,
           sgu_w_out, mem_norm_gain, xa_w_kv, xa_w_out, w_out, norm2_gain, peer_w_q, peer_keys1,
           peer_keys2, peer_u, peer_v, final_norm_gain):
    B, S, D = x.shape
    depth = w_in.shape[0]
    lo = lambda w: w.astype(MXU_DTYPE)
    row = lambda w: w.reshape(1, -1)
    assert depth == 1, "the final RMSNorm is fused with the last layer's residual add"
    l = 0
    weights = (row(norm1_gain[l]), lo(w_in[l]), lo(pool_w[l]), row(pool_scale[l]),
               row(sgu_ln_gain[l]), row(sgu_ln_bias[l]), sgu_w_s[l], sgu_b_s[l].T,
               lo(sgu_w_out[l]), lo(xa_w_out[l]), lo(w_out[l]))
    select_w = (row(norm2_gain[l]), lo(peer_w_q[l]), lo(peer_keys1[l]), lo(peer_keys2[l]))
    u_packed, v_packed = _pack_table(peer_u[l]), _pack_table(peer_v[l])
    k, v = _memkv(mem, row(mem_norm_gain[l]), lo(xa_w_kv[l]))
    fgain = row(final_norm_gain)
    chunks = BATCH_CHUNKS if sum(BATCH_CHUNKS) == B else (B,)
    b0, tc_sums, tc_parts, sc_parts = 0, [], [], []
    for c, nb in enumerate(chunks):
        rows = nb * S
        h = _mixer(x, k, v, weights, b0, nb).reshape(rows, D)
        xn, experts, gates = _select(h, *select_w)
        dots = _sc_dots(xn, experts, u_packed)
        if c > 0:
            dots, tc_sums[c - 1] = lax.optimization_barrier((dots, tc_sums[c - 1]))
        w, wp = _act(dots, gates)
        num, den = TC_WSUM_SHARE_LAST if c == len(chunks) - 1 else TC_WSUM_SHARE
        n_tc = (rows * num // den) // SC_ROW_QUANTUM * SC_ROW_QUANTUM
        tc_sums.append(_tc_wsum(w[:n_tc], experts[:n_tc], v_packed))
        tc_parts.append((h[:n_tc], b0 * S))
        sc_parts.append((h[n_tc:], _sc_wsum(wp[n_tc:], experts[n_tc:], v_packed), b0 * S + n_tc))
        b0 += nb
    parts = [(h_tc, y_tc, row0) for (h_tc, row0), y_tc in zip(tc_parts, tc_sums)]
    order = parts[:-1] + sc_parts[:-1] + parts[-1:] + sc_parts[-1:]
    out = None
    for h_part, y_part, row0 in order:
        out = _final(h_part, y_part, fgain, out, row0, B * S)
    return out.reshape(B, S, D)
```

```python
import math

import jax
import jax.numpy as jnp
from jax import lax
from jax.experimental import pallas as pl
from jax.experimental.pallas import tpu as pltpu
from jax.experimental.pallas import tpu_sc as plsc

F32 = jnp.float32
I32 = jnp.int32
MXU_DTYPE = jnp.bfloat16

RMS_EPS = 1e-6
LN_EPS = 1e-5
POOL_WINDOWS = (2, 4, 8, 16)
POOL_HALO = 16
SGU_CHUNK = 128
SGU_HEADS = 8
XA_HEADS = 4
PEER_HEADS = 8
PEER_N_KEYS = 128
PEER_TOPK = 16

V7X_VMEM_BYTES = 64 * 1024 * 1024
VMEM_LIMIT_BYTES = V7X_VMEM_BYTES - 8 * 1024 * 1024

SEQ_TILE = 256
SELECT_TILE = 256
ROW_TILE = 512
TC_WSUM_TILE = 8
TC_LANES = 128
TC_SUBLANES = 8
BATCH_CHUNKS = (1, 1, 2, 2, 2)

SC_CORES = 2
SC_SUBCORES = 16
SC_LANES = 16
SC_WORKERS = SC_CORES * SC_SUBCORES
SC_TOKENS = 32
SC_WSUM_TOKENS = 16
SC_CHUNK = 32
SC_DOTS_NBUF = 5
SC_WSUM_NBUF = 4
HIGH_HALF = -65536
SC_ROW_QUANTUM = SC_WORKERS * SC_WSUM_TOKENS
TC_WSUM_SHARE = (9, 16)
TC_WSUM_SHARE_LAST = (7, 16)


def _rmsnorm(x, gain):
    return x * lax.rsqrt(jnp.mean(x * x, axis=-1, keepdims=True) + RMS_EPS) * gain


def _gelu(x):
    return 0.5 * x * (1.0 + lax.erf(x * (1.0 / math.sqrt(2.0))))


def _mm(a, b):
    return jnp.dot(a.astype(MXU_DTYPE), b.astype(MXU_DTYPE), preferred_element_type=F32)


def _mm_nt(a, b):
    return lax.dot_general(a.astype(MXU_DTYPE), b.astype(MXU_DTYPE),
                           (((1,), (1,)), ((), ())), preferred_element_type=F32)


def _memkv_kernel(mem_ref, gain_ref, wkv_ref, k_ref, v_ref):
    width = k_ref.shape[-1]
    kv = _mm(_rmsnorm(mem_ref[0], gain_ref[...]), wkv_ref[...])
    k_ref[0] = kv[:, :width].astype(k_ref.dtype)
    v_ref[0] = kv[:, width:].astype(v_ref.dtype)


def _memkv(mem, gain, w_kv):
    B, M, D = mem.shape
    width = w_kv.shape[1] // 2
    const = lambda b: (0, 0)
    return pl.pallas_call(
        _memkv_kernel,
        grid=(B,),
        in_specs=[pl.BlockSpec((1, M, D), lambda b: (b, 0, 0)),
                  pl.BlockSpec((1, D), const),
                  pl.BlockSpec(w_kv.shape, const)],
        out_specs=[pl.BlockSpec((1, M, width), lambda b: (b, 0, 0))] * 2,
        out_shape=[jax.ShapeDtypeStruct((B, M, width), MXU_DTYPE)] * 2,
        compiler_params=pltpu.CompilerParams(dimension_semantics=("arbitrary",),
                                             vmem_limit_bytes=VMEM_LIMIT_BYTES),
    )(mem, gain, w_kv)


def _mixer_kernel(x_ref, k_ref, v_ref, n1g_ref, win_ref, poolw_ref, pools_ref, lng_ref, lnb_ref,
                  ws_ref, bst_ref, sguwo_ref, xawo_ref, wout_ref, h_ref, tail_ref):
    ts, d = x_ref.shape[1], x_ref.shape[2]
    s_idx = pl.program_id(1)
    x = x_ref[0]
    nb = _rmsnorm(x, n1g_ref[...]).astype(MXU_DTYPE)

    def proj(col):
        return jnp.dot(nb, win_ref[:, col * d:(col + 1) * d], preferred_element_type=F32)

    @pl.when(s_idx == 0)
    def _():
        tail_ref[...] = jnp.zeros_like(tail_ref)

    p = proj(0)
    ext = jnp.concatenate([tail_ref[...], p], axis=0)
    tail_ref[...] = p[ts - POOL_HALO:, :]
    pos = s_idx * ts + lax.broadcasted_iota(jnp.int32, (ts, 1), 0)
    group = d // len(POOL_WINDOWS)
    y_pool = []
    for g, w in enumerate(POOL_WINDOWS):
        acc = ext[:, g * group:(g + 1) * group]
        shift = 1
        while shift < w:
            acc = acc + pltpu.roll(acc, shift, 0)
            shift *= 2
        count = jnp.minimum(pos + 1, w).astype(F32)
        diff = acc[POOL_HALO:, :] / count - p[:, g * group:(g + 1) * group]
        y_pool.append(_mm(diff, poolw_ref[g]))
    y_pool = jnp.concatenate(y_pool, axis=1) * pools_ref[...]

    u = _gelu(proj(1))
    v = _gelu(proj(2))
    mu = jnp.mean(v, axis=-1, keepdims=True)
    vc = v - mu
    var = jnp.mean(vc * vc, axis=-1, keepdims=True)
    v = (vc * lax.rsqrt(var + LN_EPS) * lng_ref[...] + lnb_ref[...]).astype(MXU_DTYPE)
    hd = d // SGU_HEADS
    causal = (lax.broadcasted_iota(jnp.int32, (SGU_CHUNK, SGU_CHUNK), 0)
              >= lax.broadcasted_iota(jnp.int32, (SGU_CHUNK, SGU_CHUNK), 1))
    mixed_rows = []
    w_masked = [jnp.where(causal, ws_ref[h], 0.0).astype(MXU_DTYPE) for h in range(SGU_HEADS)]
    for c in range(ts // SGU_CHUNK):
        rows = slice(c * SGU_CHUNK, (c + 1) * SGU_CHUNK)
        heads = []
        for h in range(SGU_HEADS):
            mixed = jnp.dot(w_masked[h], v[rows, h * hd:(h + 1) * hd], preferred_element_type=F32)
            heads.append(mixed + bst_ref[:, h:h + 1])
        mixed_rows.append(jnp.concatenate(heads, axis=1))
    mixed = jnp.concatenate(mixed_rows, axis=0) if len(mixed_rows) > 1 else mixed_rows[0]
    y_sgu = _mm(u * mixed, sguwo_ref[...])

    q = proj(3).astype(MXU_DTYPE)
    xd = d // XA_HEADS
    outs = []
    for h in range(XA_HEADS):
        cols = slice(h * xd, (h + 1) * xd)
        s = _mm_nt(q[:, cols], k_ref[0, :, cols]) * (xd ** -0.5)
        e = jnp.exp(s - jnp.max(s, axis=-1, keepdims=True))
        probs = e / jnp.sum(e, axis=-1, keepdims=True)
        outs.append(_mm(probs, v_ref[0, :, cols]))
    y_xa = _mm(jnp.concatenate(outs, axis=1), xawo_ref[...])

    merged = (jax.nn.sigmoid(proj(4)) * y_pool + jax.nn.sigmoid(proj(5)) * y_sgu
              + jax.nn.sigmoid(proj(6)) * y_xa)
    h_ref[0] = x + _mm(merged, wout_ref[...])


def _resident(shape):
    zeros = (0,) * len(shape)
    return pl.BlockSpec(shape, lambda *_: zeros, pipeline_mode=pl.Buffered(1))


def _mixer(x, k, v, weights, b0, nb):
    _, S, D = x.shape
    ts = min(SEQ_TILE, S)
    M = k.shape[1]
    return pl.pallas_call(
        _mixer_kernel,
        grid=(nb, S // ts),
        in_specs=[pl.BlockSpec((1, ts, D), lambda b, s: (b + b0, s, 0)),
                  pl.BlockSpec((1, M, k.shape[2]), lambda b, s: (b + b0, 0, 0)),
                  pl.BlockSpec((1, M, v.shape[2]), lambda b, s: (b + b0, 0, 0))]
                 + [_resident(w.shape) for w in weights],
        out_specs=pl.BlockSpec((1, ts, D), lambda b, s: (b, s, 0)),
        out_shape=jax.ShapeDtypeStruct((nb, S, D), F32),
        scratch_shapes=[pltpu.VMEM((POOL_HALO, D), F32)],
        compiler_params=pltpu.CompilerParams(dimension_semantics=("arbitrary", "arbitrary"),
                                             vmem_limit_bytes=VMEM_LIMIT_BYTES),
    )(x, k, v, *weights)


def _topk_rows(vals, payload=None):
    n_rows, width = vals.shape
    sub = TC_SUBLANES
    n_tiles = n_rows // sub
    assert n_rows % sub == 0
    tiles = [vals[j * sub:(j + 1) * sub, :] for j in range(n_tiles)]
    pay = None if payload is None else [payload[j * sub:(j + 1) * sub, :] for j in range(n_tiles)]
    sub_row = lax.broadcasted_iota(jnp.int32, (sub, width), 0).astype(F32)
    rows = [sub_row + float(j * sub) for j in range(n_tiles)]

    def best_of(items):
        while len(items) > 1:
            merged = []
            for a, b in zip(items[0::2], items[1::2]):
                keep_a = a[0] >= b[0]
                merged.append(tuple(jnp.where(keep_a, xa, xb) for xa, xb in zip(a, b)))
            if len(items) % 2:
                merged.append(items[-1])
            items = merged
        return items[0]

    out_v, out_i = [], []
    for _ in range(PEER_TOPK):
        if pay is None:
            v8, j8 = best_of([(tiles[j], float(j)) for j in range(n_tiles)])
        else:
            v8, j8, p8 = best_of([(tiles[j], float(j), pay[j]) for j in range(n_tiles)])
        r8 = j8 * float(sub) + sub_row
        m = jnp.max(v8, axis=0, keepdims=True)
        first = jnp.min(jnp.where(v8 == m, r8, float(n_rows)), axis=0, keepdims=True)
        out_v.append(m)
        if pay is None:
            out_i.append(first)
        else:
            out_i.append(jnp.max(jnp.where(r8 == first, p8, -1.0), axis=0, keepdims=True))
        tiles = [jnp.where(rows[j] == first, -jnp.inf, tiles[j]) for j in range(n_tiles)]
    return jnp.concatenate(out_v, axis=0), jnp.concatenate(out_i, axis=0)


def _candidates(v1, i1, v2, i2):
    sub = 8
    jrow = lax.broadcasted_iota(jnp.int32, (sub, 1), 0)
    vals = [v1[0:1] + v2]
    ids = [i1[0:1] * float(PEER_N_KEYS) + i2]
    for i in range(1, sub):
        keep = jrow < (PEER_TOPK // (i + 1))
        vals.append(jnp.where(keep, v1[i:i + 1] + v2[0:sub], -jnp.inf))
        ids.append(i1[i:i + 1] * float(PEER_N_KEYS) + i2[0:sub])
    vals.append(v1[sub:] + v2[0:1])
    ids.append(i1[sub:] * float(PEER_N_KEYS) + i2[0:1])
    return jnp.concatenate(vals, axis=0), jnp.concatenate(ids, axis=0)


def _select_kernel(h_ref, n2g_ref, wq_ref, k1_ref, k2_ref, xn_ref, exp_ref, gate_ref):
    xn = _rmsnorm(h_ref[...], n2g_ref[...])
    xn_ref[...] = _pack_halves(xn)
    q = _mm(xn, wq_ref[...])
    half = k1_ref.shape[1]
    experts, gates = [], []
    for h in range(PEER_HEADS):
        q1 = q[:, (2 * h) * half:(2 * h + 1) * half]
        q2 = q[:, (2 * h + 1) * half:(2 * h + 2) * half]
        v1, i1 = _topk_rows(_mm_nt(k1_ref[...], q1))
        v2, i2 = _topk_rows(_mm_nt(k2_ref[...], q2))
        best, eid = _topk_rows(*_candidates(v1, i1, v2, i2))
        e = jnp.exp(best - jnp.max(best, axis=0, keepdims=True))
        gates.append(e / jnp.sum(e, axis=0, keepdims=True))
        experts.append(eid)
    exp_ref[...] = jnp.concatenate(experts, axis=0).T.astype(jnp.int32)
    gate_ref[...] = jnp.concatenate(gates, axis=0).T


def _select(h, n2g, w_q, keys1, keys2):
    T, D = h.shape
    tt = min(SELECT_TILE, T)
    hk = PEER_HEADS * PEER_TOPK
    return pl.pallas_call(
        _select_kernel,
        grid=(T // tt,),
        in_specs=[pl.BlockSpec((tt, D), lambda i: (i, 0)),
                  _resident(n2g.shape), _resident(w_q.shape),
                  _resident(keys1.shape), _resident(keys2.shape)],
        out_specs=[pl.BlockSpec((tt, D // 2), lambda i: (i, 0)),
                   pl.BlockSpec((tt, hk), lambda i: (i, 0)),
                   pl.BlockSpec((tt, hk), lambda i: (i, 0))],
        out_shape=[jax.ShapeDtypeStruct((T, D // 2), I32),
                   jax.ShapeDtypeStruct((T, hk), jnp.int32),
                   jax.ShapeDtypeStruct((T, hk), F32)],
        compiler_params=pltpu.CompilerParams(dimension_semantics=("arbitrary",),
                                             vmem_limit_bytes=VMEM_LIMIT_BYTES),
    )(h, n2g, w_q, keys1, keys2)


def _pack_table(table):
    half = table.shape[1] // 2
    bits = lax.bitcast_convert_type(table.astype(jnp.bfloat16), jnp.uint16).astype(jnp.uint32)
    return lax.bitcast_convert_type((bits[:, half:] << 16) | bits[:, :half], I32)


def _bf16_bits(x):
    return lax.bitcast_convert_type(x.astype(jnp.bfloat16).astype(F32), I32)


def _pack_halves(x):
    half = x.shape[1] // 2
    return (_bf16_bits(x[:, half:]) & HIGH_HALF) | lax.shift_right_logical(_bf16_bits(x[:, :half]), 16)


def _pack_splat(x):
    bits = _bf16_bits(x)
    return (bits & HIGH_HALF) | lax.shift_right_logical(bits, 16)


def _unpack_words(w):
    lo = lax.bitcast_convert_type(lax.shift_left(w, jnp.full(w.shape, 16, I32)), F32)
    hi = lax.bitcast_convert_type(lax.bitwise_and(w, jnp.full(w.shape, HIGH_HALF, I32)), F32)
    return lo, hi


def _as_bf16(w):
    return plsc.bitcast(w, jnp.bfloat16)


def _halves_f32(p):
    return _unpack_words(plsc.bitcast(p, I32))


def _sc_kernel(body, out_type, scratch_types):
    mesh = plsc.VectorSubcoreMesh(core_axis_name="c", subcore_axis_name="s")
    return pl.kernel(body, out_type=out_type, mesh=mesh, scratch_types=scratch_types,
                     compiler_params=pltpu.CompilerParams(needs_layout_passes=False))


def _sc_worker_base(tokens_per_worker):
    return (lax.axis_index("s") * SC_CORES + lax.axis_index("c")) * tokens_per_worker


def _sc_dots(xp, experts, ptab):
    T, W = xp.shape
    HK = experts.shape[1]
    L, TG, CH = SC_LANES, SC_TOKENS, SC_CHUNK
    tpw = T // SC_WORKERS
    cpt = HK // CH
    nchunk = TG * cpt
    assert W == ptab.shape[1] and T % (SC_WORKERS * TG) == 0 and HK % CH == 0 and CH % L == 0
    assert W % (2 * L) == 0
    nbuf = SC_DOTS_NBUF
    ahead = nbuf - 1
    ngroups = tpw // TG
    assert nchunk > ahead

    def body(x_hbm, idx_hbm, tab_hbm, out_hbm, x_v, idx_v, out_v, rows_v, acc_v, sem, sem_in):
        base = _sc_worker_base(tpw)
        lane = lax.iota(I32, L)

        def stage(g, gs):
            t0 = base + g * TG
            return (pltpu.make_async_copy(x_hbm.at[pl.ds(t0, TG)], x_v.at[gs], sem_in.at[0, gs]),
                    pltpu.make_async_copy(idx_hbm.at[pl.ds(t0, TG)], idx_v.at[gs], sem_in.at[1, gs]))

        def gather(gs, ci, slot):
            tok, c = ci // cpt, ci % cpt
            return pltpu.make_async_copy(tab_hbm.at[idx_v.at[gs, tok, pl.ds(c * CH, CH)]],
                                         rows_v.at[slot], sem.at[slot])

        def compute(gs, ci, slot):
            tok, c = ci // cpt, ci % cpt
            for eg in range(CH // L):
                accs = [jnp.zeros((L,), F32) for _ in range(L)]
                for j in range(0, W // L, 2):
                    xa = _as_bf16(x_v[gs, tok, pl.ds(j * L, L)])
                    xb = _as_bf16(x_v[gs, tok, pl.ds((j + 1) * L, L)])
                    for kk in range(L):
                        ra = _as_bf16(rows_v[slot, eg * L + kk, pl.ds(j * L, L)])
                        rb = _as_bf16(rows_v[slot, eg * L + kk, pl.ds((j + 1) * L, L)])
                        lo, hi = _halves_f32(ra * xa + rb * xb)
                        accs[kk] = accs[kk] + lo + hi
                for kk in range(L):
                    acc_v[kk, :] = accs[kk]

                s = jnp.zeros((L,), F32)
                for j in range(L):
                    s = s + plsc.load_gather(acc_v, [lane, jnp.full((L,), j, I32)])
                out_v[tok, pl.ds(c * CH + eg * L, L)] = s

        for cp in stage(0, 0):
            cp.start()
        for cp in stage(0, 0):
            cp.wait()
        for i in range(ahead):
            gather(0, i, i).start()

        @pl.loop(0, ngroups)
        def _(g):
            gs = g % 2
            more = g + 1 < ngroups

            @pl.when(more)
            def _():
                for cp in stage(g + 1, 1 - gs):
                    cp.start()

            @pl.loop(0, nchunk)
            def _(ci):
                gid = g * nchunk + ci
                nxt = ci + ahead

                @pl.when(nxt < nchunk)
                def _():
                    gather(gs, nxt, (gid + ahead) % nbuf).start()

                @pl.when(jnp.logical_and(nxt >= nchunk, more))
                def _():
                    @pl.when(nxt == nchunk)
                    def _():
                        for cp in stage(g + 1, 1 - gs):
                            cp.wait()

                    gather(1 - gs, nxt - nchunk, (gid + ahead) % nbuf).start()

                slot = gid % nbuf
                gather(gs, ci, slot).wait()
                compute(gs, ci, slot)

            pltpu.sync_copy(out_v, out_hbm.at[pl.ds(base + g * TG, TG)])

    return _sc_kernel(
        body, jax.ShapeDtypeStruct((T, HK), F32),
        [pltpu.VMEM((2, TG, W), I32), pltpu.VMEM((2, TG, HK), I32), pltpu.VMEM((TG, HK), F32),
         pltpu.VMEM((nbuf, CH, W), I32), pltpu.VMEM((L, L), F32),
         pltpu.SemaphoreType.DMA((nbuf,)), pltpu.SemaphoreType.DMA((2, 2))],
    )(xp, experts, ptab)


def _sc_wsum(wp, experts, ptab):
    T, HK = wp.shape
    W = ptab.shape[1]
    D = 2 * W
    L, TG, CH = SC_LANES, SC_WSUM_TOKENS, SC_CHUNK
    tpw = T // SC_WORKERS
    cpt = HK // CH
    nchunk = TG * cpt
    DG = 8
    assert T % (SC_WORKERS * TG) == 0 and HK % CH == 0 and CH % 2 == 0 and W % (DG * L) == 0
    nbuf = SC_WSUM_NBUF
    ahead = nbuf - 1
    ngroups = tpw // TG
    assert nchunk > ahead

    def body(w_hbm, idx_hbm, tab_hbm, out_hbm, w_v, idx_v, y_v, rows_v, sem, sem_in):
        base = _sc_worker_base(tpw)

        def stage(g, gs):
            t0 = base + g * TG
            return (pltpu.make_async_copy(w_hbm.at[pl.ds(t0, TG)], w_v.at[gs], sem_in.at[0, gs]),
                    pltpu.make_async_copy(idx_hbm.at[pl.ds(t0, TG)], idx_v.at[gs], sem_in.at[1, gs]))

        def gather(gs, ci, slot):
            tok, c = ci // cpt, ci % cpt
            return pltpu.make_async_copy(tab_hbm.at[idx_v.at[gs, tok, pl.ds(c * CH, CH)]],
                                         rows_v.at[slot], sem.at[slot])

        def compute(gs, ci, slot):
            tok, c = ci // cpt, ci % cpt
            gsv = jnp.full((L,), gs, I32)
            tokv = jnp.full((L,), tok, I32)
            cont = jnp.full((L,), c, I32) != 0

            @pl.loop(0, W // (DG * L))
            def _(dg):
                off = dg * (DG * L)
                zero = jnp.zeros((L,), F32)
                alo = [jnp.where(cont, y_v[tok, pl.ds(off + j * L, L)], zero) for j in range(DG)]
                ahi = [jnp.where(cont, y_v[tok, pl.ds(W + off + j * L, L)], zero) for j in range(DG)]
                for kk in range(0, CH, 2):
                    wa = _as_bf16(plsc.load_gather(w_v, [gsv, tokv, jnp.full((L,), c * CH + kk, I32)]))
                    wb = _as_bf16(plsc.load_gather(w_v, [gsv, tokv, jnp.full((L,), c * CH + kk + 1, I32)]))
                    for j in range(DG):
                        ra = _as_bf16(rows_v[slot, kk, pl.ds(off + j * L, L)])
                        rb = _as_bf16(rows_v[slot, kk + 1, pl.ds(off + j * L, L)])
                        lo, hi = _halves_f32(wa * ra + wb * rb)
                        alo[j] = alo[j] + lo
                        ahi[j] = ahi[j] + hi
                for j in range(DG):
                    y_v[tok, pl.ds(off + j * L, L)] = alo[j]
                    y_v[tok, pl.ds(W + off + j * L, L)] = ahi[j]

        for cp in stage(0, 0):
            cp.start()
        for cp in stage(0, 0):
            cp.wait()
        for i in range(ahead):
            gather(0, i, i).start()

        @pl.loop(0, ngroups)
        def _(g):
            gs = g % 2
            more = g + 1 < ngroups

            @pl.when(more)
            def _():
                for cp in stage(g + 1, 1 - gs):
                    cp.start()

            @pl.loop(0, nchunk)
            def _(ci):
                gid = g * nchunk + ci
                nxt = ci + ahead

                @pl.when(nxt < nchunk)
                def _():
                    gather(gs, nxt, (gid + ahead) % nbuf).start()

                @pl.when(jnp.logical_and(nxt >= nchunk, more))
                def _():
                    @pl.when(nxt == nchunk)
                    def _():
                        for cp in stage(g + 1, 1 - gs):
                            cp.wait()

                    gather(1 - gs, nxt - nchunk, (gid + ahead) % nbuf).start()

                slot = gid % nbuf
                gather(gs, ci, slot).wait()
                compute(gs, ci, slot)

            pltpu.sync_copy(y_v, out_hbm.at[pl.ds(base + g * TG, TG)])

    return _sc_kernel(
        body, jax.ShapeDtypeStruct((T, D), F32),
        [pltpu.VMEM((2, TG, HK), I32), pltpu.VMEM((2, TG, HK), I32), pltpu.VMEM((TG, D), F32),
         pltpu.VMEM((nbuf, CH, W), I32), pltpu.SemaphoreType.DMA((nbuf,)),
         pltpu.SemaphoreType.DMA((2, 2))],
    )(wp, experts, ptab)


def _tc_wsum_kernel(exp_ref, w_ref, tab_ref, y_ref):
    tb, hk = exp_ref.shape
    sub, lanes = tab_ref.shape[1:]
    for t in range(tb):
        lo_acc = jnp.zeros((sub, lanes), F32)
        hi_acc = jnp.zeros((sub, lanes), F32)
        for k in range(hk):
            row = tab_ref[exp_ref[t, k]]
            w = w_ref[t, k]
            lo_acc = lo_acc + w * lax.bitcast_convert_type(lax.shift_left(row, 16), F32)
            hi_acc = hi_acc + w * lax.bitcast_convert_type(row & HIGH_HALF, F32)
        y_ref[t, pl.ds(0, sub), :] = lo_acc
        y_ref[t, pl.ds(sub, sub), :] = hi_acc


def _tc_wsum(w, experts, ptab):
    T, hk = w.shape
    E, W = ptab.shape
    sub = W // TC_LANES
    tb = min(TC_WSUM_TILE, T)
    smem = pl.BlockSpec((tb, hk), lambda i: (i, 0), memory_space=pltpu.SMEM)
    y = pl.pallas_call(
        _tc_wsum_kernel, grid=(T // tb,),
        in_specs=[smem, smem, _resident((E, sub, TC_LANES))],
        out_specs=pl.BlockSpec((tb, 2 * sub, TC_LANES), lambda i: (i, 0, 0)),
        out_shape=jax.ShapeDtypeStruct((T, 2 * sub, TC_LANES), F32),
        cost_estimate=pl.CostEstimate(flops=4 * T * hk * W, transcendentals=0,
                                      bytes_accessed=4 * (T * hk * W + E * W + 2 * T * W + 2 * T * hk)),
        compiler_params=pltpu.CompilerParams(dimension_semantics=("arbitrary",),
                                             vmem_limit_bytes=VMEM_LIMIT_BYTES),
    )(experts, w, ptab.reshape(E, sub, TC_LANES))
    return y.reshape(T, 2 * W)


def _act_kernel(dots_ref, gate_ref, w_ref, wp_ref):
    w = _gelu(dots_ref[...]) * gate_ref[...]
    w_ref[...] = w
    wp_ref[...] = _pack_splat(w)


def _act(dots, gates):
    T, hk = dots.shape
    tt = min(ROW_TILE, T)
    spec = pl.BlockSpec((tt, hk), lambda i: (i, 0))
    return pl.pallas_call(
        _act_kernel, grid=(T // tt,), in_specs=[spec, spec], out_specs=[spec, spec],
        out_shape=[jax.ShapeDtypeStruct((T, hk), F32), jax.ShapeDtypeStruct((T, hk), I32)],
        compiler_params=pltpu.CompilerParams(dimension_semantics=("arbitrary",)),
    )(dots, gates)


def _final_kernel(h_ref, y_ref, g_ref, *rest):
    o_ref = rest[-1]
    o_ref[...] = _rmsnorm(h_ref[...] + y_ref[...], g_ref[...])


def _final(h, y, gain, out, row0, total_rows):
    T, D = h.shape
    tt = min(ROW_TILE, T)
    assert row0 % tt == 0
    spec = pl.BlockSpec((tt, D), lambda i: (i, 0))
    out_spec = pl.BlockSpec((tt, D), lambda i: (i + row0 // tt, 0))
    in_specs = [spec, spec, pl.BlockSpec((1, D), lambda i: (0, 0))]
    args = (h, y, gain)
    aliases = {}
    if out is not None:
        in_specs.append(pl.BlockSpec(memory_space=pl.ANY))
        args += (out,)
        aliases = {3: 0}
    return pl.pallas_call(
        _final_kernel, grid=(T // tt,), in_specs=in_specs, out_specs=out_spec,
        out_shape=jax.ShapeDtypeStruct((total_rows, D), F32),
        input_output_aliases=aliases,
        compiler_params=pltpu.CompilerParams(dimension_semantics=("arbitrary",)),
    )(*args)


def kernel(x, mem, norm1_gain, w_in, pool_w, pool_scale, sgu_ln_gain, sgu_ln_bias, sgu_w_s, sgu_b_s,
           sgu_w_out, mem_norm_gain, xa_w_kv, xa_w_out, w_out, norm2_gain, peer_w_q, peer_keys1,
           peer_keys2, peer_u, peer_v, final_norm_gain):
    B, S, D = x.shape
    depth = w_in.shape[0]
    lo = lambda w: w.astype(MXU_DTYPE)
    row = lambda w: w.reshape(1, -1)
    assert depth == 1, "the final RMSNorm is fused with the last layer's residual add"
    l = 0
    weights = (row(norm1_gain[l]), lo(w_in[l]), lo(pool_w[l]), row(pool_scale[l]),
               row(sgu_ln_gain[l]), row(sgu_ln_bias[l]), sgu_w_s[l], sgu_b_s[l].T,
               lo(sgu_w_out[l]), lo(xa_w_out[l]), lo(w_out[l]))
    select_w = (row(norm2_gain[l]), lo(peer_w_q[l]), lo(peer_keys1[l]), lo(peer_keys2[l]))
    u_packed, v_packed = _pack_table(peer_u[l]), _pack_table(peer_v[l])
    k, v = _memkv(mem, row(mem_norm_gain[l]), lo(xa_w_kv[l]))
    fgain = row(final_norm_gain)
    chunks = BATCH_CHUNKS if sum(BATCH_CHUNKS) == B else (B,)
    b0, tc_sums, tc_parts, sc_parts = 0, [], [], []
    for c, nb in enumerate(chunks):
        rows = nb * S
        h = _mixer(x, k, v, weights, b0, nb).reshape(rows, D)
        xn, experts, gates = _select(h, *select_w)
        dots = _sc_dots(xn, experts, u_packed)
        if c > 0:
            dots, tc_sums[c - 1] = lax.optimization_barrier((dots, tc_sums[c - 1]))
        w, wp = _act(dots, gates)
        num, den = TC_WSUM_SHARE_LAST if c == len(chunks) - 1 else TC_WSUM_SHARE
        n_tc = (rows * num // den) // SC_ROW_QUANTUM * SC_ROW_QUANTUM
        tc_sums.append(_tc_wsum(w[:n_tc], experts[:n_tc], v_packed))
        tc_parts.append((h[:n_tc], b0 * S))
        sc_parts.append((h[n_tc:], _sc_wsum(wp[n_tc:], experts[n_tc:], v_packed), b0 * S + n_tc))
        b0 += nb
    parts = [(h_tc, y_tc, row0) for (h_tc, row0), y_tc in zip(tc_parts, tc_sums)]
    order = parts[:-1] + sc_parts[:-1] + parts[-1:] + sc_parts[-1:]
    out = None
    for h_part, y_part, row0 in order:
        out = _final(h_part, y_part, fgain, out, row0, B * S)
    return out.reshape(B, S, D)
```
